```python
import math
import jax, jax.numpy as jnp
from jax import lax
import numpy as np

D_MODEL = 2048
BATCH = 8
SEQ = 4096
DEPTH = 2

GRID_W = 64
CTX_LEN = 256
HEAD_DIM = 128
MIX_WIDTH = D_MODEL
N_HEADS_TOTAL = MIX_WIDTH // HEAD_DIM
SWA_HEADS = N_HEADS_TOTAL // 4
SWA_KV_HEADS = SWA_HEADS // 2
SWA_WINDOW = 128
SWA_BLOCK = 128
DN_HEADS = N_HEADS_TOTAL // 2
DN_CONV = 5
DN_CHUNK = 64
NA_HEADS = N_HEADS_TOTAL - SWA_HEADS - DN_HEADS
NA_KH = 8
NA_KW = 16
FFN_HIDDEN = ((8 * D_MODEL // 3 + 255) // 256) * 256
ROPE_THETA = 10000.0
EPS = 1e-6

SWA_Q = SWA_HEADS * HEAD_DIM
SWA_KV = SWA_KV_HEADS * HEAD_DIM
DN_DIM = DN_HEADS * HEAD_DIM
NA_DIM = NA_HEADS * HEAD_DIM
IN_SPLITS = (SWA_Q, SWA_KV, SWA_KV, 3 * DN_DIM, DN_DIM, 4 * DN_HEADS, NA_DIM, NA_DIM, NA_DIM)
N_IN = sum(IN_SPLITS)

kernel_name = 'hybrid_swa_gdn_natten_prefix_dit'


def rms_norm(x, g):
    xf = x.astype(jnp.float32)
    y = xf * lax.rsqrt(jnp.mean(xf * xf, axis=-1, keepdims=True) + EPS)
    return (y * g.astype(jnp.float32)).astype(x.dtype)


def l2_normalize(t):
    return t * lax.rsqrt(jnp.sum(t * t, axis=-1, keepdims=True) + EPS)


def modulate(h, shift, scale):
    return h * (1 + scale) + shift


def heads(t, n):
    return t.reshape(t.shape[:-1] + (n, HEAD_DIM))


def split_cols(p):
    outs, start = [], 0
    for w in IN_SPLITS:
        outs.append(p[..., start:start + w])
        start += w
    return outs


def swiglu(h, w_gate, w_up, w_down):
    return (jax.nn.silu(h @ w_gate) * (h @ w_up)) @ w_down


def axial_rope_tables(L, dtype):
    t = jnp.arange(L, dtype=jnp.int32)
    row = (t // GRID_W).astype(jnp.float32)
    col = (t % GRID_W).astype(jnp.float32)
    n_freq = HEAD_DIM // 4
    inv = ROPE_THETA ** (-jnp.arange(n_freq, dtype=jnp.float32) / n_freq)
    ang_r = row[:, None] * inv[None, :]
    ang_c = col[:, None] * inv[None, :]
    ang = jnp.concatenate([ang_r, ang_r, ang_c, ang_c], axis=-1)
    return jnp.cos(ang).astype(dtype), jnp.sin(ang).astype(dtype)


def apply_axial_rope(x, cos, sin):
    xa = x.reshape(x.shape[:-1] + (2, 2, HEAD_DIM // 4))
    rot = jnp.stack([-xa[..., 1, :], xa[..., 0, :]], axis=-2).reshape(x.shape)
    return x * cos[None, :, None, :] + rot * sin[None, :, None, :]


def context_attention(qc, kc, vc, sink):
    Bn, Lc, G, R, hd = qc.shape
    s = jnp.einsum('bqgrd,bkgd->bgrqk', qc, kc).astype(jnp.float32) * hd ** -0.5
    if sink is not None:
        s_sink = jnp.broadcast_to(sink.astype(jnp.float32)[None, :, :, None, None], (Bn, G, R, Lc, 1))
        s = jnp.concatenate([s, s_sink], axis=-1)
    p = jax.nn.softmax(s, axis=-1)[..., :Lc].astype(qc.dtype)
    o = jnp.einsum('bgrqk,bkgd->bqgrd', p, vc)
    return o.reshape(Bn, Lc, G * R * hd)


def swa_attention(q, k, v, kc, vc, sink):
    Bn, L, G, R, hd = q.shape
    T = SWA_BLOCK
    nb = L // T
    Lc = kc.shape[1]
    qb = q.reshape(Bn, nb, T, G, R, hd)

    def band(t):
        tp = jnp.pad(t, ((0, 0), (T, T), (0, 0), (0, 0))).reshape(Bn, nb + 2, T, G, hd)
        return jnp.concatenate([tp[:, :-2], tp[:, 1:-1], tp[:, 2:]], axis=2)

    kb, vb = band(k), band(v)
    qpos = jnp.arange(nb)[:, None] * T + jnp.arange(T)[None, :]
    kpos = jnp.arange(nb)[:, None] * T - T + jnp.arange(3 * T)[None, :]
    rel = kpos[:, None, :] - qpos[:, :, None]
    valid = (jnp.abs(rel) <= SWA_WINDOW) & (kpos[:, None, :] >= 0) & (kpos[:, None, :] < L)
    scale = hd ** -0.5
    s_loc = jnp.einsum('bnqgrd,bnkgd->bgrnqk', qb, kb).astype(jnp.float32) * scale
    s_loc = jnp.where(valid[None, None, None], s_loc, -jnp.inf)
    s_ctx = jnp.einsum('bnqgrd,bcgd->bgrnqc', qb, kc).astype(jnp.float32) * scale
    s_sink = jnp.broadcast_to(sink.astype(jnp.float32)[None, :, :, None, None, None], (Bn, G, R, nb, T, 1))
    p = jax.nn.softmax(jnp.concatenate([s_loc, s_ctx, s_sink], axis=-1), axis=-1).astype(q.dtype)
    o = (jnp.einsum('bgrnqk,bnkgd->bnqgrd', p[..., :3 * T], vb)
         + jnp.einsum('bgrnqc,bcgd->bnqgrd', p[..., 3 * T:3 * T + Lc], vc))
    return o.reshape(Bn, L, G * R * hd)


def neighborhood_attention(q, k, v, kc, vc, rpb):
    Bn, L, H, hd = q.shape
    W = GRID_W
    R = L // W
    KH = min(NA_KH, R)
    KW = NA_KW
    scale = hd ** -0.5
    qg = q.reshape(Bn, R, W, H, hd)
    kg = k.reshape(Bn, R, W, H, hd)
    vg = v.reshape(Bn, R, W, H, hd)
    rows = jnp.arange(R)
    row_start = jnp.clip(rows - KH // 2, 0, R - KH)
    row_idx = row_start[:, None] + jnp.arange(KH)[None, :]
    k_rows = kg[:, row_idx]
    v_rows = vg[:, row_idx]
    cols = jnp.arange(W)
    col_start = jnp.clip(cols - KW // 2, 0, W - KW)
    col_in = (cols[None, :] >= col_start[:, None]) & (cols[None, :] < col_start[:, None] + KW)
    dr = row_idx - rows[:, None] + (NA_KH - 1)
    dc = jnp.clip(cols[None, :] - cols[:, None], -(KW - 1), KW - 1) + (KW - 1)
    bias = rpb[:, dr[:, None, :, None], dc[None, :, None, :]].astype(jnp.float32)
    s_nb = jnp.einsum('brqhd,brjkhd->bhrqjk', qg, k_rows).astype(jnp.float32) * scale + bias[None]
    s_nb = jnp.where(col_in[:, None, :], s_nb, -jnp.inf)
    s_ctx = jnp.einsum('brqhd,bchd->bhrqc', qg, kc).astype(jnp.float32) * scale
    n_nb = KH * W
    s = jnp.concatenate([s_nb.reshape(Bn, H, R, W, n_nb), s_ctx], axis=-1)
    p = jax.nn.softmax(s, axis=-1).astype(q.dtype)
    p_nb = p[..., :n_nb].reshape(Bn, H, R, W, KH, W)
    o = (jnp.einsum('bhrqjk,brjkhd->brqhd', p_nb, v_rows)
         + jnp.einsum('bhrqc,bchd->brqhd', p[..., n_nb:], vc))
    return o.reshape(Bn, L, H * hd)


def short_conv_silu(x, w):
    C = x.shape[-1]
    pad = (DN_CONV - 1) // 2
    y = lax.conv_general_dilated(x, w[:, None, :].astype(x.dtype), window_strides=(1,),
                                 padding=[(pad, pad)], dimension_numbers=('NWC', 'WIO', 'NWC'),
                                 feature_group_count=C)
    return jax.nn.silu(y)


def gdn_inputs(qkv, ab, conv_w, A_log, dt_bias):
    y = short_conv_silu(qkv, conv_w).astype(jnp.float32)
    q, k, v = (heads(t, DN_HEADS) for t in jnp.split(y, 3, axis=-1))
    q = l2_normalize(q) * HEAD_DIM ** -0.5
    k = l2_normalize(k)
    ab = ab.astype(jnp.float32).reshape(ab.shape[:2] + (2, 2, DN_HEADS))
    log_a = -jnp.exp(A_log.astype(jnp.float32)) * jax.nn.softplus(ab[:, :, 0] + dt_bias.astype(jnp.float32))
    beta = jax.nn.sigmoid(ab[:, :, 1])
    return q, k, v, log_a, beta


def gdn_chunked(q, k, v, log_a, beta, state0):
    Bn, L, H, dk = q.shape
    dv = v.shape[-1]
    C = DN_CHUNK
    n = L // C

    def chunks(t):
        t = t.reshape((Bn, n, C, H) + t.shape[3:])
        return jnp.moveaxis(t, 3, 1)

    q, k, v, beta = chunks(q), chunks(k), chunks(v), chunks(beta)
    g = jnp.cumsum(chunks(log_a), axis=-1)
    idx = jnp.arange(C)
    incl = idx[:, None] >= idx[None, :]
    strict = idx[:, None] > idx[None, :]
    decay = jnp.exp(jnp.where(incl, g[..., :, None] - g[..., None, :], -jnp.inf))
    kb = k * beta[..., None]
    lower = jnp.where(strict, jnp.einsum('bhnid,bhnjd->bhnij', kb, k) * decay, 0.0)
    eye = jnp.eye(C, dtype=q.dtype)
    rhs = jnp.concatenate([v * beta[..., None], kb * jnp.exp(g)[..., None]], axis=-1)
    sol = lax.linalg.triangular_solve(lower + eye, rhs, left_side=True, lower=True, unit_diagonal=True)
    u, w = sol[..., :dv], sol[..., dv:]
    intra = jnp.einsum('bhnid,bhnjd->bhnij', q, k) * decay
    q_dec = q * jnp.exp(g)[..., None]
    g_last = g[..., -1]
    k_dec = k * jnp.exp(g_last[..., None] - g)[..., None]

    def step(S, xs):
        u_c, w_c, q_c, a_c, k_c, gl_c = xs
        v_new = u_c - jnp.einsum('bhcd,bhde->bhce', w_c, S)
        o_c = jnp.einsum('bhcd,bhde->bhce', q_c, S) + jnp.einsum('bhij,bhje->bhie', a_c, v_new)
        S = S * jnp.exp(gl_c)[..., None, None] + jnp.einsum('bhcd,bhce->bhde', k_c, v_new)
        return S, o_c

    xs = tuple(jnp.moveaxis(t, 2, 0) for t in (u, w, q_dec, intra, k_dec, g_last))
    state, o = lax.scan(step, state0, xs)
    o = jnp.transpose(o, (1, 0, 3, 2, 4)).reshape(Bn, L, H, dv)
    return o, state


def gated_rms_norm(o, z, g):
    y = rms_norm(o, g) * jax.nn.silu(z.reshape(o.shape).astype(jnp.float32))
    return y.reshape(o.shape[:2] + (-1,))


def gdn_mixer(qkv, ab, z, qkv_c, ab_c, z_c, conv_w, A_log, dt_bias, out_g, ctx_out):
    q, k, v, la, be = gdn_inputs(qkv, ab, conv_w, A_log, dt_bias)
    qc, kc, vc, lac, bec = gdn_inputs(qkv_c, ab_c, conv_w, A_log, dt_bias)
    s0 = jnp.zeros((q.shape[0], DN_HEADS, HEAD_DIM, HEAD_DIM), jnp.float32)
    flip = lambda t: t[:, ::-1]
    oc_f, s_f = gdn_chunked(qc, kc, vc, lac[:, :, 0], bec[:, :, 0], s0)
    oc_b, s_b = gdn_chunked(flip(qc), flip(kc), flip(vc), flip(lac[:, :, 1]), flip(bec[:, :, 1]), s0)
    o_f, _ = gdn_chunked(q, k, v, la[:, :, 0], be[:, :, 0], s_f)
    o_b, _ = gdn_chunked(flip(q), flip(k), flip(v), flip(la[:, :, 1]), flip(be[:, :, 1]), s_b)
    out = gated_rms_norm(o_f + flip(o_b), z, out_g).astype(qkv.dtype)
    out_c = gated_rms_norm(oc_f + flip(oc_b), z_c, out_g).astype(qkv.dtype) if ctx_out else None
    return out, out_c


def hybrid_layer(x, xc, c_silu, c_ctx_silu, cos, sin, w_ada, b_ada, norm1_g, norm2_g, w_in,
                 swa_q_g, swa_k_g, swa_sink, dn_conv_w, dn_A_log, dn_dt_bias, dn_out_g,
                 na_q_g, na_k_g, na_rpb, w_out, w_gate, w_up, w_down, ctx_out):
    G, R = SWA_KV_HEADS, SWA_HEADS // SWA_KV_HEADS
    mod = (c_silu @ w_ada + b_ada)[:, None, :]
    mod_c = c_ctx_silu @ w_ada + b_ada
    sh1, sc1, g1, sh2, sc2, g2 = jnp.split(mod, 6, axis=-1)
    csh1, csc1, cg1, csh2, csc2, cg2 = jnp.split(mod_c, 6, axis=-1)

    h = modulate(rms_norm(x, norm1_g), sh1, sc1)
    hc = modulate(rms_norm(xc, norm1_g), csh1, csc1)
    aq, ak, av, d_qkv, d_z, d_ab, nq, nk, nv = split_cols(h @ w_in)
    aqc, akc, avc, d_qkvc, d_zc, d_abc, nqc, nkc, nvc = split_cols(hc @ w_in)

    q_a = apply_axial_rope(rms_norm(heads(aq, SWA_HEADS), swa_q_g), cos, sin)
    k_a = apply_axial_rope(rms_norm(heads(ak, G), swa_k_g), cos, sin)
    v_a = heads(av, G)
    kc_a = rms_norm(heads(akc, G), swa_k_g)
    vc_a = heads(avc, G)
    sink = swa_sink.reshape(G, R)
    o_a = swa_attention(q_a.reshape(q_a.shape[:2] + (G, R, HEAD_DIM)), k_a, v_a, kc_a, vc_a, sink)

    o_b, o_bc = gdn_mixer(d_qkv, d_ab, d_z, d_qkvc, d_abc, d_zc, dn_conv_w, dn_A_log, dn_dt_bias, dn_out_g, ctx_out)

    q_n = rms_norm(heads(nq, NA_HEADS), na_q_g)
    k_n = rms_norm(heads(nk, NA_HEADS), na_k_g)
    v_n = heads(nv, NA_HEADS)
    kc_n = rms_norm(heads(nkc, NA_HEADS), na_k_g)
    vc_n = heads(nvc, NA_HEADS)
    o_c = neighborhood_attention(q_n, k_n, v_n, kc_n, vc_n, na_rpb)

    x = x + g1 * (jnp.concatenate([o_a, o_b, o_c], axis=-1) @ w_out)
    x = x + g2 * swiglu(modulate(rms_norm(x, norm2_g), sh2, sc2), w_gate, w_up, w_down)

    if ctx_out:
        qc_a = rms_norm(heads(aqc, SWA_HEADS), swa_q_g)
        o_ac = context_attention(qc_a.reshape(qc_a.shape[:2] + (G, R, HEAD_DIM)), kc_a, vc_a, sink)
        qc_n = rms_norm(heads(nqc, NA_HEADS), na_q_g)
        o_cc = context_attention(qc_n[..., None, :], kc_n, vc_n, None)
        xc = xc + cg1 * (jnp.concatenate([o_ac, o_bc, o_cc], axis=-1) @ w_out)
        xc = xc + cg2 * swiglu(modulate(rms_norm(xc, norm2_g), csh2, csc2), w_gate, w_up, w_down)
    return x, xc


def _fwd_setup_inputs(seed: int = 0) -> dict:
    key = jax.random.key(seed)
    ks = jax.random.split(key, 24)
    f32 = jnp.float32

    def nrm(k, shape, scale):
        return jax.random.normal(k, shape, f32) * scale

    dt = jnp.exp(jax.random.uniform(ks[14], (DEPTH, 2, DN_HEADS), f32, math.log(1e-3), math.log(1e-1)))
    return {
        'x': nrm(ks[0], (BATCH, SEQ, D_MODEL), 1.0),
        'c': nrm(ks[1], (BATCH, D_MODEL), 1.0),
        'ctx': nrm(ks[2], (BATCH, CTX_LEN, D_MODEL), 1.0),
        'c_ctx': nrm(ks[3], (D_MODEL,), 1.0),
        'w_ada': nrm(ks[4], (DEPTH, D_MODEL, 6 * D_MODEL), 0.5 * D_MODEL ** -0.5),
        'b_ada': nrm(ks[5], (DEPTH, 6 * D_MODEL), 0.01),
        'norm1_g': 1.0 + nrm(ks[6], (DEPTH, D_MODEL), 0.05),
        'norm2_g': 1.0 + nrm(ks[7], (DEPTH, D_MODEL), 0.05),
        'w_in': nrm(ks[8], (DEPTH, D_MODEL, N_IN), D_MODEL ** -0.5),
        'swa_q_g': 1.0 + nrm(ks[9], (DEPTH, HEAD_DIM), 0.05),
        'swa_k_g': 1.0 + nrm(ks[10], (DEPTH, HEAD_DIM), 0.05),
        'swa_sink': nrm(ks[11], (DEPTH, SWA_HEADS), 0.5),
        'dn_conv_w': nrm(ks[12], (DEPTH, DN_CONV, 3 * DN_DIM), DN_CONV ** -0.5),
        'dn_A_log': jnp.log(jax.random.uniform(ks[13], (DEPTH, 2, DN_HEADS), f32, 1.0, 16.0)),
        'dn_dt_bias': dt + jnp.log(-jnp.expm1(-dt)),
        'dn_out_g': 1.0 + nrm(ks[15], (DEPTH, HEAD_DIM), 0.05),
        'na_q_g': 1.0 + nrm(ks[16], (DEPTH, HEAD_DIM), 0.05),
        'na_k_g': 1.0 + nrm(ks[17], (DEPTH, HEAD_DIM), 0.05),
        'na_rpb': nrm(ks[18], (DEPTH, NA_HEADS, 2 * NA_KH - 1, 2 * NA_KW - 1), 0.1),
        'w_out': nrm(ks[19], (DEPTH, MIX_WIDTH, D_MODEL), MIX_WIDTH ** -0.5),
        'w_gate': nrm(ks[20], (DEPTH, D_MODEL, FFN_HIDDEN), D_MODEL ** -0.5),
        'w_up': nrm(ks[21], (DEPTH, D_MODEL, FFN_HIDDEN), D_MODEL ** -0.5),
        'w_down': nrm(ks[22], (DEPTH, FFN_HIDDEN, D_MODEL), FFN_HIDDEN ** -0.5),
    }


def _fwd_reference(x, c, ctx, c_ctx, w_ada, b_ada, norm1_g, norm2_g, w_in, swa_q_g, swa_k_g, swa_sink,
              dn_conv_w, dn_A_log, dn_dt_bias, dn_out_g, na_q_g, na_k_g, na_rpb, w_out,
              w_gate, w_up, w_down):
    L = x.shape[1]
    cos, sin = axial_rope_tables(L, x.dtype)
    c_silu = jax.nn.silu(c)
    c_ctx_silu = jax.nn.silu(c_ctx)
    xc = ctx
    for l in range(DEPTH):
        x, xc = hybrid_layer(x, xc, c_silu, c_ctx_silu, cos, sin, w_ada[l], b_ada[l], norm1_g[l], norm2_g[l],
                             w_in[l], swa_q_g[l], swa_k_g[l], swa_sink[l], dn_conv_w[l], dn_A_log[l],
                             dn_dt_bias[l], dn_out_g[l], na_q_g[l], na_k_g[l], na_rpb[l], w_out[l],
                             w_gate[l], w_up[l], w_down[l], ctx_out=(l < DEPTH - 1))
    return x


import jax as _jax
import jax.numpy as _jnp

TWIN_FORMAT = 'train_step'
FWD_PARAMS = ['x', 'c', 'ctx', 'c_ctx', 'w_ada', 'b_ada', 'norm1_g', 'norm2_g', 'w_in', 'swa_q_g', 'swa_k_g', 'swa_sink', 'dn_conv_w', 'dn_A_log', 'dn_dt_bias', 'dn_out_g', 'na_q_g', 'na_k_g', 'na_rpb', 'w_out', 'w_gate', 'w_up', 'w_down']
TWIN_WEIGHTS = ['c_ctx', 'w_ada', 'b_ada', 'norm1_g', 'norm2_g', 'w_in', 'swa_q_g', 'swa_k_g', 'swa_sink', 'dn_conv_w', 'dn_A_log', 'dn_dt_bias', 'dn_out_g', 'na_q_g', 'na_k_g', 'na_rpb', 'w_out', 'w_gate', 'w_up', 'w_down']
TWIN_DIFF_INPUT = 'x'
TWIN_INPUTS = ['x', 'c', 'ctx', 'c_ctx', 'w_ada', 'b_ada', 'norm1_g', 'norm2_g', 'w_in', 'swa_q_g', 'swa_k_g', 'swa_sink', 'dn_conv_w', 'dn_A_log', 'dn_dt_bias', 'dn_out_g', 'na_q_g', 'na_k_g', 'na_rpb', 'w_out', 'w_gate', 'w_up', 'w_down', 'loss_target', 'm_c_ctx', 'm_w_ada', 'm_b_ada', 'm_norm1_g', 'm_norm2_g', 'm_w_in', 'm_swa_q_g', 'm_swa_k_g', 'm_swa_sink', 'm_dn_conv_w', 'm_dn_A_log', 'm_dn_dt_bias', 'm_dn_out_g', 'm_na_q_g', 'm_na_k_g', 'm_na_rpb', 'm_w_out', 'm_w_gate', 'm_w_up', 'm_w_down', 'v_c_ctx', 'v_w_ada', 'v_b_ada', 'v_norm1_g', 'v_norm2_g', 'v_w_in', 'v_swa_q_g', 'v_swa_k_g', 'v_swa_sink', 'v_dn_conv_w', 'v_dn_A_log', 'v_dn_dt_bias', 'v_dn_out_g', 'v_na_q_g', 'v_na_k_g', 'v_na_rpb', 'v_w_out', 'v_w_gate', 'v_w_up', 'v_w_down']
TWIN_OUTPUTS = ['loss', 'grad_x', 'grad_c_ctx', 'grad_w_ada', 'grad_b_ada', 'grad_norm1_g', 'grad_norm2_g', 'grad_w_in', 'grad_swa_q_g', 'grad_swa_k_g', 'grad_swa_sink', 'grad_dn_conv_w', 'grad_dn_A_log', 'grad_dn_dt_bias', 'grad_dn_out_g', 'grad_na_q_g', 'grad_na_k_g', 'grad_na_rpb', 'grad_w_out', 'grad_w_gate', 'grad_w_up', 'grad_w_down', 'delta_c_ctx', 'delta_w_ada', 'delta_b_ada', 'delta_norm1_g', 'delta_norm2_g', 'delta_w_in', 'delta_swa_q_g', 'delta_swa_k_g', 'delta_swa_sink', 'delta_dn_conv_w', 'delta_dn_A_log', 'delta_dn_dt_bias', 'delta_dn_out_g', 'delta_na_q_g', 'delta_na_k_g', 'delta_na_rpb', 'delta_w_out', 'delta_w_gate', 'delta_w_up', 'delta_w_down', 'new_m_c_ctx', 'new_m_w_ada', 'new_m_b_ada', 'new_m_norm1_g', 'new_m_norm2_g', 'new_m_w_in', 'new_m_swa_q_g', 'new_m_swa_k_g', 'new_m_swa_sink', 'new_m_dn_conv_w', 'new_m_dn_A_log', 'new_m_dn_dt_bias', 'new_m_dn_out_g', 'new_m_na_q_g', 'new_m_na_k_g', 'new_m_na_rpb', 'new_m_w_out', 'new_m_w_gate', 'new_m_w_up', 'new_m_w_down', 'new_v_c_ctx', 'new_v_w_ada', 'new_v_b_ada', 'new_v_norm1_g', 'new_v_norm2_g', 'new_v_w_in', 'new_v_swa_q_g', 'new_v_swa_k_g', 'new_v_swa_sink', 'new_v_dn_conv_w', 'new_v_dn_A_log', 'new_v_dn_dt_bias', 'new_v_dn_out_g', 'new_v_na_q_g', 'new_v_na_k_g', 'new_v_na_rpb', 'new_v_w_out', 'new_v_w_gate', 'new_v_w_up', 'new_v_w_down']
TWIN_LEAF_KINDS = {'loss': 'loss', 'grad_x': 'grad_x', 'grad_c_ctx': 'grad_w', 'grad_w_ada': 'grad_w', 'grad_b_ada': 'grad_w', 'grad_norm1_g': 'grad_w', 'grad_norm2_g': 'grad_w', 'grad_w_in': 'grad_w', 'grad_swa_q_g': 'grad_w', 'grad_swa_k_g': 'grad_w', 'grad_swa_sink': 'grad_w', 'grad_dn_conv_w': 'grad_w', 'grad_dn_A_log': 'grad_w', 'grad_dn_dt_bias': 'grad_w', 'grad_dn_out_g': 'grad_w', 'grad_na_q_g': 'grad_w', 'grad_na_k_g': 'grad_w', 'grad_na_rpb': 'grad_w', 'grad_w_out': 'grad_w', 'grad_w_gate': 'grad_w', 'grad_w_up': 'grad_w', 'grad_w_down': 'grad_w', 'delta_c_ctx': 'delta_w', 'delta_w_ada': 'delta_w', 'delta_b_ada': 'delta_w', 'delta_norm1_g': 'delta_w', 'delta_norm2_g': 'delta_w', 'delta_w_in': 'delta_w', 'delta_swa_q_g': 'delta_w', 'delta_swa_k_g': 'delta_w', 'delta_swa_sink': 'delta_w', 'delta_dn_conv_w': 'delta_w', 'delta_dn_A_log': 'delta_w', 'delta_dn_dt_bias': 'delta_w', 'delta_dn_out_g': 'delta_w', 'delta_na_q_g': 'delta_w', 'delta_na_k_g': 'delta_w', 'delta_na_rpb': 'delta_w', 'delta_w_out': 'delta_w', 'delta_w_gate': 'delta_w', 'delta_w_up': 'delta_w', 'delta_w_down': 'delta_w', 'new_m_c_ctx': 'new_m', 'new_m_w_ada': 'new_m', 'new_m_b_ada': 'new_m', 'new_m_norm1_g': 'new_m', 'new_m_norm2_g': 'new_m', 'new_m_w_in': 'new_m', 'new_m_swa_q_g': 'new_m', 'new_m_swa_k_g': 'new_m', 'new_m_swa_sink': 'new_m', 'new_m_dn_conv_w': 'new_m', 'new_m_dn_A_log': 'new_m', 'new_m_dn_dt_bias': 'new_m', 'new_m_dn_out_g': 'new_m', 'new_m_na_q_g': 'new_m', 'new_m_na_k_g': 'new_m', 'new_m_na_rpb': 'new_m', 'new_m_w_out': 'new_m', 'new_m_w_gate': 'new_m', 'new_m_w_up': 'new_m', 'new_m_w_down': 'new_m', 'new_v_c_ctx': 'new_v', 'new_v_w_ada': 'new_v', 'new_v_b_ada': 'new_v', 'new_v_norm1_g': 'new_v', 'new_v_norm2_g': 'new_v', 'new_v_w_in': 'new_v', 'new_v_swa_q_g': 'new_v', 'new_v_swa_k_g': 'new_v', 'new_v_swa_sink': 'new_v', 'new_v_dn_conv_w': 'new_v', 'new_v_dn_A_log': 'new_v', 'new_v_dn_dt_bias': 'new_v', 'new_v_dn_out_g': 'new_v', 'new_v_na_q_g': 'new_v', 'new_v_na_k_g': 'new_v', 'new_v_na_rpb': 'new_v', 'new_v_w_out': 'new_v', 'new_v_w_gate': 'new_v', 'new_v_w_up': 'new_v', 'new_v_w_down': 'new_v'}


def _forward(args):
    return _fwd_reference(*[args[k] for k in FWD_PARAMS])


def _output_shape():
    def fwd():
        inp = _fwd_setup_inputs(0)
        return _fwd_reference(*[inp[k] for k in FWD_PARAMS])
    out = _jax.eval_shape(fwd)
    return out.shape, out.dtype

N_MICROBATCH = 1
ADAM_LR = 0.001
ADAM_B1 = 0.9
ADAM_B2 = 0.999
ADAM_EPS = 1e-08
ADAM_WD = 0.01
ADAM_STEP = 10
PER_EXAMPLE_BATCH_AXIS = {'x': 0, 'c': 0, 'ctx': 0, 'loss_target': 0}
SHARED_INPUTS = []
_WEIGHT_DTYPES = {'c_ctx': _jnp.float32, 'w_ada': _jnp.float32, 'b_ada': _jnp.float32, 'norm1_g': _jnp.float32, 'norm2_g': _jnp.float32, 'w_in': _jnp.float32, 'swa_q_g': _jnp.float32, 'swa_k_g': _jnp.float32, 'swa_sink': _jnp.float32, 'dn_conv_w': _jnp.float32, 'dn_A_log': _jnp.float32, 'dn_dt_bias': _jnp.float32, 'dn_out_g': _jnp.float32, 'na_q_g': _jnp.float32, 'na_k_g': _jnp.float32, 'na_rpb': _jnp.float32, 'w_out': _jnp.float32, 'w_gate': _jnp.float32, 'w_up': _jnp.float32, 'w_down': _jnp.float32}
MOMENT_SCALE = {'c_ctx': 1.046782e-01, 'w_ada': 3.096122e-01, 'b_ada': 8.305300e-01, 'norm1_g': 3.435460e-01, 'norm2_g': 1.520630e+00, 'w_in': 6.083215e-02, 'swa_q_g': 3.356965e-02, 'swa_k_g': 3.317830e-02, 'swa_sink': 1.683709e-02, 'dn_conv_w': 6.702213e-02, 'dn_A_log': 2.948075e-01, 'dn_dt_bias': 2.832570e-01, 'dn_out_g': 5.377334e+00, 'na_q_g': 3.949332e-02, 'na_k_g': 3.946618e-02, 'na_rpb': 1.871277e-03, 'w_out': 6.464728e-02, 'w_gate': 3.046014e-02, 'w_up': 2.229856e-02, 'w_down': 3.482505e-02}


def _to_microbatches(a, axis):
    t = _jnp.moveaxis(a, axis, 0)
    t = t.reshape((N_MICROBATCH, t.shape[0] // N_MICROBATCH) + t.shape[1:])
    return _jnp.moveaxis(t, 1, axis + 1)


def setup_inputs(seed: int = 0) -> dict:
    inp = _fwd_setup_inputs(seed)
    key = _jax.random.fold_in(_jax.random.key(seed), 7919)
    shape, _ = _output_shape()
    out = dict(inp)
    out["loss_target"] = _jax.random.normal(_jax.random.fold_in(key, 0), shape, _jnp.float32)
    for i, name in enumerate(TWIN_WEIGHTS):
        w = inp[name].astype(_jnp.float32)
        if MOMENT_SCALE is None:
            s = _jnp.sqrt(_jnp.mean(_jnp.square(w)) + 1e-30)
        else:
            s = MOMENT_SCALE[name]
        km, kv = _jax.random.split(_jax.random.fold_in(key, i + 1))
        out[name] = w
        out["m_" + name] = s * _jax.random.normal(km, w.shape, _jnp.float32)
        out["v_" + name] = (s * s) * _jax.random.uniform(kv, w.shape, _jnp.float32, 0.5, 1.5)
    if N_MICROBATCH > 1:
        for name, axis in PER_EXAMPLE_BATCH_AXIS.items():
            out[name] = _to_microbatches(out[name], axis)
    return {'x': out['x'], 'c': out['c'], 'ctx': out['ctx'], 'c_ctx': out['c_ctx'], 'w_ada': out['w_ada'], 'b_ada': out['b_ada'], 'norm1_g': out['norm1_g'], 'norm2_g': out['norm2_g'], 'w_in': out['w_in'], 'swa_q_g': out['swa_q_g'], 'swa_k_g': out['swa_k_g'], 'swa_sink': out['swa_sink'], 'dn_conv_w': out['dn_conv_w'], 'dn_A_log': out['dn_A_log'], 'dn_dt_bias': out['dn_dt_bias'], 'dn_out_g': out['dn_out_g'], 'na_q_g': out['na_q_g'], 'na_k_g': out['na_k_g'], 'na_rpb': out['na_rpb'], 'w_out': out['w_out'], 'w_gate': out['w_gate'], 'w_up': out['w_up'], 'w_down': out['w_down'], 'loss_target': out['loss_target'], 'm_c_ctx': out['m_c_ctx'], 'm_w_ada': out['m_w_ada'], 'm_b_ada': out['m_b_ada'], 'm_norm1_g': out['m_norm1_g'], 'm_norm2_g': out['m_norm2_g'], 'm_w_in': out['m_w_in'], 'm_swa_q_g': out['m_swa_q_g'], 'm_swa_k_g': out['m_swa_k_g'], 'm_swa_sink': out['m_swa_sink'], 'm_dn_conv_w': out['m_dn_conv_w'], 'm_dn_A_log': out['m_dn_A_log'], 'm_dn_dt_bias': out['m_dn_dt_bias'], 'm_dn_out_g': out['m_dn_out_g'], 'm_na_q_g': out['m_na_q_g'], 'm_na_k_g': out['m_na_k_g'], 'm_na_rpb': out['m_na_rpb'], 'm_w_out': out['m_w_out'], 'm_w_gate': out['m_w_gate'], 'm_w_up': out['m_w_up'], 'm_w_down': out['m_w_down'], 'v_c_ctx': out['v_c_ctx'], 'v_w_ada': out['v_w_ada'], 'v_b_ada': out['v_b_ada'], 'v_norm1_g': out['v_norm1_g'], 'v_norm2_g': out['v_norm2_g'], 'v_w_in': out['v_w_in'], 'v_swa_q_g': out['v_swa_q_g'], 'v_swa_k_g': out['v_swa_k_g'], 'v_swa_sink': out['v_swa_sink'], 'v_dn_conv_w': out['v_dn_conv_w'], 'v_dn_A_log': out['v_dn_A_log'], 'v_dn_dt_bias': out['v_dn_dt_bias'], 'v_dn_out_g': out['v_dn_out_g'], 'v_na_q_g': out['v_na_q_g'], 'v_na_k_g': out['v_na_k_g'], 'v_na_rpb': out['v_na_rpb'], 'v_w_out': out['v_w_out'], 'v_w_gate': out['v_w_gate'], 'v_w_up': out['v_w_up'], 'v_w_down': out['v_w_down']}


def _loss(weights, diff, rest, loss_target):
    with _jax.named_scope("forward"):
        args = {**rest, TWIN_DIFF_INPUT: diff, **{k: w.astype(_WEIGHT_DTYPES[k]) for k, w in weights.items()}}
        y = _forward(args)
    with _jax.named_scope("loss_head"):
        err = _jnp.square(y.astype(_jnp.float32) - loss_target)
        return 0.5 * _jnp.sum(_jnp.mean(err, axis=-1)) if err.ndim else 0.5 * err


def _adamw(w, g, m, v):
    m = ADAM_B1 * m + (1.0 - ADAM_B1) * g
    v = ADAM_B2 * v + (1.0 - ADAM_B2) * _jnp.square(g)
    m_hat = m / (1.0 - ADAM_B1 ** ADAM_STEP)
    v_hat = v / (1.0 - ADAM_B2 ** ADAM_STEP)
    delta = -ADAM_LR * (m_hat / (_jnp.sqrt(v_hat) + ADAM_EPS) + ADAM_WD * w)
    return delta, m, v


def reference(x, c, ctx, c_ctx, w_ada, b_ada, norm1_g, norm2_g, w_in, swa_q_g, swa_k_g, swa_sink, dn_conv_w, dn_A_log, dn_dt_bias, dn_out_g, na_q_g, na_k_g, na_rpb, w_out, w_gate, w_up, w_down, loss_target, m_c_ctx, m_w_ada, m_b_ada, m_norm1_g, m_norm2_g, m_w_in, m_swa_q_g, m_swa_k_g, m_swa_sink, m_dn_conv_w, m_dn_A_log, m_dn_dt_bias, m_dn_out_g, m_na_q_g, m_na_k_g, m_na_rpb, m_w_out, m_w_gate, m_w_up, m_w_down, v_c_ctx, v_w_ada, v_b_ada, v_norm1_g, v_norm2_g, v_w_in, v_swa_q_g, v_swa_k_g, v_swa_sink, v_dn_conv_w, v_dn_A_log, v_dn_dt_bias, v_dn_out_g, v_na_q_g, v_na_k_g, v_na_rpb, v_w_out, v_w_gate, v_w_up, v_w_down):
    given = dict(x=x, c=c, ctx=ctx, c_ctx=c_ctx, w_ada=w_ada, b_ada=b_ada, norm1_g=norm1_g, norm2_g=norm2_g, w_in=w_in, swa_q_g=swa_q_g, swa_k_g=swa_k_g, swa_sink=swa_sink, dn_conv_w=dn_conv_w, dn_A_log=dn_A_log, dn_dt_bias=dn_dt_bias, dn_out_g=dn_out_g, na_q_g=na_q_g, na_k_g=na_k_g, na_rpb=na_rpb, w_out=w_out, w_gate=w_gate, w_up=w_up, w_down=w_down, loss_target=loss_target, m_c_ctx=m_c_ctx, m_w_ada=m_w_ada, m_b_ada=m_b_ada, m_norm1_g=m_norm1_g, m_norm2_g=m_norm2_g, m_w_in=m_w_in, m_swa_q_g=m_swa_q_g, m_swa_k_g=m_swa_k_g, m_swa_sink=m_swa_sink, m_dn_conv_w=m_dn_conv_w, m_dn_A_log=m_dn_A_log, m_dn_dt_bias=m_dn_dt_bias, m_dn_out_g=m_dn_out_g, m_na_q_g=m_na_q_g, m_na_k_g=m_na_k_g, m_na_rpb=m_na_rpb, m_w_out=m_w_out, m_w_gate=m_w_gate, m_w_up=m_w_up, m_w_down=m_w_down, v_c_ctx=v_c_ctx, v_w_ada=v_w_ada, v_b_ada=v_b_ada, v_norm1_g=v_norm1_g, v_norm2_g=v_norm2_g, v_w_in=v_w_in, v_swa_q_g=v_swa_q_g, v_swa_k_g=v_swa_k_g, v_swa_sink=v_swa_sink, v_dn_conv_w=v_dn_conv_w, v_dn_A_log=v_dn_A_log, v_dn_dt_bias=v_dn_dt_bias, v_dn_out_g=v_dn_out_g, v_na_q_g=v_na_q_g, v_na_k_g=v_na_k_g, v_na_rpb=v_na_rpb, v_w_out=v_w_out, v_w_gate=v_w_gate, v_w_up=v_w_up, v_w_down=v_w_down)
    weights = {n: given[n] for n in TWIN_WEIGHTS}
    shared = {n: given[n] for n in SHARED_INPUTS}
    per_example = {n: given[n] for n in ['x', 'c', 'ctx']}
    grad_fn = _jax.value_and_grad(_loss, argnums=(0, 1))

    def one_microbatch(ex, loss_target):
        ex = dict(ex)
        diff = ex.pop(TWIN_DIFF_INPUT)
        return grad_fn(weights, diff, {**shared, **ex}, loss_target)

    if N_MICROBATCH == 1:
        loss, (grad_w, grad_x) = one_microbatch(per_example, given["loss_target"])
    else:
        def body(carry, xs):
            loss_sum, grad_sum = carry
            l_k, (gw_k, gx_k) = one_microbatch(xs[0], xs[1])
            with _jax.named_scope("update"):
                return (loss_sum + l_k, _jax.tree.map(_jnp.add, grad_sum, gw_k)), gx_k

        init = (_jnp.zeros((), _jnp.float32), _jax.tree.map(_jnp.zeros_like, weights))
        (loss, grad_w), grad_x = _jax.lax.scan(body, init, (per_example, given["loss_target"]))
    with _jax.named_scope("update"):
        delta_w, new_m, new_v = {}, {}, {}
        for n in TWIN_WEIGHTS:
            delta_w[n], new_m[n], new_v[n] = _adamw(weights[n], grad_w[n], given["m_" + n], given["v_" + n])
    return (loss, grad_x, *[grad_w[n] for n in TWIN_WEIGHTS], *[delta_w[n] for n in TWIN_WEIGHTS],
            *[new_m[n] for n in TWIN_WEIGHTS], *[new_v[n] for n in TWIN_WEIGHTS])
```

```python
import functools
import math

import jax
import jax.numpy as jnp
import numpy as np
from jax import lax
from jax.experimental import pallas as pl
from jax.experimental.pallas import tpu as pltpu

F32, BF16 = jnp.float32, jnp.bfloat16
MESH = pl.DeviceIdType.MESH

GRID_W = 64
HEAD = 128
SWA_WINDOW = 128
DN_CONV = 5
DN_CHUNK = 64
NA_KH, NA_KW = 8, 16
ROPE_THETA = 10000.0
EPS = 1e-6
ADAM_LR, ADAM_B1, ADAM_B2, ADAM_EPS, ADAM_WD, ADAM_STEP = 0.001, 0.9, 0.999, 1e-08, 0.01, 10
N_CHIPS = 4
N_DEV = 8
TOK = 256
VMEM_LIMIT = 56 * 2 ** 20

_CALL_KW = {}


def _cparams(sem=None, **kw):
    if sem is not None:
        kw["dimension_semantics"] = sem
    return pltpu.CompilerParams(vmem_limit_bytes=VMEM_LIMIT, **kw)


def _pick(n, cands):
    for cnd in cands:
        if n % cnd == 0:
            return cnd
    raise ValueError(f"no tile for {n} in {cands}")


def _axes():
    return lax.axis_index("x"), lax.axis_index("y"), lax.axis_index("c")


def _matmul(name, a, b, kind, out_dtype=F32, tm=None, tn=None, tk=None):
    if kind == "nn":
        (M, K), (K2, N) = a.shape, b.shape
    elif kind == "nt":
        (M, K), (N, K2) = a.shape, b.shape
    else:
        (K, M), (K2, N) = a.shape, b.shape
    assert K == K2, (name, a.shape, b.shape)
    tm = tm or _pick(M, (1024, 512, 256, 128) if kind == "tn" else (1088, 1024, 704, 512, 256, 128, 64, 32, 16, 8))
    tn = tn or _pick(N, (512, 256, 128))
    tk = tk or (K if K <= 4352 else _pick(K, (3328, 2816, 2048, 1024, 512)))
    nk = K // tk
    dims = {"nn": (((1,), (0,)), ((), ())), "nt": (((1,), (1,)), ((), ())), "tn": (((0,), (0,)), ((), ()))}[kind]

    def body(a_ref, b_ref, o_ref, *scr):
        part = lax.dot_general(a_ref[...].astype(BF16), b_ref[...].astype(BF16), dims, preferred_element_type=F32)
        if nk == 1:
            o_ref[...] = part.astype(o_ref.dtype)
        else:
            acc = scr[0]
            k = pl.program_id(2)

            @pl.when(k == 0)
            def _():
                acc[...] = part

            @pl.when(k > 0)
            def _():
                acc[...] += part

            @pl.when(k == nk - 1)
            def _():
                o_ref[...] = acc[...].astype(o_ref.dtype)

    a_spec = {"nn": pl.BlockSpec((tm, tk), lambda i, j, k: (i, k)), "nt": pl.BlockSpec((tm, tk), lambda i, j, k: (i, k)),
              "tn": pl.BlockSpec((tk, tm), lambda i, j, k: (k, i))}[kind]
    b_spec = {"nn": pl.BlockSpec((tk, tn), lambda i, j, k: (k, j)), "nt": pl.BlockSpec((tn, tk), lambda i, j, k: (j, k)),
              "tn": pl.BlockSpec((tk, tn), lambda i, j, k: (k, j))}[kind]
    return pl.pallas_call(
        body, name=name, grid=(M // tm, N // tn, nk), in_specs=[a_spec, b_spec],
        out_specs=pl.BlockSpec((tm, tn), lambda i, j, k: (i, j)), out_shape=jax.ShapeDtypeStruct((M, N), out_dtype),
        scratch_shapes=[pltpu.VMEM((tm, tn), F32)] if nk > 1 else [],
        compiler_params=_cparams(("parallel", "parallel", "arbitrary")), **_CALL_KW)(a, b)


def _exchange(name, ins, out_shapes, plan, n_local, n_remote, aliases=None):
    n_in, n_out = len(ins), len(out_shapes)

    def body(*refs):
        in_refs, out_refs = refs[:n_in], refs[n_in:n_in + n_out]
        send_sems, recv_sems, loc_sems = refs[n_in + n_out:]
        x, y, c = _axes()
        local, remote = plan(x, y, c, in_refs, out_refs)
        assert len(local) == n_local and len(remote) == n_remote, (name, len(local), len(remote))
        lcs = [pltpu.make_async_copy(s, d, loc_sems.at[i]) for i, (s, d) in enumerate(local)]
        rcs = [pltpu.make_async_remote_copy(src_ref=s, dst_ref=d, send_sem=send_sems.at[i], recv_sem=recv_sems.at[i],
                                            device_id=dev, device_id_type=MESH) for i, (s, d, dev) in enumerate(remote)]
        for cp in lcs + rcs:
            cp.start()
        for cp in rcs + lcs:
            cp.wait()

    any_spec = pl.BlockSpec(memory_space=pl.ANY)
    return pl.pallas_call(
        body, name=name, in_specs=[any_spec] * n_in, out_specs=[any_spec] * n_out,
        out_shape=[jax.ShapeDtypeStruct(s, d) for s, d in out_shapes],
        scratch_shapes=[pltpu.SemaphoreType.DMA((max(n_remote, 1),)), pltpu.SemaphoreType.DMA((max(n_remote, 1),)),
                        pltpu.SemaphoreType.DMA((max(n_local, 1),))],
        input_output_aliases=aliases or {},
        compiler_params=pltpu.CompilerParams(has_side_effects=True), **_CALL_KW)(*ins)


def _chip_of(k):
    return k // 2, k % 2


def _gather_weights(shards):
    n = len(shards)

    def plan_ici(x, y, c, ins, outs):
        me = 2 * x + y
        local, remote = [], []
        for w, g in zip(ins, outs):
            half = w.shape[1] // 2
            rows = pl.ds(c * half, half)
            local.append((w, g.at[me]))
            for j in (1, 2, 3):
                px, py = _chip_of(me ^ j)
                remote.append((w.at[:, rows], g.at[me, :, rows], (px, py, c)))
        return local, remote

    gath = _exchange("gather_w_ici", shards, [((N_CHIPS,) + w.shape, w.dtype) for w in shards], plan_ici, n, 3 * n)

    def plan_d2d(x, y, c, ins, outs):
        me = 2 * x + y
        remote = []
        for g in outs:
            half = g.shape[2] // 2
            rows = pl.ds(c * half, half)
            for j in (1, 2, 3):
                remote.append((g.at[me ^ j, :, rows], g.at[me ^ j, :, rows], (x, y, 1 - c)))
        return [], remote

    return _exchange("gather_w_d2d", gath, [(g.shape, g.dtype) for g in gath], plan_d2d, 0, 3 * n,
                     aliases={i: i for i in range(n)})


def _elementwise(name, fn, ins, out_dtypes, block_rows=None, n_out=None):
    shape = ins[0].shape
    lead, (R, C) = shape[:-2], shape[-2:]
    budget = (16 * 2 ** 20) // (8 * (len(ins) + len(out_dtypes)) * (-(-C // 128) * 128))
    br = block_rows or _pick(R, [r for r in (512, 256, 128, 352, 64, 32, 16, 8) if r <= max(budget, 8)] + [R])
    nl = len(lead)

    def body(*refs):
        outs = fn(*[r[...] for r in refs[:len(ins)]])
        for r, o in zip(refs[len(ins):], outs):
            r[...] = o.astype(r.dtype)

    blk = (None,) * nl + (br, C)
    spec = pl.BlockSpec(blk, lambda *g: tuple(g[:nl]) + (g[nl], 0))
    return pl.pallas_call(
        body, name=name, grid=tuple(lead) + (R // br,), in_specs=[spec] * len(ins), out_specs=[spec] * len(out_dtypes),
        out_shape=[jax.ShapeDtypeStruct(shape, d) for d in out_dtypes],
        compiler_params=_cparams(("parallel",) * (nl + 1)), **_CALL_KW)(*ins)


def _reduce_grads(parts):
    n = len(parts)

    def plan_a(x, y, c, ins, outs):
        remote = []
        for p, r in zip(ins, outs):
            half = p.shape[2] // 2
            remote.append((p.at[:, :, pl.ds((1 - c) * half, half)], r, (x, y, 1 - c)))
        return [], remote

    halves = [((p.shape[0], p.shape[1], p.shape[2] // 2, p.shape[3]), p.dtype) for p in parts]
    got = _exchange("reduce_g_d2d", parts, halves, plan_a, 0, n)

    c = lax.axis_index("c")
    pair = []
    for idx, (p, r) in enumerate(zip(parts, got)):
        half = p.shape[2] // 2
        br = _pick(half, (512, 256, 352, 128, 64, 32, 16))
        nb = half // br

        def body(c_ref, p_ref, r_ref, o_ref):
            o_ref[...] = (p_ref[...].astype(F32) + r_ref[...].astype(F32)).astype(o_ref.dtype)

        blk = (None, None, br, p.shape[3])
        pair.append(pl.pallas_call(
            body, name=f"reduce_g_pair{idx}",
            grid_spec=pltpu.PrefetchScalarGridSpec(
                num_scalar_prefetch=1, grid=(N_CHIPS, 2, nb),
                in_specs=[pl.BlockSpec(blk, lambda k, l, i, cr, nb=nb: (k, l, cr[0] * nb + i, 0)),
                          pl.BlockSpec(blk, lambda k, l, i, cr: (k, l, i, 0))],
                out_specs=pl.BlockSpec(blk, lambda k, l, i, cr: (k, l, i, 0))),
            out_shape=jax.ShapeDtypeStruct(r.shape, BF16),
            compiler_params=_cparams(("parallel",) * 3), **_CALL_KW)(jnp.reshape(c, (1,)).astype(jnp.int32), p, r))

    def plan_b(x, y, c, ins, outs):
        me = 2 * x + y
        local, remote = [], []
        for p, r in zip(ins, outs):
            local.append((p.at[me], r.at[me]))
            for j in (1, 2, 3):
                px, py = _chip_of(me ^ j)
                remote.append((p.at[me ^ j], r.at[me], (px, py, c)))
        return local, remote

    got = _exchange("reduce_g_ici", pair, [(p.shape, p.dtype) for p in pair], plan_b, n, 3 * n)

    sums = []
    for idx, r in enumerate(got):
        _, _, half, C = r.shape
        br = _pick(half, (512, 256, 352, 128, 64, 32, 16))

        def body(r_ref, o_ref):
            acc = r_ref[0].astype(F32)
            for k in range(1, N_CHIPS):
                acc = acc + r_ref[k].astype(F32)
            o_ref[...] = acc

        sums.append(pl.pallas_call(
            body, name=f"reduce_g_sum{idx}", grid=(2, half // br),
            in_specs=[pl.BlockSpec((N_CHIPS, None, br, C), lambda l, i: (0, l, i, 0))],
            out_specs=pl.BlockSpec((None, br, C), lambda l, i: (l, i, 0)),
            out_shape=jax.ShapeDtypeStruct((2, half, C), F32),
            compiler_params=_cparams(("parallel",) * 2), **_CALL_KW)(r))

    def plan_c(x, y, c, ins, outs):
        local, remote = [], []
        for s, f in zip(ins, outs):
            half = s.shape[1]
            rows = pl.ds(c * half, half)
            local.append((s, f.at[:, rows]))
            remote.append((s, f.at[:, rows], (x, y, 1 - c)))
        return local, remote

    return _exchange("reduce_g_bcast", sums, [((2, 2 * s.shape[1], s.shape[2]), F32) for s in sums], plan_c, n, n)


def _adamw_math(w, g, m, v):
    m = ADAM_B1 * m + (1.0 - ADAM_B1) * g
    v = ADAM_B2 * v + (1.0 - ADAM_B2) * (g * g)
    m_hat = m / (1.0 - ADAM_B1 ** ADAM_STEP)
    v_hat = v / (1.0 - ADAM_B2 ** ADAM_STEP)
    delta = -ADAM_LR * (m_hat / (jnp.sqrt(v_hat) + ADAM_EPS) + ADAM_WD * w)
    return delta, m, v


def _adamw(name, w, g, m, v):
    return _elementwise(name, _adamw_math, [w, g, m, v], [F32, F32, F32])


def _allgather_small(name, v):
    def plan(x, y, c, ins, outs):
        me = 4 * x + 2 * y + c
        (src,), (dst,) = ins, outs
        remote = []
        for j in range(1, N_DEV):
            p = me ^ j
            remote.append((src, dst.at[me], (p // 4, (p // 2) % 2, p % 2)))
        return [(src, dst.at[me])], remote

    return _exchange(name, [v], [((N_DEV,) + v.shape, v.dtype)], plan, 1, N_DEV - 1)[0]


def _seg_spec(rows, D, ctx_tiles):
    return pl.BlockSpec((None, rows, D), lambda i: (jnp.minimum(i // ctx_tiles, 1), 0, 0))


def _norm_mod(name, x, g, modv, r0, ctx_tiles):
    T, D = x.shape

    def body(x_ref, g_ref, m_ref, o_ref):
        xv = x_ref[...]
        r = lax.rsqrt(jnp.mean(xv * xv, axis=-1, keepdims=True) + EPS)
        y = xv * r * g_ref[...]
        o_ref[...] = (y * (1.0 + m_ref[r0 + 1:r0 + 2, :]) + m_ref[r0:r0 + 1, :]).astype(BF16)

    row = pl.BlockSpec((TOK, D), lambda i: (i, 0))
    return pl.pallas_call(
        body, name=name, grid=(T // TOK,), in_specs=[row, pl.BlockSpec((1, D), lambda i: (0, 0)), _seg_spec(6, D, ctx_tiles)],
        out_specs=row, out_shape=jax.ShapeDtypeStruct((T, D), BF16), compiler_params=_cparams(("parallel",)), **_CALL_KW)(x, g, modv)


def _norm_mod_bwd(name, x, g, modv, r0, dh, dh_b, dres, ctx_tiles):
    T, D = x.shape

    def body(x_ref, g_ref, m_ref, dh_ref, dhb_ref, dres_ref, dx_ref, dg_ref, dsh_ref, dsc_ref):
        i = pl.program_id(0)
        xv = x_ref[...]
        r = lax.rsqrt(jnp.mean(xv * xv, axis=-1, keepdims=True) + EPS)
        xn = xv * r
        y = xn * g_ref[...]
        dhv = dh_ref[...] + dhb_ref[...]

        @pl.when(i == 0)
        def _():
            dg_ref[...] = jnp.zeros_like(dg_ref)

        @pl.when((i == 0) | (i == ctx_tiles))
        def _():
            dsh_ref[...] = jnp.zeros_like(dsh_ref)
            dsc_ref[...] = jnp.zeros_like(dsc_ref)

        dsh_ref[...] += jnp.sum(dhv, axis=0, keepdims=True)
        dsc_ref[...] += jnp.sum(dhv * y, axis=0, keepdims=True)
        dy = dhv * (1.0 + m_ref[r0 + 1:r0 + 2, :])
        dg_ref[...] += jnp.sum(dy * xn, axis=0, keepdims=True)
        u = dy * g_ref[...]
        dx_ref[...] = dres_ref[...] + r * (u - xn * jnp.mean(u * xn, axis=-1, keepdims=True))

    row = pl.BlockSpec((TOK, D), lambda i: (i, 0))
    one = pl.BlockSpec((1, D), lambda i: (0, 0))
    return pl.pallas_call(
        body, name=name, grid=(T // TOK,), in_specs=[row, one, _seg_spec(6, D, ctx_tiles), row, row, row],
        out_specs=[row, one, _seg_spec(1, D, ctx_tiles), _seg_spec(1, D, ctx_tiles)],
        out_shape=[jax.ShapeDtypeStruct((T, D), F32), jax.ShapeDtypeStruct((1, D), F32),
                   jax.ShapeDtypeStruct((2, 1, D), F32), jax.ShapeDtypeStruct((2, 1, D), F32)],
        compiler_params=_cparams(("arbitrary",)), **_CALL_KW)(x, g, modv, dh, dh_b, dres)


def _resid_gate(name, x, y, modv, r, ctx_tiles):
    T, D = x.shape

    def body(x_ref, y_ref, m_ref, o_ref):
        o_ref[...] = x_ref[...] + m_ref[r:r + 1, :] * y_ref[...]

    row = pl.BlockSpec((TOK, D), lambda i: (i, 0))
    return pl.pallas_call(
        body, name=name, grid=(T // TOK,), in_specs=[row, row, _seg_spec(6, D, ctx_tiles)], out_specs=row,
        out_shape=jax.ShapeDtypeStruct((T, D), F32), compiler_params=_cparams(("parallel",)), **_CALL_KW)(x, y, modv)


def _resid_gate_bwd(name, dx, y, modv, r, ctx_tiles):
    T, D = dx.shape

    def body(dx_ref, y_ref, m_ref, dy_ref, dgt_ref):
        i = pl.program_id(0)

        @pl.when((i == 0) | (i == ctx_tiles))
        def _():
            dgt_ref[...] = jnp.zeros_like(dgt_ref)

        dxv = dx_ref[...]
        dgt_ref[...] += jnp.sum(dxv * y_ref[...], axis=0, keepdims=True)
        dy_ref[...] = (dxv * m_ref[r:r + 1, :]).astype(BF16)

    row = pl.BlockSpec((TOK, D), lambda i: (i, 0))
    return pl.pallas_call(
        body, name=name, grid=(T // TOK,), in_specs=[row, row, _seg_spec(6, D, ctx_tiles)],
        out_specs=[row, _seg_spec(1, D, ctx_tiles)],
        out_shape=[jax.ShapeDtypeStruct((T, D), BF16), jax.ShapeDtypeStruct((2, 1, D), F32)],
        compiler_params=_cparams(("arbitrary",)), **_CALL_KW)(dx, y, modv)


def _sigmoid(x):
    return 1.0 / (1.0 + jnp.exp(-x))


SWI_ROWS = 128


def _swiglu(name, gu):
    T, F2 = gu.shape
    F = F2 // 2

    def body(gu_ref, o_ref):
        g, u = gu_ref[:, :F], gu_ref[:, F:]
        o_ref[...] = ((g * _sigmoid(g)) * u).astype(BF16)

    return pl.pallas_call(
        body, name=name, grid=(T // SWI_ROWS,), in_specs=[pl.BlockSpec((SWI_ROWS, F2), lambda i: (i, 0))],
        out_specs=pl.BlockSpec((SWI_ROWS, F), lambda i: (i, 0)), out_shape=jax.ShapeDtypeStruct((T, F), BF16),
        compiler_params=_cparams(("parallel",)), **_CALL_KW)(gu)


def _swiglu_bwd(name, gu, dact):
    T, F2 = gu.shape
    F = F2 // 2

    def body(gu_ref, d_ref, o_ref):
        g, u, d = gu_ref[:, :F], gu_ref[:, F:], d_ref[...]
        s = _sigmoid(g)
        o_ref[:, :F] = (d * u * (s * (1.0 + g * (1.0 - s)))).astype(BF16)
        o_ref[:, F:] = (d * (g * s)).astype(BF16)

    return pl.pallas_call(
        body, name=name, grid=(T // SWI_ROWS,),
        in_specs=[pl.BlockSpec((SWI_ROWS, F2), lambda i: (i, 0)), pl.BlockSpec((SWI_ROWS, F), lambda i: (i, 0))],
        out_specs=pl.BlockSpec((SWI_ROWS, F2), lambda i: (i, 0)), out_shape=jax.ShapeDtypeStruct((T, F2), BF16),
        compiler_params=_cparams(("parallel",)), **_CALL_KW)(gu, dact)


def _loss_and_grad(name, y, target, ctx_tiles):
    T, D = y.shape

    def body(y_ref, t_ref, l_ref, dy_ref):
        i = pl.program_id(0)

        @pl.when(i == 0)
        def _():
            l_ref[...] = jnp.zeros_like(l_ref)

        lat = i >= ctx_tiles
        e = jnp.where(lat, y_ref[...] - t_ref[...], 0.0)
        dy_ref[...] = e * (1.0 / D)
        l_ref[...] += 0.5 * jnp.sum(jnp.sum(e * e, axis=-1, keepdims=True) * (1.0 / D), axis=0, keepdims=True)

    row = pl.BlockSpec((TOK, D), lambda i: (i, 0))
    return pl.pallas_call(
        body, name=name, grid=(T // TOK,),
        in_specs=[row, pl.BlockSpec((TOK, D), lambda i: (jnp.maximum(i - ctx_tiles, 0), 0))],
        out_specs=[pl.BlockSpec((8, 128), lambda i: (0, 0)), row],
        out_shape=[jax.ShapeDtypeStruct((8, 128), F32), jax.ShapeDtypeStruct((T, D), F32)],
        compiler_params=_cparams(("arbitrary",)), **_CALL_KW)(y, target)


def _rot_half(x):
    lane = lax.broadcasted_iota(jnp.int32, x.shape, 1)
    return jnp.where((lane % 64) < 32, -pltpu.roll(x, 96, 1), pltpu.roll(x, 32, 1))


def _head_prep(name, src, col_blk, n_heads, g, cos, sin, t_pad, norm, rope):
    T = src.shape[0]
    W = n_heads * HEAD
    nt = T // TOK

    def body(s_ref, g_ref, cos_ref, sin_ref, o_ref):
        i = pl.program_id(0)
        outs = []
        for h in range(n_heads):
            xv = s_ref[:, h * HEAD:(h + 1) * HEAD].astype(F32)
            if norm:
                xv = xv * lax.rsqrt(jnp.mean(xv * xv, axis=-1, keepdims=True) + EPS) * g_ref[...]
            if rope:
                xv = xv * cos_ref[...] + _rot_half(xv) * sin_ref[...]
            outs.append(jnp.where(i < nt, xv, 0.0).astype(BF16))
        o_ref[...] = jnp.concatenate(outs, axis=-1) if n_heads > 1 else outs[0]

    tab = pl.BlockSpec((TOK, HEAD), lambda i: (i, 0))
    return pl.pallas_call(
        body, name=name, grid=(t_pad // TOK,),
        in_specs=[pl.BlockSpec((TOK, W), lambda i: (jnp.minimum(i, nt - 1), col_blk)), pl.BlockSpec((1, HEAD), lambda i: (0, 0)), tab, tab],
        out_specs=pl.BlockSpec((TOK, W), lambda i: (i, 0)), out_shape=jax.ShapeDtypeStruct((t_pad, W), BF16),
        compiler_params=_cparams(("parallel",)), **_CALL_KW)(src, g, cos, sin)


def _head_prep_bwd(name, src, col_blk, n_heads, g, cos, sin, dout, rope):
    T = src.shape[0]
    W = n_heads * HEAD

    def body(s_ref, g_ref, cos_ref, sin_ref, d_ref, ds_ref, dg_ref):
        i = pl.program_id(0)

        @pl.when(i == 0)
        def _():
            dg_ref[...] = jnp.zeros_like(dg_ref)

        outs = []
        dg = jnp.zeros((1, HEAD), F32)
        for h in range(n_heads):
            xv = s_ref[:, h * HEAD:(h + 1) * HEAD].astype(F32)
            dz = d_ref[:, h * HEAD:(h + 1) * HEAD]
            if rope:
                dz = dz * cos_ref[...] - _rot_half(dz * sin_ref[...])
            r = lax.rsqrt(jnp.mean(xv * xv, axis=-1, keepdims=True) + EPS)
            xn = xv * r
            dg = dg + jnp.sum(dz * xn, axis=0, keepdims=True)
            u = dz * g_ref[...]
            outs.append((r * (u - xn * jnp.mean(u * xn, axis=-1, keepdims=True))).astype(BF16))
        dg_ref[...] += dg
        ds_ref[...] = jnp.concatenate(outs, axis=-1) if n_heads > 1 else outs[0]

    tab = pl.BlockSpec((TOK, HEAD), lambda i: (i, 0))
    row = pl.BlockSpec((TOK, W), lambda i: (i, 0))
    one = pl.BlockSpec((1, HEAD), lambda i: (0, 0))
    return pl.pallas_call(
        body, name=name, grid=(T // TOK,),
        in_specs=[pl.BlockSpec((TOK, W), lambda i: (i, col_blk)), one, tab, tab, row],
        out_specs=[row, one], out_shape=[jax.ShapeDtypeStruct((T, W), BF16), jax.ShapeDtypeStruct((1, HEAD), F32)],
        compiler_params=_cparams(("arbitrary",)), **_CALL_KW)(src, g, cos, sin, dout)


NEG = -1e30


def _attn_geometry(kind, blk, ctx, seq):
    if kind == "swa":
        bq, W = 128, 384
        nctx = ctx // bq
        lat = blk >= nctx
        n = blk - nctx
        s0 = jnp.where(lat, ctx + (n - 1) * bq, 0)
        i = lax.broadcasted_iota(jnp.int32, (bq, W), 0)
        j = lax.broadcasted_iota(jnp.int32, (bq, W), 1)
        kpos = (n - 1) * bq + j
        rel = j - bq - i
        valid = lat & (rel <= SWA_WINDOW) & (rel >= -SWA_WINDOW) & (kpos >= 0) & (kpos < seq)
        return s0, valid, 0
    bq, W = GRID_W, NA_KH * GRID_W
    nctx = ctx // bq
    rows = seq // GRID_W
    lat = blk >= nctx
    rr = jnp.clip(blk - nctx, 0, rows - 1)
    rs = jnp.clip(rr - NA_KH // 2, 0, rows - NA_KH)
    s0 = ctx + rs * GRID_W
    i = lax.broadcasted_iota(jnp.int32, (bq, W), 0)
    j = lax.broadcasted_iota(jnp.int32, (bq, W), 1)
    kcol = j % GRID_W
    cs = jnp.clip(i - NA_KW // 2, 0, GRID_W - NA_KW)
    valid = lat & (kcol >= cs) & (kcol < cs + NA_KW)
    return s0, valid, rr - rs


def _attn_probs(kind, h, blk, q, k_ref, sink_ref, bias_ref, ctx, seq):
    bq, W = (128, 384) if kind == "swa" else (GRID_W, NA_KH * GRID_W)
    scale = HEAD ** -0.5
    s0, valid, _ = _attn_geometry(kind, blk, ctx, seq)
    s0 = pl.multiple_of(s0, GRID_W)
    kl = k_ref[pl.ds(s0, W), :]
    kc = k_ref[0:ctx, :]
    nt_dims = (((1,), (1,)), ((), ()))
    sl = lax.dot_general(q, kl, nt_dims, preferred_element_type=F32) * scale
    if bias_ref is not None:
        sl = sl + bias_ref[...]
    sl = jnp.where(valid, sl, NEG)
    sc = lax.dot_general(q, kc, nt_dims, preferred_element_type=F32) * scale
    sk = sink_ref[h]
    m = jnp.maximum(jnp.maximum(jnp.max(sl, axis=-1, keepdims=True), jnp.max(sc, axis=-1, keepdims=True)), sk)
    el, ec, es = jnp.exp(sl - m), jnp.exp(sc - m), jnp.exp(sk - m)
    inv = 1.0 / (jnp.sum(el, axis=-1, keepdims=True) + jnp.sum(ec, axis=-1, keepdims=True) + es)
    return s0, kl, kc, el * inv, ec * inv, es * inv


def _attn_specs(kind, n_q, n_kv, t_pad, bias):
    bq = 128 if kind == "swa" else GRID_W
    rep = n_q // n_kv
    qspec = pl.BlockSpec((bq, HEAD), lambda h, b: (b, h))
    kvspec = pl.BlockSpec((t_pad, HEAD), lambda h, b: (0, h // rep))
    specs = [qspec, kvspec, kvspec, pl.BlockSpec(memory_space=pltpu.SMEM)]
    return bq, rep, qspec, kvspec, specs


def _bias_spec(ctx, seq):
    W = NA_KH * GRID_W

    def idx(h, b):
        rows = seq // GRID_W
        rr = jnp.clip(b - ctx // GRID_W, 0, rows - 1)
        return (h, rr - jnp.clip(rr - NA_KH // 2, 0, rows - NA_KH), 0, 0)

    return pl.BlockSpec((None, None, GRID_W, W), idx)


def _attn_fwd(name, kind, q, k, v, sink, bias, ctx, seq):
    t_pad = q.shape[0]
    T = ctx + seq
    n_q, n_kv = q.shape[1] // HEAD, k.shape[1] // HEAD
    bq, rep, qspec, kvspec, specs = _attn_specs(kind, n_q, n_kv, t_pad, bias)

    def body(q_ref, k_ref, v_ref, sink_ref, *rest):
        bias_ref = rest[0] if bias is not None else None
        o_ref = rest[-1]
        h, blk = pl.program_id(0), pl.program_id(1)
        s0, kl, kc, p_l, p_c, _ = _attn_probs(kind, h, blk, q_ref[...], k_ref, sink_ref, bias_ref, ctx, seq)
        W = kl.shape[0]
        o = jnp.dot(p_l.astype(BF16), v_ref[pl.ds(s0, W), :], preferred_element_type=F32)
        o_ref[...] = o + jnp.dot(p_c.astype(BF16), v_ref[0:ctx, :], preferred_element_type=F32)

    ins = [q, k, v, sink] + ([bias] if bias is not None else [])
    if bias is not None:
        specs = specs + [_bias_spec(ctx, seq)]
    return pl.pallas_call(
        body, name=name, grid=(n_q, T // bq), in_specs=specs, out_specs=qspec,
        out_shape=jax.ShapeDtypeStruct((T, n_q * HEAD), F32), compiler_params=_cparams(("parallel", "arbitrary")), **_CALL_KW)(*ins)


def _attn_bwd(name, kind, q, k, v, sink, bias, do, do_head0, ctx, seq):
    t_pad = q.shape[0]
    T = ctx + seq
    n_q, n_kv = q.shape[1] // HEAD, k.shape[1] // HEAD
    bq, rep, qspec, kvspec, specs = _attn_specs(kind, n_q, n_kv, t_pad, bias)
    scale = HEAD ** -0.5
    tn_dims = (((0,), (0,)), ((), ()))
    nt_dims = (((1,), (1,)), ((), ()))

    def body(q_ref, k_ref, v_ref, sink_ref, *rest):
        if bias is not None:
            bias_ref, do_ref, dq_ref, dk_ref, dv_ref, dsk_ref, db_ref = rest
        else:
            bias_ref, db_ref = None, None
            do_ref, dq_ref, dk_ref, dv_ref, dsk_ref = rest
        h, blk = pl.program_id(0), pl.program_id(1)
        qv = q_ref[...]
        s0, kl, kc, p_l, p_c, p_s = _attn_probs(kind, h, blk, qv, k_ref, sink_ref, bias_ref, ctx, seq)
        W = kl.shape[0]
        vl = v_ref[pl.ds(s0, W), :]
        vc = v_ref[0:ctx, :]
        dov = do_ref[...]
        dob = dov.astype(BF16)
        pl_b, pc_b = p_l.astype(BF16), p_c.astype(BF16)
        o = jnp.dot(pl_b, vl, preferred_element_type=F32) + jnp.dot(pc_b, vc, preferred_element_type=F32)
        delta = jnp.sum(dov * o, axis=-1, keepdims=True)
        ds_l = p_l * (lax.dot_general(dob, vl, nt_dims, preferred_element_type=F32) - delta)
        ds_c = p_c * (lax.dot_general(dob, vc, nt_dims, preferred_element_type=F32) - delta)
        dsl_b, dsc_b = ds_l.astype(BF16), ds_c.astype(BF16)
        dq_ref[...] = (jnp.dot(dsl_b, kl, preferred_element_type=F32) + jnp.dot(dsc_b, kc, preferred_element_type=F32)) * scale

        @pl.when((h % rep == 0) & (blk == 0))
        def _():
            dk_ref[...] = jnp.zeros_like(dk_ref)
            dv_ref[...] = jnp.zeros_like(dv_ref)

        @pl.when(blk == 0)
        def _():
            dsk_ref[...] = jnp.zeros_like(dsk_ref)

        dk_ref[pl.ds(s0, W), :] += lax.dot_general(dsl_b, qv, tn_dims, preferred_element_type=F32) * scale
        dv_ref[pl.ds(s0, W), :] += lax.dot_general(pl_b, dob, tn_dims, preferred_element_type=F32)
        dk_ref[0:ctx, :] += lax.dot_general(dsc_b, qv, tn_dims, preferred_element_type=F32) * scale
        dv_ref[0:ctx, :] += lax.dot_general(pc_b, dob, tn_dims, preferred_element_type=F32)
        dsk_ref[...] += jnp.sum(-p_s * delta, axis=0, keepdims=True)
        if bias is not None:
            _, _, pat = _attn_geometry(kind, blk, ctx, seq)
            _, _, pat_prev = _attn_geometry(kind, jnp.maximum(blk - 1, 0), ctx, seq)

            @pl.when((blk == 0) | (pat != pat_prev))
            def _():
                db_ref[...] = jnp.zeros_like(db_ref)

            db_ref[...] += ds_l

    ins = [q, k, v, sink] + ([bias] if bias is not None else []) + [do]
    in_specs = specs + ([_bias_spec(ctx, seq)] if bias is not None else []) + [pl.BlockSpec((bq, HEAD), lambda h, b: (b, do_head0 + h))]
    out_specs = [qspec, kvspec, kvspec, pl.BlockSpec((None, 8, HEAD), lambda h, b: (h, 0, 0))]
    out_shape = [jax.ShapeDtypeStruct((T, n_q * HEAD), F32), jax.ShapeDtypeStruct((t_pad, n_kv * HEAD), F32),
                 jax.ShapeDtypeStruct((t_pad, n_kv * HEAD), F32), jax.ShapeDtypeStruct((n_q, 8, HEAD), F32)]
    if bias is not None:
        out_specs.append(_bias_spec(ctx, seq))
        out_shape.append(jax.ShapeDtypeStruct(bias.shape, F32))
    res = pl.pallas_call(
        body, name=name, grid=(n_q, T // bq), in_specs=in_specs, out_specs=out_specs, out_shape=out_shape,
        compiler_params=_cparams(("arbitrary", "arbitrary")), **_CALL_KW)(*ins)
    return res if bias is not None else list(res) + [None]


HALO = 8


def _halo_specs(width, col0, T, ctx_tiles):
    per = TOK // HALO
    main = pl.BlockSpec((TOK, width), lambda jc, i: (i, col0 + jc))
    prev = pl.BlockSpec((HALO, width), lambda jc, i: (jnp.maximum(i * per - 1, 0), col0 + jc))
    nxt = pl.BlockSpec((HALO, width), lambda jc, i: (jnp.minimum((i + 1) * per, T // HALO - 1), col0 + jc))
    return main, prev, nxt


def _with_halo(i, nt, ctx_tiles, prev, main, nxt):
    has_prev = (i != 0) & (i != ctx_tiles)
    has_next = (i != ctx_tiles - 1) & (i != nt - 1)
    return jnp.concatenate([jnp.where(has_prev, prev, 0.0), main, jnp.where(has_next, nxt, 0.0)], axis=0)


def _shifted(ext, s):
    n = ext.shape[0]
    return pltpu.roll(ext, (-s) % n, 0)[HALO:HALO + TOK]


def _conv_fwd(name, p, col0, conv_w, ctx_tiles):
    T = p.shape[0]
    nt = T // TOK
    ncol = 3
    Wc = conv_w.shape[1] // ncol
    pad = (DN_CONV - 1) // 2

    def body(m_ref, p_ref, n_ref, w_ref, o_ref):
        i = pl.program_id(1)
        ext = _with_halo(i, nt, ctx_tiles, p_ref[...].astype(F32), m_ref[...].astype(F32), n_ref[...].astype(F32))
        acc = jnp.zeros((TOK, Wc), F32)
        for j in range(DN_CONV):
            acc = acc + w_ref[j:j + 1, :] * _shifted(ext, j - pad)
        o_ref[...] = acc

    main, prev, nxt = _halo_specs(Wc, col0, T, ctx_tiles)
    return pl.pallas_call(
        body, name=name, grid=(ncol, nt), in_specs=[main, prev, nxt, pl.BlockSpec((DN_CONV, Wc), lambda jc, i: (0, jc))],
        out_specs=pl.BlockSpec((TOK, Wc), lambda jc, i: (i, jc)), out_shape=jax.ShapeDtypeStruct((T, ncol * Wc), F32),
        compiler_params=_cparams(("parallel", "parallel")), **_CALL_KW)(p, p, p, conv_w)


def _conv_bwd(name, p, col0, conv_w, dpre, ctx_tiles):
    T = p.shape[0]
    nt = T // TOK
    ncol = 3
    Wc = conv_w.shape[1] // ncol
    pad = (DN_CONV - 1) // 2

    def body(m_ref, p_ref, n_ref, dm_ref, dp_ref, dn_ref, w_ref, dx_ref, dw_ref):
        i = pl.program_id(1)
        ext_x = _with_halo(i, nt, ctx_tiles, p_ref[...].astype(F32), m_ref[...].astype(F32), n_ref[...].astype(F32))
        ext_d = _with_halo(i, nt, ctx_tiles, dp_ref[...], dm_ref[...], dn_ref[...])
        dmain = dm_ref[...]

        @pl.when(i == 0)
        def _():
            dw_ref[...] = jnp.zeros_like(dw_ref)

        acc = jnp.zeros((TOK, Wc), F32)
        for j in range(DN_CONV):
            acc = acc + w_ref[j:j + 1, :] * _shifted(ext_d, pad - j)
            dw_ref[j:j + 1, :] += jnp.sum(dmain * _shifted(ext_x, j - pad), axis=0, keepdims=True)
        dx_ref[...] = acc.astype(BF16)

    main, prev, nxt = _halo_specs(Wc, col0, T, ctx_tiles)
    dmain, dprev, dnxt = _halo_specs(Wc, 0, T, ctx_tiles)
    return pl.pallas_call(
        body, name=name, grid=(ncol, nt),
        in_specs=[main, prev, nxt, dmain, dprev, dnxt, pl.BlockSpec((DN_CONV, Wc), lambda jc, i: (0, jc))],
        out_specs=[pl.BlockSpec((TOK, Wc), lambda jc, i: (i, jc)), pl.BlockSpec((8, Wc), lambda jc, i: (0, jc))],
        out_shape=[jax.ShapeDtypeStruct((T, ncol * Wc), BF16), jax.ShapeDtypeStruct((8, ncol * Wc), F32)],
        compiler_params=_cparams(("parallel", "arbitrary")), **_CALL_KW)(p, p, p, dpre, dpre, dpre, conv_w)


def _softplus(x):
    return jnp.maximum(x, 0.0) + jnp.log(1.0 + jnp.exp(-jnp.abs(x)))


def _gdn_point(name, pre, dab, a_log, dt_bias, n_heads):
    T = pre.shape[0]
    Wd = n_heads * HEAD
    ng = 2 * n_heads

    def body(pre_ref, ab_ref, al_ref, dt_ref, q_ref, k_ref, v_ref, la_ref, be_ref):
        for h in range(n_heads):
            for part, ref in enumerate((q_ref, k_ref, v_ref)):
                xv = pre_ref[:, part * Wd + h * HEAD:part * Wd + (h + 1) * HEAD]
                s = xv * _sigmoid(xv)
                if part < 2:
                    s = s * lax.rsqrt(jnp.sum(s * s, axis=-1, keepdims=True) + EPS) * (HEAD ** -0.5 if part == 0 else 1.0)
                ref[:, h * HEAD:(h + 1) * HEAD] = s
        ab = ab_ref[...].astype(F32)
        lane = lax.broadcasted_iota(jnp.int32, ab.shape, 1)
        la_ref[...] = jnp.where(lane < ng, -jnp.exp(al_ref[...]) * _softplus(ab + dt_ref[...]), 0.0)
        be_ref[...] = jnp.where(lane < ng, _sigmoid(pltpu.roll(ab, HEAD - ng, 1)), 0.0)

    row = lambda w: pl.BlockSpec((TOK, w), lambda i: (i, 0))
    one = pl.BlockSpec((1, HEAD), lambda i: (0, 0))
    return pl.pallas_call(
        body, name=name, grid=(T // TOK,), in_specs=[row(3 * Wd), row(HEAD), one, one],
        out_specs=[row(Wd), row(Wd), row(Wd), row(HEAD), row(HEAD)],
        out_shape=[jax.ShapeDtypeStruct((T, Wd), F32)] * 3 + [jax.ShapeDtypeStruct((T, HEAD), F32)] * 2,
        compiler_params=_cparams(("parallel",)), **_CALL_KW)(pre, dab, a_log, dt_bias)


def _gdn_point_bwd(name, pre, dab, a_log, dt_bias, n_heads, dq, dk, dv, dla, dbe):
    T = pre.shape[0]
    Wd = n_heads * HEAD
    ng = 2 * n_heads

    def body(pre_ref, ab_ref, al_ref, dt_ref, dq_ref, dk_ref, dv_ref, dla_ref, dbe_ref, dpre_ref, dab_ref, dal_ref, ddt_ref):
        i = pl.program_id(0)

        @pl.when(i == 0)
        def _():
            dal_ref[...] = jnp.zeros_like(dal_ref)
            ddt_ref[...] = jnp.zeros_like(ddt_ref)

        for h in range(n_heads):
            for part, ref in enumerate((dq_ref, dk_ref, dv_ref)):
                cols = slice(part * Wd + h * HEAD, part * Wd + (h + 1) * HEAD)
                xv = pre_ref[:, cols]
                sg = _sigmoid(xv)
                s = xv * sg
                dy = ref[0, :, h * HEAD:(h + 1) * HEAD] + ref[1, :, h * HEAD:(h + 1) * HEAD]
                if part < 2:
                    c0 = HEAD ** -0.5 if part == 0 else 1.0
                    r = lax.rsqrt(jnp.sum(s * s, axis=-1, keepdims=True) + EPS)
                    ds = c0 * (r * dy - s * (r * r * r) * jnp.sum(dy * s, axis=-1, keepdims=True))
                else:
                    ds = dy
                dpre_ref[:, cols] = ds * (sg * (1.0 + xv * (1.0 - sg)))
        ab = ab_ref[...].astype(F32)
        lane = lax.broadcasted_iota(jnp.int32, ab.shape, 1)
        ea = jnp.exp(al_ref[...])
        z = ab + dt_ref[...]
        dlav = jnp.where(lane < ng, dla_ref[0] + dla_ref[1], 0.0)
        da = dlav * (-ea) * _sigmoid(z)
        dal_ref[...] += jnp.sum(dlav * (-ea) * _softplus(z), axis=0, keepdims=True)
        ddt_ref[...] += jnp.sum(da, axis=0, keepdims=True)
        be = _sigmoid(pltpu.roll(ab, HEAD - ng, 1))
        db = jnp.where(lane < ng, (dbe_ref[0] + dbe_ref[1]) * be * (1.0 - be), 0.0)
        dab_ref[...] = (da + pltpu.roll(db, ng, 1)).astype(BF16)

    row = lambda w: pl.BlockSpec((TOK, w), lambda i: (i, 0))
    two = lambda w: pl.BlockSpec((2, TOK, w), lambda i: (0, i, 0))
    one = pl.BlockSpec((1, HEAD), lambda i: (0, 0))
    return pl.pallas_call(
        body, name=name, grid=(T // TOK,),
        in_specs=[row(3 * Wd), row(HEAD), one, one, two(Wd), two(Wd), two(Wd), two(HEAD), two(HEAD)],
        out_specs=[row(3 * Wd), row(HEAD), one, one],
        out_shape=[jax.ShapeDtypeStruct((T, 3 * Wd), F32), jax.ShapeDtypeStruct((T, HEAD), BF16),
                   jax.ShapeDtypeStruct((1, HEAD), F32), jax.ShapeDtypeStruct((1, HEAD), F32)],
        compiler_params=_cparams(("arbitrary",)), **_CALL_KW)(pre, dab, a_log, dt_bias, dq, dk, dv, dla, dbe)


def _mm3_raw(a, b):
    ah, bh = a.astype(BF16), b.astype(BF16)
    al, bl = (a - ah.astype(F32)).astype(BF16), (b - bh.astype(F32)).astype(BF16)
    d = functools.partial(jnp.dot, preferred_element_type=F32)
    return d(ah, bh) + (d(ah, bl) + d(al, bh))


@jax.custom_vjp
def _mm3(a, b):
    return _mm3_raw(a, b)


def _mm3_fwd(a, b):
    return _mm3_raw(a, b), (a, b)


def _mm3_bwd(res, g):
    a, b = res
    return _mm3_raw(g, b.T), _mm3_raw(a.T, g)


_mm3.defvjp(_mm3_fwd, _mm3_bwd)

_NN, _NT, _TN = "nn", "nt", "tn"
_DIMS = {"nn": (((1,), (0,)), ((), ())), "nt": (((1,), (1,)), ((), ())), "tn": (((0,), (0,)), ((), ()))}


def _bdot_raw(a, b, kind):
    return lax.dot_general(a.astype(BF16), b.astype(BF16), _DIMS[kind], preferred_element_type=F32)


@functools.partial(jax.custom_vjp, nondiff_argnums=(2,))
def _bdot(a, b, kind=_NN):
    return _bdot_raw(a, b, kind)


def _bdot_fwd(a, b, kind):
    return _bdot_raw(a, b, kind), (a, b)


def _bdot_bwd(kind, res, g):
    a, b = res
    if kind == "nn":
        return _bdot_raw(g, b, "nt"), _bdot_raw(a, g, "tn")
    if kind == "nt":
        return _bdot_raw(g, b, "nn"), _bdot_raw(g, a, "tn")
    return _bdot_raw(b, g, "nt"), _bdot_raw(a, g, "nn")


_bdot.defvjp(_bdot_fwd, _bdot_bwd)


def _chunk_masks(rev):
    C = DN_CHUNK
    ii = lax.broadcasted_iota(jnp.int32, (C, C), 0)
    jj = lax.broadcasted_iota(jnp.int32, (C, C), 1)
    diff = jnp.where(rev, jj - ii, ii - jj)
    incl = diff >= 0
    strict = diff > 0
    rowsel = (lax.broadcasted_iota(jnp.int32, (C, 1), 0) == jnp.where(rev, 0, C - 1)).astype(F32)
    return incl, strict, rowsel, (ii == jj).astype(F32)


def _chunk_gates(g, gt, be, sel, selt):
    g_col = jnp.sum(g * sel, axis=1, keepdims=True)
    g_row = jnp.sum(gt * selt, axis=0, keepdims=True)
    b_col = jnp.sum(be * sel, axis=1, keepdims=True)
    return g_col, g_row, b_col


def _chunk_lower(k, g, gt, be, sel, selt, incl, strict):
    g_col, g_row, b_col = _chunk_gates(g, gt, be, sel, selt)
    decay = jnp.where(incl, jnp.exp(jnp.where(incl, g_col - g_row, 0.0)), 0.0)
    return jnp.where(strict, _bdot(k * b_col, k, _NT) * decay, 0.0)


def _chunk_inverse(low, eye):
    m = -low
    x = eye + m
    p = m
    for _ in range(int(math.log2(DN_CHUNK)) - 1):
        p = _mm3(p, p)
        x = x + _mm3(x, p)
    return x


def _chunk_step(q, k, v, g, gt, be, S, X, sel, selt, incl, rowsel):
    g_col, g_row, b_col = _chunk_gates(g, gt, be, sel, selt)
    decay = jnp.where(incl, jnp.exp(jnp.where(incl, g_col - g_row, 0.0)), 0.0)
    eg = jnp.exp(g_col)
    u = _mm3(X, v * b_col)
    w = _mm3(X, k * (b_col * eg))
    intra = _bdot(q, k, _NT) * decay
    g_last = jnp.sum(g_col * rowsel, axis=0, keepdims=True)
    v_new = u - _bdot(w, S)
    o = _bdot(q * eg, S) + _bdot(intra, v_new)
    S_new = S * jnp.exp(g_last) + _bdot(k * jnp.exp(g_last - g_col), v_new, _TN)
    return o, S_new


def _scan_index(ctx_chunks, n_chunks):
    def idx(d, n):
        return jnp.where(d == 0, n, jnp.where(n < ctx_chunks, ctx_chunks - 1 - n, n_chunks + ctx_chunks - 1 - n))
    return idx


def _cumsum_mats(rev):
    C = DN_CHUNK
    ii = lax.broadcasted_iota(jnp.int32, (C, C), 0)
    jj = lax.broadcasted_iota(jnp.int32, (C, C), 1)
    return jnp.where(jnp.where(rev, jj - ii, ii - jj) >= 0, 1.0, 0.0).astype(F32)


def _gdn_scan(name, q, k, v, la, be, n_heads, ctx):
    T, Wd = q.shape
    C = DN_CHUNK
    nch = T // C
    cidx = _scan_index(ctx // C, nch)

    def body(q_ref, k_ref, v_ref, la_ref, be_ref, o_ref, s_ref, x_ref, state):
        d, n = pl.program_id(0), pl.program_id(1)
        rev = d == 1

        @pl.when(n == 0)
        def _():
            state[...] = jnp.zeros_like(state)

        incl, strict, rowsel, eye = _chunk_masks(rev)
        g = jnp.dot(_cumsum_mats(rev), la_ref[...], precision=lax.Precision.HIGHEST, preferred_element_type=F32)
        gt = g.T
        bev = be_ref[...]
        lane = lax.broadcasted_iota(jnp.int32, (1, HEAD), 1)
        sub = lax.broadcasted_iota(jnp.int32, (HEAD, 1), 0)
        for h in range(n_heads):
            cols = slice(h * HEAD, (h + 1) * HEAD)
            sel = (lane == d * n_heads + h).astype(F32)
            selt = (sub == d * n_heads + h).astype(F32)
            qv, kv, vv = q_ref[:, cols], k_ref[:, cols], v_ref[:, cols]
            S = state[h]
            X = _chunk_inverse(_chunk_lower(kv, g, gt, bev, sel, selt, incl, strict), eye)
            o, S_new = _chunk_step(qv, kv, vv, g, gt, bev, S, X, sel, selt, incl, rowsel)
            s_ref[h] = S
            x_ref[h] = X
            o_ref[:, cols] = o
            state[h] = S_new

    tok = lambda w: pl.BlockSpec((C, w), lambda d, n: (cidx(d, n), 0))
    return pl.pallas_call(
        body, name=name, grid=(2, nch), in_specs=[tok(Wd), tok(Wd), tok(Wd), tok(HEAD), tok(HEAD)],
        out_specs=[pl.BlockSpec((None, C, Wd), lambda d, n: (d, cidx(d, n), 0)),
                   pl.BlockSpec((None, None, n_heads, HEAD, HEAD), lambda d, n: (d, n, 0, 0, 0)),
                   pl.BlockSpec((None, None, n_heads, C, C), lambda d, n: (d, n, 0, 0, 0))],
        out_shape=[jax.ShapeDtypeStruct((2, T, Wd), F32), jax.ShapeDtypeStruct((2, nch, n_heads, HEAD, HEAD), F32),
                   jax.ShapeDtypeStruct((2, nch, n_heads, C, C), F32)],
        scratch_shapes=[pltpu.VMEM((n_heads, HEAD, HEAD), F32)],
        compiler_params=_cparams(("arbitrary", "arbitrary")), **_CALL_KW)(q, k, v, la, be)


def _gdn_scan_bwd(name, q, k, v, la, be, states, invs, do, n_heads, ctx):
    T, Wd = q.shape
    C = DN_CHUNK
    nch = T // C
    cidx = _scan_index(ctx // C, nch)

    def body(q_ref, k_ref, v_ref, la_ref, be_ref, s_ref, x_ref, do_ref, dq_ref, dk_ref, dv_ref, dla_ref, dbe_ref, dstate):
        d, n = pl.program_id(0), pl.program_id(1)
        rev = d == 1

        @pl.when(n == 0)
        def _():
            dstate[...] = jnp.zeros_like(dstate)

        incl, strict, rowsel, eye = _chunk_masks(rev)
        tri = _cumsum_mats(rev)
        g = jnp.dot(tri, la_ref[...], precision=lax.Precision.HIGHEST, preferred_element_type=F32)
        gt = g.T
        bev = be_ref[...]
        lane = lax.broadcasted_iota(jnp.int32, (1, HEAD), 1)
        sub = lax.broadcasted_iota(jnp.int32, (HEAD, 1), 0)
        dg = jnp.zeros((C, HEAD), F32)
        dgt = jnp.zeros((HEAD, C), F32)
        dbe = jnp.zeros((C, HEAD), F32)
        for h in range(n_heads):
            cols = slice(h * HEAD, (h + 1) * HEAD)
            sel = (lane == d * n_heads + h).astype(F32)
            selt = (sub == d * n_heads + h).astype(F32)
            qv, kv, vv = q_ref[:, cols], k_ref[:, cols], v_ref[:, cols]
            S, X = s_ref[h], x_ref[h]
            step = lambda q_, k_, v_, g_, gt_, be_, S_, X_: _chunk_step(q_, k_, v_, g_, gt_, be_, S_, X_, sel, selt, incl, rowsel)
            _, vjp_step = jax.vjp(step, qv, kv, vv, g, gt, bev, S, X)
            dq, dk1, dv_, dg1, dgt1, dbe1, dS, dX = vjp_step((do_ref[:, cols], dstate[h]))
            xt = X.T
            dlow = -_mm3(_mm3(xt, dX), xt)
            low_fn = lambda k_, g_, gt_, be_: _chunk_lower(k_, g_, gt_, be_, sel, selt, incl, strict)
            _, vjp_low = jax.vjp(low_fn, kv, g, gt, bev)
            dk2, dg2, dgt2, dbe2 = vjp_low(dlow)
            dq_ref[:, cols] = dq
            dk_ref[:, cols] = dk1 + dk2
            dv_ref[:, cols] = dv_
            dstate[h] = dS
            dg, dgt, dbe = dg + dg1 + dg2, dgt + dgt1 + dgt2, dbe + dbe1 + dbe2
        dg = dg + dgt.T
        dla_ref[...] = lax.dot_general(tri, dg, _DIMS["tn"], precision=lax.Precision.HIGHEST, preferred_element_type=F32)
        dbe_ref[...] = dbe

    rn = lambda d, n: cidx(d, nch - 1 - n)
    tok = lambda w: pl.BlockSpec((C, w), lambda d, n: (rn(d, n), 0))
    otok = lambda w: pl.BlockSpec((None, C, w), lambda d, n: (d, rn(d, n), 0))
    return pl.pallas_call(
        body, name=name, grid=(2, nch),
        in_specs=[tok(Wd), tok(Wd), tok(Wd), tok(HEAD), tok(HEAD),
                  pl.BlockSpec((None, None, n_heads, HEAD, HEAD), lambda d, n: (d, nch - 1 - n, 0, 0, 0)),
                  pl.BlockSpec((None, None, n_heads, C, C), lambda d, n: (d, nch - 1 - n, 0, 0, 0)), tok(Wd)],
        out_specs=[otok(Wd), otok(Wd), otok(Wd), otok(HEAD), otok(HEAD)],
        out_shape=[jax.ShapeDtypeStruct((2, T, Wd), F32)] * 3 + [jax.ShapeDtypeStruct((2, T, HEAD), F32)] * 2,
        scratch_shapes=[pltpu.VMEM((n_heads, HEAD, HEAD), F32)],
        compiler_params=_cparams(("arbitrary", "arbitrary")), **_CALL_KW)(q, k, v, la, be, states, invs, do)


def _gated_norm(name, o2, p, zblk, g, n_heads):
    _, T, Wd = o2.shape

    def body(o_ref, z_ref, g_ref, y_ref):
        for h in range(n_heads):
            cols = slice(h * HEAD, (h + 1) * HEAD)
            ov = o_ref[0, :, cols] + o_ref[1, :, cols]
            zv = z_ref[:, cols].astype(F32)
            y = ov * lax.rsqrt(jnp.mean(ov * ov, axis=-1, keepdims=True) + EPS) * g_ref[...]
            y_ref[:, cols] = (y * (zv * _sigmoid(zv))).astype(BF16)

    return pl.pallas_call(
        body, name=name, grid=(T // TOK,),
        in_specs=[pl.BlockSpec((2, TOK, Wd), lambda i: (0, i, 0)), pl.BlockSpec((TOK, Wd), lambda i: (i, zblk)),
                  pl.BlockSpec((1, HEAD), lambda i: (0, 0))],
        out_specs=pl.BlockSpec((TOK, Wd), lambda i: (i, 0)), out_shape=jax.ShapeDtypeStruct((T, Wd), BF16),
        compiler_params=_cparams(("parallel",)), **_CALL_KW)(o2, p, g)


def _gated_norm_bwd(name, o2, p, zblk, g, n_heads, dmix, dblk):
    _, T, Wd = o2.shape

    def body(o_ref, z_ref, g_ref, dy_ref, do_ref, dz_ref, dg_ref):
        i = pl.program_id(0)

        @pl.when(i == 0)
        def _():
            dg_ref[...] = jnp.zeros_like(dg_ref)

        dg = jnp.zeros((1, HEAD), F32)
        for h in range(n_heads):
            cols = slice(h * HEAD, (h + 1) * HEAD)
            ov = o_ref[0, :, cols] + o_ref[1, :, cols]
            zv = z_ref[:, cols].astype(F32)
            dy = dy_ref[:, cols].astype(F32)
            r = lax.rsqrt(jnp.mean(ov * ov, axis=-1, keepdims=True) + EPS)
            on = ov * r
            sg = _sigmoid(zv)
            sz = zv * sg
            dz_ref[:, cols] = (dy * (on * g_ref[...]) * (sg * (1.0 + zv * (1.0 - sg)))).astype(BF16)
            dyn = dy * sz
            dg = dg + jnp.sum(dyn * on, axis=0, keepdims=True)
            u = dyn * g_ref[...]
            do_ref[:, cols] = r * (u - on * jnp.mean(u * on, axis=-1, keepdims=True))
        dg_ref[...] += dg

    row = pl.BlockSpec((TOK, Wd), lambda i: (i, 0))
    one = pl.BlockSpec((1, HEAD), lambda i: (0, 0))
    return pl.pallas_call(
        body, name=name, grid=(T // TOK,),
        in_specs=[pl.BlockSpec((2, TOK, Wd), lambda i: (0, i, 0)), pl.BlockSpec((TOK, Wd), lambda i: (i, zblk)), one,
                  pl.BlockSpec((TOK, Wd), lambda i: (i, dblk))],
        out_specs=[row, row, one],
        out_shape=[jax.ShapeDtypeStruct((T, Wd), F32), jax.ShapeDtypeStruct((T, Wd), BF16), jax.ShapeDtypeStruct((1, HEAD), F32)],
        compiler_params=_cparams(("arbitrary",)), **_CALL_KW)(o2, p, g, dmix)


class _Dims:
    def __init__(self, D, seq, ctx, ffn):
        self.D, self.seq, self.ctx, self.ffn = D, seq, ctx, ffn
        self.T = seq + ctx
        self.t_pad = -(-(self.T + 128) // TOK) * TOK
        self.ctx_tiles = ctx // TOK
        nh = D // HEAD
        self.swa_h, self.kv_h, self.dn_h = nh // 4, nh // 8, nh // 2
        self.na_h = nh - self.swa_h - self.dn_h
        self.swa_q, self.swa_kv, self.Wd, self.na = self.swa_h * HEAD, self.kv_h * HEAD, self.dn_h * HEAD, self.na_h * HEAD
        self.n_ab = 4 * self.dn_h
        self.o_ab = self.swa_q + 2 * self.swa_kv + 4 * self.Wd
        self.n_in = self.o_ab + self.n_ab + 3 * self.na
        self.n_main = self.n_in - self.n_ab
        assert ctx % TOK == 0 and seq % TOK == 0 and self.swa_q == 2 * self.swa_kv == self.na and 2 * self.na == self.Wd


def _rope_tables(dm):
    t = jnp.arange(dm.t_pad, dtype=jnp.int32) - dm.ctx
    lat = (t >= 0) & (t < dm.seq)
    row = (t // GRID_W).astype(F32)
    col = (t % GRID_W).astype(F32)
    n_freq = HEAD // 4
    inv = ROPE_THETA ** (-jnp.arange(n_freq, dtype=F32) / n_freq)
    ang = jnp.concatenate([row[:, None] * inv, row[:, None] * inv, col[:, None] * inv, col[:, None] * inv], axis=-1)
    ang = jnp.where(lat[:, None], ang, 0.0)
    return jnp.cos(ang), jnp.sin(ang)


def _bias_indices():
    o = np.arange(NA_KH)[:, None]
    jr = np.arange(NA_KH)[None, :]
    idx_r = jr - o + (NA_KH - 1)
    cols = np.arange(GRID_W)
    idx_c = np.clip(cols[None, :] - cols[:, None], -(NA_KW - 1), NA_KW - 1) + (NA_KW - 1)
    return idx_r, idx_c


def _bias_table(rpb):
    idx_r, idx_c = _bias_indices()
    b = rpb[:, idx_r[:, None, :, None], idx_c[None, :, None, :]]
    return b.reshape(rpb.shape[0], NA_KH, GRID_W, NA_KH * GRID_W)


def _lane_row(v):
    v = v.reshape(-1)
    return jnp.pad(v, (0, HEAD - v.shape[0])).reshape(1, HEAD)


def _layer_weights(dm, l, g_in, g_out, g_gate, g_up, g_down):
    w_in = jnp.concatenate([g_in[k, l] for k in range(N_CHIPS)], axis=1)
    w_main = jnp.concatenate([w_in[:, :dm.o_ab], w_in[:, dm.o_ab + dm.n_ab:]], axis=1)
    w_ab = jnp.pad(w_in[:, dm.o_ab:dm.o_ab + dm.n_ab], ((0, 0), (0, HEAD - dm.n_ab)))
    w_out = g_out[:, l].reshape(dm.D, dm.D)
    w_out = jnp.concatenate([w_out[dm.swa_q:dm.swa_q + dm.Wd], w_out[:dm.swa_q], w_out[dm.swa_q + dm.Wd:]], axis=0)
    w_gu = jnp.concatenate([g_gate[k, l] for k in range(N_CHIPS)] + [g_up[k, l] for k in range(N_CHIPS)], axis=1)
    w_down = g_down[:, l].reshape(dm.ffn, dm.D)
    return dict(main=w_main, ab=w_ab, out=w_out, gu=w_gu, down=w_down)


def _layer_fwd(dm, x, W, sp, modv, cos, sin):
    ct = dm.ctx_tiles
    h = _norm_mod("norm1", x, sp["norm1_g"], modv, 0, ct)
    P = _matmul("in_proj", h, W["main"], "nn")
    Pab = _matmul("in_proj_ab", h, W["ab"], "nn", tn=HEAD)
    one = jnp.ones((1, HEAD), F32)
    qa = _head_prep("swa_q_prep", P, 0, dm.swa_h, sp["swa_q_g"], cos, sin, dm.t_pad, True, True)
    ka = _head_prep("swa_k_prep", P, 2, dm.kv_h, sp["swa_k_g"], cos, sin, dm.t_pad, True, True)
    va = _head_prep("swa_v_prep", P, 3, dm.kv_h, one, cos, sin, dm.t_pad, False, False)
    oa = _attn_fwd("swa_fwd", "swa", qa, ka, va, sp["swa_sink"], None, dm.ctx, dm.seq)
    qn = _head_prep("na_q_prep", P, 10, dm.na_h, sp["na_q_g"], cos, sin, dm.t_pad, True, False)
    kn = _head_prep("na_k_prep", P, 11, dm.na_h, sp["na_k_g"], cos, sin, dm.t_pad, True, False)
    vn = _head_prep("na_v_prep", P, 12, dm.na_h, one, cos, sin, dm.t_pad, False, False)
    no_sink = jnp.full((dm.na_h,), NEG, F32)
    bias = _bias_table(sp["na_rpb"])
    oc = _attn_fwd("na_fwd", "na", qn, kn, vn, no_sink, bias, dm.ctx, dm.seq)
    pre = _conv_fwd("dn_conv", P, 1, sp["dn_conv_w"], ct)
    a_row, dt_row = _lane_row(sp["dn_A_log"]), _lane_row(sp["dn_dt_bias"])
    qh, kh, vh, la, be = _gdn_point("dn_point", pre, Pab, a_row, dt_row, dm.dn_h)
    o2, states, invs = _gdn_scan("dn_scan", qh, kh, vh, la, be, dm.dn_h, dm.ctx)
    ob = _gated_norm("dn_out_norm", o2, P, 4, sp["dn_out_g"], dm.dn_h)
    mix = jnp.concatenate([ob, oa.astype(BF16), oc.astype(BF16)], axis=-1)
    ao = _matmul("out_proj", mix, W["out"], "nn")
    x1 = _resid_gate("resid1", x, ao, modv, 2, ct)
    h2 = _norm_mod("norm2", x1, sp["norm2_g"], modv, 3, ct)
    gu = _matmul("ffn_gate_up", h2, W["gu"], "nn")
    act = _swiglu("ffn_act", gu)
    fo = _matmul("ffn_down", act, W["down"], "nn")
    x2 = _resid_gate("resid2", x1, fo, modv, 5, ct)
    res = dict(x=x, h=h, P=P, Pab=Pab, qa=qa, ka=ka, va=va, qn=qn, kn=kn, vn=vn, bias=bias, no_sink=no_sink, pre=pre,
               a_row=a_row, dt_row=dt_row, qh=qh, kh=kh, vh=vh, la=la, be=be, o2=o2, states=states, invs=invs, mix=mix,
               ao=ao, x1=x1, h2=h2, gu=gu, act=act, fo=fo)
    return x2, res


def _layer_bwd(dm, dx2, W, sp, modv, cos, sin, r):
    ct = dm.ctx_tiles
    T, D = dm.T, dm.D
    one = jnp.ones((1, HEAD), F32)
    dfo, dgate2 = _resid_gate_bwd("resid2_bwd", dx2, r["fo"], modv, 5, ct)
    dact = _matmul("ffn_down_dx", dfo, W["down"], "nt")
    dw_down = _matmul("ffn_down_dw", r["act"], dfo, "tn")
    dgu = _swiglu_bwd("ffn_act_bwd", r["gu"], dact)
    dh2 = _matmul("ffn_gate_up_dx", dgu, W["gu"], "nt")
    dw_gu = _matmul("ffn_gate_up_dw", r["h2"], dgu, "tn")
    zero = jnp.zeros((T, D), F32)
    dx1, dn2g, dsh2, dsc2 = _norm_mod_bwd("norm2_bwd", r["x1"], sp["norm2_g"], modv, 3, dh2, zero, dx2, ct)
    dao, dgate1 = _resid_gate_bwd("resid1_bwd", dx1, r["ao"], modv, 2, ct)
    dmix = _matmul("out_proj_dx", dao, W["out"], "nt")
    dw_out = _matmul("out_proj_dw", r["mix"], dao, "tn")
    do_, dz, d_out_g = _gated_norm_bwd("dn_out_norm_bwd", r["o2"], r["P"], 4, sp["dn_out_g"], dm.dn_h, dmix, 0)
    dq2, dk2, dv2, dla2, dbe2 = _gdn_scan_bwd("dn_scan_bwd", r["qh"], r["kh"], r["vh"], r["la"], r["be"], r["states"], r["invs"],
                                              do_, dm.dn_h, dm.ctx)
    dpre, dPab, d_alog, d_dtb = _gdn_point_bwd("dn_point_bwd", r["pre"], r["Pab"], r["a_row"], r["dt_row"], dm.dn_h,
                                               dq2, dk2, dv2, dla2, dbe2)
    dqkv, d_conv = _conv_bwd("dn_conv_bwd", r["P"], 1, sp["dn_conv_w"], dpre, ct)
    dqa, dka, dva, dsink, _ = _attn_bwd("swa_bwd", "swa", r["qa"], r["ka"], r["va"], sp["swa_sink"], None, dmix,
                                        dm.Wd // HEAD, dm.ctx, dm.seq)
    daq, d_swa_q_g = _head_prep_bwd("swa_q_prep_bwd", r["P"], 0, dm.swa_h, sp["swa_q_g"], cos, sin, dqa, True)
    dak, d_swa_k_g = _head_prep_bwd("swa_k_prep_bwd", r["P"], 2, dm.kv_h, sp["swa_k_g"], cos, sin, dka, True)
    dqn, dkn, dvn, _, dbias = _attn_bwd("na_bwd", "na", r["qn"], r["kn"], r["vn"], r["no_sink"], r["bias"], dmix,
                                        (dm.Wd + dm.swa_q) // HEAD, dm.ctx, dm.seq)
    dnq, d_na_q_g = _head_prep_bwd("na_q_prep_bwd", r["P"], 10, dm.na_h, sp["na_q_g"], cos, sin, dqn, False)
    dnk, d_na_k_g = _head_prep_bwd("na_k_prep_bwd", r["P"], 11, dm.na_h, sp["na_k_g"], cos, sin, dkn, False)
    _, rpb_vjp = jax.vjp(_bias_table, sp["na_rpb"])
    (d_rpb,) = rpb_vjp(dbias)
    dP = jnp.concatenate([daq, dak, dva[:T].astype(BF16), dqkv, dz, dnq, dnk, dvn[:T].astype(BF16)], axis=-1)
    dw_main = _matmul("in_proj_dw", r["h"], dP, "tn")
    dw_ab = _matmul("in_proj_ab_dw", r["h"], dPab, "tn", tn=HEAD)
    dh = _matmul("in_proj_dx", dP, W["main"], "nt")
    dh_b = _matmul("in_proj_ab_dx", dPab, W["ab"], "nt")
    dx, dn1g, dsh1, dsc1 = _norm_mod_bwd("norm1_bwd", r["x"], sp["norm1_g"], modv, 0, dh, dh_b, dx1, ct)
    dmodv = jnp.concatenate([dsh1, dsc1, dgate1, dsh2, dsc2, dgate2], axis=1)
    big = dict(main=dw_main, ab=dw_ab, out=dw_out, gu=dw_gu, down=dw_down)
    small = dict(norm1_g=dn1g[0], norm2_g=dn2g[0], swa_q_g=d_swa_q_g[0], swa_k_g=d_swa_k_g[0], swa_sink=dsink[:, 0, 0],
                 dn_conv_w=d_conv[:DN_CONV], dn_A_log=d_alog[0, :2 * dm.dn_h].reshape(2, dm.dn_h),
                 dn_dt_bias=d_dtb[0, :2 * dm.dn_h].reshape(2, dm.dn_h), dn_out_g=d_out_g[0], na_q_g=d_na_q_g[0],
                 na_k_g=d_na_k_g[0], na_rpb=d_rpb)
    return dx, big, small, dmodv


def _grad_chunks(dm, bigs):
    def cols(w):
        return w.reshape(w.shape[0], N_CHIPS, -1).transpose(1, 0, 2)

    def rows(w):
        return w.reshape(N_CHIPS, -1, w.shape[1])

    g_in, g_out, g_gate, g_up, g_down = [], [], [], [], []
    for b in bigs:
        m = b["main"]
        g_in.append(cols(jnp.concatenate([m[:, :dm.o_ab], b["ab"][:, :dm.n_ab], m[:, dm.o_ab:]], axis=1)))
        o = b["out"]
        g_out.append(rows(jnp.concatenate([o[dm.Wd:dm.Wd + dm.swa_q], o[:dm.Wd], o[dm.Wd + dm.swa_q:]], axis=0)))
        g_gate.append(cols(b["gu"][:, :dm.ffn]))
        g_up.append(cols(b["gu"][:, dm.ffn:]))
        g_down.append(rows(b["down"]))
    return [jnp.stack(g, axis=1).astype(BF16) for g in (g_in, g_out, g_gate, g_up, g_down)]


SMALL = ("norm1_g", "norm2_g", "swa_q_g", "swa_k_g", "swa_sink", "dn_conv_w", "dn_A_log", "dn_dt_bias", "dn_out_g",
         "na_q_g", "na_k_g", "na_rpb")


def _pack(arrs):
    flat = jnp.concatenate([a.reshape(-1).astype(F32) for a in arrs])
    n = flat.shape[0]
    rows = -(-n // (8 * HEAD)) * 8
    return jnp.pad(flat, (0, rows * HEAD - n)).reshape(rows, HEAD)


def _unpack(packed, like):
    flat = packed.reshape(-1)
    out, o = [], 0
    for a in like:
        out.append(flat[o:o + a.size].reshape(a.shape))
        o += a.size
    return out


def _sum_devices(name, g, which):
    _, R, _ = g.shape

    def body(g_ref, o_ref):
        acc = g_ref[which[0]]
        for b in which[1:]:
            acc = acc + g_ref[b]
        o_ref[...] = acc

    return pl.pallas_call(
        body, name=name, grid=(R // 8,), in_specs=[pl.BlockSpec((N_DEV, 8, HEAD), lambda i: (0, i, 0))],
        out_specs=pl.BlockSpec((8, HEAD), lambda i: (i, 0)), out_shape=jax.ShapeDtypeStruct((R, HEAD), F32),
        compiler_params=_cparams(("parallel",)), **_CALL_KW)(g)


def _silu_rows(name, c_rows):
    return _elementwise(name, lambda c: (c * _sigmoid(c),), [c_rows], [BF16])[0]


def _ada_cotangent(name, dm_all, b_ada_shape):
    _, L, _, N6 = dm_all.shape
    tn = _pick(N6, (1024, 512, 256, 128))

    def body(d_ref, o_ref, b_ref):
        csum = d_ref[0, 0:1, :]
        for b in range(1, N_DEV):
            csum = csum + d_ref[b, 0:1, :]
        tot = csum
        for b in range(N_DEV):
            o_ref[b:b + 1, :] = d_ref[b, 1:2, :]
            tot = tot + d_ref[b, 1:2, :]
        first = lax.broadcasted_iota(jnp.int32, (8, tn), 0) == 0
        o_ref[N_DEV:, :] = jnp.where(first, jnp.broadcast_to(csum, (8, tn)), 0.0)
        b_ref[...] = jnp.broadcast_to(tot, (8, tn))

    return pl.pallas_call(
        body, name=name, grid=(L, N6 // tn), in_specs=[pl.BlockSpec((N_DEV, None, 2, tn), lambda l, j: (0, l, 0, j))],
        out_specs=[pl.BlockSpec((None, 16, tn), lambda l, j: (l, 0, j)), pl.BlockSpec((None, 8, tn), lambda l, j: (l, 0, j))],
        out_shape=[jax.ShapeDtypeStruct((L, 16, N6), F32), jax.ShapeDtypeStruct((L, 8, N6), F32)],
        compiler_params=_cparams(("parallel", "parallel")), **_CALL_KW)(dm_all)


def kernel(x, c, ctx, c_ctx, w_ada, b_ada, norm1_g, norm2_g, w_in, swa_q_g, swa_k_g, swa_sink, dn_conv_w, dn_A_log, dn_dt_bias, dn_out_g, na_q_g, na_k_g, na_rpb, w_out, w_gate, w_up, w_down, loss_target, m_c_ctx, m_w_ada, m_b_ada, m_norm1_g, m_norm2_g, m_w_in, m_swa_q_g, m_swa_k_g, m_swa_sink, m_dn_conv_w, m_dn_A_log, m_dn_dt_bias, m_dn_out_g, m_na_q_g, m_na_k_g, m_na_rpb, m_w_out, m_w_gate, m_w_up, m_w_down, v_c_ctx, v_w_ada, v_b_ada, v_norm1_g, v_norm2_g, v_w_in, v_swa_q_g, v_swa_k_g, v_swa_sink, v_dn_conv_w, v_dn_A_log, v_dn_dt_bias, v_dn_out_g, v_na_q_g, v_na_k_g, v_na_rpb, v_w_out, v_w_gate, v_w_up, v_w_down):
    L = w_in.shape[0]
    D, seq, n_ctx = x.shape[-1], x.shape[1], ctx.shape[1]
    dm = _Dims(D, seq, n_ctx, w_gate.shape[-1] * N_CHIPS)
    xi, yi, ci = _axes()
    chip = 2 * xi + yi
    dev = 4 * xi + 2 * yi + ci
    n6 = 6 * D
    n6s = n6 // N_CHIPS

    shards = [_elementwise(f"cast_{n}", lambda w: (w,), [w], [BF16])[0]
              for n, w in (("w_in", w_in), ("w_out", w_out), ("w_gate", w_gate), ("w_up", w_up), ("w_down", w_down))]
    gathered = _gather_weights(shards)
    Ws = [_layer_weights(dm, l, *gathered) for l in range(L)]
    conv_all = _allgather_small("gather_conv_w", _pack([dn_conv_w]))
    conv_full = jnp.concatenate([_unpack(conv_all[2 * k], [dn_conv_w])[0] for k in range(N_CHIPS)], axis=-1)

    c_all = _allgather_small("gather_c", _pack([c]))
    c_rows = jnp.concatenate([c_all[:, :D // HEAD].reshape(N_DEV, D), c_ctx[None], jnp.zeros((16 - N_DEV - 1, D), F32)], axis=0)
    a_rows = _silu_rows("ada_silu", c_rows)
    b_sh = lax.dynamic_slice_in_dim(b_ada, chip * n6s, n6s, axis=1)
    mod_sh = [_matmul(f"ada_mod{l}", a_rows, w_ada[l], "nn", tm=16) for l in range(L)]
    mod_all = _allgather_small("gather_mod", _pack(mod_sh))
    mods = []
    for l in range(L):
        per_chip = [_unpack(mod_all[2 * k], mod_sh)[l] for k in range(N_CHIPS)]
        mods.append(jnp.concatenate(per_chip, axis=1))
    modvs = []
    for l in range(L):
        rows = jnp.stack([mods[l][N_DEV], lax.dynamic_index_in_dim(mods[l], dev, 0, keepdims=False)])
        modvs.append(_elementwise(f"ada_bias{l}", lambda m, b: (m + b,), [rows, jnp.broadcast_to(b_ada[l][None], (2, n6))], [F32])[0]
                     .reshape(2, 6, D))

    cos, sin = _rope_tables(dm)
    sps = [dict(norm1_g=norm1_g[l][None], norm2_g=norm2_g[l][None], swa_q_g=swa_q_g[l][None], swa_k_g=swa_k_g[l][None],
                swa_sink=swa_sink[l], dn_conv_w=conv_full[l], dn_A_log=dn_A_log[l], dn_dt_bias=dn_dt_bias[l],
                dn_out_g=dn_out_g[l][None], na_q_g=na_q_g[l][None], na_k_g=na_k_g[l][None], na_rpb=na_rpb[l]) for l in range(L)]
    xs = jnp.concatenate([ctx[0], x[0]], axis=0)
    ress = []
    for l in range(L):
        xs, r = _layer_fwd(dm, xs, Ws[l], sps[l], modvs[l], cos, sin)
        ress.append(r)
    loss_blk, dxs = _loss_and_grad("loss", xs, loss_target[0], dm.ctx_tiles)
    loss = lax.psum(loss_blk[0, 0], ("x", "y", "c"))

    bigs, smalls, dmodvs = [None] * L, [None] * L, [None] * L
    for l in reversed(range(L)):
        dxs, bigs[l], smalls[l], dmodvs[l] = _layer_bwd(dm, dxs, Ws[l], sps[l], modvs[l], cos, sin, ress[l])
    grad_x = dxs[n_ctx:][None]

    dm_mine = jnp.stack([d.reshape(2, n6) for d in dmodvs])
    dm_all = _allgather_small("gather_dmod", _pack([dm_mine]))
    dm_all = jnp.stack([_unpack(dm_all[b], [dm_mine])[0] for b in range(N_DEV)])
    dm_rows, d_b_ada = _ada_cotangent("ada_cot", dm_all, b_ada.shape)
    dm_sh = lax.dynamic_slice_in_dim(dm_rows, chip * n6s, n6s, axis=2).astype(BF16)
    g_w_ada = jnp.stack([_matmul(f"ada_dw{l}", a_rows, dm_sh[l], "tn") for l in range(L)])
    dc_part = [_matmul(f"ada_dc{l}", dm_sh[l], w_ada[l], "nt", tm=16) for l in range(L)]
    dc_mine = dc_part[0][N_DEV]
    for l in range(1, L):
        dc_mine = dc_mine + dc_part[l][N_DEV]

    small_list = [jnp.stack([smalls[l][n] for l in range(L)]) for n in SMALL]
    sm_all = _allgather_small("gather_small", _pack(small_list + [dc_mine]))
    sm_sum = _sum_devices("sum_small", sm_all, tuple(range(N_DEV)))
    dc_sum = _sum_devices("sum_dc", sm_all, tuple(range(0, N_DEV, 2)))
    g_small = dict(zip(SMALL, _unpack(sm_sum, small_list)))
    dcs = _unpack(dc_sum, small_list + [dc_mine])[-1]
    def silu_bwd(d, cc):
        s = _sigmoid(cc)
        return (d * (s * (1.0 + cc * (1.0 - s))),)

    g_c_ctx = _elementwise("c_ctx_silu_bwd", silu_bwd, [dcs.reshape(-1, HEAD), c_ctx.reshape(-1, HEAD)], [F32])[0]
    wd3 = dn_conv_w.shape[-1]
    g_small["dn_conv_w"] = lax.dynamic_slice_in_dim(g_small["dn_conv_w"], chip * wd3, wd3, axis=2)

    g_in, g_out, g_gate, g_up, g_down = _reduce_grads(_grad_chunks(dm, bigs))
    grads = dict(g_small, c_ctx=g_c_ctx, w_ada=g_w_ada, b_ada=d_b_ada[:, 0], w_in=g_in, w_out=g_out, w_gate=g_gate, w_up=g_up,
                 w_down=g_down)
    weights = dict(c_ctx=c_ctx, w_ada=w_ada, b_ada=b_ada, norm1_g=norm1_g, norm2_g=norm2_g, w_in=w_in, swa_q_g=swa_q_g,
                   swa_k_g=swa_k_g, swa_sink=swa_sink, dn_conv_w=dn_conv_w, dn_A_log=dn_A_log, dn_dt_bias=dn_dt_bias,
                   dn_out_g=dn_out_g, na_q_g=na_q_g, na_k_g=na_k_g, na_rpb=na_rpb, w_out=w_out, w_gate=w_gate, w_up=w_up,
                   w_down=w_down)
    ms = dict(c_ctx=m_c_ctx, w_ada=m_w_ada, b_ada=m_b_ada, norm1_g=m_norm1_g, norm2_g=m_norm2_g, w_in=m_w_in, swa_q_g=m_swa_q_g,
              swa_k_g=m_swa_k_g, swa_sink=m_swa_sink, dn_conv_w=m_dn_conv_w, dn_A_log=m_dn_A_log, dn_dt_bias=m_dn_dt_bias,
              dn_out_g=m_dn_out_g, na_q_g=m_na_q_g, na_k_g=m_na_k_g, na_rpb=m_na_rpb, w_out=m_w_out, w_gate=m_w_gate, w_up=m_w_up,
              w_down=m_w_down)
    vs = dict(c_ctx=v_c_ctx, w_ada=v_w_ada, b_ada=v_b_ada, norm1_g=v_norm1_g, norm2_g=v_norm2_g, w_in=v_w_in, swa_q_g=v_swa_q_g,
              swa_k_g=v_swa_k_g, swa_sink=v_swa_sink, dn_conv_w=v_dn_conv_w, dn_A_log=v_dn_A_log, dn_dt_bias=v_dn_dt_bias,
              dn_out_g=v_dn_out_g, na_q_g=v_na_q_g, na_k_g=v_na_k_g, na_rpb=v_na_rpb, w_out=v_w_out, w_gate=v_w_gate, w_up=v_w_up,
              w_down=v_w_down)
    order = ("c_ctx", "w_ada", "b_ada", "norm1_g", "norm2_g", "w_in", "swa_q_g", "swa_k_g", "swa_sink", "dn_conv_w", "dn_A_log",
             "dn_dt_bias", "dn_out_g", "na_q_g", "na_k_g", "na_rpb", "w_out", "w_gate", "w_up", "w_down")
    big_names = ("w_ada", "w_in", "w_out", "w_gate", "w_up", "w_down")
    grads = {n: grads[n].reshape(weights[n].shape) for n in order}
    delta, new_m, new_v = {}, {}, {}
    for n in big_names:
        delta[n], new_m[n], new_v[n] = _adamw(f"adamw_{n}", weights[n], grads[n], ms[n], vs[n])
    small_names = [n for n in order if n not in big_names]
    packed = [_pack([d[n] for n in small_names]) for d in (weights, grads, ms, vs)]
    outs = _adamw("adamw_small", *packed)
    like = [weights[n] for n in small_names]
    for d, o in zip((delta, new_m, new_v), outs):
        d.update(dict(zip(small_names, _unpack(o, like))))
    return (loss, grad_x, *[grads[n] for n in order], *[delta[n] for n in order], *[new_m[n] for n in order],
            *[new_v[n] for n in order])
```

```python
import functools
import math

import jax
import jax.numpy as jnp
import numpy as np
from jax import lax
from jax.experimental import pallas as pl
from jax.experimental.pallas import tpu as pltpu

F32, BF16 = jnp.float32, jnp.bfloat16
MESH = pl.DeviceIdType.MESH

GRID_W = 64
HEAD = 128
SWA_WINDOW = 128
DN_CONV = 5
DN_CHUNK = 64
NA_KH, NA_KW = 8, 16
ROPE_THETA = 10000.0
EPS = 1e-6
ADAM_LR, ADAM_B1, ADAM_B2, ADAM_EPS, ADAM_WD, ADAM_STEP = 0.001, 0.9, 0.999, 1e-08, 0.01, 10
N_CHIPS = 4
N_DEV = 8
TOK = 256
VMEM_LIMIT = 56 * 2 ** 20

_CALL_KW = {}


def _cparams(sem=None, **kw):
    if sem is not None:
        kw["dimension_semantics"] = sem
    return pltpu.CompilerParams(vmem_limit_bytes=VMEM_LIMIT, **kw)


def _pick(n, cands):
    for cnd in cands:
        if n % cnd == 0:
            return cnd
    raise ValueError(f"no tile for {n} in {cands}")


def _axes():
    return lax.axis_index("x"), lax.axis_index("y"), lax.axis_index("c")


def _matmul(name, a, b, kind, out_dtype=F32, tm=None, tn=None, tk=None):
    if kind == "nn":
        (M, K), (K2, N) = a.shape, b.shape
    elif kind == "nt":
        (M, K), (N, K2) = a.shape, b.shape
    else:
        (K, M), (K2, N) = a.shape, b.shape
    assert K == K2, (name, a.shape, b.shape)
    tm = tm or _pick(M, (1024, 512, 256, 128) if kind == "tn" else (1088, 1024, 704, 512, 256, 128, 64, 32, 16, 8))
    tn = tn or _pick(N, (512, 256, 128))
    tk = tk or (K if K <= 4352 else _pick(K, (3328, 2816, 2048, 1024, 512)))
    nk = K // tk
    dims = {"nn": (((1,), (0,)), ((), ())), "nt": (((1,), (1,)), ((), ())), "tn": (((0,), (0,)), ((), ()))}[kind]

    def body(a_ref, b_ref, o_ref, *scr):
        part = lax.dot_general(a_ref[...].astype(BF16), b_ref[...].astype(BF16), dims, preferred_element_type=F32)
        if nk == 1:
            o_ref[...] = part.astype(o_ref.dtype)
        else:
            acc = scr[0]
            k = pl.program_id(2)

            @pl.when(k == 0)
            def _():
                acc[...] = part

            @pl.when(k > 0)
            def _():
                acc[...] += part

            @pl.when(k == nk - 1)
            def _():
                o_ref[...] = acc[...].astype(o_ref.dtype)

    a_spec = {"nn": pl.BlockSpec((tm, tk), lambda i, j, k: (i, k)), "nt": pl.BlockSpec((tm, tk), lambda i, j, k: (i, k)),
              "tn": pl.BlockSpec((tk, tm), lambda i, j, k: (k, i))}[kind]
    b_spec = {"nn": pl.BlockSpec((tk, tn), lambda i, j, k: (k, j)), "nt": pl.BlockSpec((tn, tk), lambda i, j, k: (j, k)),
              "tn": pl.BlockSpec((tk, tn), lambda i, j, k: (k, j))}[kind]
    return pl.pallas_call(
        body, name=name, grid=(M // tm, N // tn, nk), in_specs=[a_spec, b_spec],
        out_specs=pl.BlockSpec((tm, tn), lambda i, j, k: (i, j)), out_shape=jax.ShapeDtypeStruct((M, N), out_dtype),
        scratch_shapes=[pltpu.VMEM((tm, tn), F32)] if nk > 1 else [],
        compiler_params=_cparams(("parallel", "parallel", "arbitrary")), **_CALL_KW)(a, b)


def _exchange(name, ins, out_shapes, plan, n_local, n_remote, aliases=None):
    n_in, n_out = len(ins), len(out_shapes)

    def body(*refs):
        in_refs, out_refs = refs[:n_in], refs[n_in:n_in + n_out]
        send_sems, recv_sems, loc_sems = refs[n_in + n_out:]
        x, y, c = _axes()
        local, remote = plan(x, y, c, in_refs, out_refs)
        assert len(local) == n_local and len(remote) == n_remote, (name, len(local), len(remote))
        lcs = [pltpu.make_async_copy(s, d, loc_sems.at[i]) for i, (s, d) in enumerate(local)]
        rcs = [pltpu.make_async_remote_copy(src_ref=s, dst_ref=d, send_sem=send_sems.at[i], recv_sem=recv_sems.at[i],
                                            device_id=dev, device_id_type=MESH) for i, (s, d, dev) in enumerate(remote)]
        for cp in lcs + rcs:
            cp.start()
        for cp in rcs + lcs:
            cp.wait()

    any_spec = pl.BlockSpec(memory_space=pl.ANY)
    return pl.pallas_call(
        body, name=name, in_specs=[any_spec] * n_in, out_specs=[any_spec] * n_out,
        out_shape=[jax.ShapeDtypeStruct(s, d) for s, d in out_shapes],
        scratch_shapes=[pltpu.SemaphoreType.DMA((max(n_remote, 1),)), pltpu.SemaphoreType.DMA((max(n_remote, 1),)),
                        pltpu.SemaphoreType.DMA((max(n_local, 1),))],
        input_output_aliases=aliases or {},
        compiler_params=pltpu.CompilerParams(has_side_effects=True), **_CALL_KW)(*ins)


def _chip_of(k):
    return k // 2, k % 2


def _gather_weights(shards):
    n = len(shards)

    def plan_ici(x, y, c, ins, outs):
        me = 2 * x + y
        remote = []
        for w, g in zip(ins, outs):
            half = w.shape[1] // 2
            rows = pl.ds(c * half, half)
            for j in (1, 2, 3):
                px, py = _chip_of(me ^ j)
                remote.append((w.at[:, rows], g.at[me, :, rows], (px, py, c)))
        return [], remote

    gath = _exchange("gather_w_ici", shards, [((N_CHIPS,) + w.shape, w.dtype) for w in shards], plan_ici, 0, 3 * n)

    def plan_d2d(x, y, c, ins, outs):
        me = 2 * x + y
        remote = []
        for g in outs:
            half = g.shape[2] // 2
            rows = pl.ds(c * half, half)
            for j in (1, 2, 3):
                remote.append((g.at[me ^ j, :, rows], g.at[me ^ j, :, rows], (x, y, 1 - c)))
        return [], remote

    return _exchange("gather_w_d2d", gath, [(g.shape, g.dtype) for g in gath], plan_d2d, 0, 3 * n,
                     aliases={i: i for i in range(n)})


def _elementwise(name, fn, ins, out_dtypes, block_rows=None, n_out=None):
    shape = ins[0].shape
    lead, (R, C) = shape[:-2], shape[-2:]
    budget = (16 * 2 ** 20) // (8 * (len(ins) + len(out_dtypes)) * (-(-C // 128) * 128))
    br = block_rows or _pick(R, [r for r in (512, 256, 128, 352, 64, 32, 16, 8) if r <= max(budget, 8)] + [R])
    nl = len(lead)

    def body(*refs):
        outs = fn(*[r[...] for r in refs[:len(ins)]])
        for r, o in zip(refs[len(ins):], outs):
            r[...] = o.astype(r.dtype)

    blk = (None,) * nl + (br, C)
    spec = pl.BlockSpec(blk, lambda *g: tuple(g[:nl]) + (g[nl], 0))
    return pl.pallas_call(
        body, name=name, grid=tuple(lead) + (R // br,), in_specs=[spec] * len(ins), out_specs=[spec] * len(out_dtypes),
        out_shape=[jax.ShapeDtypeStruct(shape, d) for d in out_dtypes],
        compiler_params=_cparams(("parallel",) * (nl + 1)), **_CALL_KW)(*ins)


def _reduce_grads(parts):
    n = len(parts)

    def plan_a(x, y, c, ins, outs):
        remote = []
        for p, r in zip(ins, outs):
            half = p.shape[2] // 2
            remote.append((p.at[:, :, pl.ds((1 - c) * half, half)], r, (x, y, 1 - c)))
        return [], remote

    halves = [((p.shape[0], p.shape[1], p.shape[2] // 2, p.shape[3]), p.dtype) for p in parts]
    got = _exchange("reduce_g_d2d", parts, halves, plan_a, 0, n)

    c = lax.axis_index("c")
    pair = []
    for idx, (p, r) in enumerate(zip(parts, got)):
        half = p.shape[2] // 2
        br = _pick(half, (512, 256, 352, 128, 64, 32, 16))
        nb = half // br

        def body(c_ref, p_ref, r_ref, o_ref):
            o_ref[...] = (p_ref[...].astype(F32) + r_ref[...].astype(F32)).astype(o_ref.dtype)

        blk = (None, None, br, p.shape[3])
        pair.append(pl.pallas_call(
            body, name=f"reduce_g_pair{idx}",
            grid_spec=pltpu.PrefetchScalarGridSpec(
                num_scalar_prefetch=1, grid=(N_CHIPS, 2, nb),
                in_specs=[pl.BlockSpec(blk, lambda k, l, i, cr, nb=nb: (k, l, cr[0] * nb + i, 0)),
                          pl.BlockSpec(blk, lambda k, l, i, cr: (k, l, i, 0))],
                out_specs=pl.BlockSpec(blk, lambda k, l, i, cr: (k, l, i, 0))),
            out_shape=jax.ShapeDtypeStruct(r.shape, BF16),
            compiler_params=_cparams(("parallel",) * 3), **_CALL_KW)(jnp.reshape(c, (1,)).astype(jnp.int32), p, r))

    def plan_b(x, y, c, ins, outs):
        me = 2 * x + y
        remote = []
        for p, r in zip(ins, outs):
            for j in (1, 2, 3):
                px, py = _chip_of(me ^ j)
                remote.append((p.at[me ^ j], r.at[me], (px, py, c)))
        return [], remote

    got = _exchange("reduce_g_ici", pair, [(p.shape, p.dtype) for p in pair], plan_b, 0, 3 * n)
    me_chip = 2 * lax.axis_index("x") + lax.axis_index("y")

    sums = []
    for idx, r in enumerate(got):
        _, _, half, C = r.shape
        br = _pick(half, (512, 256, 352, 128, 64, 32, 16))
        nb = half // br

        def body(c_ref, p_ref, r_ref, o_ref):
            me = c_ref[1]
            acc = None
            for k in range(N_CHIPS):
                term = jnp.where(me == k, p_ref[k], r_ref[k]).astype(F32)
                acc = term if acc is None else acc + term
            o_ref[...] = acc

        blk4 = pl.BlockSpec((N_CHIPS, None, br, C), lambda l, i, cr: (0, l, i, 0))
        sums.append(pl.pallas_call(
            body, name=f"reduce_g_sum{idx}",
            grid_spec=pltpu.PrefetchScalarGridSpec(
                num_scalar_prefetch=1, grid=(2, nb), in_specs=[blk4, blk4],
                out_specs=pl.BlockSpec((None, br, C), lambda l, i, cr, nb=nb: (l, cr[0] * nb + i, 0))),
            out_shape=jax.ShapeDtypeStruct((2, 2 * half, C), F32),
            compiler_params=_cparams(("parallel",) * 2), **_CALL_KW)(jnp.stack([c, me_chip]).astype(jnp.int32), pair[idx], r))

    def plan_c(x, y, c, ins, outs):
        remote = []
        for f in outs:
            half = f.shape[1] // 2
            rows = pl.ds(c * half, half)
            remote.append((f.at[:, rows], f.at[:, rows], (x, y, 1 - c)))
        return [], remote

    return _exchange("reduce_g_bcast", sums, [(s.shape, F32) for s in sums], plan_c, 0, n, aliases={i: i for i in range(n)})


def _adamw_math(w, g, m, v):
    m = ADAM_B1 * m + (1.0 - ADAM_B1) * g
    v = ADAM_B2 * v + (1.0 - ADAM_B2) * (g * g)
    m_hat = m / (1.0 - ADAM_B1 ** ADAM_STEP)
    v_hat = v / (1.0 - ADAM_B2 ** ADAM_STEP)
    delta = -ADAM_LR * (m_hat / (jnp.sqrt(v_hat) + ADAM_EPS) + ADAM_WD * w)
    return delta, m, v


def _adamw(name, w, g, m, v):
    return _elementwise(name, _adamw_math, [w, g, m, v], [F32, F32, F32])


def _allgather_small(name, v):
    def plan(x, y, c, ins, outs):
        me = 4 * x + 2 * y + c
        (src,), (dst,) = ins, outs
        remote = []
        for j in range(1, N_DEV):
            p = me ^ j
            remote.append((src, dst.at[me], (p // 4, (p // 2) % 2, p % 2)))
        return [(src, dst.at[me])], remote

    return _exchange(name, [v], [((N_DEV,) + v.shape, v.dtype)], plan, 1, N_DEV - 1)[0]


def _seg_spec(rows, D, ctx_tiles):
    return pl.BlockSpec((None, rows, D), lambda i: (jnp.minimum(i // ctx_tiles, 1), 0, 0))


def _norm_mod(name, x, g, modv, r0, ctx_tiles):
    T, D = x.shape

    def body(x_ref, g_ref, m_ref, o_ref):
        xv = x_ref[...]
        r = lax.rsqrt(jnp.mean(xv * xv, axis=-1, keepdims=True) + EPS)
        y = xv * r * g_ref[...]
        o_ref[...] = (y * (1.0 + m_ref[r0 + 1:r0 + 2, :]) + m_ref[r0:r0 + 1, :]).astype(BF16)

    row = pl.BlockSpec((TOK, D), lambda i: (i, 0))
    return pl.pallas_call(
        body, name=name, grid=(T // TOK,), in_specs=[row, pl.BlockSpec((1, D), lambda i: (0, 0)), _seg_spec(6, D, ctx_tiles)],
        out_specs=row, out_shape=jax.ShapeDtypeStruct((T, D), BF16), compiler_params=_cparams(("parallel",)), **_CALL_KW)(x, g, modv)


def _norm_mod_bwd(name, x, g, modv, r0, dh, dh_b, dres, ctx_tiles):
    T, D = x.shape

    def body(x_ref, g_ref, m_ref, dh_ref, dhb_ref, dres_ref, dx_ref, dg_ref, dsh_ref, dsc_ref):
        i = pl.program_id(0)
        xv = x_ref[...]
        r = lax.rsqrt(jnp.mean(xv * xv, axis=-1, keepdims=True) + EPS)
        xn = xv * r
        y = xn * g_ref[...]
        dhv = dh_ref[...] + dhb_ref[...]

        @pl.when(i == 0)
        def _():
            dg_ref[...] = jnp.zeros_like(dg_ref)

        @pl.when((i == 0) | (i == ctx_tiles))
        def _():
            dsh_ref[...] = jnp.zeros_like(dsh_ref)
            dsc_ref[...] = jnp.zeros_like(dsc_ref)

        dsh_ref[...] += jnp.sum(dhv, axis=0, keepdims=True)
        dsc_ref[...] += jnp.sum(dhv * y, axis=0, keepdims=True)
        dy = dhv * (1.0 + m_ref[r0 + 1:r0 + 2, :])
        dg_ref[...] += jnp.sum(dy * xn, axis=0, keepdims=True)
        u = dy * g_ref[...]
        dx_ref[...] = dres_ref[...] + r * (u - xn * jnp.mean(u * xn, axis=-1, keepdims=True))

    row = pl.BlockSpec((TOK, D), lambda i: (i, 0))
    one = pl.BlockSpec((1, D), lambda i: (0, 0))
    return pl.pallas_call(
        body, name=name, grid=(T // TOK,), in_specs=[row, one, _seg_spec(6, D, ctx_tiles), row, row, row],
        out_specs=[row, one, _seg_spec(1, D, ctx_tiles), _seg_spec(1, D, ctx_tiles)],
        out_shape=[jax.ShapeDtypeStruct((T, D), F32), jax.ShapeDtypeStruct((1, D), F32),
                   jax.ShapeDtypeStruct((2, 1, D), F32), jax.ShapeDtypeStruct((2, 1, D), F32)],
        compiler_params=_cparams(("arbitrary",)), **_CALL_KW)(x, g, modv, dh, dh_b, dres)


def _resid_gate(name, x, y, modv, r, ctx_tiles):
    T, D = x.shape

    def body(x_ref, y_ref, m_ref, o_ref):
        o_ref[...] = x_ref[...] + m_ref[r:r + 1, :] * y_ref[...]

    row = pl.BlockSpec((TOK, D), lambda i: (i, 0))
    return pl.pallas_call(
        body, name=name, grid=(T // TOK,), in_specs=[row, row, _seg_spec(6, D, ctx_tiles)], out_specs=row,
        out_shape=jax.ShapeDtypeStruct((T, D), F32), compiler_params=_cparams(("parallel",)), **_CALL_KW)(x, y, modv)


def _resid_gate_bwd(name, dx, y, modv, r, ctx_tiles):
    T, D = dx.shape

    def body(dx_ref, y_ref, m_ref, dy_ref, dgt_ref):
        i = pl.program_id(0)

        @pl.when((i == 0) | (i == ctx_tiles))
        def _():
            dgt_ref[...] = jnp.zeros_like(dgt_ref)

        dxv = dx_ref[...]
        dgt_ref[...] += jnp.sum(dxv * y_ref[...], axis=0, keepdims=True)
        dy_ref[...] = (dxv * m_ref[r:r + 1, :]).astype(BF16)

    row = pl.BlockSpec((TOK, D), lambda i: (i, 0))
    return pl.pallas_call(
        body, name=name, grid=(T // TOK,), in_specs=[row, row, _seg_spec(6, D, ctx_tiles)],
        out_specs=[row, _seg_spec(1, D, ctx_tiles)],
        out_shape=[jax.ShapeDtypeStruct((T, D), BF16), jax.ShapeDtypeStruct((2, 1, D), F32)],
        compiler_params=_cparams(("arbitrary",)), **_CALL_KW)(dx, y, modv)


def _sigmoid(x):
    return 1.0 / (1.0 + jnp.exp(-x))


SWI_ROWS = 128


def _swiglu(name, gu):
    T, F2 = gu.shape
    F = F2 // 2

    def body(gu_ref, o_ref):
        g, u = gu_ref[:, :F], gu_ref[:, F:]
        o_ref[...] = ((g * _sigmoid(g)) * u).astype(BF16)

    return pl.pallas_call(
        body, name=name, grid=(T // SWI_ROWS,), in_specs=[pl.BlockSpec((SWI_ROWS, F2), lambda i: (i, 0))],
        out_specs=pl.BlockSpec((SWI_ROWS, F), lambda i: (i, 0)), out_shape=jax.ShapeDtypeStruct((T, F), BF16),
        compiler_params=_cparams(("parallel",)), **_CALL_KW)(gu)


def _swiglu_bwd(name, gu, dact):
    T, F2 = gu.shape
    F = F2 // 2

    def body(gu_ref, d_ref, o_ref):
        g, u, d = gu_ref[:, :F], gu_ref[:, F:], d_ref[...]
        s = _sigmoid(g)
        o_ref[:, :F] = (d * u * (s * (1.0 + g * (1.0 - s)))).astype(BF16)
        o_ref[:, F:] = (d * (g * s)).astype(BF16)

    return pl.pallas_call(
        body, name=name, grid=(T // SWI_ROWS,),
        in_specs=[pl.BlockSpec((SWI_ROWS, F2), lambda i: (i, 0)), pl.BlockSpec((SWI_ROWS, F), lambda i: (i, 0))],
        out_specs=pl.BlockSpec((SWI_ROWS, F2), lambda i: (i, 0)), out_shape=jax.ShapeDtypeStruct((T, F2), BF16),
        compiler_params=_cparams(("parallel",)), **_CALL_KW)(gu, dact)


def _loss_and_grad(name, y, target, ctx_tiles):
    T, D = y.shape

    def body(y_ref, t_ref, l_ref, dy_ref):
        i = pl.program_id(0)

        @pl.when(i == 0)
        def _():
            l_ref[...] = jnp.zeros_like(l_ref)

        lat = i >= ctx_tiles
        e = jnp.where(lat, y_ref[...] - t_ref[...], 0.0)
        dy_ref[...] = e * (1.0 / D)
        l_ref[...] += 0.5 * jnp.sum(jnp.sum(e * e, axis=-1, keepdims=True) * (1.0 / D), axis=0, keepdims=True)

    row = pl.BlockSpec((TOK, D), lambda i: (i, 0))
    return pl.pallas_call(
        body, name=name, grid=(T // TOK,),
        in_specs=[row, pl.BlockSpec((TOK, D), lambda i: (jnp.maximum(i - ctx_tiles, 0), 0))],
        out_specs=[pl.BlockSpec((8, 128), lambda i: (0, 0)), row],
        out_shape=[jax.ShapeDtypeStruct((8, 128), F32), jax.ShapeDtypeStruct((T, D), F32)],
        compiler_params=_cparams(("arbitrary",)), **_CALL_KW)(y, target)


def _rot_half(x):
    lane = lax.broadcasted_iota(jnp.int32, x.shape, 1)
    return jnp.where((lane % 64) < 32, -pltpu.roll(x, 96, 1), pltpu.roll(x, 32, 1))


def _head_prep(name, src, col_blk, n_heads, g, cos, sin, t_pad, norm, rope):
    T = src.shape[0]
    W = n_heads * HEAD
    nt = T // TOK

    def body(s_ref, g_ref, cos_ref, sin_ref, o_ref):
        i = pl.program_id(0)
        outs = []
        for h in range(n_heads):
            xv = s_ref[:, h * HEAD:(h + 1) * HEAD].astype(F32)
            if norm:
                xv = xv * lax.rsqrt(jnp.mean(xv * xv, axis=-1, keepdims=True) + EPS) * g_ref[...]
            if rope:
                xv = xv * cos_ref[...] + _rot_half(xv) * sin_ref[...]
            outs.append(jnp.where(i < nt, xv, 0.0).astype(BF16))
        o_ref[...] = jnp.concatenate(outs, axis=-1) if n_heads > 1 else outs[0]

    tab = pl.BlockSpec((TOK, HEAD), lambda i: (i, 0))
    return pl.pallas_call(
        body, name=name, grid=(t_pad // TOK,),
        in_specs=[pl.BlockSpec((TOK, W), lambda i: (jnp.minimum(i, nt - 1), col_blk)), pl.BlockSpec((1, HEAD), lambda i: (0, 0)), tab, tab],
        out_specs=pl.BlockSpec((TOK, W), lambda i: (i, 0)), out_shape=jax.ShapeDtypeStruct((t_pad, W), BF16),
        compiler_params=_cparams(("parallel",)), **_CALL_KW)(src, g, cos, sin)


def _head_prep_bwd(name, src, col_blk, n_heads, g, cos, sin, dout, rope):
    T = src.shape[0]
    W = n_heads * HEAD

    def body(s_ref, g_ref, cos_ref, sin_ref, d_ref, ds_ref, dg_ref):
        i = pl.program_id(0)

        @pl.when(i == 0)
        def _():
            dg_ref[...] = jnp.zeros_like(dg_ref)

        outs = []
        dg = jnp.zeros((1, HEAD), F32)
        for h in range(n_heads):
            xv = s_ref[:, h * HEAD:(h + 1) * HEAD].astype(F32)
            dz = d_ref[:, h * HEAD:(h + 1) * HEAD]
            if rope:
                dz = dz * cos_ref[...] - _rot_half(dz * sin_ref[...])
            r = lax.rsqrt(jnp.mean(xv * xv, axis=-1, keepdims=True) + EPS)
            xn = xv * r
            dg = dg + jnp.sum(dz * xn, axis=0, keepdims=True)
            u = dz * g_ref[...]
            outs.append((r * (u - xn * jnp.mean(u * xn, axis=-1, keepdims=True))).astype(BF16))
        dg_ref[...] += dg
        ds_ref[...] = jnp.concatenate(outs, axis=-1) if n_heads > 1 else outs[0]

    tab = pl.BlockSpec((TOK, HEAD), lambda i: (i, 0))
    row = pl.BlockSpec((TOK, W), lambda i: (i, 0))
    one = pl.BlockSpec((1, HEAD), lambda i: (0, 0))
    return pl.pallas_call(
        body, name=name, grid=(T // TOK,),
        in_specs=[pl.BlockSpec((TOK, W), lambda i: (i, col_blk)), one, tab, tab, row],
        out_specs=[row, one], out_shape=[jax.ShapeDtypeStruct((T, W), BF16), jax.ShapeDtypeStruct((1, HEAD), F32)],
        compiler_params=_cparams(("arbitrary",)), **_CALL_KW)(src, g, cos, sin, dout)


NEG = -1e30


def _attn_geometry(kind, blk, ctx, seq):
    if kind == "swa":
        bq, W = 128, 384
        nctx = ctx // bq
        lat = blk >= nctx
        n = blk - nctx
        s0 = jnp.where(lat, ctx + (n - 1) * bq, 0)
        i = lax.broadcasted_iota(jnp.int32, (bq, W), 0)
        j = lax.broadcasted_iota(jnp.int32, (bq, W), 1)
        kpos = (n - 1) * bq + j
        rel = j - bq - i
        valid = lat & (rel <= SWA_WINDOW) & (rel >= -SWA_WINDOW) & (kpos >= 0) & (kpos < seq)
        return s0, valid, 0
    bq, W = GRID_W, NA_KH * GRID_W
    nctx = ctx // bq
    rows = seq // GRID_W
    lat = blk >= nctx
    rr = jnp.clip(blk - nctx, 0, rows - 1)
    rs = jnp.clip(rr - NA_KH // 2, 0, rows - NA_KH)
    s0 = ctx + rs * GRID_W
    i = lax.broadcasted_iota(jnp.int32, (bq, W), 0)
    j = lax.broadcasted_iota(jnp.int32, (bq, W), 1)
    kcol = j % GRID_W
    cs = jnp.clip(i - NA_KW // 2, 0, GRID_W - NA_KW)
    valid = lat & (kcol >= cs) & (kcol < cs + NA_KW)
    return s0, valid, rr - rs


def _attn_probs(kind, h, blk, q, k_ref, sink_ref, bias_ref, ctx, seq):
    bq, W = (128, 384) if kind == "swa" else (GRID_W, NA_KH * GRID_W)
    scale = HEAD ** -0.5
    s0, valid, _ = _attn_geometry(kind, blk, ctx, seq)
    s0 = pl.multiple_of(s0, GRID_W)
    kl = k_ref[pl.ds(s0, W), :]
    kc = k_ref[0:ctx, :]
    nt_dims = (((1,), (1,)), ((), ()))
    sl = lax.dot_general(q, kl, nt_dims, preferred_element_type=F32) * scale
    if bias_ref is not None:
        sl = sl + bias_ref[...]
    sl = jnp.where(valid, sl, NEG)
    sc = lax.dot_general(q, kc, nt_dims, preferred_element_type=F32) * scale
    sk = sink_ref[h]
    m = jnp.maximum(jnp.maximum(jnp.max(sl, axis=-1, keepdims=True), jnp.max(sc, axis=-1, keepdims=True)), sk)
    el, ec, es = jnp.exp(sl - m), jnp.exp(sc - m), jnp.exp(sk - m)
    inv = 1.0 / (jnp.sum(el, axis=-1, keepdims=True) + jnp.sum(ec, axis=-1, keepdims=True) + es)
    return s0, kl, kc, el * inv, ec * inv, es * inv


def _attn_specs(kind, n_q, n_kv, t_pad, bias):
    bq = 128 if kind == "swa" else GRID_W
    rep = n_q // n_kv
    qspec = pl.BlockSpec((bq, HEAD), lambda h, b: (b, h))
    kvspec = pl.BlockSpec((t_pad, HEAD), lambda h, b: (0, h // rep))
    specs = [qspec, kvspec, kvspec, pl.BlockSpec(memory_space=pltpu.SMEM)]
    return bq, rep, qspec, kvspec, specs


def _bias_spec(ctx, seq):
    W = NA_KH * GRID_W

    def idx(h, b):
        rows = seq // GRID_W
        rr = jnp.clip(b - ctx // GRID_W, 0, rows - 1)
        return (h, rr - jnp.clip(rr - NA_KH // 2, 0, rows - NA_KH), 0, 0)

    return pl.BlockSpec((None, None, GRID_W, W), idx)


def _attn_fwd(name, kind, q, k, v, sink, bias, ctx, seq):
    t_pad = q.shape[0]
    T = ctx + seq
    n_q, n_kv = q.shape[1] // HEAD, k.shape[1] // HEAD
    bq, rep, qspec, kvspec, specs = _attn_specs(kind, n_q, n_kv, t_pad, bias)

    def body(q_ref, k_ref, v_ref, sink_ref, *rest):
        bias_ref = rest[0] if bias is not None else None
        o_ref = rest[-1]
        h, blk = pl.program_id(0), pl.program_id(1)
        s0, kl, kc, p_l, p_c, _ = _attn_probs(kind, h, blk, q_ref[...], k_ref, sink_ref, bias_ref, ctx, seq)
        W = kl.shape[0]
        o = jnp.dot(p_l.astype(BF16), v_ref[pl.ds(s0, W), :], preferred_element_type=F32)
        o_ref[...] = o + jnp.dot(p_c.astype(BF16), v_ref[0:ctx, :], preferred_element_type=F32)

    ins = [q, k, v, sink] + ([bias] if bias is not None else [])
    if bias is not None:
        specs = specs + [_bias_spec(ctx, seq)]
    return pl.pallas_call(
        body, name=name, grid=(n_q, T // bq), in_specs=specs, out_specs=qspec,
        out_shape=jax.ShapeDtypeStruct((T, n_q * HEAD), F32), compiler_params=_cparams(("parallel", "arbitrary")), **_CALL_KW)(*ins)


def _attn_bwd(name, kind, q, k, v, sink, bias, do, do_head0, ctx, seq):
    t_pad = q.shape[0]
    T = ctx + seq
    n_q, n_kv = q.shape[1] // HEAD, k.shape[1] // HEAD
    bq, rep, qspec, kvspec, specs = _attn_specs(kind, n_q, n_kv, t_pad, bias)
    scale = HEAD ** -0.5
    tn_dims = (((0,), (0,)), ((), ()))
    nt_dims = (((1,), (1,)), ((), ()))

    def body(q_ref, k_ref, v_ref, sink_ref, *rest):
        if bias is not None:
            bias_ref, do_ref, dq_ref, dk_ref, dv_ref, dsk_ref, db_ref = rest
        else:
            bias_ref, db_ref = None, None
            do_ref, dq_ref, dk_ref, dv_ref, dsk_ref = rest
        h, blk = pl.program_id(0), pl.program_id(1)
        qv = q_ref[...]
        s0, kl, kc, p_l, p_c, p_s = _attn_probs(kind, h, blk, qv, k_ref, sink_ref, bias_ref, ctx, seq)
        W = kl.shape[0]
        vl = v_ref[pl.ds(s0, W), :]
        vc = v_ref[0:ctx, :]
        dov = do_ref[...]
        dob = dov.astype(BF16)
        pl_b, pc_b = p_l.astype(BF16), p_c.astype(BF16)
        o = jnp.dot(pl_b, vl, preferred_element_type=F32) + jnp.dot(pc_b, vc, preferred_element_type=F32)
        delta = jnp.sum(dov * o, axis=-1, keepdims=True)
        ds_l = p_l * (lax.dot_general(dob, vl, nt_dims, preferred_element_type=F32) - delta)
        ds_c = p_c * (lax.dot_general(dob, vc, nt_dims, preferred_element_type=F32) - delta)
        dsl_b, dsc_b = ds_l.astype(BF16), ds_c.astype(BF16)
        dq_ref[...] = (jnp.dot(dsl_b, kl, preferred_element_type=F32) + jnp.dot(dsc_b, kc, preferred_element_type=F32)) * scale

        @pl.when((h % rep == 0) & (blk == 0))
        def _():
            dk_ref[...] = jnp.zeros_like(dk_ref)
            dv_ref[...] = jnp.zeros_like(dv_ref)

        @pl.when(blk == 0)
        def _():
            dsk_ref[...] = jnp.zeros_like(dsk_ref)

        dk_ref[pl.ds(s0, W), :] += lax.dot_general(dsl_b, qv, tn_dims, preferred_element_type=F32) * scale
        dv_ref[pl.ds(s0, W), :] += lax.dot_general(pl_b, dob, tn_dims, preferred_element_type=F32)
        dk_ref[0:ctx, :] += lax.dot_general(dsc_b, qv, tn_dims, preferred_element_type=F32) * scale
        dv_ref[0:ctx, :] += lax.dot_general(pc_b, dob, tn_dims, preferred_element_type=F32)
        dsk_ref[...] += jnp.sum(-p_s * delta, axis=0, keepdims=True)
        if bias is not None:
            _, _, pat = _attn_geometry(kind, blk, ctx, seq)
            _, _, pat_prev = _attn_geometry(kind, jnp.maximum(blk - 1, 0), ctx, seq)

            @pl.when((blk == 0) | (pat != pat_prev))
            def _():
                db_ref[...] = jnp.zeros_like(db_ref)

            db_ref[...] += ds_l

    ins = [q, k, v, sink] + ([bias] if bias is not None else []) + [do]
    in_specs = specs + ([_bias_spec(ctx, seq)] if bias is not None else []) + [pl.BlockSpec((bq, HEAD), lambda h, b: (b, do_head0 + h))]
    out_specs = [qspec, kvspec, kvspec, pl.BlockSpec((None, 8, HEAD), lambda h, b: (h, 0, 0))]
    out_shape = [jax.ShapeDtypeStruct((T, n_q * HEAD), F32), jax.ShapeDtypeStruct((t_pad, n_kv * HEAD), F32),
                 jax.ShapeDtypeStruct((t_pad, n_kv * HEAD), F32), jax.ShapeDtypeStruct((n_q, 8, HEAD), F32)]
    if bias is not None:
        out_specs.append(_bias_spec(ctx, seq))
        out_shape.append(jax.ShapeDtypeStruct(bias.shape, F32))
    res = pl.pallas_call(
        body, name=name, grid=(n_q, T // bq), in_specs=in_specs, out_specs=out_specs, out_shape=out_shape,
        compiler_params=_cparams(("arbitrary", "arbitrary")), **_CALL_KW)(*ins)
    return res if bias is not None else list(res) + [None]


HALO = 8


def _halo_specs(width, col0, T, ctx_tiles):
    per = TOK // HALO
    main = pl.BlockSpec((TOK, width), lambda jc, i: (i, col0 + jc))
    prev = pl.BlockSpec((HALO, width), lambda jc, i: (jnp.maximum(i * per - 1, 0), col0 + jc))
    nxt = pl.BlockSpec((HALO, width), lambda jc, i: (jnp.minimum((i + 1) * per, T // HALO - 1), col0 + jc))
    return main, prev, nxt


def _with_halo(i, nt, ctx_tiles, prev, main, nxt):
    has_prev = (i != 0) & (i != ctx_tiles)
    has_next = (i != ctx_tiles - 1) & (i != nt - 1)
    return jnp.concatenate([jnp.where(has_prev, prev, 0.0), main, jnp.where(has_next, nxt, 0.0)], axis=0)


def _shifted(ext, s):
    n = ext.shape[0]
    return pltpu.roll(ext, (-s) % n, 0)[HALO:HALO + TOK]


def _conv_fwd(name, p, col0, conv_w, ctx_tiles):
    T = p.shape[0]
    nt = T // TOK
    ncol = 3
    Wc = conv_w.shape[1] // ncol
    pad = (DN_CONV - 1) // 2

    def body(m_ref, p_ref, n_ref, w_ref, o_ref):
        i = pl.program_id(1)
        ext = _with_halo(i, nt, ctx_tiles, p_ref[...].astype(F32), m_ref[...].astype(F32), n_ref[...].astype(F32))
        acc = jnp.zeros((TOK, Wc), F32)
        for j in range(DN_CONV):
            acc = acc + w_ref[j:j + 1, :] * _shifted(ext, j - pad)
        o_ref[...] = acc

    main, prev, nxt = _halo_specs(Wc, col0, T, ctx_tiles)
    return pl.pallas_call(
        body, name=name, grid=(ncol, nt), in_specs=[main, prev, nxt, pl.BlockSpec((DN_CONV, Wc), lambda jc, i: (0, jc))],
        out_specs=pl.BlockSpec((TOK, Wc), lambda jc, i: (i, jc)), out_shape=jax.ShapeDtypeStruct((T, ncol * Wc), F32),
        compiler_params=_cparams(("parallel", "parallel")), **_CALL_KW)(p, p, p, conv_w)


def _conv_bwd(name, p, col0, conv_w, dpre, ctx_tiles):
    T = p.shape[0]
    nt = T // TOK
    ncol = 3
    Wc = conv_w.shape[1] // ncol
    pad = (DN_CONV - 1) // 2

    def body(m_ref, p_ref, n_ref, dm_ref, dp_ref, dn_ref, w_ref, dx_ref, dw_ref):
        i = pl.program_id(1)
        ext_x = _with_halo(i, nt, ctx_tiles, p_ref[...].astype(F32), m_ref[...].astype(F32), n_ref[...].astype(F32))
        ext_d = _with_halo(i, nt, ctx_tiles, dp_ref[...], dm_ref[...], dn_ref[...])
        dmain = dm_ref[...]

        @pl.when(i == 0)
        def _():
            dw_ref[...] = jnp.zeros_like(dw_ref)

        acc = jnp.zeros((TOK, Wc), F32)
        for j in range(DN_CONV):
            acc = acc + w_ref[j:j + 1, :] * _shifted(ext_d, pad - j)
            dw_ref[j:j + 1, :] += jnp.sum(dmain * _shifted(ext_x, j - pad), axis=0, keepdims=True)
        dx_ref[...] = acc.astype(BF16)

    main, prev, nxt = _halo_specs(Wc, col0, T, ctx_tiles)
    dmain, dprev, dnxt = _halo_specs(Wc, 0, T, ctx_tiles)
    return pl.pallas_call(
        body, name=name, grid=(ncol, nt),
        in_specs=[main, prev, nxt, dmain, dprev, dnxt, pl.BlockSpec((DN_CONV, Wc), lambda jc, i: (0, jc))],
        out_specs=[pl.BlockSpec((TOK, Wc), lambda jc, i: (i, jc)), pl.BlockSpec((8, Wc), lambda jc, i: (0, jc))],
        out_shape=[jax.ShapeDtypeStruct((T, ncol * Wc), BF16), jax.ShapeDtypeStruct((8, ncol * Wc), F32)],
        compiler_params=_cparams(("parallel", "arbitrary")), **_CALL_KW)(p, p, p, dpre, dpre, dpre, conv_w)


def _softplus(x):
    return jnp.maximum(x, 0.0) + jnp.log(1.0 + jnp.exp(-jnp.abs(x)))


def _gdn_point(name, pre, dab, a_log, dt_bias, n_heads):
    T = pre.shape[0]
    Wd = n_heads * HEAD
    ng = 2 * n_heads

    def body(pre_ref, ab_ref, al_ref, dt_ref, q_ref, k_ref, v_ref, la_ref, be_ref):
        for h in range(n_heads):
            for part, ref in enumerate((q_ref, k_ref, v_ref)):
                xv = pre_ref[:, part * Wd + h * HEAD:part * Wd + (h + 1) * HEAD]
                s = xv * _sigmoid(xv)
                if part < 2:
                    s = s * lax.rsqrt(jnp.sum(s * s, axis=-1, keepdims=True) + EPS) * (HEAD ** -0.5 if part == 0 else 1.0)
                ref[:, h * HEAD:(h + 1) * HEAD] = s
        ab = ab_ref[...].astype(F32)
        lane = lax.broadcasted_iota(jnp.int32, ab.shape, 1)
        la_ref[...] = jnp.where(lane < ng, -jnp.exp(al_ref[...]) * _softplus(ab + dt_ref[...]), 0.0)
        be_ref[...] = jnp.where(lane < ng, _sigmoid(pltpu.roll(ab, HEAD - ng, 1)), 0.0)

    row = lambda w: pl.BlockSpec((TOK, w), lambda i: (i, 0))
    one = pl.BlockSpec((1, HEAD), lambda i: (0, 0))
    return pl.pallas_call(
        body, name=name, grid=(T // TOK,), in_specs=[row(3 * Wd), row(HEAD), one, one],
        out_specs=[row(Wd), row(Wd), row(Wd), row(HEAD), row(HEAD)],
        out_shape=[jax.ShapeDtypeStruct((T, Wd), F32)] * 3 + [jax.ShapeDtypeStruct((T, HEAD), F32)] * 2,
        compiler_params=_cparams(("parallel",)), **_CALL_KW)(pre, dab, a_log, dt_bias)


def _gdn_point_bwd(name, pre, dab, a_log, dt_bias, n_heads, dq, dk, dv, dla, dbe):
    T = pre.shape[0]
    Wd = n_heads * HEAD
    ng = 2 * n_heads

    def body(pre_ref, ab_ref, al_ref, dt_ref, dq_ref, dk_ref, dv_ref, dla_ref, dbe_ref, dpre_ref, dab_ref, dal_ref, ddt_ref):
        i = pl.program_id(0)

        @pl.when(i == 0)
        def _():
            dal_ref[...] = jnp.zeros_like(dal_ref)
            ddt_ref[...] = jnp.zeros_like(ddt_ref)

        for h in range(n_heads):
            for part, ref in enumerate((dq_ref, dk_ref, dv_ref)):
                cols = slice(part * Wd + h * HEAD, part * Wd + (h + 1) * HEAD)
                xv = pre_ref[:, cols]
                sg = _sigmoid(xv)
                s = xv * sg
                dy = ref[0, :, h * HEAD:(h + 1) * HEAD] + ref[1, :, h * HEAD:(h + 1) * HEAD]
                if part < 2:
                    c0 = HEAD ** -0.5 if part == 0 else 1.0
                    r = lax.rsqrt(jnp.sum(s * s, axis=-1, keepdims=True) + EPS)
                    ds = c0 * (r * dy - s * (r * r * r) * jnp.sum(dy * s, axis=-1, keepdims=True))
                else:
                    ds = dy
                dpre_ref[:, cols] = ds * (sg * (1.0 + xv * (1.0 - sg)))
        ab = ab_ref[...].astype(F32)
        lane = lax.broadcasted_iota(jnp.int32, ab.shape, 1)
        ea = jnp.exp(al_ref[...])
        z = ab + dt_ref[...]
        dlav = jnp.where(lane < ng, dla_ref[0] + dla_ref[1], 0.0)
        da = dlav * (-ea) * _sigmoid(z)
        dal_ref[...] += jnp.sum(dlav * (-ea) * _softplus(z), axis=0, keepdims=True)
        ddt_ref[...] += jnp.sum(da, axis=0, keepdims=True)
        be = _sigmoid(pltpu.roll(ab, HEAD - ng, 1))
        db = jnp.where(lane < ng, (dbe_ref[0] + dbe_ref[1]) * be * (1.0 - be), 0.0)
        dab_ref[...] = (da + pltpu.roll(db, ng, 1)).astype(BF16)

    row = lambda w: pl.BlockSpec((TOK, w), lambda i: (i, 0))
    two = lambda w: pl.BlockSpec((2, TOK, w), lambda i: (0, i, 0))
    one = pl.BlockSpec((1, HEAD), lambda i: (0, 0))
    return pl.pallas_call(
        body, name=name, grid=(T // TOK,),
        in_specs=[row(3 * Wd), row(HEAD), one, one, two(Wd), two(Wd), two(Wd), two(HEAD), two(HEAD)],
        out_specs=[row(3 * Wd), row(HEAD), one, one],
        out_shape=[jax.ShapeDtypeStruct((T, 3 * Wd), F32), jax.ShapeDtypeStruct((T, HEAD), BF16),
                   jax.ShapeDtypeStruct((1, HEAD), F32), jax.ShapeDtypeStruct((1, HEAD), F32)],
        compiler_params=_cparams(("arbitrary",)), **_CALL_KW)(pre, dab, a_log, dt_bias, dq, dk, dv, dla, dbe)


def _mm3_raw(a, b):
    ah, bh = a.astype(BF16), b.astype(BF16)
    al, bl = (a - ah.astype(F32)).astype(BF16), (b - bh.astype(F32)).astype(BF16)
    d = functools.partial(jnp.dot, preferred_element_type=F32)
    return d(ah, bh) + (d(ah, bl) + d(al, bh))


@jax.custom_vjp
def _mm3(a, b):
    return _mm3_raw(a, b)


def _mm3_fwd(a, b):
    return _mm3_raw(a, b), (a, b)


def _mm3_bwd(res, g):
    a, b = res
    return _mm3_raw(g, b.T), _mm3_raw(a.T, g)


_mm3.defvjp(_mm3_fwd, _mm3_bwd)

_NN, _NT, _TN = "nn", "nt", "tn"
_DIMS = {"nn": (((1,), (0,)), ((), ())), "nt": (((1,), (1,)), ((), ())), "tn": (((0,), (0,)), ((), ()))}


def _bdot_raw(a, b, kind):
    return lax.dot_general(a.astype(BF16), b.astype(BF16), _DIMS[kind], preferred_element_type=F32)


@functools.partial(jax.custom_vjp, nondiff_argnums=(2,))
def _bdot(a, b, kind=_NN):
    return _bdot_raw(a, b, kind)


def _bdot_fwd(a, b, kind):
    return _bdot_raw(a, b, kind), (a, b)


def _bdot_bwd(kind, res, g):
    a, b = res
    if kind == "nn":
        return _bdot_raw(g, b, "nt"), _bdot_raw(a, g, "tn")
    if kind == "nt":
        return _bdot_raw(g, b, "nn"), _bdot_raw(g, a, "tn")
    return _bdot_raw(b, g, "nt"), _bdot_raw(a, g, "nn")


_bdot.defvjp(_bdot_fwd, _bdot_bwd)


def _chunk_masks(rev):
    C = DN_CHUNK
    ii = lax.broadcasted_iota(jnp.int32, (C, C), 0)
    jj = lax.broadcasted_iota(jnp.int32, (C, C), 1)
    diff = jnp.where(rev, jj - ii, ii - jj)
    incl = diff >= 0
    strict = diff > 0
    rowsel = (lax.broadcasted_iota(jnp.int32, (C, 1), 0) == jnp.where(rev, 0, C - 1)).astype(F32)
    return incl, strict, rowsel, (ii == jj).astype(F32)


def _chunk_gates(g, gt, be, sel, selt):
    g_col = jnp.sum(g * sel, axis=1, keepdims=True)
    g_row = jnp.sum(gt * selt, axis=0, keepdims=True)
    b_col = jnp.sum(be * sel, axis=1, keepdims=True)
    return g_col, g_row, b_col


def _chunk_lower(k, g, gt, be, sel, selt, incl, strict):
    g_col, g_row, b_col = _chunk_gates(g, gt, be, sel, selt)
    decay = jnp.where(incl, jnp.exp(jnp.where(incl, g_col - g_row, 0.0)), 0.0)
    return jnp.where(strict, _bdot(k * b_col, k, _NT) * decay, 0.0)


def _chunk_inverse(low, eye):
    m = -low
    x = eye + m
    p = m
    for _ in range(int(math.log2(DN_CHUNK)) - 1):
        p = _mm3(p, p)
        x = x + _mm3(x, p)
    return x


def _chunk_step(q, k, v, g, gt, be, S, X, sel, selt, incl, rowsel):
    g_col, g_row, b_col = _chunk_gates(g, gt, be, sel, selt)
    decay = jnp.where(incl, jnp.exp(jnp.where(incl, g_col - g_row, 0.0)), 0.0)
    eg = jnp.exp(g_col)
    u = _mm3(X, v * b_col)
    w = _mm3(X, k * (b_col * eg))
    intra = _bdot(q, k, _NT) * decay
    g_last = jnp.sum(g_col * rowsel, axis=0, keepdims=True)
    v_new = u - _bdot(w, S)
    o = _bdot(q * eg, S) + _bdot(intra, v_new)
    S_new = S * jnp.exp(g_last) + _bdot(k * jnp.exp(g_last - g_col), v_new, _TN)
    return o, S_new


def _scan_index(ctx_chunks, n_chunks):
    def idx(d, n):
        return jnp.where(d == 0, n, jnp.where(n < ctx_chunks, ctx_chunks - 1 - n, n_chunks + ctx_chunks - 1 - n))
    return idx


def _cumsum_mats(rev):
    C = DN_CHUNK
    ii = lax.broadcasted_iota(jnp.int32, (C, C), 0)
    jj = lax.broadcasted_iota(jnp.int32, (C, C), 1)
    return jnp.where(jnp.where(rev, jj - ii, ii - jj) >= 0, 1.0, 0.0).astype(F32)


def _gdn_scan(name, q, k, v, la, be, n_heads, ctx):
    T, Wd = q.shape
    C = DN_CHUNK
    nch = T // C
    cidx = _scan_index(ctx // C, nch)

    def body(q_ref, k_ref, v_ref, la_ref, be_ref, o_ref, s_ref, x_ref, state):
        d, n = pl.program_id(0), pl.program_id(1)
        rev = d == 1

        @pl.when(n == 0)
        def _():
            state[...] = jnp.zeros_like(state)

        incl, strict, rowsel, eye = _chunk_masks(rev)
        g = jnp.dot(_cumsum_mats(rev), la_ref[...], precision=lax.Precision.HIGHEST, preferred_element_type=F32)
        gt = g.T
        bev = be_ref[...]
        lane = lax.broadcasted_iota(jnp.int32, (1, HEAD), 1)
        sub = lax.broadcasted_iota(jnp.int32, (HEAD, 1), 0)
        hs = range(n_heads)
        cols = [slice(h * HEAD, (h + 1) * HEAD) for h in hs]
        sels = [(lane == d * n_heads + h).astype(F32) for h in hs]
        selts = [(sub == d * n_heads + h).astype(F32) for h in hs]
        qs, ks, vs = [q_ref[:, c] for c in cols], [k_ref[:, c] for c in cols], [v_ref[:, c] for c in cols]
        Ss = [state[h] for h in hs]
        ps = [-_chunk_lower(ks[h], g, gt, bev, sels[h], selts[h], incl, strict) for h in hs]
        Xs = [eye + p for p in ps]
        for _ in range(int(math.log2(C)) - 1):
            ps = [_mm3(p, p) for p in ps]
            Xs = [x + _mm3(x, p) for x, p in zip(Xs, ps)]
        outs = [_chunk_step(qs[h], ks[h], vs[h], g, gt, bev, Ss[h], Xs[h], sels[h], selts[h], incl, rowsel) for h in hs]
        for h in hs:
            s_ref[h] = Ss[h]
            x_ref[h] = Xs[h]
            o_ref[:, cols[h]] = outs[h][0]
            state[h] = outs[h][1]

    tok = lambda w: pl.BlockSpec((C, w), lambda d, n: (cidx(d, n), 0))
    return pl.pallas_call(
        body, name=name, grid=(2, nch), in_specs=[tok(Wd), tok(Wd), tok(Wd), tok(HEAD), tok(HEAD)],
        out_specs=[pl.BlockSpec((None, C, Wd), lambda d, n: (d, cidx(d, n), 0)),
                   pl.BlockSpec((None, None, n_heads, HEAD, HEAD), lambda d, n: (d, n, 0, 0, 0)),
                   pl.BlockSpec((None, None, n_heads, C, C), lambda d, n: (d, n, 0, 0, 0))],
        out_shape=[jax.ShapeDtypeStruct((2, T, Wd), F32), jax.ShapeDtypeStruct((2, nch, n_heads, HEAD, HEAD), F32),
                   jax.ShapeDtypeStruct((2, nch, n_heads, C, C), F32)],
        scratch_shapes=[pltpu.VMEM((n_heads, HEAD, HEAD), F32)],
        compiler_params=_cparams(("arbitrary", "arbitrary")), **_CALL_KW)(q, k, v, la, be)


def _gdn_scan_bwd(name, q, k, v, la, be, states, invs, do, n_heads, ctx):
    T, Wd = q.shape
    C = DN_CHUNK
    nch = T // C
    cidx = _scan_index(ctx // C, nch)

    def body(q_ref, k_ref, v_ref, la_ref, be_ref, s_ref, x_ref, do_ref, dq_ref, dk_ref, dv_ref, dla_ref, dbe_ref, dstate):
        d, n = pl.program_id(0), pl.program_id(1)
        rev = d == 1

        @pl.when(n == 0)
        def _():
            dstate[...] = jnp.zeros_like(dstate)

        incl, strict, rowsel, eye = _chunk_masks(rev)
        tri = _cumsum_mats(rev)
        g = jnp.dot(tri, la_ref[...], precision=lax.Precision.HIGHEST, preferred_element_type=F32)
        gt = g.T
        bev = be_ref[...]
        lane = lax.broadcasted_iota(jnp.int32, (1, HEAD), 1)
        sub = lax.broadcasted_iota(jnp.int32, (HEAD, 1), 0)
        dg = jnp.zeros((C, HEAD), F32)
        dgt = jnp.zeros((HEAD, C), F32)
        dbe = jnp.zeros((C, HEAD), F32)
        hs = range(n_heads)
        cols = [slice(h * HEAD, (h + 1) * HEAD) for h in hs]
        loaded = [(q_ref[:, c], k_ref[:, c], v_ref[:, c], do_ref[:, c]) for c in cols]
        SX = [(s_ref[h], x_ref[h], dstate[h]) for h in hs]
        results = []
        for h in hs:
            sel = (lane == d * n_heads + h).astype(F32)
            selt = (sub == d * n_heads + h).astype(F32)
            qv, kv, vv, dov = loaded[h]
            S, X, dS_in = SX[h]
            step = functools.partial(_chunk_step, sel=sel, selt=selt, incl=incl, rowsel=rowsel)
            _, vjp_step = jax.vjp(step, qv, kv, vv, g, gt, bev, S, X)
            dq, dk1, dv_, dg1, dgt1, dbe1, dS, dX = vjp_step((dov, dS_in))
            xt = X.T
            dlow = -_mm3(_mm3(xt, dX), xt)
            low_fn = functools.partial(_chunk_lower, sel=sel, selt=selt, incl=incl, strict=strict)
            _, vjp_low = jax.vjp(low_fn, kv, g, gt, bev)
            dk2, dg2, dgt2, dbe2 = vjp_low(dlow)
            results.append((dq, dk1 + dk2, dv_, dS))
            dg, dgt, dbe = dg + dg1 + dg2, dgt + dgt1 + dgt2, dbe + dbe1 + dbe2
        for h in hs:
            dq_ref[:, cols[h]], dk_ref[:, cols[h]], dv_ref[:, cols[h]] = results[h][0], results[h][1], results[h][2]
            dstate[h] = results[h][3]
        dg = dg + dgt.T
        dla_ref[...] = lax.dot_general(tri, dg, _DIMS["tn"], precision=lax.Precision.HIGHEST, preferred_element_type=F32)
        dbe_ref[...] = dbe

    rn = lambda d, n: cidx(d, nch - 1 - n)
    tok = lambda w: pl.BlockSpec((C, w), lambda d, n: (rn(d, n), 0))
    otok = lambda w: pl.BlockSpec((None, C, w), lambda d, n: (d, rn(d, n), 0))
    return pl.pallas_call(
        body, name=name, grid=(2, nch),
        in_specs=[tok(Wd), tok(Wd), tok(Wd), tok(HEAD), tok(HEAD),
                  pl.BlockSpec((None, None, n_heads, HEAD, HEAD), lambda d, n: (d, nch - 1 - n, 0, 0, 0)),
                  pl.BlockSpec((None, None, n_heads, C, C), lambda d, n: (d, nch - 1 - n, 0, 0, 0)), tok(Wd)],
        out_specs=[otok(Wd), otok(Wd), otok(Wd), otok(HEAD), otok(HEAD)],
        out_shape=[jax.ShapeDtypeStruct((2, T, Wd), F32)] * 3 + [jax.ShapeDtypeStruct((2, T, HEAD), F32)] * 2,
        scratch_shapes=[pltpu.VMEM((n_heads, HEAD, HEAD), F32)],
        compiler_params=_cparams(("arbitrary", "arbitrary")), **_CALL_KW)(q, k, v, la, be, states, invs, do)


def _gated_norm(name, o2, p, zblk, g, n_heads):
    _, T, Wd = o2.shape

    def body(o_ref, z_ref, g_ref, y_ref):
        for h in range(n_heads):
            cols = slice(h * HEAD, (h + 1) * HEAD)
            ov = o_ref[0, :, cols] + o_ref[1, :, cols]
            zv = z_ref[:, cols].astype(F32)
            y = ov * lax.rsqrt(jnp.mean(ov * ov, axis=-1, keepdims=True) + EPS) * g_ref[...]
            y_ref[:, cols] = (y * (zv * _sigmoid(zv))).astype(BF16)

    return pl.pallas_call(
        body, name=name, grid=(T // TOK,),
        in_specs=[pl.BlockSpec((2, TOK, Wd), lambda i: (0, i, 0)), pl.BlockSpec((TOK, Wd), lambda i: (i, zblk)),
                  pl.BlockSpec((1, HEAD), lambda i: (0, 0))],
        out_specs=pl.BlockSpec((TOK, Wd), lambda i: (i, 0)), out_shape=jax.ShapeDtypeStruct((T, Wd), BF16),
        compiler_params=_cparams(("parallel",)), **_CALL_KW)(o2, p, g)


def _gated_norm_bwd(name, o2, p, zblk, g, n_heads, dmix, dblk):
    _, T, Wd = o2.shape

    def body(o_ref, z_ref, g_ref, dy_ref, do_ref, dz_ref, dg_ref):
        i = pl.program_id(0)

        @pl.when(i == 0)
        def _():
            dg_ref[...] = jnp.zeros_like(dg_ref)

        dg = jnp.zeros((1, HEAD), F32)
        for h in range(n_heads):
            cols = slice(h * HEAD, (h + 1) * HEAD)
            ov = o_ref[0, :, cols] + o_ref[1, :, cols]
            zv = z_ref[:, cols].astype(F32)
            dy = dy_ref[:, cols].astype(F32)
            r = lax.rsqrt(jnp.mean(ov * ov, axis=-1, keepdims=True) + EPS)
            on = ov * r
            sg = _sigmoid(zv)
            sz = zv * sg
            dz_ref[:, cols] = (dy * (on * g_ref[...]) * (sg * (1.0 + zv * (1.0 - sg)))).astype(BF16)
            dyn = dy * sz
            dg = dg + jnp.sum(dyn * on, axis=0, keepdims=True)
            u = dyn * g_ref[...]
            do_ref[:, cols] = r * (u - on * jnp.mean(u * on, axis=-1, keepdims=True))
        dg_ref[...] += dg

    row = pl.BlockSpec((TOK, Wd), lambda i: (i, 0))
    one = pl.BlockSpec((1, HEAD), lambda i: (0, 0))
    return pl.pallas_call(
        body, name=name, grid=(T // TOK,),
        in_specs=[pl.BlockSpec((2, TOK, Wd), lambda i: (0, i, 0)), pl.BlockSpec((TOK, Wd), lambda i: (i, zblk)), one,
                  pl.BlockSpec((TOK, Wd), lambda i: (i, dblk))],
        out_specs=[row, row, one],
        out_shape=[jax.ShapeDtypeStruct((T, Wd), F32), jax.ShapeDtypeStruct((T, Wd), BF16), jax.ShapeDtypeStruct((1, HEAD), F32)],
        compiler_params=_cparams(("arbitrary",)), **_CALL_KW)(o2, p, g, dmix)


class _Dims:
    def __init__(self, D, seq, ctx, ffn):
        self.D, self.seq, self.ctx, self.ffn = D, seq, ctx, ffn
        self.T = seq + ctx
        self.t_pad = -(-(self.T + 128) // TOK) * TOK
        self.ctx_tiles = ctx // TOK
        nh = D // HEAD
        self.swa_h, self.kv_h, self.dn_h = nh // 4, nh // 8, nh // 2
        self.na_h = nh - self.swa_h - self.dn_h
        self.swa_q, self.swa_kv, self.Wd, self.na = self.swa_h * HEAD, self.kv_h * HEAD, self.dn_h * HEAD, self.na_h * HEAD
        self.n_ab = 4 * self.dn_h
        self.o_ab = self.swa_q + 2 * self.swa_kv + 4 * self.Wd
        self.n_in = self.o_ab + self.n_ab + 3 * self.na
        self.n_main = self.n_in - self.n_ab
        assert ctx % TOK == 0 and seq % TOK == 0 and self.swa_q == 2 * self.swa_kv == self.na and 2 * self.na == self.Wd


def _rope_tables(dm):
    t = jnp.arange(dm.t_pad, dtype=jnp.int32) - dm.ctx
    lat = (t >= 0) & (t < dm.seq)
    row = (t // GRID_W).astype(F32)
    col = (t % GRID_W).astype(F32)
    n_freq = HEAD // 4
    inv = ROPE_THETA ** (-jnp.arange(n_freq, dtype=F32) / n_freq)
    ang = jnp.concatenate([row[:, None] * inv, row[:, None] * inv, col[:, None] * inv, col[:, None] * inv], axis=-1)
    ang = jnp.where(lat[:, None], ang, 0.0)
    return jnp.cos(ang), jnp.sin(ang)


def _bias_indices():
    o = np.arange(NA_KH)[:, None]
    jr = np.arange(NA_KH)[None, :]
    idx_r = jr - o + (NA_KH - 1)
    cols = np.arange(GRID_W)
    idx_c = np.clip(cols[None, :] - cols[:, None], -(NA_KW - 1), NA_KW - 1) + (NA_KW - 1)
    return idx_r, idx_c


def _bias_onehots():
    idx_r, idx_c = _bias_indices()
    sel_r = (idx_r.reshape(-1)[:, None] == np.arange(2 * NA_KH)[None, :]).astype(np.float32)
    sel_c = (np.arange(HEAD)[:, None] == idx_c.reshape(-1)[None, :]).astype(np.float32)
    return jnp.asarray(sel_r), jnp.asarray(sel_c)


def _rpb_pad(rpb):
    return jnp.pad(rpb, ((0, 0), (0, 2 * NA_KH - rpb.shape[1]), (0, HEAD - rpb.shape[2])))


def _bias_table(name, rpb):
    H = rpb.shape[0]
    sel_r, sel_c = _bias_onehots()
    hi = lax.Precision.HIGHEST

    def body(r_ref, sr_ref, sc_ref, o_ref):
        t = jnp.dot(sr_ref[...], r_ref[...], precision=hi, preferred_element_type=F32)
        o_ref[...] = jnp.dot(t, sc_ref[...], precision=hi, preferred_element_type=F32)

    n_r, n_c = sel_r.shape[0], sel_c.shape[1]
    tab = pl.pallas_call(
        body, name=name, grid=(H,),
        in_specs=[pl.BlockSpec((None, 2 * NA_KH, HEAD), lambda h: (h, 0, 0)), pl.BlockSpec(sel_r.shape, lambda h: (0, 0)),
                  pl.BlockSpec(sel_c.shape, lambda h: (0, 0))],
        out_specs=pl.BlockSpec((None, n_r, n_c), lambda h: (h, 0, 0)), out_shape=jax.ShapeDtypeStruct((H, n_r, n_c), F32),
        compiler_params=_cparams(("parallel",)), **_CALL_KW)(_rpb_pad(rpb), sel_r, sel_c)
    tab = tab.reshape(H, NA_KH, NA_KH, GRID_W, GRID_W).transpose(0, 1, 3, 2, 4)
    return tab.reshape(H, NA_KH, GRID_W, NA_KH * GRID_W)


def _bias_table_bwd(name, dbias, rpb_shape):
    H = dbias.shape[0]
    sel_r, sel_c = _bias_onehots()
    hi = lax.Precision.HIGHEST
    d = dbias.reshape(H, NA_KH, GRID_W, NA_KH, GRID_W).transpose(0, 1, 3, 2, 4).reshape(H, NA_KH * NA_KH, GRID_W * GRID_W)

    def body(d_ref, sr_ref, sc_ref, o_ref):
        dt = lax.dot_general(d_ref[...], sc_ref[...], _DIMS["nt"], precision=hi, preferred_element_type=F32)
        o_ref[...] = lax.dot_general(sr_ref[...], dt, _DIMS["tn"], precision=hi, preferred_element_type=F32)

    out = pl.pallas_call(
        body, name=name, grid=(H,),
        in_specs=[pl.BlockSpec((None,) + d.shape[1:], lambda h: (h, 0, 0)), pl.BlockSpec(sel_r.shape, lambda h: (0, 0)),
                  pl.BlockSpec(sel_c.shape, lambda h: (0, 0))],
        out_specs=pl.BlockSpec((None, 2 * NA_KH, HEAD), lambda h: (h, 0, 0)),
        out_shape=jax.ShapeDtypeStruct((H, 2 * NA_KH, HEAD), F32),
        compiler_params=_cparams(("parallel",)), **_CALL_KW)(d, sel_r, sel_c)
    return out[:, :rpb_shape[1], :rpb_shape[2]]


def _lane_row(v):
    v = v.reshape(-1)
    return jnp.pad(v, (0, HEAD - v.shape[0])).reshape(1, HEAD)


def _layer_weights(dm, l, chip, gathered, own):
    g_in, g_out, g_gate, g_up, g_down = [
        [jnp.where(chip == k, o[l], g[k, l]) for k in range(N_CHIPS)] for g, o in zip(gathered, own)]
    g_out, g_down = jnp.stack(g_out), jnp.stack(g_down)
    w_in = jnp.concatenate(g_in, axis=1)
    w_main = jnp.concatenate([w_in[:, :dm.o_ab], w_in[:, dm.o_ab + dm.n_ab:]], axis=1)
    w_ab = jnp.pad(w_in[:, dm.o_ab:dm.o_ab + dm.n_ab], ((0, 0), (0, HEAD - dm.n_ab)))
    w_out = g_out.reshape(dm.D, dm.D)
    w_out = jnp.concatenate([w_out[dm.swa_q:dm.swa_q + dm.Wd], w_out[:dm.swa_q], w_out[dm.swa_q + dm.Wd:]], axis=0)
    w_gu = jnp.concatenate(g_gate + g_up, axis=1)
    w_down = g_down.reshape(dm.ffn, dm.D)
    return dict(main=w_main, ab=w_ab, out=w_out, gu=w_gu, down=w_down)


def _layer_fwd(dm, x, W, sp, modv, cos, sin):
    ct = dm.ctx_tiles
    h = _norm_mod("norm1", x, sp["norm1_g"], modv, 0, ct)
    P = _matmul("in_proj", h, W["main"], "nn")
    Pab = _matmul("in_proj_ab", h, W["ab"], "nn", tn=HEAD)
    one = jnp.ones((1, HEAD), F32)
    qa = _head_prep("swa_q_prep", P, 0, dm.swa_h, sp["swa_q_g"], cos, sin, dm.t_pad, True, True)
    ka = _head_prep("swa_k_prep", P, 2, dm.kv_h, sp["swa_k_g"], cos, sin, dm.t_pad, True, True)
    va = _head_prep("swa_v_prep", P, 3, dm.kv_h, one, cos, sin, dm.t_pad, False, False)
    oa = _attn_fwd("swa_fwd", "swa", qa, ka, va, sp["swa_sink"], None, dm.ctx, dm.seq)
    qn = _head_prep("na_q_prep", P, 10, dm.na_h, sp["na_q_g"], cos, sin, dm.t_pad, True, False)
    kn = _head_prep("na_k_prep", P, 11, dm.na_h, sp["na_k_g"], cos, sin, dm.t_pad, True, False)
    vn = _head_prep("na_v_prep", P, 12, dm.na_h, one, cos, sin, dm.t_pad, False, False)
    no_sink = jnp.full((dm.na_h,), NEG, F32)
    bias = _bias_table("na_bias", sp["na_rpb"])
    oc = _attn_fwd("na_fwd", "na", qn, kn, vn, no_sink, bias, dm.ctx, dm.seq)
    pre = _conv_fwd("dn_conv", P, 1, sp["dn_conv_w"], ct)
    a_row, dt_row = _lane_row(sp["dn_A_log"]), _lane_row(sp["dn_dt_bias"])
    qh, kh, vh, la, be = _gdn_point("dn_point", pre, Pab, a_row, dt_row, dm.dn_h)
    o2, states, invs = _gdn_scan("dn_scan", qh, kh, vh, la, be, dm.dn_h, dm.ctx)
    ob = _gated_norm("dn_out_norm", o2, P, 4, sp["dn_out_g"], dm.dn_h)
    mix = jnp.concatenate([ob, oa.astype(BF16), oc.astype(BF16)], axis=-1)
    ao = _matmul("out_proj", mix, W["out"], "nn")
    x1 = _resid_gate("resid1", x, ao, modv, 2, ct)
    h2 = _norm_mod("norm2", x1, sp["norm2_g"], modv, 3, ct)
    gu = _matmul("ffn_gate_up", h2, W["gu"], "nn")
    act = _swiglu("ffn_act", gu)
    fo = _matmul("ffn_down", act, W["down"], "nn")
    x2 = _resid_gate("resid2", x1, fo, modv, 5, ct)
    res = dict(x=x, h=h, P=P, Pab=Pab, qa=qa, ka=ka, va=va, qn=qn, kn=kn, vn=vn, bias=bias, no_sink=no_sink, pre=pre,
               a_row=a_row, dt_row=dt_row, qh=qh, kh=kh, vh=vh, la=la, be=be, o2=o2, states=states, invs=invs, mix=mix,
               ao=ao, x1=x1, h2=h2, gu=gu, act=act, fo=fo)
    return x2, res


def _layer_bwd(dm, dx2, W, sp, modv, cos, sin, r):
    ct = dm.ctx_tiles
    T, D = dm.T, dm.D
    one = jnp.ones((1, HEAD), F32)
    dfo, dgate2 = _resid_gate_bwd("resid2_bwd", dx2, r["fo"], modv, 5, ct)
    dact = _matmul("ffn_down_dx", dfo, W["down"], "nt")
    dw_down = _matmul("ffn_down_dw", r["act"], dfo, "tn")
    dgu = _swiglu_bwd("ffn_act_bwd", r["gu"], dact)
    dh2 = _matmul("ffn_gate_up_dx", dgu, W["gu"], "nt")
    dw_gu = _matmul("ffn_gate_up_dw", r["h2"], dgu, "tn")
    zero = jnp.zeros((T, D), F32)
    dx1, dn2g, dsh2, dsc2 = _norm_mod_bwd("norm2_bwd", r["x1"], sp["norm2_g"], modv, 3, dh2, zero, dx2, ct)
    dao, dgate1 = _resid_gate_bwd("resid1_bwd", dx1, r["ao"], modv, 2, ct)
    dmix = _matmul("out_proj_dx", dao, W["out"], "nt")
    dw_out = _matmul("out_proj_dw", r["mix"], dao, "tn")
    do_, dz, d_out_g = _gated_norm_bwd("dn_out_norm_bwd", r["o2"], r["P"], 4, sp["dn_out_g"], dm.dn_h, dmix, 0)
    dq2, dk2, dv2, dla2, dbe2 = _gdn_scan_bwd("dn_scan_bwd", r["qh"], r["kh"], r["vh"], r["la"], r["be"], r["states"], r["invs"],
                                              do_, dm.dn_h, dm.ctx)
    dpre, dPab, d_alog, d_dtb = _gdn_point_bwd("dn_point_bwd", r["pre"], r["Pab"], r["a_row"], r["dt_row"], dm.dn_h,
                                               dq2, dk2, dv2, dla2, dbe2)
    dqkv, d_conv = _conv_bwd("dn_conv_bwd", r["P"], 1, sp["dn_conv_w"], dpre, ct)
    dqa, dka, dva, dsink, _ = _attn_bwd("swa_bwd", "swa", r["qa"], r["ka"], r["va"], sp["swa_sink"], None, dmix,
                                        dm.Wd // HEAD, dm.ctx, dm.seq)
    daq, d_swa_q_g = _head_prep_bwd("swa_q_prep_bwd", r["P"], 0, dm.swa_h, sp["swa_q_g"], cos, sin, dqa, True)
    dak, d_swa_k_g = _head_prep_bwd("swa_k_prep_bwd", r["P"], 2, dm.kv_h, sp["swa_k_g"], cos, sin, dka, True)
    dqn, dkn, dvn, _, dbias = _attn_bwd("na_bwd", "na", r["qn"], r["kn"], r["vn"], r["no_sink"], r["bias"], dmix,
                                        (dm.Wd + dm.swa_q) // HEAD, dm.ctx, dm.seq)
    dnq, d_na_q_g = _head_prep_bwd("na_q_prep_bwd", r["P"], 10, dm.na_h, sp["na_q_g"], cos, sin, dqn, False)
    dnk, d_na_k_g = _head_prep_bwd("na_k_prep_bwd", r["P"], 11, dm.na_h, sp["na_k_g"], cos, sin, dkn, False)
    d_rpb = _bias_table_bwd("na_bias_bwd", dbias, sp["na_rpb"].shape)
    dP = jnp.concatenate([daq, dak, dva[:T].astype(BF16), dqkv, dz, dnq, dnk, dvn[:T].astype(BF16)], axis=-1)
    dw_main = _matmul("in_proj_dw", r["h"], dP, "tn")
    dw_ab = _matmul("in_proj_ab_dw", r["h"], dPab, "tn", tn=HEAD)
    dh = _matmul("in_proj_dx", dP, W["main"], "nt")
    dh_b = _matmul("in_proj_ab_dx", dPab, W["ab"], "nt")
    dx, dn1g, dsh1, dsc1 = _norm_mod_bwd("norm1_bwd", r["x"], sp["norm1_g"], modv, 0, dh, dh_b, dx1, ct)
    dmodv = jnp.concatenate([dsh1, dsc1, dgate1, dsh2, dsc2, dgate2], axis=1)
    big = dict(main=dw_main, ab=dw_ab, out=dw_out, gu=dw_gu, down=dw_down)
    small = dict(norm1_g=dn1g[0], norm2_g=dn2g[0], swa_q_g=d_swa_q_g[0], swa_k_g=d_swa_k_g[0], swa_sink=dsink[:, 0, 0],
                 dn_conv_w=d_conv[:DN_CONV], dn_A_log=d_alog[0, :2 * dm.dn_h].reshape(2, dm.dn_h),
                 dn_dt_bias=d_dtb[0, :2 * dm.dn_h].reshape(2, dm.dn_h), dn_out_g=d_out_g[0], na_q_g=d_na_q_g[0],
                 na_k_g=d_na_k_g[0], na_rpb=d_rpb)
    return dx, big, small, dmodv


def _grad_chunks(dm, bigs):
    def cols(w):
        return w.reshape(w.shape[0], N_CHIPS, -1).transpose(1, 0, 2)

    def rows(w):
        return w.reshape(N_CHIPS, -1, w.shape[1])

    g_in, g_out, g_gate, g_up, g_down = [], [], [], [], []
    for b in bigs:
        m = b["main"]
        g_in.append(cols(jnp.concatenate([m[:, :dm.o_ab], b["ab"][:, :dm.n_ab], m[:, dm.o_ab:]], axis=1)))
        o = b["out"]
        g_out.append(rows(jnp.concatenate([o[dm.Wd:dm.Wd + dm.swa_q], o[:dm.Wd], o[dm.Wd + dm.swa_q:]], axis=0)))
        g_gate.append(cols(b["gu"][:, :dm.ffn]))
        g_up.append(cols(b["gu"][:, dm.ffn:]))
        g_down.append(rows(b["down"]))
    return [jnp.stack(g, axis=1).astype(BF16) for g in (g_in, g_out, g_gate, g_up, g_down)]


SMALL = ("norm1_g", "norm2_g", "swa_q_g", "swa_k_g", "swa_sink", "dn_conv_w", "dn_A_log", "dn_dt_bias", "dn_out_g",
         "na_q_g", "na_k_g", "na_rpb")


def _pack(arrs):
    flat = jnp.concatenate([a.reshape(-1).astype(F32) for a in arrs])
    n = flat.shape[0]
    rows = -(-n // (8 * HEAD)) * 8
    return jnp.pad(flat, (0, rows * HEAD - n)).reshape(rows, HEAD)


def _unpack(packed, like):
    flat = packed.reshape(-1)
    out, o = [], 0
    for a in like:
        out.append(flat[o:o + a.size].reshape(a.shape))
        o += a.size
    return out


def _sum_devices(name, g, which):
    _, R, _ = g.shape

    def body(g_ref, o_ref):
        acc = g_ref[which[0]]
        for b in which[1:]:
            acc = acc + g_ref[b]
        o_ref[...] = acc

    return pl.pallas_call(
        body, name=name, grid=(R // 8,), in_specs=[pl.BlockSpec((N_DEV, 8, HEAD), lambda i: (0, i, 0))],
        out_specs=pl.BlockSpec((8, HEAD), lambda i: (i, 0)), out_shape=jax.ShapeDtypeStruct((R, HEAD), F32),
        compiler_params=_cparams(("parallel",)), **_CALL_KW)(g)


def _silu_rows(name, c_rows):
    return _elementwise(name, lambda c: (c * _sigmoid(c),), [c_rows], [BF16])[0]


def _ada_cotangent(name, dm_all, b_ada_shape):
    _, L, _, N6 = dm_all.shape
    tn = _pick(N6, (1024, 512, 256, 128))

    def body(d_ref, o_ref, b_ref):
        csum = d_ref[0, 0:1, :]
        for b in range(1, N_DEV):
            csum = csum + d_ref[b, 0:1, :]
        tot = csum
        for b in range(N_DEV):
            o_ref[b:b + 1, :] = d_ref[b, 1:2, :]
            tot = tot + d_ref[b, 1:2, :]
        first = lax.broadcasted_iota(jnp.int32, (8, tn), 0) == 0
        o_ref[N_DEV:, :] = jnp.where(first, jnp.broadcast_to(csum, (8, tn)), 0.0)
        b_ref[...] = jnp.broadcast_to(tot, (8, tn))

    return pl.pallas_call(
        body, name=name, grid=(L, N6 // tn), in_specs=[pl.BlockSpec((N_DEV, None, 2, tn), lambda l, j: (0, l, 0, j))],
        out_specs=[pl.BlockSpec((None, 16, tn), lambda l, j: (l, 0, j)), pl.BlockSpec((None, 8, tn), lambda l, j: (l, 0, j))],
        out_shape=[jax.ShapeDtypeStruct((L, 16, N6), F32), jax.ShapeDtypeStruct((L, 8, N6), F32)],
        compiler_params=_cparams(("parallel", "parallel")), **_CALL_KW)(dm_all)


def kernel(x, c, ctx, c_ctx, w_ada, b_ada, norm1_g, norm2_g, w_in, swa_q_g, swa_k_g, swa_sink, dn_conv_w, dn_A_log, dn_dt_bias, dn_out_g, na_q_g, na_k_g, na_rpb, w_out, w_gate, w_up, w_down, loss_target, m_c_ctx, m_w_ada, m_b_ada, m_norm1_g, m_norm2_g, m_w_in, m_swa_q_g, m_swa_k_g, m_swa_sink, m_dn_conv_w, m_dn_A_log, m_dn_dt_bias, m_dn_out_g, m_na_q_g, m_na_k_g, m_na_rpb, m_w_out, m_w_gate, m_w_up, m_w_down, v_c_ctx, v_w_ada, v_b_ada, v_norm1_g, v_norm2_g, v_w_in, v_swa_q_g, v_swa_k_g, v_swa_sink, v_dn_conv_w, v_dn_A_log, v_dn_dt_bias, v_dn_out_g, v_na_q_g, v_na_k_g, v_na_rpb, v_w_out, v_w_gate, v_w_up, v_w_down):
    L = w_in.shape[0]
    D, seq, n_ctx = x.shape[-1], x.shape[1], ctx.shape[1]
    dm = _Dims(D, seq, n_ctx, w_gate.shape[-1] * N_CHIPS)
    xi, yi, ci = _axes()
    chip = 2 * xi + yi
    dev = 4 * xi + 2 * yi + ci
    n6 = 6 * D
    n6s = n6 // N_CHIPS

    shards = [_elementwise(f"cast_{n}", lambda w: (w,), [w], [BF16])[0]
              for n, w in (("w_in", w_in), ("w_out", w_out), ("w_gate", w_gate), ("w_up", w_up), ("w_down", w_down))]
    gathered = _gather_weights(shards)
    Ws = [_layer_weights(dm, l, chip, gathered, shards) for l in range(L)]
    conv_all = _allgather_small("gather_conv_w", _pack([dn_conv_w]))
    conv_full = jnp.concatenate([_unpack(conv_all[2 * k], [dn_conv_w])[0] for k in range(N_CHIPS)], axis=-1)

    c_all = _allgather_small("gather_c", _pack([c]))
    c_rows = jnp.concatenate([c_all[:, :D // HEAD].reshape(N_DEV, D), c_ctx[None], jnp.zeros((16 - N_DEV - 1, D), F32)], axis=0)
    a_rows = _silu_rows("ada_silu", c_rows)
    b_sh = lax.dynamic_slice_in_dim(b_ada, chip * n6s, n6s, axis=1)
    mod_sh = [_matmul(f"ada_mod{l}", a_rows, w_ada[l], "nn", tm=16) for l in range(L)]
    mod_all = _allgather_small("gather_mod", _pack(mod_sh))
    mods = []
    for l in range(L):
        per_chip = [_unpack(mod_all[2 * k], mod_sh)[l] for k in range(N_CHIPS)]
        mods.append(jnp.concatenate(per_chip, axis=1))
    modvs = []
    for l in range(L):
        rows = jnp.stack([mods[l][N_DEV], lax.dynamic_index_in_dim(mods[l], dev, 0, keepdims=False)])
        modvs.append(_elementwise(f"ada_bias{l}", lambda m, b: (m + b,), [rows, jnp.broadcast_to(b_ada[l][None], (2, n6))], [F32])[0]
                     .reshape(2, 6, D))

    cos, sin = _rope_tables(dm)
    sps = [dict(norm1_g=norm1_g[l][None], norm2_g=norm2_g[l][None], swa_q_g=swa_q_g[l][None], swa_k_g=swa_k_g[l][None],
                swa_sink=swa_sink[l], dn_conv_w=conv_full[l], dn_A_log=dn_A_log[l], dn_dt_bias=dn_dt_bias[l],
                dn_out_g=dn_out_g[l][None], na_q_g=na_q_g[l][None], na_k_g=na_k_g[l][None], na_rpb=na_rpb[l]) for l in range(L)]
    xs = jnp.concatenate([ctx[0], x[0]], axis=0)
    ress = []
    for l in range(L):
        xs, r = _layer_fwd(dm, xs, Ws[l], sps[l], modvs[l], cos, sin)
        ress.append(r)
    loss_blk, dxs = _loss_and_grad("loss", xs, loss_target[0], dm.ctx_tiles)
    loss = lax.psum(loss_blk[0, 0], ("x", "y", "c"))

    bigs, smalls, dmodvs = [None] * L, [None] * L, [None] * L
    for l in reversed(range(L)):
        dxs, bigs[l], smalls[l], dmodvs[l] = _layer_bwd(dm, dxs, Ws[l], sps[l], modvs[l], cos, sin, ress[l])
    grad_x = dxs[n_ctx:][None]

    dm_mine = jnp.stack([d.reshape(2, n6) for d in dmodvs])
    dm_all = _allgather_small("gather_dmod", _pack([dm_mine]))
    dm_all = jnp.stack([_unpack(dm_all[b], [dm_mine])[0] for b in range(N_DEV)])
    dm_rows, d_b_ada = _ada_cotangent("ada_cot", dm_all, b_ada.shape)
    dm_sh = lax.dynamic_slice_in_dim(dm_rows, chip * n6s, n6s, axis=2).astype(BF16)
    g_w_ada = jnp.stack([_matmul(f"ada_dw{l}", a_rows, dm_sh[l], "tn") for l in range(L)])
    dc_part = [_matmul(f"ada_dc{l}", dm_sh[l], w_ada[l], "nt", tm=16) for l in range(L)]
    dc_mine = dc_part[0][N_DEV]
    for l in range(1, L):
        dc_mine = dc_mine + dc_part[l][N_DEV]

    small_list = [jnp.stack([smalls[l][n] for l in range(L)]) for n in SMALL]
    sm_all = _allgather_small("gather_small", _pack(small_list + [dc_mine]))
    sm_sum = _sum_devices("sum_small", sm_all, tuple(range(N_DEV)))
    dc_sum = _sum_devices("sum_dc", sm_all, tuple(range(0, N_DEV, 2)))
    g_small = dict(zip(SMALL, _unpack(sm_sum, small_list)))
    dcs = _unpack(dc_sum, small_list + [dc_mine])[-1]
    def silu_bwd(d, cc):
        s = _sigmoid(cc)
        return (d * (s * (1.0 + cc * (1.0 - s))),)

    g_c_ctx = _elementwise("c_ctx_silu_bwd", silu_bwd, [dcs.reshape(-1, HEAD), c_ctx.reshape(-1, HEAD)], [F32])[0]
    wd3 = dn_conv_w.shape[-1]
    g_small["dn_conv_w"] = lax.dynamic_slice_in_dim(g_small["dn_conv_w"], chip * wd3, wd3, axis=2)

    g_in, g_out, g_gate, g_up, g_down = _reduce_grads(_grad_chunks(dm, bigs))
    grads = dict(g_small, c_ctx=g_c_ctx, w_ada=g_w_ada, b_ada=d_b_ada[:, 0], w_in=g_in, w_out=g_out, w_gate=g_gate, w_up=g_up,
                 w_down=g_down)
    weights = dict(c_ctx=c_ctx, w_ada=w_ada, b_ada=b_ada, norm1_g=norm1_g, norm2_g=norm2_g, w_in=w_in, swa_q_g=swa_q_g,
                   swa_k_g=swa_k_g, swa_sink=swa_sink, dn_conv_w=dn_conv_w, dn_A_log=dn_A_log, dn_dt_bias=dn_dt_bias,
                   dn_out_g=dn_out_g, na_q_g=na_q_g, na_k_g=na_k_g, na_rpb=na_rpb, w_out=w_out, w_gate=w_gate, w_up=w_up,
                   w_down=w_down)
    ms = dict(c_ctx=m_c_ctx, w_ada=m_w_ada, b_ada=m_b_ada, norm1_g=m_norm1_g, norm2_g=m_norm2_g, w_in=m_w_in, swa_q_g=m_swa_q_g,
              swa_k_g=m_swa_k_g, swa_sink=m_swa_sink, dn_conv_w=m_dn_conv_w, dn_A_log=m_dn_A_log, dn_dt_bias=m_dn_dt_bias,
              dn_out_g=m_dn_out_g, na_q_g=m_na_q_g, na_k_g=m_na_k_g, na_rpb=m_na_rpb, w_out=m_w_out, w_gate=m_w_gate, w_up=m_w_up,
              w_down=m_w_down)
    vs = dict(c_ctx=v_c_ctx, w_ada=v_w_ada, b_ada=v_b_ada, norm1_g=v_norm1_g, norm2_g=v_norm2_g, w_in=v_w_in, swa_q_g=v_swa_q_g,
              swa_k_g=v_swa_k_g, swa_sink=v_swa_sink, dn_conv_w=v_dn_conv_w, dn_A_log=v_dn_A_log, dn_dt_bias=v_dn_dt_bias,
              dn_out_g=v_dn_out_g, na_q_g=v_na_q_g, na_k_g=v_na_k_g, na_rpb=v_na_rpb, w_out=v_w_out, w_gate=v_w_gate, w_up=v_w_up,
              w_down=v_w_down)
    order = ("c_ctx", "w_ada", "b_ada", "norm1_g", "norm2_g", "w_in", "swa_q_g", "swa_k_g", "swa_sink", "dn_conv_w", "dn_A_log",
             "dn_dt_bias", "dn_out_g", "na_q_g", "na_k_g", "na_rpb", "w_out", "w_gate", "w_up", "w_down")
    big_names = ("w_ada", "w_in", "w_out", "w_gate", "w_up", "w_down")
    grads = {n: grads[n].reshape(weights[n].shape) for n in order}
    delta, new_m, new_v = {}, {}, {}
    for n in big_names:
        delta[n], new_m[n], new_v[n] = _adamw(f"adamw_{n}", weights[n], grads[n], ms[n], vs[n])
    small_names = [n for n in order if n not in big_names]
    packed = [_pack([d[n] for n in small_names]) for d in (weights, grads, ms, vs)]
    outs = _adamw("adamw_small", *packed)
    like = [weights[n] for n in small_names]
    for d, o in zip((delta, new_m, new_v), outs):
        d.update(dict(zip(small_names, _unpack(o, like))))
    return (loss, grad_x, *[grads[n] for n in order], *[delta[n] for n in order], *[new_m[n] for n in order],
            *[new_v[n] for n in order])
```

```python
import functools
import math

import jax
import jax.numpy as jnp
import numpy as np
from jax import lax
from jax.experimental import pallas as pl
from jax.experimental.pallas import tpu as pltpu

F32, BF16 = jnp.float32, jnp.bfloat16
MESH = pl.DeviceIdType.MESH

GRID_W = 64
HEAD = 128
SWA_WINDOW = 128
DN_CONV = 5
DN_CHUNK = 64
NA_KH, NA_KW = 8, 16
ROPE_THETA = 10000.0
EPS = 1e-6
ADAM_LR, ADAM_B1, ADAM_B2, ADAM_EPS, ADAM_WD, ADAM_STEP = 0.001, 0.9, 0.999, 1e-08, 0.01, 10
N_CHIPS = 4
N_DEV = 8
TOK = 256
VMEM_LIMIT = 56 * 2 ** 20

_CALL_KW = {}


def _cparams(sem=None, **kw):
    if sem is not None:
        kw["dimension_semantics"] = sem
    return pltpu.CompilerParams(vmem_limit_bytes=VMEM_LIMIT, **kw)


def _pick(n, cands):
    for cnd in cands:
        if n % cnd == 0:
            return cnd
    raise ValueError(f"no tile for {n} in {cands}")


def _axes():
    return lax.axis_index("x"), lax.axis_index("y"), lax.axis_index("c")


def _matmul(name, a, b, kind, out_dtype=F32, tm=None, tn=None, tk=None):
    if kind == "nn":
        (M, K), (K2, N) = a.shape, b.shape
    elif kind == "nt":
        (M, K), (N, K2) = a.shape, b.shape
    else:
        (K, M), (K2, N) = a.shape, b.shape
    assert K == K2, (name, a.shape, b.shape)
    tm = tm or _pick(M, (1024, 512, 256, 128) if kind == "tn" else (1088, 1024, 704, 512, 256, 128, 64, 32, 16, 8))
    tn = tn or _pick(N, (512, 256, 128))
    tk = tk or (K if K <= 4352 else _pick(K, (3328, 2816, 2048, 1024, 512)))
    nk = K // tk
    dims = {"nn": (((1,), (0,)), ((), ())), "nt": (((1,), (1,)), ((), ())), "tn": (((0,), (0,)), ((), ()))}[kind]

    def body(a_ref, b_ref, o_ref, *scr):
        part = lax.dot_general(a_ref[...].astype(BF16), b_ref[...].astype(BF16), dims, preferred_element_type=F32)
        if nk == 1:
            o_ref[...] = part.astype(o_ref.dtype)
        else:
            acc = scr[0]
            k = pl.program_id(2)

            @pl.when(k == 0)
            def _():
                acc[...] = part

            @pl.when(k > 0)
            def _():
                acc[...] += part

            @pl.when(k == nk - 1)
            def _():
                o_ref[...] = acc[...].astype(o_ref.dtype)

    a_spec = {"nn": pl.BlockSpec((tm, tk), lambda i, j, k: (i, k)), "nt": pl.BlockSpec((tm, tk), lambda i, j, k: (i, k)),
              "tn": pl.BlockSpec((tk, tm), lambda i, j, k: (k, i))}[kind]
    b_spec = {"nn": pl.BlockSpec((tk, tn), lambda i, j, k: (k, j)), "nt": pl.BlockSpec((tn, tk), lambda i, j, k: (j, k)),
              "tn": pl.BlockSpec((tk, tn), lambda i, j, k: (k, j))}[kind]
    return pl.pallas_call(
        body, name=name, grid=(M // tm, N // tn, nk), in_specs=[a_spec, b_spec],
        out_specs=pl.BlockSpec((tm, tn), lambda i, j, k: (i, j)), out_shape=jax.ShapeDtypeStruct((M, N), out_dtype),
        scratch_shapes=[pltpu.VMEM((tm, tn), F32)] if nk > 1 else [],
        compiler_params=_cparams(("parallel", "parallel", "arbitrary")), **_CALL_KW)(a, b)


class _Comm:
    def __init__(self, ins, out_shapes, plan, n_local, n_remote, aliases=None):
        self.ins, self.out_shapes, self.plan = list(ins), list(out_shapes), plan
        self.n_local, self.n_remote, self.aliases = n_local, n_remote, aliases or {}

    def sem_shapes(self):
        return [pltpu.SemaphoreType.DMA((max(self.n_remote, 1),)), pltpu.SemaphoreType.DMA((max(self.n_remote, 1),)),
                pltpu.SemaphoreType.DMA((max(self.n_local, 1),))]

    def out_sds(self):
        return [jax.ShapeDtypeStruct(s, d) for s, d in self.out_shapes]

    def copies(self, in_refs, out_refs, sems):
        send_sems, recv_sems, loc_sems = sems
        x, y, c = _axes()
        local, remote = self.plan(x, y, c, in_refs, out_refs)
        assert len(local) == self.n_local and len(remote) == self.n_remote, (len(local), len(remote))
        lcs = [pltpu.make_async_copy(s, d, loc_sems.at[i]) for i, (s, d) in enumerate(local)]
        rcs = [pltpu.make_async_remote_copy(src_ref=s, dst_ref=d, send_sem=send_sems.at[i], recv_sem=recv_sems.at[i],
                                            device_id=dev, device_id_type=MESH) for i, (s, d, dev) in enumerate(remote)]
        return lcs + rcs


def _exchange(name, comm):
    n_in, n_out = len(comm.ins), len(comm.out_shapes)

    def body(*refs):
        cps = comm.copies(refs[:n_in], refs[n_in:n_in + n_out], refs[n_in + n_out:])
        for cp in cps:
            cp.start()
        for cp in cps:
            cp.wait()

    any_spec = pl.BlockSpec(memory_space=pl.ANY)
    return pl.pallas_call(
        body, name=name, in_specs=[any_spec] * n_in, out_specs=[any_spec] * n_out, out_shape=comm.out_sds(),
        scratch_shapes=comm.sem_shapes(), input_output_aliases=comm.aliases,
        compiler_params=pltpu.CompilerParams(has_side_effects=True), **_CALL_KW)(*comm.ins)


def _chip_of(k):
    return k // 2, k % 2


def _gather_ici_comm(shards, layer):
    def plan(x, y, c, ins, outs):
        me = 2 * x + y
        remote = []
        for w, g in zip(ins, outs):
            half = w.shape[1] // 2
            rows = pl.ds(c * half, half)
            for j in (1, 2, 3):
                px, py = _chip_of(me ^ j)
                remote.append((w.at[layer, rows], g.at[me, rows], (px, py, c)))
        return [], remote

    return _Comm(shards, [((N_CHIPS,) + w.shape[1:], w.dtype) for w in shards], plan, 0, 3 * len(shards))


def _gather_d2d(name, gath):
    def plan(x, y, c, ins, outs):
        me = 2 * x + y
        remote = []
        for g in outs:
            half = g.shape[1] // 2
            rows = pl.ds(c * half, half)
            for j in (1, 2, 3):
                remote.append((g.at[me ^ j, rows], g.at[me ^ j, rows], (x, y, 1 - c)))
        return [], remote

    n = len(gath)
    return _exchange(name, _Comm(gath, [(g.shape, g.dtype) for g in gath], plan, 0, 3 * n, aliases={i: i for i in range(n)}))


def _elementwise(name, fn, ins, out_dtypes, block_rows=None, n_out=None):
    shape = ins[0].shape
    lead, (R, C) = shape[:-2], shape[-2:]
    budget = (16 * 2 ** 20) // (8 * (len(ins) + len(out_dtypes)) * (-(-C // 128) * 128))
    br = block_rows or _pick(R, [r for r in (512, 256, 128, 352, 64, 32, 16, 8) if r <= max(budget, 8)] + [R])
    nl = len(lead)

    def body(*refs):
        outs = fn(*[r[...] for r in refs[:len(ins)]])
        for r, o in zip(refs[len(ins):], outs):
            r[...] = o.astype(r.dtype)

    blk = (None,) * nl + (br, C)
    spec = pl.BlockSpec(blk, lambda *g: tuple(g[:nl]) + (g[nl], 0))
    return pl.pallas_call(
        body, name=name, grid=tuple(lead) + (R // br,), in_specs=[spec] * len(ins), out_specs=[spec] * len(out_dtypes),
        out_shape=[jax.ShapeDtypeStruct(shape, d) for d in out_dtypes],
        compiler_params=_cparams(("parallel",) * (nl + 1)), **_CALL_KW)(*ins)


def _reduce_pre(tag, parts):
    n = len(parts)

    def plan_a(x, y, c, ins, outs):
        remote = []
        for p, r in zip(ins, outs):
            half = p.shape[2] // 2
            remote.append((p.at[:, :, pl.ds((1 - c) * half, half)], r, (x, y, 1 - c)))
        return [], remote

    halves = [((p.shape[0], p.shape[1], p.shape[2] // 2, p.shape[3]), p.dtype) for p in parts]
    got = _exchange(f"reduce_{tag}_d2d", _Comm(parts, halves, plan_a, 0, n))

    c = lax.axis_index("c")
    pair = []
    for idx, (p, r) in enumerate(zip(parts, got)):
        half = p.shape[2] // 2
        br = _pick(half, (512, 256, 352, 128, 64, 32, 16))
        nb = half // br

        def body(c_ref, p_ref, r_ref, o_ref):
            o_ref[...] = (p_ref[...].astype(F32) + r_ref[...].astype(F32)).astype(o_ref.dtype)

        blk = (None, None, br, p.shape[3])
        pair.append(pl.pallas_call(
            body, name=f"reduce_{tag}_pair{idx}",
            grid_spec=pltpu.PrefetchScalarGridSpec(
                num_scalar_prefetch=1, grid=(N_CHIPS, p.shape[1], nb),
                in_specs=[pl.BlockSpec(blk, lambda k, l, i, cr, nb=nb: (k, l, cr[0] * nb + i, 0)),
                          pl.BlockSpec(blk, lambda k, l, i, cr: (k, l, i, 0))],
                out_specs=pl.BlockSpec(blk, lambda k, l, i, cr: (k, l, i, 0))),
            out_shape=jax.ShapeDtypeStruct(r.shape, BF16),
            compiler_params=_cparams(("parallel",) * 3), **_CALL_KW)(jnp.reshape(c, (1,)).astype(jnp.int32), p, r))
    return pair


def _reduce_ici_comm(pair):
    def plan_b(x, y, c, ins, outs):
        me = 2 * x + y
        remote = []
        for p, r in zip(ins, outs):
            for j in (1, 2, 3):
                px, py = _chip_of(me ^ j)
                remote.append((p.at[me ^ j], r.at[me], (px, py, c)))
        return [], remote

    return _Comm(pair, [(p.shape, p.dtype) for p in pair], plan_b, 0, 3 * len(pair))


def _reduce_post(tag, pair, got):
    n = len(pair)
    c = lax.axis_index("c")
    me_chip = 2 * lax.axis_index("x") + lax.axis_index("y")

    sums = []
    for idx, r in enumerate(got):
        _, L, half, C = r.shape
        br = _pick(half, (512, 256, 352, 128, 64, 32, 16))
        nb = half // br

        def body(c_ref, p_ref, r_ref, o_ref):
            me = c_ref[1]
            acc = None
            for k in range(N_CHIPS):
                term = jnp.where(me == k, p_ref[k], r_ref[k]).astype(F32)
                acc = term if acc is None else acc + term
            o_ref[...] = acc

        blk4 = pl.BlockSpec((N_CHIPS, None, br, C), lambda l, i, cr: (0, l, i, 0))
        sums.append(pl.pallas_call(
            body, name=f"reduce_{tag}_sum{idx}",
            grid_spec=pltpu.PrefetchScalarGridSpec(
                num_scalar_prefetch=1, grid=(L, nb), in_specs=[blk4, blk4],
                out_specs=pl.BlockSpec((None, br, C), lambda l, i, cr, nb=nb: (l, cr[0] * nb + i, 0))),
            out_shape=jax.ShapeDtypeStruct((L, 2 * half, C), F32),
            compiler_params=_cparams(("parallel",) * 2), **_CALL_KW)(jnp.stack([c, me_chip]).astype(jnp.int32), pair[idx], r))

    def plan_c(x, y, c, ins, outs):
        remote = []
        for f in outs:
            half = f.shape[1] // 2
            rows = pl.ds(c * half, half)
            remote.append((f.at[:, rows], f.at[:, rows], (x, y, 1 - c)))
        return [], remote

    return _exchange(f"reduce_{tag}_bcast", _Comm(sums, [(s.shape, F32) for s in sums], plan_c, 0, n, aliases={i: i for i in range(n)}))


def _adamw_math(w, g, m, v):
    m = ADAM_B1 * m + (1.0 - ADAM_B1) * g
    v = ADAM_B2 * v + (1.0 - ADAM_B2) * (g * g)
    m_hat = m / (1.0 - ADAM_B1 ** ADAM_STEP)
    v_hat = v / (1.0 - ADAM_B2 ** ADAM_STEP)
    delta = -ADAM_LR * (m_hat / (jnp.sqrt(v_hat) + ADAM_EPS) + ADAM_WD * w)
    return delta, m, v


def _adamw(name, w, g, m, v):
    return _elementwise(name, _adamw_math, [w, g, m, v], [F32, F32, F32])


def _allgather_small(name, v):
    def plan(x, y, c, ins, outs):
        me = 4 * x + 2 * y + c
        (src,), (dst,) = ins, outs
        remote = []
        for j in range(1, N_DEV):
            p = me ^ j
            remote.append((src, dst.at[me], (p // 4, (p // 2) % 2, p % 2)))
        return [(src, dst.at[me])], remote

    return _exchange(name, _Comm([v], [((N_DEV,) + v.shape, v.dtype)], plan, 1, N_DEV - 1))[0]


def _seg_spec(rows, D, ctx_tiles):
    return pl.BlockSpec((None, rows, D), lambda i: (jnp.minimum(i // ctx_tiles, 1), 0, 0))


def _norm_mod(name, x, g, modv, r0, ctx_tiles):
    T, D = x.shape

    def body(x_ref, g_ref, m_ref, o_ref):
        xv = x_ref[...]
        r = lax.rsqrt(jnp.mean(xv * xv, axis=-1, keepdims=True) + EPS)
        y = xv * r * g_ref[...]
        o_ref[...] = (y * (1.0 + m_ref[r0 + 1:r0 + 2, :]) + m_ref[r0:r0 + 1, :]).astype(BF16)

    row = pl.BlockSpec((TOK, D), lambda i: (i, 0))
    return pl.pallas_call(
        body, name=name, grid=(T // TOK,), in_specs=[row, pl.BlockSpec((1, D), lambda i: (0, 0)), _seg_spec(6, D, ctx_tiles)],
        out_specs=row, out_shape=jax.ShapeDtypeStruct((T, D), BF16), compiler_params=_cparams(("parallel",)), **_CALL_KW)(x, g, modv)


def _norm_mod_bwd(name, x, g, modv, r0, dh, dh_b, dres, ctx_tiles):
    T, D = x.shape

    def body(x_ref, g_ref, m_ref, dh_ref, dhb_ref, dres_ref, dx_ref, dg_ref, dsh_ref, dsc_ref):
        i = pl.program_id(0)
        xv = x_ref[...]
        r = lax.rsqrt(jnp.mean(xv * xv, axis=-1, keepdims=True) + EPS)
        xn = xv * r
        y = xn * g_ref[...]
        dhv = dh_ref[...] + dhb_ref[...]

        @pl.when(i == 0)
        def _():
            dg_ref[...] = jnp.zeros_like(dg_ref)

        @pl.when((i == 0) | (i == ctx_tiles))
        def _():
            dsh_ref[...] = jnp.zeros_like(dsh_ref)
            dsc_ref[...] = jnp.zeros_like(dsc_ref)

        dsh_ref[...] += jnp.sum(dhv, axis=0, keepdims=True)
        dsc_ref[...] += jnp.sum(dhv * y, axis=0, keepdims=True)
        dy = dhv * (1.0 + m_ref[r0 + 1:r0 + 2, :])
        dg_ref[...] += jnp.sum(dy * xn, axis=0, keepdims=True)
        u = dy * g_ref[...]
        dx_ref[...] = dres_ref[...] + r * (u - xn * jnp.mean(u * xn, axis=-1, keepdims=True))

    row = pl.BlockSpec((TOK, D), lambda i: (i, 0))
    one = pl.BlockSpec((1, D), lambda i: (0, 0))
    return pl.pallas_call(
        body, name=name, grid=(T // TOK,), in_specs=[row, one, _seg_spec(6, D, ctx_tiles), row, row, row],
        out_specs=[row, one, _seg_spec(1, D, ctx_tiles), _seg_spec(1, D, ctx_tiles)],
        out_shape=[jax.ShapeDtypeStruct((T, D), F32), jax.ShapeDtypeStruct((1, D), F32),
                   jax.ShapeDtypeStruct((2, 1, D), F32), jax.ShapeDtypeStruct((2, 1, D), F32)],
        compiler_params=_cparams(("arbitrary",)), **_CALL_KW)(x, g, modv, dh, dh_b, dres)


def _resid_gate(name, x, y, modv, r, ctx_tiles):
    T, D = x.shape

    def body(x_ref, y_ref, m_ref, o_ref):
        o_ref[...] = x_ref[...] + m_ref[r:r + 1, :] * y_ref[...]

    row = pl.BlockSpec((TOK, D), lambda i: (i, 0))
    return pl.pallas_call(
        body, name=name, grid=(T // TOK,), in_specs=[row, row, _seg_spec(6, D, ctx_tiles)], out_specs=row,
        out_shape=jax.ShapeDtypeStruct((T, D), F32), compiler_params=_cparams(("parallel",)), **_CALL_KW)(x, y, modv)


def _resid_gate_bwd(name, dx, y, modv, r, ctx_tiles):
    T, D = dx.shape

    def body(dx_ref, y_ref, m_ref, dy_ref, dgt_ref):
        i = pl.program_id(0)

        @pl.when((i == 0) | (i == ctx_tiles))
        def _():
            dgt_ref[...] = jnp.zeros_like(dgt_ref)

        dxv = dx_ref[...]
        dgt_ref[...] += jnp.sum(dxv * y_ref[...], axis=0, keepdims=True)
        dy_ref[...] = (dxv * m_ref[r:r + 1, :]).astype(BF16)

    row = pl.BlockSpec((TOK, D), lambda i: (i, 0))
    return pl.pallas_call(
        body, name=name, grid=(T // TOK,), in_specs=[row, row, _seg_spec(6, D, ctx_tiles)],
        out_specs=[row, _seg_spec(1, D, ctx_tiles)],
        out_shape=[jax.ShapeDtypeStruct((T, D), BF16), jax.ShapeDtypeStruct((2, 1, D), F32)],
        compiler_params=_cparams(("arbitrary",)), **_CALL_KW)(dx, y, modv)


def _sigmoid(x):
    return 1.0 / (1.0 + jnp.exp(-x))


SWI_ROWS = 128


def _swiglu(name, gu):
    T, F2 = gu.shape
    F = F2 // 2

    def body(gu_ref, o_ref):
        g, u = gu_ref[:, :F], gu_ref[:, F:]
        o_ref[...] = ((g * _sigmoid(g)) * u).astype(BF16)

    return pl.pallas_call(
        body, name=name, grid=(T // SWI_ROWS,), in_specs=[pl.BlockSpec((SWI_ROWS, F2), lambda i: (i, 0))],
        out_specs=pl.BlockSpec((SWI_ROWS, F), lambda i: (i, 0)), out_shape=jax.ShapeDtypeStruct((T, F), BF16),
        compiler_params=_cparams(("parallel",)), **_CALL_KW)(gu)


def _swiglu_bwd(name, gu, dact):
    T, F2 = gu.shape
    F = F2 // 2

    def body(gu_ref, d_ref, o_ref):
        g, u, d = gu_ref[:, :F], gu_ref[:, F:], d_ref[...]
        s = _sigmoid(g)
        o_ref[:, :F] = (d * u * (s * (1.0 + g * (1.0 - s)))).astype(BF16)
        o_ref[:, F:] = (d * (g * s)).astype(BF16)

    return pl.pallas_call(
        body, name=name, grid=(T // SWI_ROWS,),
        in_specs=[pl.BlockSpec((SWI_ROWS, F2), lambda i: (i, 0)), pl.BlockSpec((SWI_ROWS, F), lambda i: (i, 0))],
        out_specs=pl.BlockSpec((SWI_ROWS, F2), lambda i: (i, 0)), out_shape=jax.ShapeDtypeStruct((T, F2), BF16),
        compiler_params=_cparams(("parallel",)), **_CALL_KW)(gu, dact)


def _loss_and_grad(name, y, target, ctx_tiles):
    T, D = y.shape

    def body(y_ref, t_ref, l_ref, dy_ref):
        i = pl.program_id(0)

        @pl.when(i == 0)
        def _():
            l_ref[...] = jnp.zeros_like(l_ref)

        lat = i >= ctx_tiles
        e = jnp.where(lat, y_ref[...] - t_ref[...], 0.0)
        dy_ref[...] = e * (1.0 / D)
        l_ref[...] += 0.5 * jnp.sum(jnp.sum(e * e, axis=-1, keepdims=True) * (1.0 / D), axis=0, keepdims=True)

    row = pl.BlockSpec((TOK, D), lambda i: (i, 0))
    return pl.pallas_call(
        body, name=name, grid=(T // TOK,),
        in_specs=[row, pl.BlockSpec((TOK, D), lambda i: (jnp.maximum(i - ctx_tiles, 0), 0))],
        out_specs=[pl.BlockSpec((8, 128), lambda i: (0, 0)), row],
        out_shape=[jax.ShapeDtypeStruct((8, 128), F32), jax.ShapeDtypeStruct((T, D), F32)],
        compiler_params=_cparams(("arbitrary",)), **_CALL_KW)(y, target)


def _rot_half(x):
    lane = lax.broadcasted_iota(jnp.int32, x.shape, 1)
    return jnp.where((lane % 64) < 32, -pltpu.roll(x, 96, 1), pltpu.roll(x, 32, 1))


def _head_prep(name, src, col_blk, n_heads, g, cos, sin, t_pad, norm, rope):
    T = src.shape[0]
    W = n_heads * HEAD
    nt = T // TOK

    def body(s_ref, g_ref, cos_ref, sin_ref, o_ref):
        i = pl.program_id(0)
        outs = []
        for h in range(n_heads):
            xv = s_ref[:, h * HEAD:(h + 1) * HEAD].astype(F32)
            if norm:
                xv = xv * lax.rsqrt(jnp.mean(xv * xv, axis=-1, keepdims=True) + EPS) * g_ref[...]
            if rope:
                xv = xv * cos_ref[...] + _rot_half(xv) * sin_ref[...]
            outs.append(jnp.where(i < nt, xv, 0.0).astype(BF16))
        o_ref[...] = jnp.concatenate(outs, axis=-1) if n_heads > 1 else outs[0]

    tab = pl.BlockSpec((TOK, HEAD), lambda i: (i, 0))
    return pl.pallas_call(
        body, name=name, grid=(t_pad // TOK,),
        in_specs=[pl.BlockSpec((TOK, W), lambda i: (jnp.minimum(i, nt - 1), col_blk)), pl.BlockSpec((1, HEAD), lambda i: (0, 0)), tab, tab],
        out_specs=pl.BlockSpec((TOK, W), lambda i: (i, 0)), out_shape=jax.ShapeDtypeStruct((t_pad, W), BF16),
        compiler_params=_cparams(("parallel",)), **_CALL_KW)(src, g, cos, sin)


def _head_prep_bwd(name, src, col_blk, n_heads, g, cos, sin, dout, rope):
    T = src.shape[0]
    W = n_heads * HEAD

    def body(s_ref, g_ref, cos_ref, sin_ref, d_ref, ds_ref, dg_ref):
        i = pl.program_id(0)

        @pl.when(i == 0)
        def _():
            dg_ref[...] = jnp.zeros_like(dg_ref)

        outs = []
        dg = jnp.zeros((1, HEAD), F32)
        for h in range(n_heads):
            xv = s_ref[:, h * HEAD:(h + 1) * HEAD].astype(F32)
            dz = d_ref[:, h * HEAD:(h + 1) * HEAD]
            if rope:
                dz = dz * cos_ref[...] - _rot_half(dz * sin_ref[...])
            r = lax.rsqrt(jnp.mean(xv * xv, axis=-1, keepdims=True) + EPS)
            xn = xv * r
            dg = dg + jnp.sum(dz * xn, axis=0, keepdims=True)
            u = dz * g_ref[...]
            outs.append((r * (u - xn * jnp.mean(u * xn, axis=-1, keepdims=True))).astype(BF16))
        dg_ref[...] += dg
        ds_ref[...] = jnp.concatenate(outs, axis=-1) if n_heads > 1 else outs[0]

    tab = pl.BlockSpec((TOK, HEAD), lambda i: (i, 0))
    row = pl.BlockSpec((TOK, W), lambda i: (i, 0))
    one = pl.BlockSpec((1, HEAD), lambda i: (0, 0))
    return pl.pallas_call(
        body, name=name, grid=(T // TOK,),
        in_specs=[pl.BlockSpec((TOK, W), lambda i: (i, col_blk)), one, tab, tab, row],
        out_specs=[row, one], out_shape=[jax.ShapeDtypeStruct((T, W), BF16), jax.ShapeDtypeStruct((1, HEAD), F32)],
        compiler_params=_cparams(("arbitrary",)), **_CALL_KW)(src, g, cos, sin, dout)


NEG = -1e30


def _attn_geometry(kind, blk, ctx, seq):
    if kind == "swa":
        bq, W = 128, 384
        nctx = ctx // bq
        lat = blk >= nctx
        n = blk - nctx
        s0 = jnp.where(lat, ctx + (n - 1) * bq, 0)
        i = lax.broadcasted_iota(jnp.int32, (bq, W), 0)
        j = lax.broadcasted_iota(jnp.int32, (bq, W), 1)
        kpos = (n - 1) * bq + j
        rel = j - bq - i
        valid = lat & (rel <= SWA_WINDOW) & (rel >= -SWA_WINDOW) & (kpos >= 0) & (kpos < seq)
        return s0, valid, 0
    bq, W = GRID_W, NA_KH * GRID_W
    nctx = ctx // bq
    rows = seq // GRID_W
    lat = blk >= nctx
    rr = jnp.clip(blk - nctx, 0, rows - 1)
    rs = jnp.clip(rr - NA_KH // 2, 0, rows - NA_KH)
    s0 = ctx + rs * GRID_W
    i = lax.broadcasted_iota(jnp.int32, (bq, W), 0)
    j = lax.broadcasted_iota(jnp.int32, (bq, W), 1)
    kcol = j % GRID_W
    cs = jnp.clip(i - NA_KW // 2, 0, GRID_W - NA_KW)
    valid = lat & (kcol >= cs) & (kcol < cs + NA_KW)
    return s0, valid, rr - rs


def _attn_probs(kind, h, blk, q, k_ref, sink_ref, bias_ref, ctx, seq):
    bq, W = (128, 384) if kind == "swa" else (GRID_W, NA_KH * GRID_W)
    scale = HEAD ** -0.5
    s0, valid, _ = _attn_geometry(kind, blk, ctx, seq)
    s0 = pl.multiple_of(s0, GRID_W)
    kl = k_ref[pl.ds(s0, W), :]
    kc = k_ref[0:ctx, :]
    nt_dims = (((1,), (1,)), ((), ()))
    sl = lax.dot_general(q, kl, nt_dims, preferred_element_type=F32) * scale
    if bias_ref is not None:
        sl = sl + bias_ref[...]
    sl = jnp.where(valid, sl, NEG)
    sc = lax.dot_general(q, kc, nt_dims, preferred_element_type=F32) * scale
    sk = sink_ref[h]
    m = jnp.maximum(jnp.maximum(jnp.max(sl, axis=-1, keepdims=True), jnp.max(sc, axis=-1, keepdims=True)), sk)
    el, ec, es = jnp.exp(sl - m), jnp.exp(sc - m), jnp.exp(sk - m)
    inv = 1.0 / (jnp.sum(el, axis=-1, keepdims=True) + jnp.sum(ec, axis=-1, keepdims=True) + es)
    return s0, kl, kc, el * inv, ec * inv, es * inv


def _attn_specs(kind, n_q, n_kv, t_pad, bias):
    bq = 128 if kind == "swa" else GRID_W
    rep = n_q // n_kv
    qspec = pl.BlockSpec((bq, HEAD), lambda h, b: (b, h))
    kvspec = pl.BlockSpec((t_pad, HEAD), lambda h, b: (0, h // rep))
    specs = [qspec, kvspec, kvspec, pl.BlockSpec(memory_space=pltpu.SMEM)]
    return bq, rep, qspec, kvspec, specs


def _bias_spec(ctx, seq):
    W = NA_KH * GRID_W

    def idx(h, b):
        rows = seq // GRID_W
        rr = jnp.clip(b - ctx // GRID_W, 0, rows - 1)
        return (h, rr - jnp.clip(rr - NA_KH // 2, 0, rows - NA_KH), 0, 0)

    return pl.BlockSpec((None, None, GRID_W, W), idx)


def _attn_fwd(name, kind, q, k, v, sink, bias, ctx, seq):
    t_pad = q.shape[0]
    T = ctx + seq
    n_q, n_kv = q.shape[1] // HEAD, k.shape[1] // HEAD
    bq, rep, qspec, kvspec, specs = _attn_specs(kind, n_q, n_kv, t_pad, bias)

    def body(q_ref, k_ref, v_ref, sink_ref, *rest):
        bias_ref = rest[0] if bias is not None else None
        o_ref = rest[-1]
        h, blk = pl.program_id(0), pl.program_id(1)
        s0, kl, kc, p_l, p_c, _ = _attn_probs(kind, h, blk, q_ref[...], k_ref, sink_ref, bias_ref, ctx, seq)
        W = kl.shape[0]
        o = jnp.dot(p_l.astype(BF16), v_ref[pl.ds(s0, W), :], preferred_element_type=F32)
        o_ref[...] = o + jnp.dot(p_c.astype(BF16), v_ref[0:ctx, :], preferred_element_type=F32)

    ins = [q, k, v, sink] + ([bias] if bias is not None else [])
    if bias is not None:
        specs = specs + [_bias_spec(ctx, seq)]
    return pl.pallas_call(
        body, name=name, grid=(n_q, T // bq), in_specs=specs, out_specs=qspec,
        out_shape=jax.ShapeDtypeStruct((T, n_q * HEAD), F32), compiler_params=_cparams(("parallel", "arbitrary")), **_CALL_KW)(*ins)


def _attn_bwd(name, kind, q, k, v, sink, bias, do, do_head0, ctx, seq):
    t_pad = q.shape[0]
    T = ctx + seq
    n_q, n_kv = q.shape[1] // HEAD, k.shape[1] // HEAD
    bq, rep, qspec, kvspec, specs = _attn_specs(kind, n_q, n_kv, t_pad, bias)
    scale = HEAD ** -0.5
    tn_dims = (((0,), (0,)), ((), ()))
    nt_dims = (((1,), (1,)), ((), ()))

    def body(q_ref, k_ref, v_ref, sink_ref, *rest):
        if bias is not None:
            bias_ref, do_ref, dq_ref, dk_ref, dv_ref, dsk_ref, db_ref = rest
        else:
            bias_ref, db_ref = None, None
            do_ref, dq_ref, dk_ref, dv_ref, dsk_ref = rest
        h, blk = pl.program_id(0), pl.program_id(1)
        qv = q_ref[...]
        s0, kl, kc, p_l, p_c, p_s = _attn_probs(kind, h, blk, qv, k_ref, sink_ref, bias_ref, ctx, seq)
        W = kl.shape[0]
        vl = v_ref[pl.ds(s0, W), :]
        vc = v_ref[0:ctx, :]
        dov = do_ref[...]
        dob = dov.astype(BF16)
        pl_b, pc_b = p_l.astype(BF16), p_c.astype(BF16)
        o = jnp.dot(pl_b, vl, preferred_element_type=F32) + jnp.dot(pc_b, vc, preferred_element_type=F32)
        delta = jnp.sum(dov * o, axis=-1, keepdims=True)
        ds_l = p_l * (lax.dot_general(dob, vl, nt_dims, preferred_element_type=F32) - delta)
        ds_c = p_c * (lax.dot_general(dob, vc, nt_dims, preferred_element_type=F32) - delta)
        dsl_b, dsc_b = ds_l.astype(BF16), ds_c.astype(BF16)
        dq_ref[...] = (jnp.dot(dsl_b, kl, preferred_element_type=F32) + jnp.dot(dsc_b, kc, preferred_element_type=F32)) * scale

        @pl.when((h % rep == 0) & (blk == 0))
        def _():
            dk_ref[...] = jnp.zeros_like(dk_ref)
            dv_ref[...] = jnp.zeros_like(dv_ref)

        @pl.when(blk == 0)
        def _():
            dsk_ref[...] = jnp.zeros_like(dsk_ref)

        dk_ref[pl.ds(s0, W), :] += lax.dot_general(dsl_b, qv, tn_dims, preferred_element_type=F32) * scale
        dv_ref[pl.ds(s0, W), :] += lax.dot_general(pl_b, dob, tn_dims, preferred_element_type=F32)
        dk_ref[0:ctx, :] += lax.dot_general(dsc_b, qv, tn_dims, preferred_element_type=F32) * scale
        dv_ref[0:ctx, :] += lax.dot_general(pc_b, dob, tn_dims, preferred_element_type=F32)
        dsk_ref[...] += jnp.sum(-p_s * delta, axis=0, keepdims=True)
        if bias is not None:
            _, _, pat = _attn_geometry(kind, blk, ctx, seq)
            _, _, pat_prev = _attn_geometry(kind, jnp.maximum(blk - 1, 0), ctx, seq)

            @pl.when((blk == 0) | (pat != pat_prev))
            def _():
                db_ref[...] = jnp.zeros_like(db_ref)

            db_ref[...] += ds_l

    ins = [q, k, v, sink] + ([bias] if bias is not None else []) + [do]
    in_specs = specs + ([_bias_spec(ctx, seq)] if bias is not None else []) + [pl.BlockSpec((bq, HEAD), lambda h, b: (b, do_head0 + h))]
    out_specs = [qspec, kvspec, kvspec, pl.BlockSpec((None, 8, HEAD), lambda h, b: (h, 0, 0))]
    out_shape = [jax.ShapeDtypeStruct((T, n_q * HEAD), F32), jax.ShapeDtypeStruct((t_pad, n_kv * HEAD), F32),
                 jax.ShapeDtypeStruct((t_pad, n_kv * HEAD), F32), jax.ShapeDtypeStruct((n_q, 8, HEAD), F32)]
    if bias is not None:
        out_specs.append(_bias_spec(ctx, seq))
        out_shape.append(jax.ShapeDtypeStruct(bias.shape, F32))
    res = pl.pallas_call(
        body, name=name, grid=(n_q, T // bq), in_specs=in_specs, out_specs=out_specs, out_shape=out_shape,
        compiler_params=_cparams(("arbitrary", "arbitrary")), **_CALL_KW)(*ins)
    return res if bias is not None else list(res) + [None]


HALO = 8


def _halo_specs(width, col0, T, ctx_tiles):
    per = TOK // HALO
    main = pl.BlockSpec((TOK, width), lambda jc, i: (i, col0 + jc))
    prev = pl.BlockSpec((HALO, width), lambda jc, i: (jnp.maximum(i * per - 1, 0), col0 + jc))
    nxt = pl.BlockSpec((HALO, width), lambda jc, i: (jnp.minimum((i + 1) * per, T // HALO - 1), col0 + jc))
    return main, prev, nxt


def _with_halo(i, nt, ctx_tiles, prev, main, nxt):
    has_prev = (i != 0) & (i != ctx_tiles)
    has_next = (i != ctx_tiles - 1) & (i != nt - 1)
    return jnp.concatenate([jnp.where(has_prev, prev, 0.0), main, jnp.where(has_next, nxt, 0.0)], axis=0)


def _shifted(ext, s):
    n = ext.shape[0]
    return pltpu.roll(ext, (-s) % n, 0)[HALO:HALO + TOK]


def _conv_fwd(name, p, col0, conv_w, ctx_tiles):
    T = p.shape[0]
    nt = T // TOK
    ncol = 3
    Wc = conv_w.shape[1] // ncol
    pad = (DN_CONV - 1) // 2

    def body(m_ref, p_ref, n_ref, w_ref, o_ref):
        i = pl.program_id(1)
        ext = _with_halo(i, nt, ctx_tiles, p_ref[...].astype(F32), m_ref[...].astype(F32), n_ref[...].astype(F32))
        acc = jnp.zeros((TOK, Wc), F32)
        for j in range(DN_CONV):
            acc = acc + w_ref[j:j + 1, :] * _shifted(ext, j - pad)
        o_ref[...] = acc

    main, prev, nxt = _halo_specs(Wc, col0, T, ctx_tiles)
    return pl.pallas_call(
        body, name=name, grid=(ncol, nt), in_specs=[main, prev, nxt, pl.BlockSpec((DN_CONV, Wc), lambda jc, i: (0, jc))],
        out_specs=pl.BlockSpec((TOK, Wc), lambda jc, i: (i, jc)), out_shape=jax.ShapeDtypeStruct((T, ncol * Wc), F32),
        compiler_params=_cparams(("parallel", "parallel")), **_CALL_KW)(p, p, p, conv_w)


def _conv_bwd(name, p, col0, conv_w, dpre, ctx_tiles):
    T = p.shape[0]
    nt = T // TOK
    ncol = 3
    Wc = conv_w.shape[1] // ncol
    pad = (DN_CONV - 1) // 2

    def body(m_ref, p_ref, n_ref, dm_ref, dp_ref, dn_ref, w_ref, dx_ref, dw_ref):
        i = pl.program_id(1)
        ext_x = _with_halo(i, nt, ctx_tiles, p_ref[...].astype(F32), m_ref[...].astype(F32), n_ref[...].astype(F32))
        ext_d = _with_halo(i, nt, ctx_tiles, dp_ref[...], dm_ref[...], dn_ref[...])
        dmain = dm_ref[...]

        @pl.when(i == 0)
        def _():
            dw_ref[...] = jnp.zeros_like(dw_ref)

        acc = jnp.zeros((TOK, Wc), F32)
        for j in range(DN_CONV):
            acc = acc + w_ref[j:j + 1, :] * _shifted(ext_d, pad - j)
            dw_ref[j:j + 1, :] += jnp.sum(dmain * _shifted(ext_x, j - pad), axis=0, keepdims=True)
        dx_ref[...] = acc.astype(BF16)

    main, prev, nxt = _halo_specs(Wc, col0, T, ctx_tiles)
    dmain, dprev, dnxt = _halo_specs(Wc, 0, T, ctx_tiles)
    return pl.pallas_call(
        body, name=name, grid=(ncol, nt),
        in_specs=[main, prev, nxt, dmain, dprev, dnxt, pl.BlockSpec((DN_CONV, Wc), lambda jc, i: (0, jc))],
        out_specs=[pl.BlockSpec((TOK, Wc), lambda jc, i: (i, jc)), pl.BlockSpec((8, Wc), lambda jc, i: (0, jc))],
        out_shape=[jax.ShapeDtypeStruct((T, ncol * Wc), BF16), jax.ShapeDtypeStruct((8, ncol * Wc), F32)],
        compiler_params=_cparams(("parallel", "arbitrary")), **_CALL_KW)(p, p, p, dpre, dpre, dpre, conv_w)


def _softplus(x):
    return jnp.maximum(x, 0.0) + jnp.log(1.0 + jnp.exp(-jnp.abs(x)))


def _gdn_point(name, pre, dab, a_log, dt_bias, n_heads):
    T = pre.shape[0]
    Wd = n_heads * HEAD
    ng = 2 * n_heads

    def body(pre_ref, ab_ref, al_ref, dt_ref, q_ref, k_ref, v_ref, la_ref, be_ref):
        for h in range(n_heads):
            for part, ref in enumerate((q_ref, k_ref, v_ref)):
                xv = pre_ref[:, part * Wd + h * HEAD:part * Wd + (h + 1) * HEAD]
                s = xv * _sigmoid(xv)
                if part < 2:
                    s = s * lax.rsqrt(jnp.sum(s * s, axis=-1, keepdims=True) + EPS) * (HEAD ** -0.5 if part == 0 else 1.0)
                ref[:, h * HEAD:(h + 1) * HEAD] = s
        ab = ab_ref[...].astype(F32)
        lane = lax.broadcasted_iota(jnp.int32, ab.shape, 1)
        la_ref[...] = jnp.where(lane < ng, -jnp.exp(al_ref[...]) * _softplus(ab + dt_ref[...]), 0.0)
        be_ref[...] = jnp.where(lane < ng, _sigmoid(pltpu.roll(ab, HEAD - ng, 1)), 0.0)

    row = lambda w: pl.BlockSpec((TOK, w), lambda i: (i, 0))
    one = pl.BlockSpec((1, HEAD), lambda i: (0, 0))
    return pl.pallas_call(
        body, name=name, grid=(T // TOK,), in_specs=[row(3 * Wd), row(HEAD), one, one],
        out_specs=[row(Wd), row(Wd), row(Wd), row(HEAD), row(HEAD)],
        out_shape=[jax.ShapeDtypeStruct((T, Wd), F32)] * 3 + [jax.ShapeDtypeStruct((T, HEAD), F32)] * 2,
        compiler_params=_cparams(("parallel",)), **_CALL_KW)(pre, dab, a_log, dt_bias)


def _gdn_point_bwd(name, pre, dab, a_log, dt_bias, n_heads, dq, dk, dv, dla, dbe):
    T = pre.shape[0]
    Wd = n_heads * HEAD
    ng = 2 * n_heads

    def body(pre_ref, ab_ref, al_ref, dt_ref, dq_ref, dk_ref, dv_ref, dla_ref, dbe_ref, dpre_ref, dab_ref, dal_ref, ddt_ref):
        i = pl.program_id(0)

        @pl.when(i == 0)
        def _():
            dal_ref[...] = jnp.zeros_like(dal_ref)
            ddt_ref[...] = jnp.zeros_like(ddt_ref)

        for h in range(n_heads):
            for part, ref in enumerate((dq_ref, dk_ref, dv_ref)):
                cols = slice(part * Wd + h * HEAD, part * Wd + (h + 1) * HEAD)
                xv = pre_ref[:, cols]
                sg = _sigmoid(xv)
                s = xv * sg
                dy = ref[0, :, h * HEAD:(h + 1) * HEAD] + ref[1, :, h * HEAD:(h + 1) * HEAD]
                if part < 2:
                    c0 = HEAD ** -0.5 if part == 0 else 1.0
                    r = lax.rsqrt(jnp.sum(s * s, axis=-1, keepdims=True) + EPS)
                    ds = c0 * (r * dy - s * (r * r * r) * jnp.sum(dy * s, axis=-1, keepdims=True))
                else:
                    ds = dy
                dpre_ref[:, cols] = ds * (sg * (1.0 + xv * (1.0 - sg)))
        ab = ab_ref[...].astype(F32)
        lane = lax.broadcasted_iota(jnp.int32, ab.shape, 1)
        ea = jnp.exp(al_ref[...])
        z = ab + dt_ref[...]
        dlav = jnp.where(lane < ng, dla_ref[0] + dla_ref[1], 0.0)
        da = dlav * (-ea) * _sigmoid(z)
        dal_ref[...] += jnp.sum(dlav * (-ea) * _softplus(z), axis=0, keepdims=True)
        ddt_ref[...] += jnp.sum(da, axis=0, keepdims=True)
        be = _sigmoid(pltpu.roll(ab, HEAD - ng, 1))
        db = jnp.where(lane < ng, (dbe_ref[0] + dbe_ref[1]) * be * (1.0 - be), 0.0)
        dab_ref[...] = (da + pltpu.roll(db, ng, 1)).astype(BF16)

    row = lambda w: pl.BlockSpec((TOK, w), lambda i: (i, 0))
    two = lambda w: pl.BlockSpec((2, TOK, w), lambda i: (0, i, 0))
    one = pl.BlockSpec((1, HEAD), lambda i: (0, 0))
    return pl.pallas_call(
        body, name=name, grid=(T // TOK,),
        in_specs=[row(3 * Wd), row(HEAD), one, one, two(Wd), two(Wd), two(Wd), two(HEAD), two(HEAD)],
        out_specs=[row(3 * Wd), row(HEAD), one, one],
        out_shape=[jax.ShapeDtypeStruct((T, 3 * Wd), F32), jax.ShapeDtypeStruct((T, HEAD), BF16),
                   jax.ShapeDtypeStruct((1, HEAD), F32), jax.ShapeDtypeStruct((1, HEAD), F32)],
        compiler_params=_cparams(("arbitrary",)), **_CALL_KW)(pre, dab, a_log, dt_bias, dq, dk, dv, dla, dbe)


_NN, _NT, _TN = "nn", "nt", "tn"
_DIMS = {"nn": (((1,), (0,)), ((), ())), "nt": (((1,), (1,)), ((), ())), "tn": (((0,), (0,)), ((), ()))}
_BDIMS = {"nn": (((2,), (1,)), ((0,), (0,))), "nt": (((2,), (2,)), ((0,), (0,))), "tn": (((1,), (1,)), ((0,), (0,)))}


def _dims(a, kind):
    return _BDIMS[kind] if a.ndim == 3 else _DIMS[kind]


def _mm3_raw(a, b, kind=_NN):
    ah, bh = a.astype(BF16), b.astype(BF16)
    al, bl = (a - ah.astype(F32)).astype(BF16), (b - bh.astype(F32)).astype(BF16)
    d = functools.partial(lax.dot_general, dimension_numbers=_dims(a, kind), preferred_element_type=F32)
    return d(ah, bh) + (d(ah, bl) + d(al, bh))


@jax.custom_vjp
def _mm3(a, b):
    return _mm3_raw(a, b)


def _mm3_fwd(a, b):
    return _mm3_raw(a, b), (a, b)


def _mm3_bwd(res, g):
    a, b = res
    return _mm3_raw(g, b, _NT), _mm3_raw(a, g, _TN)


_mm3.defvjp(_mm3_fwd, _mm3_bwd)


def _bdot_raw(a, b, kind):
    return lax.dot_general(a.astype(BF16), b.astype(BF16), _dims(a, kind), preferred_element_type=F32)


@functools.partial(jax.custom_vjp, nondiff_argnums=(2,))
def _bdot(a, b, kind=_NN):
    return _bdot_raw(a, b, kind)


def _bdot_fwd(a, b, kind):
    return _bdot_raw(a, b, kind), (a, b)


def _bdot_bwd(kind, res, g):
    a, b = res
    if kind == "nn":
        return _bdot_raw(g, b, "nt"), _bdot_raw(a, g, "tn")
    if kind == "nt":
        return _bdot_raw(g, b, "nn"), _bdot_raw(g, a, "tn")
    return _bdot_raw(b, g, "nt"), _bdot_raw(a, g, "nn")


_bdot.defvjp(_bdot_fwd, _bdot_bwd)


def _chunk_masks(rev):
    C = DN_CHUNK
    ii = lax.broadcasted_iota(jnp.int32, (C, C), 0)
    jj = lax.broadcasted_iota(jnp.int32, (C, C), 1)
    diff = jnp.where(rev, jj - ii, ii - jj)
    incl = diff >= 0
    strict = diff > 0
    rowsel = (lax.broadcasted_iota(jnp.int32, (C, 1), 0) == jnp.where(rev, 0, C - 1)).astype(F32)
    return incl, strict, rowsel, (ii == jj).astype(F32)


def _head_stack(ref, n_heads):
    return jnp.stack([ref[:, h * HEAD:(h + 1) * HEAD] for h in range(n_heads)])


def _gate_views(g, gt, be, d, n_heads):
    lane = lax.broadcasted_iota(jnp.int32, (1, HEAD), 1)
    sub = lax.broadcasted_iota(jnp.int32, (HEAD, 1), 0)
    sels = [(lane == d * n_heads + h).astype(F32) for h in range(n_heads)]
    selts = [(sub == d * n_heads + h).astype(F32) for h in range(n_heads)]
    g_col = jnp.stack([jnp.sum(g * s, axis=1, keepdims=True) for s in sels])
    b_col = jnp.stack([jnp.sum(be * s, axis=1, keepdims=True) for s in sels])
    g_row = jnp.stack([jnp.sum(gt * s, axis=0, keepdims=True) for s in selts])
    return g_col, g_row, b_col, sels, selts


def _chunk_decay(g_col, g_row, incl):
    return jnp.where(incl, jnp.exp(jnp.where(incl, g_col - g_row, 0.0)), 0.0)


def _chunk_lower(k, g_col, g_row, b_col, incl, strict):
    return jnp.where(strict, _bdot(k * b_col, k, _NT) * _chunk_decay(g_col, g_row, incl), 0.0)


def _chunk_inverse(low, eye):
    m = -low
    x = eye + m
    p = m
    for _ in range(int(math.log2(DN_CHUNK)) - 1):
        p = _mm3(p, p)
        x = x + _mm3(x, p)
    return x


def _chunk_step(q, k, v, g_col, g_row, b_col, S, X, incl, rowsel):
    decay = _chunk_decay(g_col, g_row, incl)
    eg = jnp.exp(g_col)
    u = _mm3(X, v * b_col)
    w = _mm3(X, k * (b_col * eg))
    intra = _bdot(q, k, _NT) * decay
    g_last = jnp.sum(g_col * rowsel, axis=1, keepdims=True)
    v_new = u - _bdot(w, S)
    o = _bdot(q * eg, S) + _bdot(intra, v_new)
    S_new = S * jnp.exp(g_last) + _bdot(k * jnp.exp(g_last - g_col), v_new, _TN)
    return o, S_new


def _scan_index(ctx_chunks, n_chunks):
    def idx(d, n):
        return jnp.where(d == 0, n, jnp.where(n < ctx_chunks, ctx_chunks - 1 - n, n_chunks + ctx_chunks - 1 - n))
    return idx


def _cumsum_mats(rev):
    C = DN_CHUNK
    ii = lax.broadcasted_iota(jnp.int32, (C, C), 0)
    jj = lax.broadcasted_iota(jnp.int32, (C, C), 1)
    return jnp.where(jnp.where(rev, jj - ii, ii - jj) >= 0, 1.0, 0.0).astype(F32)


def _hosted(comm, n_in, n_out, refs):
    n_ci, n_co = (len(comm.ins), len(comm.out_shapes)) if comm is not None else (0, 0)
    ins, cin = refs[:n_in], refs[n_in:n_in + n_ci]
    outs, cout = refs[n_in + n_ci:n_in + n_ci + n_out], refs[n_in + n_ci + n_out:n_in + n_ci + n_out + n_co]
    rest = refs[n_in + n_ci + n_out + n_co:]
    n_sem = 3 if comm is not None else 0
    return ins, outs, rest[:len(rest) - n_sem], (cin, cout, rest[len(rest) - n_sem:])


def _hosted_start(comm, cref, first):
    if comm is not None:
        @pl.when(first)
        def _():
            for cp in comm.copies(*cref):
                cp.start()


def _hosted_wait(comm, cref, last):
    if comm is not None:
        @pl.when(last)
        def _():
            for cp in comm.copies(*cref):
                cp.wait()


def _gdn_scan(name, q, k, v, la, be, n_heads, ctx, comm=None):
    T, Wd = q.shape
    C = DN_CHUNK
    nch = T // C
    cidx = _scan_index(ctx // C, nch)

    def body(*refs):
        (q_ref, k_ref, v_ref, la_ref, be_ref), (o_ref, s_ref, x_ref), (state,), cref = _hosted(comm, 5, 3, refs)
        d, n = pl.program_id(0), pl.program_id(1)
        rev = d == 1
        _hosted_start(comm, cref, (d == 0) & (n == 0))

        @pl.when(n == 0)
        def _():
            state[...] = jnp.zeros_like(state)

        incl, strict, rowsel, eye = _chunk_masks(rev)
        g = jnp.dot(_cumsum_mats(rev), la_ref[...], precision=lax.Precision.HIGHEST, preferred_element_type=F32)
        g_col, g_row, b_col, _, _ = _gate_views(g, g.T, be_ref[...], d, n_heads)
        qs, ks, vs = _head_stack(q_ref, n_heads), _head_stack(k_ref, n_heads), _head_stack(v_ref, n_heads)
        S = state[...]
        X = _chunk_inverse(_chunk_lower(ks, g_col, g_row, b_col, incl, strict), eye)
        o, S_new = _chunk_step(qs, ks, vs, g_col, g_row, b_col, S, X, incl, rowsel)
        s_ref[...] = S
        x_ref[...] = X
        state[...] = S_new
        for h in range(n_heads):
            o_ref[:, h * HEAD:(h + 1) * HEAD] = o[h]
        _hosted_wait(comm, cref, (d == 1) & (n == nch - 1))

    tok = lambda w: pl.BlockSpec((C, w), lambda d, n: (cidx(d, n), 0))
    any_spec = pl.BlockSpec(memory_space=pl.ANY)
    c_ins, c_sds, c_sems = (comm.ins, comm.out_sds(), comm.sem_shapes()) if comm is not None else ([], [], [])
    res = pl.pallas_call(
        body, name=name, grid=(2, nch), in_specs=[tok(Wd), tok(Wd), tok(Wd), tok(HEAD), tok(HEAD)] + [any_spec] * len(c_ins),
        out_specs=[pl.BlockSpec((None, C, Wd), lambda d, n: (d, cidx(d, n), 0)),
                   pl.BlockSpec((None, None, n_heads, HEAD, HEAD), lambda d, n: (d, n, 0, 0, 0)),
                   pl.BlockSpec((None, None, n_heads, C, C), lambda d, n: (d, n, 0, 0, 0))] + [any_spec] * len(c_sds),
        out_shape=[jax.ShapeDtypeStruct((2, T, Wd), F32), jax.ShapeDtypeStruct((2, nch, n_heads, HEAD, HEAD), F32),
                   jax.ShapeDtypeStruct((2, nch, n_heads, C, C), F32)] + c_sds,
        scratch_shapes=[pltpu.VMEM((n_heads, HEAD, HEAD), F32)] + c_sems,
        compiler_params=_cparams(("arbitrary", "arbitrary"), has_side_effects=comm is not None), **_CALL_KW)(q, k, v, la, be, *c_ins)
    return res[0], res[1], res[2], list(res[3:])


def _gdn_scan_bwd(name, q, k, v, la, be, states, invs, do, n_heads, ctx, comm=None):
    T, Wd = q.shape
    C = DN_CHUNK
    nch = T // C
    cidx = _scan_index(ctx // C, nch)

    def body(*refs):
        ins, outs, (dstate,), cref = _hosted(comm, 8, 5, refs)
        q_ref, k_ref, v_ref, la_ref, be_ref, s_ref, x_ref, do_ref = ins
        dq_ref, dk_ref, dv_ref, dla_ref, dbe_ref = outs
        d, n = pl.program_id(0), pl.program_id(1)
        rev = d == 1
        _hosted_start(comm, cref, (d == 0) & (n == 0))

        @pl.when(n == 0)
        def _():
            dstate[...] = jnp.zeros_like(dstate)

        incl, strict, rowsel, eye = _chunk_masks(rev)
        tri = _cumsum_mats(rev)
        g = jnp.dot(tri, la_ref[...], precision=lax.Precision.HIGHEST, preferred_element_type=F32)
        g_col, g_row, b_col, sels, selts = _gate_views(g, g.T, be_ref[...], d, n_heads)
        qs, ks, vs = _head_stack(q_ref, n_heads), _head_stack(k_ref, n_heads), _head_stack(v_ref, n_heads)
        dos = _head_stack(do_ref, n_heads)
        S, X = s_ref[...], x_ref[...]
        step = functools.partial(_chunk_step, incl=incl, rowsel=rowsel)
        _, vjp_step = jax.vjp(step, qs, ks, vs, g_col, g_row, b_col, S, X)
        dq, dk1, dv_, dgc1, dgr1, dbc1, dS, dX = vjp_step((dos, dstate[...]))
        dlow = -_mm3_raw(_mm3_raw(X, dX, _TN), X, _NT)
        low_fn = functools.partial(_chunk_lower, incl=incl, strict=strict)
        _, vjp_low = jax.vjp(low_fn, ks, g_col, g_row, b_col)
        dk2, dgc2, dgr2, dbc2 = vjp_low(dlow)
        dk = dk1 + dk2
        dstate[...] = dS
        dgc, dgr, dbc = dgc1 + dgc2, dgr1 + dgr2, dbc1 + dbc2
        dg = jnp.zeros((C, HEAD), F32)
        dgt = jnp.zeros((HEAD, C), F32)
        dbe = jnp.zeros((C, HEAD), F32)
        for h in range(n_heads):
            cols = slice(h * HEAD, (h + 1) * HEAD)
            dq_ref[:, cols], dk_ref[:, cols], dv_ref[:, cols] = dq[h], dk[h], dv_[h]
            dg = dg + dgc[h] * sels[h]
            dgt = dgt + selts[h] * dgr[h]
            dbe = dbe + dbc[h] * sels[h]
        dg = dg + dgt.T
        dla_ref[...] = lax.dot_general(tri, dg, _DIMS["tn"], precision=lax.Precision.HIGHEST, preferred_element_type=F32)
        dbe_ref[...] = dbe
        _hosted_wait(comm, cref, (d == 1) & (n == nch - 1))

    rn = lambda d, n: cidx(d, nch - 1 - n)
    tok = lambda w: pl.BlockSpec((C, w), lambda d, n: (rn(d, n), 0))
    otok = lambda w: pl.BlockSpec((None, C, w), lambda d, n: (d, rn(d, n), 0))
    any_spec = pl.BlockSpec(memory_space=pl.ANY)
    c_ins, c_sds, c_sems = (comm.ins, comm.out_sds(), comm.sem_shapes()) if comm is not None else ([], [], [])
    res = pl.pallas_call(
        body, name=name, grid=(2, nch),
        in_specs=[tok(Wd), tok(Wd), tok(Wd), tok(HEAD), tok(HEAD),
                  pl.BlockSpec((None, None, n_heads, HEAD, HEAD), lambda d, n: (d, nch - 1 - n, 0, 0, 0)),
                  pl.BlockSpec((None, None, n_heads, C, C), lambda d, n: (d, nch - 1 - n, 0, 0, 0)), tok(Wd)] + [any_spec] * len(c_ins),
        out_specs=[otok(Wd), otok(Wd), otok(Wd), otok(HEAD), otok(HEAD)] + [any_spec] * len(c_sds),
        out_shape=[jax.ShapeDtypeStruct((2, T, Wd), F32)] * 3 + [jax.ShapeDtypeStruct((2, T, HEAD), F32)] * 2 + c_sds,
        scratch_shapes=[pltpu.VMEM((n_heads, HEAD, HEAD), F32)] + c_sems,
        compiler_params=_cparams(("arbitrary", "arbitrary"), has_side_effects=comm is not None), **_CALL_KW)(
            q, k, v, la, be, states, invs, do, *c_ins)
    return tuple(res[:5]) + (list(res[5:]),)


def _gated_norm(name, o2, p, zblk, g, n_heads):
    _, T, Wd = o2.shape

    def body(o_ref, z_ref, g_ref, y_ref):
        for h in range(n_heads):
            cols = slice(h * HEAD, (h + 1) * HEAD)
            ov = o_ref[0, :, cols] + o_ref[1, :, cols]
            zv = z_ref[:, cols].astype(F32)
            y = ov * lax.rsqrt(jnp.mean(ov * ov, axis=-1, keepdims=True) + EPS) * g_ref[...]
            y_ref[:, cols] = (y * (zv * _sigmoid(zv))).astype(BF16)

    return pl.pallas_call(
        body, name=name, grid=(T // TOK,),
        in_specs=[pl.BlockSpec((2, TOK, Wd), lambda i: (0, i, 0)), pl.BlockSpec((TOK, Wd), lambda i: (i, zblk)),
                  pl.BlockSpec((1, HEAD), lambda i: (0, 0))],
        out_specs=pl.BlockSpec((TOK, Wd), lambda i: (i, 0)), out_shape=jax.ShapeDtypeStruct((T, Wd), BF16),
        compiler_params=_cparams(("parallel",)), **_CALL_KW)(o2, p, g)


def _gated_norm_bwd(name, o2, p, zblk, g, n_heads, dmix, dblk):
    _, T, Wd = o2.shape

    def body(o_ref, z_ref, g_ref, dy_ref, do_ref, dz_ref, dg_ref):
        i = pl.program_id(0)

        @pl.when(i == 0)
        def _():
            dg_ref[...] = jnp.zeros_like(dg_ref)

        dg = jnp.zeros((1, HEAD), F32)
        for h in range(n_heads):
            cols = slice(h * HEAD, (h + 1) * HEAD)
            ov = o_ref[0, :, cols] + o_ref[1, :, cols]
            zv = z_ref[:, cols].astype(F32)
            dy = dy_ref[:, cols].astype(F32)
            r = lax.rsqrt(jnp.mean(ov * ov, axis=-1, keepdims=True) + EPS)
            on = ov * r
            sg = _sigmoid(zv)
            sz = zv * sg
            dz_ref[:, cols] = (dy * (on * g_ref[...]) * (sg * (1.0 + zv * (1.0 - sg)))).astype(BF16)
            dyn = dy * sz
            dg = dg + jnp.sum(dyn * on, axis=0, keepdims=True)
            u = dyn * g_ref[...]
            do_ref[:, cols] = r * (u - on * jnp.mean(u * on, axis=-1, keepdims=True))
        dg_ref[...] += dg

    row = pl.BlockSpec((TOK, Wd), lambda i: (i, 0))
    one = pl.BlockSpec((1, HEAD), lambda i: (0, 0))
    return pl.pallas_call(
        body, name=name, grid=(T // TOK,),
        in_specs=[pl.BlockSpec((2, TOK, Wd), lambda i: (0, i, 0)), pl.BlockSpec((TOK, Wd), lambda i: (i, zblk)), one,
                  pl.BlockSpec((TOK, Wd), lambda i: (i, dblk))],
        out_specs=[row, row, one],
        out_shape=[jax.ShapeDtypeStruct((T, Wd), F32), jax.ShapeDtypeStruct((T, Wd), BF16), jax.ShapeDtypeStruct((1, HEAD), F32)],
        compiler_params=_cparams(("arbitrary",)), **_CALL_KW)(o2, p, g, dmix)


class _Dims:
    def __init__(self, D, seq, ctx, ffn):
        self.D, self.seq, self.ctx, self.ffn = D, seq, ctx, ffn
        self.T = seq + ctx
        self.t_pad = -(-(self.T + 128) // TOK) * TOK
        self.ctx_tiles = ctx // TOK
        nh = D // HEAD
        self.swa_h, self.kv_h, self.dn_h = nh // 4, nh // 8, nh // 2
        self.na_h = nh - self.swa_h - self.dn_h
        self.swa_q, self.swa_kv, self.Wd, self.na = self.swa_h * HEAD, self.kv_h * HEAD, self.dn_h * HEAD, self.na_h * HEAD
        self.n_ab = 4 * self.dn_h
        self.o_ab = self.swa_q + 2 * self.swa_kv + 4 * self.Wd
        self.n_in = self.o_ab + self.n_ab + 3 * self.na
        self.n_main = self.n_in - self.n_ab
        assert ctx % TOK == 0 and seq % TOK == 0 and self.swa_q == 2 * self.swa_kv == self.na and 2 * self.na == self.Wd


def _rope_tables(dm):
    t = jnp.arange(dm.t_pad, dtype=jnp.int32) - dm.ctx
    lat = (t >= 0) & (t < dm.seq)
    row = (t // GRID_W).astype(F32)
    col = (t % GRID_W).astype(F32)
    n_freq = HEAD // 4
    inv = ROPE_THETA ** (-jnp.arange(n_freq, dtype=F32) / n_freq)
    ang = jnp.concatenate([row[:, None] * inv, row[:, None] * inv, col[:, None] * inv, col[:, None] * inv], axis=-1)
    ang = jnp.where(lat[:, None], ang, 0.0)
    return jnp.cos(ang), jnp.sin(ang)


def _bias_indices():
    o = np.arange(NA_KH)[:, None]
    jr = np.arange(NA_KH)[None, :]
    idx_r = jr - o + (NA_KH - 1)
    cols = np.arange(GRID_W)
    idx_c = np.clip(cols[None, :] - cols[:, None], -(NA_KW - 1), NA_KW - 1) + (NA_KW - 1)
    return idx_r, idx_c


def _bias_onehots():
    idx_r, idx_c = _bias_indices()
    sel_r = (idx_r.reshape(-1)[:, None] == np.arange(2 * NA_KH)[None, :]).astype(np.float32)
    sel_c = (np.arange(HEAD)[:, None] == idx_c.reshape(-1)[None, :]).astype(np.float32)
    return jnp.asarray(sel_r), jnp.asarray(sel_c)


def _rpb_pad(rpb):
    return jnp.pad(rpb, ((0, 0), (0, 2 * NA_KH - rpb.shape[1]), (0, HEAD - rpb.shape[2])))


def _bias_table(name, rpb):
    H = rpb.shape[0]
    sel_r, sel_c = _bias_onehots()
    hi = lax.Precision.HIGHEST

    def body(r_ref, sr_ref, sc_ref, o_ref):
        t = jnp.dot(sr_ref[...], r_ref[...], precision=hi, preferred_element_type=F32)
        o_ref[...] = jnp.dot(t, sc_ref[...], precision=hi, preferred_element_type=F32)

    n_r, n_c = sel_r.shape[0], sel_c.shape[1]
    tab = pl.pallas_call(
        body, name=name, grid=(H,),
        in_specs=[pl.BlockSpec((None, 2 * NA_KH, HEAD), lambda h: (h, 0, 0)), pl.BlockSpec(sel_r.shape, lambda h: (0, 0)),
                  pl.BlockSpec(sel_c.shape, lambda h: (0, 0))],
        out_specs=pl.BlockSpec((None, n_r, n_c), lambda h: (h, 0, 0)), out_shape=jax.ShapeDtypeStruct((H, n_r, n_c), F32),
        compiler_params=_cparams(("parallel",)), **_CALL_KW)(_rpb_pad(rpb), sel_r, sel_c)
    tab = tab.reshape(H, NA_KH, NA_KH, GRID_W, GRID_W).transpose(0, 1, 3, 2, 4)
    return tab.reshape(H, NA_KH, GRID_W, NA_KH * GRID_W)


def _bias_table_bwd(name, dbias, rpb_shape):
    H = dbias.shape[0]
    sel_r, sel_c = _bias_onehots()
    hi = lax.Precision.HIGHEST
    d = dbias.reshape(H, NA_KH, GRID_W, NA_KH, GRID_W).transpose(0, 1, 3, 2, 4).reshape(H, NA_KH * NA_KH, GRID_W * GRID_W)

    def body(d_ref, sr_ref, sc_ref, o_ref):
        dt = lax.dot_general(d_ref[...], sc_ref[...], _DIMS["nt"], precision=hi, preferred_element_type=F32)
        o_ref[...] = lax.dot_general(sr_ref[...], dt, _DIMS["tn"], precision=hi, preferred_element_type=F32)

    out = pl.pallas_call(
        body, name=name, grid=(H,),
        in_specs=[pl.BlockSpec((None,) + d.shape[1:], lambda h: (h, 0, 0)), pl.BlockSpec(sel_r.shape, lambda h: (0, 0)),
                  pl.BlockSpec(sel_c.shape, lambda h: (0, 0))],
        out_specs=pl.BlockSpec((None, 2 * NA_KH, HEAD), lambda h: (h, 0, 0)),
        out_shape=jax.ShapeDtypeStruct((H, 2 * NA_KH, HEAD), F32),
        compiler_params=_cparams(("parallel",)), **_CALL_KW)(d, sel_r, sel_c)
    return out[:, :rpb_shape[1], :rpb_shape[2]]


def _lane_row(v):
    v = v.reshape(-1)
    return jnp.pad(v, (0, HEAD - v.shape[0])).reshape(1, HEAD)


def _layer_weights(dm, l, chip, gathered, own):
    g_in, g_out, g_gate, g_up, g_down = [
        [jnp.where(chip == k, o[l], g[k]) for k in range(N_CHIPS)] for g, o in zip(gathered, own)]
    g_out, g_down = jnp.stack(g_out), jnp.stack(g_down)
    w_in = jnp.concatenate(g_in, axis=1)
    w_main = jnp.concatenate([w_in[:, :dm.o_ab], w_in[:, dm.o_ab + dm.n_ab:]], axis=1)
    w_ab = jnp.pad(w_in[:, dm.o_ab:dm.o_ab + dm.n_ab], ((0, 0), (0, HEAD - dm.n_ab)))
    w_out = g_out.reshape(dm.D, dm.D)
    w_out = jnp.concatenate([w_out[dm.swa_q:dm.swa_q + dm.Wd], w_out[:dm.swa_q], w_out[dm.swa_q + dm.Wd:]], axis=0)
    w_gu = jnp.concatenate(g_gate + g_up, axis=1)
    w_down = g_down.reshape(dm.ffn, dm.D)
    return dict(main=w_main, ab=w_ab, out=w_out, gu=w_gu, down=w_down)


def _layer_fwd(dm, x, W, sp, modv, cos, sin, comm=None):
    ct = dm.ctx_tiles
    h = _norm_mod("norm1", x, sp["norm1_g"], modv, 0, ct)
    P = _matmul("in_proj", h, W["main"], "nn")
    Pab = _matmul("in_proj_ab", h, W["ab"], "nn", tn=HEAD)
    one = jnp.ones((1, HEAD), F32)
    qa = _head_prep("swa_q_prep", P, 0, dm.swa_h, sp["swa_q_g"], cos, sin, dm.t_pad, True, True)
    ka = _head_prep("swa_k_prep", P, 2, dm.kv_h, sp["swa_k_g"], cos, sin, dm.t_pad, True, True)
    va = _head_prep("swa_v_prep", P, 3, dm.kv_h, one, cos, sin, dm.t_pad, False, False)
    oa = _attn_fwd("swa_fwd", "swa", qa, ka, va, sp["swa_sink"], None, dm.ctx, dm.seq)
    qn = _head_prep("na_q_prep", P, 10, dm.na_h, sp["na_q_g"], cos, sin, dm.t_pad, True, False)
    kn = _head_prep("na_k_prep", P, 11, dm.na_h, sp["na_k_g"], cos, sin, dm.t_pad, True, False)
    vn = _head_prep("na_v_prep", P, 12, dm.na_h, one, cos, sin, dm.t_pad, False, False)
    no_sink = jnp.full((dm.na_h,), NEG, F32)
    bias = _bias_table("na_bias", sp["na_rpb"])
    oc = _attn_fwd("na_fwd", "na", qn, kn, vn, no_sink, bias, dm.ctx, dm.seq)
    pre = _conv_fwd("dn_conv", P, 1, sp["dn_conv_w"], ct)
    a_row, dt_row = _lane_row(sp["dn_A_log"]), _lane_row(sp["dn_dt_bias"])
    qh, kh, vh, la, be = _gdn_point("dn_point", pre, Pab, a_row, dt_row, dm.dn_h)
    o2, states, invs, hosted = _gdn_scan("dn_scan", qh, kh, vh, la, be, dm.dn_h, dm.ctx, comm)
    ob = _gated_norm("dn_out_norm", o2, P, 4, sp["dn_out_g"], dm.dn_h)
    mix = jnp.concatenate([ob, oa.astype(BF16), oc.astype(BF16)], axis=-1)
    ao = _matmul("out_proj", mix, W["out"], "nn")
    x1 = _resid_gate("resid1", x, ao, modv, 2, ct)
    h2 = _norm_mod("norm2", x1, sp["norm2_g"], modv, 3, ct)
    gu = _matmul("ffn_gate_up", h2, W["gu"], "nn")
    act = _swiglu("ffn_act", gu)
    fo = _matmul("ffn_down", act, W["down"], "nn")
    x2 = _resid_gate("resid2", x1, fo, modv, 5, ct)
    res = dict(x=x, h=h, P=P, Pab=Pab, qa=qa, ka=ka, va=va, qn=qn, kn=kn, vn=vn, bias=bias, no_sink=no_sink, pre=pre,
               a_row=a_row, dt_row=dt_row, qh=qh, kh=kh, vh=vh, la=la, be=be, o2=o2, states=states, invs=invs, mix=mix,
               ao=ao, x1=x1, h2=h2, gu=gu, act=act, fo=fo)
    return x2, res, hosted


def _layer_bwd(dm, dx2, W, sp, modv, cos, sin, r, host=None):
    ct = dm.ctx_tiles
    T, D = dm.T, dm.D
    one = jnp.ones((1, HEAD), F32)
    dfo, dgate2 = _resid_gate_bwd("resid2_bwd", dx2, r["fo"], modv, 5, ct)
    dact = _matmul("ffn_down_dx", dfo, W["down"], "nt")
    dw_down = _matmul("ffn_down_dw", r["act"], dfo, "tn")
    dgu = _swiglu_bwd("ffn_act_bwd", r["gu"], dact)
    dh2 = _matmul("ffn_gate_up_dx", dgu, W["gu"], "nt")
    dw_gu = _matmul("ffn_gate_up_dw", r["h2"], dgu, "tn")
    zero = jnp.zeros((T, D), F32)
    dx1, dn2g, dsh2, dsc2 = _norm_mod_bwd("norm2_bwd", r["x1"], sp["norm2_g"], modv, 3, dh2, zero, dx2, ct)
    dao, dgate1 = _resid_gate_bwd("resid1_bwd", dx1, r["ao"], modv, 2, ct)
    dmix = _matmul("out_proj_dx", dao, W["out"], "nt")
    dw_out = _matmul("out_proj_dw", r["mix"], dao, "tn")
    do_, dz, d_out_g = _gated_norm_bwd("dn_out_norm_bwd", r["o2"], r["P"], 4, sp["dn_out_g"], dm.dn_h, dmix, 0)
    comm = host(dict(out=dw_out, gu=dw_gu, down=dw_down)) if host is not None else None
    dq2, dk2, dv2, dla2, dbe2, hosted = _gdn_scan_bwd("dn_scan_bwd", r["qh"], r["kh"], r["vh"], r["la"], r["be"], r["states"],
                                                      r["invs"], do_, dm.dn_h, dm.ctx, comm)
    dpre, dPab, d_alog, d_dtb = _gdn_point_bwd("dn_point_bwd", r["pre"], r["Pab"], r["a_row"], r["dt_row"], dm.dn_h,
                                               dq2, dk2, dv2, dla2, dbe2)
    dqkv, d_conv = _conv_bwd("dn_conv_bwd", r["P"], 1, sp["dn_conv_w"], dpre, ct)
    dqa, dka, dva, dsink, _ = _attn_bwd("swa_bwd", "swa", r["qa"], r["ka"], r["va"], sp["swa_sink"], None, dmix,
                                        dm.Wd // HEAD, dm.ctx, dm.seq)
    daq, d_swa_q_g = _head_prep_bwd("swa_q_prep_bwd", r["P"], 0, dm.swa_h, sp["swa_q_g"], cos, sin, dqa, True)
    dak, d_swa_k_g = _head_prep_bwd("swa_k_prep_bwd", r["P"], 2, dm.kv_h, sp["swa_k_g"], cos, sin, dka, True)
    dqn, dkn, dvn, _, dbias = _attn_bwd("na_bwd", "na", r["qn"], r["kn"], r["vn"], r["no_sink"], r["bias"], dmix,
                                        (dm.Wd + dm.swa_q) // HEAD, dm.ctx, dm.seq)
    dnq, d_na_q_g = _head_prep_bwd("na_q_prep_bwd", r["P"], 10, dm.na_h, sp["na_q_g"], cos, sin, dqn, False)
    dnk, d_na_k_g = _head_prep_bwd("na_k_prep_bwd", r["P"], 11, dm.na_h, sp["na_k_g"], cos, sin, dkn, False)
    d_rpb = _bias_table_bwd("na_bias_bwd", dbias, sp["na_rpb"].shape)
    dP = jnp.concatenate([daq, dak, dva[:T].astype(BF16), dqkv, dz, dnq, dnk, dvn[:T].astype(BF16)], axis=-1)
    dw_main = _matmul("in_proj_dw", r["h"], dP, "tn")
    dw_ab = _matmul("in_proj_ab_dw", r["h"], dPab, "tn", tn=HEAD)
    dh = _matmul("in_proj_dx", dP, W["main"], "nt")
    dh_b = _matmul("in_proj_ab_dx", dPab, W["ab"], "nt")
    dx, dn1g, dsh1, dsc1 = _norm_mod_bwd("norm1_bwd", r["x"], sp["norm1_g"], modv, 0, dh, dh_b, dx1, ct)
    dmodv = jnp.concatenate([dsh1, dsc1, dgate1, dsh2, dsc2, dgate2], axis=1)
    big = dict(main=dw_main, ab=dw_ab, out=dw_out, gu=dw_gu, down=dw_down)
    small = dict(norm1_g=dn1g[0], norm2_g=dn2g[0], swa_q_g=d_swa_q_g[0], swa_k_g=d_swa_k_g[0], swa_sink=dsink[:, 0, 0],
                 dn_conv_w=d_conv[:DN_CONV], dn_A_log=d_alog[0, :2 * dm.dn_h].reshape(2, dm.dn_h),
                 dn_dt_bias=d_dtb[0, :2 * dm.dn_h].reshape(2, dm.dn_h), dn_out_g=d_out_g[0], na_q_g=d_na_q_g[0],
                 na_k_g=d_na_k_g[0], na_rpb=d_rpb)
    return dx, big, small, dmodv, hosted


def _cols(w):
    return w.reshape(w.shape[0], N_CHIPS, -1).transpose(1, 0, 2)


def _rows(w):
    return w.reshape(N_CHIPS, -1, w.shape[1])


def _chunks_in(dm, bigs):
    g = [_cols(jnp.concatenate([b["main"][:, :dm.o_ab], b["ab"][:, :dm.n_ab], b["main"][:, dm.o_ab:]], axis=1)) for b in bigs]
    return jnp.stack(g, axis=1).astype(BF16)


def _chunks_rest(dm, bigs):
    g_out = [_rows(jnp.concatenate([b["out"][dm.Wd:dm.Wd + dm.swa_q], b["out"][:dm.Wd], b["out"][dm.Wd + dm.swa_q:]], axis=0))
             for b in bigs]
    g_gate = [_cols(b["gu"][:, :dm.ffn]) for b in bigs]
    g_up = [_cols(b["gu"][:, dm.ffn:]) for b in bigs]
    g_down = [_rows(b["down"]) for b in bigs]
    return [jnp.stack(g, axis=1).astype(BF16) for g in (g_out, g_gate, g_up, g_down)]


SMALL = ("norm1_g", "norm2_g", "swa_q_g", "swa_k_g", "swa_sink", "dn_conv_w", "dn_A_log", "dn_dt_bias", "dn_out_g",
         "na_q_g", "na_k_g", "na_rpb")


def _pack(arrs):
    flat = jnp.concatenate([a.reshape(-1).astype(F32) for a in arrs])
    n = flat.shape[0]
    rows = -(-n // (8 * HEAD)) * 8
    return jnp.pad(flat, (0, rows * HEAD - n)).reshape(rows, HEAD)


def _unpack(packed, like):
    flat = packed.reshape(-1)
    out, o = [], 0
    for a in like:
        out.append(flat[o:o + a.size].reshape(a.shape))
        o += a.size
    return out


def _sum_devices(name, g, which):
    _, R, _ = g.shape

    def body(g_ref, o_ref):
        acc = g_ref[which[0]]
        for b in which[1:]:
            acc = acc + g_ref[b]
        o_ref[...] = acc

    return pl.pallas_call(
        body, name=name, grid=(R // 8,), in_specs=[pl.BlockSpec((N_DEV, 8, HEAD), lambda i: (0, i, 0))],
        out_specs=pl.BlockSpec((8, HEAD), lambda i: (i, 0)), out_shape=jax.ShapeDtypeStruct((R, HEAD), F32),
        compiler_params=_cparams(("parallel",)), **_CALL_KW)(g)


def _silu_rows(name, c_rows):
    return _elementwise(name, lambda c: (c * _sigmoid(c),), [c_rows], [BF16])[0]


def _ada_cotangent(name, dm_all, b_ada_shape):
    _, L, _, N6 = dm_all.shape
    tn = _pick(N6, (1024, 512, 256, 128))

    def body(d_ref, o_ref, b_ref):
        csum = d_ref[0, 0:1, :]
        for b in range(1, N_DEV):
            csum = csum + d_ref[b, 0:1, :]
        tot = csum
        for b in range(N_DEV):
            o_ref[b:b + 1, :] = d_ref[b, 1:2, :]
            tot = tot + d_ref[b, 1:2, :]
        first = lax.broadcasted_iota(jnp.int32, (8, tn), 0) == 0
        o_ref[N_DEV:, :] = jnp.where(first, jnp.broadcast_to(csum, (8, tn)), 0.0)
        b_ref[...] = jnp.broadcast_to(tot, (8, tn))

    return pl.pallas_call(
        body, name=name, grid=(L, N6 // tn), in_specs=[pl.BlockSpec((N_DEV, None, 2, tn), lambda l, j: (0, l, 0, j))],
        out_specs=[pl.BlockSpec((None, 16, tn), lambda l, j: (l, 0, j)), pl.BlockSpec((None, 8, tn), lambda l, j: (l, 0, j))],
        out_shape=[jax.ShapeDtypeStruct((L, 16, N6), F32), jax.ShapeDtypeStruct((L, 8, N6), F32)],
        compiler_params=_cparams(("parallel", "parallel")), **_CALL_KW)(dm_all)


def kernel(x, c, ctx, c_ctx, w_ada, b_ada, norm1_g, norm2_g, w_in, swa_q_g, swa_k_g, swa_sink, dn_conv_w, dn_A_log, dn_dt_bias, dn_out_g, na_q_g, na_k_g, na_rpb, w_out, w_gate, w_up, w_down, loss_target, m_c_ctx, m_w_ada, m_b_ada, m_norm1_g, m_norm2_g, m_w_in, m_swa_q_g, m_swa_k_g, m_swa_sink, m_dn_conv_w, m_dn_A_log, m_dn_dt_bias, m_dn_out_g, m_na_q_g, m_na_k_g, m_na_rpb, m_w_out, m_w_gate, m_w_up, m_w_down, v_c_ctx, v_w_ada, v_b_ada, v_norm1_g, v_norm2_g, v_w_in, v_swa_q_g, v_swa_k_g, v_swa_sink, v_dn_conv_w, v_dn_A_log, v_dn_dt_bias, v_dn_out_g, v_na_q_g, v_na_k_g, v_na_rpb, v_w_out, v_w_gate, v_w_up, v_w_down):
    L = w_in.shape[0]
    D, seq, n_ctx = x.shape[-1], x.shape[1], ctx.shape[1]
    dm = _Dims(D, seq, n_ctx, w_gate.shape[-1] * N_CHIPS)
    xi, yi, ci = _axes()
    chip = 2 * xi + yi
    dev = 4 * xi + 2 * yi + ci
    n6 = 6 * D
    n6s = n6 // N_CHIPS

    shards = [_elementwise(f"cast_{n}", lambda w: (w,), [w], [BF16])[0]
              for n, w in (("w_in", w_in), ("w_out", w_out), ("w_gate", w_gate), ("w_up", w_up), ("w_down", w_down))]
    assert L == 2
    gathered0 = _gather_d2d("gather_w0_d2d", _exchange("gather_w0_ici", _gather_ici_comm(shards, 0)))
    Ws = [_layer_weights(dm, 0, chip, gathered0, shards), None]
    conv_all = _allgather_small("gather_conv_w", _pack([dn_conv_w]))
    conv_full = jnp.concatenate([_unpack(conv_all[2 * k], [dn_conv_w])[0] for k in range(N_CHIPS)], axis=-1)

    c_all = _allgather_small("gather_c", _pack([c]))
    c_rows = jnp.concatenate([c_all[:, :D // HEAD].reshape(N_DEV, D), c_ctx[None], jnp.zeros((16 - N_DEV - 1, D), F32)], axis=0)
    a_rows = _silu_rows("ada_silu", c_rows)
    b_sh = lax.dynamic_slice_in_dim(b_ada, chip * n6s, n6s, axis=1)
    mod_sh = [_matmul(f"ada_mod{l}", a_rows, w_ada[l], "nn", tm=16) for l in range(L)]
    mod_all = _allgather_small("gather_mod", _pack(mod_sh))
    mods = []
    for l in range(L):
        per_chip = [_unpack(mod_all[2 * k], mod_sh)[l] for k in range(N_CHIPS)]
        mods.append(jnp.concatenate(per_chip, axis=1))
    modvs = []
    for l in range(L):
        rows = jnp.stack([mods[l][N_DEV], lax.dynamic_index_in_dim(mods[l], dev, 0, keepdims=False)])
        modvs.append(_elementwise(f"ada_bias{l}", lambda m, b: (m + b,), [rows, jnp.broadcast_to(b_ada[l][None], (2, n6))], [F32])[0]
                     .reshape(2, 6, D))

    cos, sin = _rope_tables(dm)
    sps = [dict(norm1_g=norm1_g[l][None], norm2_g=norm2_g[l][None], swa_q_g=swa_q_g[l][None], swa_k_g=swa_k_g[l][None],
                swa_sink=swa_sink[l], dn_conv_w=conv_full[l], dn_A_log=dn_A_log[l], dn_dt_bias=dn_dt_bias[l],
                dn_out_g=dn_out_g[l][None], na_q_g=na_q_g[l][None], na_k_g=na_k_g[l][None], na_rpb=na_rpb[l]) for l in range(L)]
    xs = jnp.concatenate([ctx[0], x[0]], axis=0)
    ress = [None] * L
    xs, ress[0], gathered1 = _layer_fwd(dm, xs, Ws[0], sps[0], modvs[0], cos, sin, _gather_ici_comm(shards, 1))
    Ws[1] = _layer_weights(dm, 1, chip, _gather_d2d("gather_w1_d2d", gathered1), shards)
    xs, ress[1], _ = _layer_fwd(dm, xs, Ws[1], sps[1], modvs[1], cos, sin)
    loss_blk, dxs = _loss_and_grad("loss", xs, loss_target[0], dm.ctx_tiles)
    loss = lax.psum(loss_blk[0, 0], ("x", "y", "c"))

    bigs, smalls, dmodvs = [None] * L, [None] * L, [None] * L
    dxs, bigs[1], smalls[1], dmodvs[1], _ = _layer_bwd(dm, dxs, Ws[1], sps[1], modvs[1], cos, sin, ress[1])
    pairs = {}

    def host(big0):
        pairs["a"] = _reduce_pre("a", [_chunks_in(dm, [bigs[1]])] + _chunks_rest(dm, [big0, bigs[1]]))
        return _reduce_ici_comm(pairs["a"])

    dxs, bigs[0], smalls[0], dmodvs[0], got_a = _layer_bwd(dm, dxs, Ws[0], sps[0], modvs[0], cos, sin, ress[0], host)
    g_in1, g_out, g_gate, g_up, g_down = _reduce_post("a", pairs["a"], got_a)
    pair_b = _reduce_pre("b", [_chunks_in(dm, [bigs[0]])])
    (g_in0,) = _reduce_post("b", pair_b, _exchange("reduce_b_ici", _reduce_ici_comm(pair_b)))
    g_in = jnp.concatenate([g_in0, g_in1], axis=0)
    grad_x = dxs[n_ctx:][None]

    dm_mine = jnp.stack([d.reshape(2, n6) for d in dmodvs])
    dm_all = _allgather_small("gather_dmod", _pack([dm_mine]))
    dm_all = jnp.stack([_unpack(dm_all[b], [dm_mine])[0] for b in range(N_DEV)])
    dm_rows, d_b_ada = _ada_cotangent("ada_cot", dm_all, b_ada.shape)
    dm_sh = lax.dynamic_slice_in_dim(dm_rows, chip * n6s, n6s, axis=2).astype(BF16)
    g_w_ada = jnp.stack([_matmul(f"ada_dw{l}", a_rows, dm_sh[l], "tn") for l in range(L)])
    dc_part = [_matmul(f"ada_dc{l}", dm_sh[l], w_ada[l], "nt", tm=16) for l in range(L)]
    dc_mine = dc_part[0][N_DEV]
    for l in range(1, L):
        dc_mine = dc_mine + dc_part[l][N_DEV]

    small_list = [jnp.stack([smalls[l][n] for l in range(L)]) for n in SMALL]
    sm_all = _allgather_small("gather_small", _pack(small_list + [dc_mine]))
    sm_sum = _sum_devices("sum_small", sm_all, tuple(range(N_DEV)))
    dc_sum = _sum_devices("sum_dc", sm_all, tuple(range(0, N_DEV, 2)))
    g_small = dict(zip(SMALL, _unpack(sm_sum, small_list)))
    dcs = _unpack(dc_sum, small_list + [dc_mine])[-1]
    def silu_bwd(d, cc):
        s = _sigmoid(cc)
        return (d * (s * (1.0 + cc * (1.0 - s))),)

    g_c_ctx = _elementwise("c_ctx_silu_bwd", silu_bwd, [dcs.reshape(-1, HEAD), c_ctx.reshape(-1, HEAD)], [F32])[0]
    wd3 = dn_conv_w.shape[-1]
    g_small["dn_conv_w"] = lax.dynamic_slice_in_dim(g_small["dn_conv_w"], chip * wd3, wd3, axis=2)

    grads = dict(g_small, c_ctx=g_c_ctx, w_ada=g_w_ada, b_ada=d_b_ada[:, 0], w_in=g_in, w_out=g_out, w_gate=g_gate, w_up=g_up,
                 w_down=g_down)
    weights = dict(c_ctx=c_ctx, w_ada=w_ada, b_ada=b_ada, norm1_g=norm1_g, norm2_g=norm2_g, w_in=w_in, swa_q_g=swa_q_g,
                   swa_k_g=swa_k_g, swa_sink=swa_sink, dn_conv_w=dn_conv_w, dn_A_log=dn_A_log, dn_dt_bias=dn_dt_bias,
                   dn_out_g=dn_out_g, na_q_g=na_q_g, na_k_g=na_k_g, na_rpb=na_rpb, w_out=w_out, w_gate=w_gate, w_up=w_up,
                   w_down=w_down)
    ms = dict(c_ctx=m_c_ctx, w_ada=m_w_ada, b_ada=m_b_ada, norm1_g=m_norm1_g, norm2_g=m_norm2_g, w_in=m_w_in, swa_q_g=m_swa_q_g,
              swa_k_g=m_swa_k_g, swa_sink=m_swa_sink, dn_conv_w=m_dn_conv_w, dn_A_log=m_dn_A_log, dn_dt_bias=m_dn_dt_bias,
              dn_out_g=m_dn_out_g, na_q_g=m_na_q_g, na_k_g=m_na_k_g, na_rpb=m_na_rpb, w_out=m_w_out, w_gate=m_w_gate, w_up=m_w_up,
              w_down=m_w_down)
    vs = dict(c_ctx=v_c_ctx, w_ada=v_w_ada, b_ada=v_b_ada, norm1_g=v_norm1_g, norm2_g=v_norm2_g, w_in=v_w_in, swa_q_g=v_swa_q_g,
              swa_k_g=v_swa_k_g, swa_sink=v_swa_sink, dn_conv_w=v_dn_conv_w, dn_A_log=v_dn_A_log, dn_dt_bias=v_dn_dt_bias,
              dn_out_g=v_dn_out_g, na_q_g=v_na_q_g, na_k_g=v_na_k_g, na_rpb=v_na_rpb, w_out=v_w_out, w_gate=v_w_gate, w_up=v_w_up,
              w_down=v_w_down)
    order = ("c_ctx", "w_ada", "b_ada", "norm1_g", "norm2_g", "w_in", "swa_q_g", "swa_k_g", "swa_sink", "dn_conv_w", "dn_A_log",
             "dn_dt_bias", "dn_out_g", "na_q_g", "na_k_g", "na_rpb", "w_out", "w_gate", "w_up", "w_down")
    big_names = ("w_ada", "w_in", "w_out", "w_gate", "w_up", "w_down")
    grads = {n: grads[n].reshape(weights[n].shape) for n in order}
    delta, new_m, new_v = {}, {}, {}
    for n in big_names:
        delta[n], new_m[n], new_v[n] = _adamw(f"adamw_{n}", weights[n], grads[n], ms[n], vs[n])
    small_names = [n for n in order if n not in big_names]
    packed = [_pack([d[n] for n in small_names]) for d in (weights, grads, ms, vs)]
    outs = _adamw("adamw_small", *packed)
    like = [weights[n] for n in small_names]
    for d, o in zip((delta, new_m, new_v), outs):
        d.update(dict(zip(small_names, _unpack(o, like))))
    return (loss, grad_x, *[grads[n] for n in order], *[delta[n] for n in order], *[new_m[n] for n in order],
            *[new_v[n] for n in order])
```

```python
import functools
import math

import jax
import jax.numpy as jnp
import numpy as np
from jax import lax
from jax.experimental import pallas as pl
from jax.experimental.pallas import tpu as pltpu

F32, BF16 = jnp.float32, jnp.bfloat16
MESH = pl.DeviceIdType.MESH

GRID_W = 64
HEAD = 128
SWA_WINDOW = 128
DN_CONV = 5
DN_CHUNK = 64
NA_KH, NA_KW = 8, 16
ROPE_THETA = 10000.0
EPS = 1e-6
ADAM_LR, ADAM_B1, ADAM_B2, ADAM_EPS, ADAM_WD, ADAM_STEP = 0.001, 0.9, 0.999, 1e-08, 0.01, 10
N_CHIPS = 4
N_DEV = 8
TOK = 256
VMEM_LIMIT = 56 * 2 ** 20

_CALL_KW = {}


def _cparams(sem=None, **kw):
    if sem is not None:
        kw["dimension_semantics"] = sem
    return pltpu.CompilerParams(vmem_limit_bytes=VMEM_LIMIT, **kw)


def _pick(n, cands):
    for cnd in cands:
        if n % cnd == 0:
            return cnd
    raise ValueError(f"no tile for {n} in {cands}")


def _axes():
    return lax.axis_index("x"), lax.axis_index("y"), lax.axis_index("c")


def _matmul(name, a, b, kind, out_dtype=F32, tm=None, tn=None, tk=None):
    if kind == "nn":
        (M, K), (K2, N) = a.shape, b.shape
    elif kind == "nt":
        (M, K), (N, K2) = a.shape, b.shape
    else:
        (K, M), (K2, N) = a.shape, b.shape
    assert K == K2, (name, a.shape, b.shape)
    tm = tm or _pick(M, (1024, 512, 256, 128) if kind == "tn" else (1088, 1024, 704, 512, 256, 128, 64, 32, 16, 8))
    tn = tn or _pick(N, (512, 256, 128))
    tk = tk or (K if K <= 4352 else _pick(K, (3328, 2816, 2048, 1024, 512)))
    nk = K // tk
    dims = {"nn": (((1,), (0,)), ((), ())), "nt": (((1,), (1,)), ((), ())), "tn": (((0,), (0,)), ((), ()))}[kind]

    def body(a_ref, b_ref, o_ref, *scr):
        part = lax.dot_general(a_ref[...].astype(BF16), b_ref[...].astype(BF16), dims, preferred_element_type=F32)
        if nk == 1:
            o_ref[...] = part.astype(o_ref.dtype)
        else:
            acc = scr[0]
            k = pl.program_id(2)

            @pl.when(k == 0)
            def _():
                acc[...] = part

            @pl.when(k > 0)
            def _():
                acc[...] += part

            @pl.when(k == nk - 1)
            def _():
                o_ref[...] = acc[...].astype(o_ref.dtype)

    a_spec = {"nn": pl.BlockSpec((tm, tk), lambda i, j, k: (i, k)), "nt": pl.BlockSpec((tm, tk), lambda i, j, k: (i, k)),
              "tn": pl.BlockSpec((tk, tm), lambda i, j, k: (k, i))}[kind]
    b_spec = {"nn": pl.BlockSpec((tk, tn), lambda i, j, k: (k, j)), "nt": pl.BlockSpec((tn, tk), lambda i, j, k: (j, k)),
              "tn": pl.BlockSpec((tk, tn), lambda i, j, k: (k, j))}[kind]
    return pl.pallas_call(
        body, name=name, grid=(M // tm, N // tn, nk), in_specs=[a_spec, b_spec],
        out_specs=pl.BlockSpec((tm, tn), lambda i, j, k: (i, j)), out_shape=jax.ShapeDtypeStruct((M, N), out_dtype),
        scratch_shapes=[pltpu.VMEM((tm, tn), F32)] if nk > 1 else [],
        compiler_params=_cparams(("parallel", "parallel", "arbitrary")), **_CALL_KW)(a, b)


class _Comm:
    def __init__(self, ins, out_shapes, plan, n_local, n_remote, aliases=None):
        self.ins, self.out_shapes, self.plan = list(ins), list(out_shapes), plan
        self.n_local, self.n_remote, self.aliases = n_local, n_remote, aliases or {}

    def sem_shapes(self):
        return [pltpu.SemaphoreType.DMA((max(self.n_remote, 1),)), pltpu.SemaphoreType.DMA((max(self.n_remote, 1),)),
                pltpu.SemaphoreType.DMA((max(self.n_local, 1),))]

    def out_sds(self):
        return [jax.ShapeDtypeStruct(s, d) for s, d in self.out_shapes]

    def copies(self, in_refs, out_refs, sems):
        send_sems, recv_sems, loc_sems = sems
        x, y, c = _axes()
        local, remote = self.plan(x, y, c, in_refs, out_refs)
        assert len(local) == self.n_local and len(remote) == self.n_remote, (len(local), len(remote))
        lcs = [pltpu.make_async_copy(s, d, loc_sems.at[i]) for i, (s, d) in enumerate(local)]
        rcs = [pltpu.make_async_remote_copy(src_ref=s, dst_ref=d, send_sem=send_sems.at[i], recv_sem=recv_sems.at[i],
                                            device_id=dev, device_id_type=MESH) for i, (s, d, dev) in enumerate(remote)]
        return lcs + rcs


def _exchange(name, comm):
    n_in, n_out = len(comm.ins), len(comm.out_shapes)

    def body(*refs):
        cps = comm.copies(refs[:n_in], refs[n_in:n_in + n_out], refs[n_in + n_out:])
        for cp in cps:
            cp.start()
        for cp in cps:
            cp.wait()

    any_spec = pl.BlockSpec(memory_space=pl.ANY)
    return pl.pallas_call(
        body, name=name, in_specs=[any_spec] * n_in, out_specs=[any_spec] * n_out, out_shape=comm.out_sds(),
        scratch_shapes=comm.sem_shapes(), input_output_aliases=comm.aliases,
        compiler_params=pltpu.CompilerParams(has_side_effects=True), **_CALL_KW)(*comm.ins)


def _chip_of(k):
    return k // 2, k % 2


def _gather_ici_comm(shards, layer):
    def plan(x, y, c, ins, outs):
        me = 2 * x + y
        remote = []
        for w, g in zip(ins, outs):
            half = w.shape[1] // 2
            rows = pl.ds(c * half, half)
            for j in (1, 2, 3):
                px, py = _chip_of(me ^ j)
                remote.append((w.at[layer, rows], g.at[me, rows], (px, py, c)))
        return [], remote

    return _Comm(shards, [((N_CHIPS,) + w.shape[1:], w.dtype) for w in shards], plan, 0, 3 * len(shards))


def _gather_d2d(name, gath):
    def plan(x, y, c, ins, outs):
        me = 2 * x + y
        remote = []
        for g in outs:
            half = g.shape[1] // 2
            rows = pl.ds(c * half, half)
            for j in (1, 2, 3):
                remote.append((g.at[me ^ j, rows], g.at[me ^ j, rows], (x, y, 1 - c)))
        return [], remote

    n = len(gath)
    return _exchange(name, _Comm(gath, [(g.shape, g.dtype) for g in gath], plan, 0, 3 * n, aliases={i: i for i in range(n)}))


def _elementwise(name, fn, ins, out_dtypes, block_rows=None, n_out=None):
    shape = ins[0].shape
    lead, (R, C) = shape[:-2], shape[-2:]
    budget = (16 * 2 ** 20) // (8 * (len(ins) + len(out_dtypes)) * (-(-C // 128) * 128))
    br = block_rows or _pick(R, [r for r in (512, 256, 128, 352, 64, 32, 16, 8) if r <= max(budget, 8)] + [R])
    nl = len(lead)

    def body(*refs):
        outs = fn(*[r[...] for r in refs[:len(ins)]])
        for r, o in zip(refs[len(ins):], outs):
            r[...] = o.astype(r.dtype)

    blk = (None,) * nl + (br, C)
    spec = pl.BlockSpec(blk, lambda *g: tuple(g[:nl]) + (g[nl], 0))
    return pl.pallas_call(
        body, name=name, grid=tuple(lead) + (R // br,), in_specs=[spec] * len(ins), out_specs=[spec] * len(out_dtypes),
        out_shape=[jax.ShapeDtypeStruct(shape, d) for d in out_dtypes],
        compiler_params=_cparams(("parallel",) * (nl + 1)), **_CALL_KW)(*ins)


def _reduce_pre(tag, parts):
    n = len(parts)

    def plan_a(x, y, c, ins, outs):
        remote = []
        for p, r in zip(ins, outs):
            half = p.shape[2] // 2
            remote.append((p.at[:, :, pl.ds((1 - c) * half, half)], r, (x, y, 1 - c)))
        return [], remote

    halves = [((p.shape[0], p.shape[1], p.shape[2] // 2, p.shape[3]), p.dtype) for p in parts]
    got = _exchange(f"reduce_{tag}_d2d", _Comm(parts, halves, plan_a, 0, n))

    c = lax.axis_index("c")
    pair = []
    for idx, (p, r) in enumerate(zip(parts, got)):
        half = p.shape[2] // 2
        br = _pick(half, (512, 256, 352, 128, 64, 32, 16))
        nb = half // br

        def body(c_ref, p_ref, r_ref, o_ref):
            o_ref[...] = (p_ref[...].astype(F32) + r_ref[...].astype(F32)).astype(o_ref.dtype)

        blk = (None, None, br, p.shape[3])
        pair.append(pl.pallas_call(
            body, name=f"reduce_{tag}_pair{idx}",
            grid_spec=pltpu.PrefetchScalarGridSpec(
                num_scalar_prefetch=1, grid=(N_CHIPS, p.shape[1], nb),
                in_specs=[pl.BlockSpec(blk, lambda k, l, i, cr, nb=nb: (k, l, cr[0] * nb + i, 0)),
                          pl.BlockSpec(blk, lambda k, l, i, cr: (k, l, i, 0))],
                out_specs=pl.BlockSpec(blk, lambda k, l, i, cr: (k, l, i, 0))),
            out_shape=jax.ShapeDtypeStruct(r.shape, BF16),
            compiler_params=_cparams(("parallel",) * 3), **_CALL_KW)(jnp.reshape(c, (1,)).astype(jnp.int32), p, r))
    return pair


def _reduce_ici_comm(pair):
    def plan_b(x, y, c, ins, outs):
        me = 2 * x + y
        remote = []
        for p, r in zip(ins, outs):
            for j in (1, 2, 3):
                px, py = _chip_of(me ^ j)
                remote.append((p.at[me ^ j], r.at[me], (px, py, c)))
        return [], remote

    return _Comm(pair, [(p.shape, p.dtype) for p in pair], plan_b, 0, 3 * len(pair))


def _reduce_post(tag, pair, got):
    n = len(pair)
    c = lax.axis_index("c")
    me_chip = 2 * lax.axis_index("x") + lax.axis_index("y")

    sums = []
    for idx, r in enumerate(got):
        _, L, half, C = r.shape
        br = _pick(half, (512, 256, 352, 128, 64, 32, 16))
        nb = half // br

        def body(c_ref, p_ref, r_ref, o_ref):
            me = c_ref[1]
            acc = None
            for k in range(N_CHIPS):
                term = jnp.where(me == k, p_ref[k], r_ref[k]).astype(F32)
                acc = term if acc is None else acc + term
            o_ref[...] = acc

        blk4 = pl.BlockSpec((N_CHIPS, None, br, C), lambda l, i, cr: (0, l, i, 0))
        sums.append(pl.pallas_call(
            body, name=f"reduce_{tag}_sum{idx}",
            grid_spec=pltpu.PrefetchScalarGridSpec(
                num_scalar_prefetch=1, grid=(L, nb), in_specs=[blk4, blk4],
                out_specs=pl.BlockSpec((None, br, C), lambda l, i, cr, nb=nb: (l, cr[0] * nb + i, 0))),
            out_shape=jax.ShapeDtypeStruct((L, 2 * half, C), F32),
            compiler_params=_cparams(("parallel",) * 2), **_CALL_KW)(jnp.stack([c, me_chip]).astype(jnp.int32), pair[idx], r))

    def plan_c(x, y, c, ins, outs):
        remote = []
        for f in outs:
            half = f.shape[1] // 2
            rows = pl.ds(c * half, half)
            remote.append((f.at[:, rows], f.at[:, rows], (x, y, 1 - c)))
        return [], remote

    return _exchange(f"reduce_{tag}_bcast", _Comm(sums, [(s.shape, F32) for s in sums], plan_c, 0, n, aliases={i: i for i in range(n)}))


def _adamw_math(w, g, m, v):
    m = ADAM_B1 * m + (1.0 - ADAM_B1) * g
    v = ADAM_B2 * v + (1.0 - ADAM_B2) * (g * g)
    m_hat = m / (1.0 - ADAM_B1 ** ADAM_STEP)
    v_hat = v / (1.0 - ADAM_B2 ** ADAM_STEP)
    delta = -ADAM_LR * (m_hat / (jnp.sqrt(v_hat) + ADAM_EPS) + ADAM_WD * w)
    return delta, m, v


def _adamw(name, w, g, m, v):
    return _elementwise(name, _adamw_math, [w, g, m, v], [F32, F32, F32])


def _allgather_small(name, v):
    def plan(x, y, c, ins, outs):
        me = 4 * x + 2 * y + c
        (src,), (dst,) = ins, outs
        remote = []
        for j in range(1, N_DEV):
            p = me ^ j
            remote.append((src, dst.at[me], (p // 4, (p // 2) % 2, p % 2)))
        return [(src, dst.at[me])], remote

    return _exchange(name, _Comm([v], [((N_DEV,) + v.shape, v.dtype)], plan, 1, N_DEV - 1))[0]


def _seg_spec(rows, D, ctx_tiles):
    return pl.BlockSpec((None, rows, D), lambda i: (jnp.minimum(i // ctx_tiles, 1), 0, 0))


def _norm_mod(name, x, g, modv, r0, ctx_tiles):
    T, D = x.shape

    def body(x_ref, g_ref, m_ref, o_ref):
        xv = x_ref[...]
        r = lax.rsqrt(jnp.mean(xv * xv, axis=-1, keepdims=True) + EPS)
        y = xv * r * g_ref[...]
        o_ref[...] = (y * (1.0 + m_ref[r0 + 1:r0 + 2, :]) + m_ref[r0:r0 + 1, :]).astype(BF16)

    row = pl.BlockSpec((TOK, D), lambda i: (i, 0))
    return pl.pallas_call(
        body, name=name, grid=(T // TOK,), in_specs=[row, pl.BlockSpec((1, D), lambda i: (0, 0)), _seg_spec(6, D, ctx_tiles)],
        out_specs=row, out_shape=jax.ShapeDtypeStruct((T, D), BF16), compiler_params=_cparams(("parallel",)), **_CALL_KW)(x, g, modv)


def _norm_mod_bwd(name, x, g, modv, r0, dh, dh_b, dres, ctx_tiles):
    T, D = x.shape

    def body(x_ref, g_ref, m_ref, dh_ref, dhb_ref, dres_ref, dx_ref, dg_ref, dsh_ref, dsc_ref):
        i = pl.program_id(0)
        xv = x_ref[...]
        r = lax.rsqrt(jnp.mean(xv * xv, axis=-1, keepdims=True) + EPS)
        xn = xv * r
        y = xn * g_ref[...]
        dhv = dh_ref[...] + dhb_ref[...]

        @pl.when(i == 0)
        def _():
            dg_ref[...] = jnp.zeros_like(dg_ref)

        @pl.when((i == 0) | (i == ctx_tiles))
        def _():
            dsh_ref[...] = jnp.zeros_like(dsh_ref)
            dsc_ref[...] = jnp.zeros_like(dsc_ref)

        dsh_ref[...] += jnp.sum(dhv, axis=0, keepdims=True)
        dsc_ref[...] += jnp.sum(dhv * y, axis=0, keepdims=True)
        dy = dhv * (1.0 + m_ref[r0 + 1:r0 + 2, :])
        dg_ref[...] += jnp.sum(dy * xn, axis=0, keepdims=True)
        u = dy * g_ref[...]
        dx_ref[...] = dres_ref[...] + r * (u - xn * jnp.mean(u * xn, axis=-1, keepdims=True))

    row = pl.BlockSpec((TOK, D), lambda i: (i, 0))
    one = pl.BlockSpec((1, D), lambda i: (0, 0))
    return pl.pallas_call(
        body, name=name, grid=(T // TOK,), in_specs=[row, one, _seg_spec(6, D, ctx_tiles), row, row, row],
        out_specs=[row, one, _seg_spec(1, D, ctx_tiles), _seg_spec(1, D, ctx_tiles)],
        out_shape=[jax.ShapeDtypeStruct((T, D), F32), jax.ShapeDtypeStruct((1, D), F32),
                   jax.ShapeDtypeStruct((2, 1, D), F32), jax.ShapeDtypeStruct((2, 1, D), F32)],
        compiler_params=_cparams(("arbitrary",)), **_CALL_KW)(x, g, modv, dh, dh_b, dres)


def _resid_gate(name, x, y, modv, r, ctx_tiles):
    T, D = x.shape

    def body(x_ref, y_ref, m_ref, o_ref):
        o_ref[...] = x_ref[...] + m_ref[r:r + 1, :] * y_ref[...]

    row = pl.BlockSpec((TOK, D), lambda i: (i, 0))
    return pl.pallas_call(
        body, name=name, grid=(T // TOK,), in_specs=[row, row, _seg_spec(6, D, ctx_tiles)], out_specs=row,
        out_shape=jax.ShapeDtypeStruct((T, D), F32), compiler_params=_cparams(("parallel",)), **_CALL_KW)(x, y, modv)


def _resid_gate_bwd(name, dx, y, modv, r, ctx_tiles):
    T, D = dx.shape

    def body(dx_ref, y_ref, m_ref, dy_ref, dgt_ref):
        i = pl.program_id(0)

        @pl.when((i == 0) | (i == ctx_tiles))
        def _():
            dgt_ref[...] = jnp.zeros_like(dgt_ref)

        dxv = dx_ref[...]
        dgt_ref[...] += jnp.sum(dxv * y_ref[...], axis=0, keepdims=True)
        dy_ref[...] = (dxv * m_ref[r:r + 1, :]).astype(BF16)

    row = pl.BlockSpec((TOK, D), lambda i: (i, 0))
    return pl.pallas_call(
        body, name=name, grid=(T // TOK,), in_specs=[row, row, _seg_spec(6, D, ctx_tiles)],
        out_specs=[row, _seg_spec(1, D, ctx_tiles)],
        out_shape=[jax.ShapeDtypeStruct((T, D), BF16), jax.ShapeDtypeStruct((2, 1, D), F32)],
        compiler_params=_cparams(("arbitrary",)), **_CALL_KW)(dx, y, modv)


def _sigmoid(x):
    return 1.0 / (1.0 + jnp.exp(-x))


SWI_ROWS = 128


def _swiglu(name, gu):
    T, F2 = gu.shape
    F = F2 // 2

    def body(gu_ref, o_ref):
        g, u = gu_ref[:, :F], gu_ref[:, F:]
        o_ref[...] = ((g * _sigmoid(g)) * u).astype(BF16)

    return pl.pallas_call(
        body, name=name, grid=(T // SWI_ROWS,), in_specs=[pl.BlockSpec((SWI_ROWS, F2), lambda i: (i, 0))],
        out_specs=pl.BlockSpec((SWI_ROWS, F), lambda i: (i, 0)), out_shape=jax.ShapeDtypeStruct((T, F), BF16),
        compiler_params=_cparams(("parallel",)), **_CALL_KW)(gu)


def _swiglu_bwd(name, gu, dact):
    T, F2 = gu.shape
    F = F2 // 2

    def body(gu_ref, d_ref, o_ref):
        g, u, d = gu_ref[:, :F], gu_ref[:, F:], d_ref[...]
        s = _sigmoid(g)
        o_ref[:, :F] = (d * u * (s * (1.0 + g * (1.0 - s)))).astype(BF16)
        o_ref[:, F:] = (d * (g * s)).astype(BF16)

    return pl.pallas_call(
        body, name=name, grid=(T // SWI_ROWS,),
        in_specs=[pl.BlockSpec((SWI_ROWS, F2), lambda i: (i, 0)), pl.BlockSpec((SWI_ROWS, F), lambda i: (i, 0))],
        out_specs=pl.BlockSpec((SWI_ROWS, F2), lambda i: (i, 0)), out_shape=jax.ShapeDtypeStruct((T, F2), BF16),
        compiler_params=_cparams(("parallel",)), **_CALL_KW)(gu, dact)


def _loss_and_grad(name, y, target, ctx_tiles):
    T, D = y.shape

    def body(y_ref, t_ref, l_ref, dy_ref):
        i = pl.program_id(0)

        @pl.when(i == 0)
        def _():
            l_ref[...] = jnp.zeros_like(l_ref)

        lat = i >= ctx_tiles
        e = jnp.where(lat, y_ref[...] - t_ref[...], 0.0)
        dy_ref[...] = e * (1.0 / D)
        l_ref[...] += 0.5 * jnp.sum(jnp.sum(e * e, axis=-1, keepdims=True) * (1.0 / D), axis=0, keepdims=True)

    row = pl.BlockSpec((TOK, D), lambda i: (i, 0))
    return pl.pallas_call(
        body, name=name, grid=(T // TOK,),
        in_specs=[row, pl.BlockSpec((TOK, D), lambda i: (jnp.maximum(i - ctx_tiles, 0), 0))],
        out_specs=[pl.BlockSpec((8, 128), lambda i: (0, 0)), row],
        out_shape=[jax.ShapeDtypeStruct((8, 128), F32), jax.ShapeDtypeStruct((T, D), F32)],
        compiler_params=_cparams(("arbitrary",)), **_CALL_KW)(y, target)


def _rot_half(x):
    lane = lax.broadcasted_iota(jnp.int32, x.shape, 1)
    return jnp.where((lane % 64) < 32, -pltpu.roll(x, 96, 1), pltpu.roll(x, 32, 1))


def _head_prep(name, src, col_blk, n_heads, g, cos, sin, t_pad, norm, rope):
    T = src.shape[0]
    W = n_heads * HEAD
    nt = T // TOK

    def body(s_ref, g_ref, cos_ref, sin_ref, o_ref):
        i = pl.program_id(0)
        outs = []
        for h in range(n_heads):
            xv = s_ref[:, h * HEAD:(h + 1) * HEAD].astype(F32)
            if norm:
                xv = xv * lax.rsqrt(jnp.mean(xv * xv, axis=-1, keepdims=True) + EPS) * g_ref[...]
            if rope:
                xv = xv * cos_ref[...] + _rot_half(xv) * sin_ref[...]
            outs.append(jnp.where(i < nt, xv, 0.0).astype(BF16))
        o_ref[...] = jnp.concatenate(outs, axis=-1) if n_heads > 1 else outs[0]

    tab = pl.BlockSpec((TOK, HEAD), lambda i: (i, 0))
    return pl.pallas_call(
        body, name=name, grid=(t_pad // TOK,),
        in_specs=[pl.BlockSpec((TOK, W), lambda i: (jnp.minimum(i, nt - 1), col_blk)), pl.BlockSpec((1, HEAD), lambda i: (0, 0)), tab, tab],
        out_specs=pl.BlockSpec((TOK, W), lambda i: (i, 0)), out_shape=jax.ShapeDtypeStruct((t_pad, W), BF16),
        compiler_params=_cparams(("parallel",)), **_CALL_KW)(src, g, cos, sin)


def _head_prep_bwd(name, src, col_blk, n_heads, g, cos, sin, dout, norm, rope, dst):
    T = src.shape[0]
    W = n_heads * HEAD

    def body(s_ref, g_ref, cos_ref, sin_ref, d_ref, dst_ref, ds_ref, dg_ref):
        i = pl.program_id(0)

        @pl.when(i == 0)
        def _():
            dg_ref[...] = jnp.zeros_like(dg_ref)

        outs = []
        dg = jnp.zeros((1, HEAD), F32)
        for h in range(n_heads):
            dz = d_ref[:, h * HEAD:(h + 1) * HEAD]
            if rope:
                dz = dz * cos_ref[...] - _rot_half(dz * sin_ref[...])
            if norm:
                xv = s_ref[:, h * HEAD:(h + 1) * HEAD].astype(F32)
                r = lax.rsqrt(jnp.mean(xv * xv, axis=-1, keepdims=True) + EPS)
                xn = xv * r
                dg = dg + jnp.sum(dz * xn, axis=0, keepdims=True)
                u = dz * g_ref[...]
                dz = r * (u - xn * jnp.mean(u * xn, axis=-1, keepdims=True))
            outs.append(dz.astype(BF16))
        dg_ref[...] += dg
        ds_ref[...] = jnp.concatenate(outs, axis=-1) if n_heads > 1 else outs[0]

    tab = pl.BlockSpec((TOK, HEAD), lambda i: (i, 0))
    col = pl.BlockSpec((TOK, W), lambda i: (i, col_blk))
    one = pl.BlockSpec((1, HEAD), lambda i: (0, 0))
    return pl.pallas_call(
        body, name=name, grid=(T // TOK,),
        in_specs=[col, one, tab, tab, pl.BlockSpec((TOK, W), lambda i: (i, 0)), pl.BlockSpec(memory_space=pl.ANY)],
        out_specs=[col, one], out_shape=[jax.ShapeDtypeStruct(dst.shape, dst.dtype), jax.ShapeDtypeStruct((1, HEAD), F32)],
        input_output_aliases={5: 0}, compiler_params=_cparams(("arbitrary",)), **_CALL_KW)(src, g, cos, sin, dout, dst)


NEG = -1e30


def _attn_geometry(kind, blk, ctx, seq):
    if kind == "swa":
        bq, W = 128, 384
        nctx = ctx // bq
        lat = blk >= nctx
        n = blk - nctx
        s0 = jnp.where(lat, ctx + (n - 1) * bq, 0)
        i = lax.broadcasted_iota(jnp.int32, (bq, W), 0)
        j = lax.broadcasted_iota(jnp.int32, (bq, W), 1)
        kpos = (n - 1) * bq + j
        rel = j - bq - i
        valid = lat & (rel <= SWA_WINDOW) & (rel >= -SWA_WINDOW) & (kpos >= 0) & (kpos < seq)
        return s0, valid, 0
    bq, W = GRID_W, NA_KH * GRID_W
    nctx = ctx // bq
    rows = seq // GRID_W
    lat = blk >= nctx
    rr = jnp.clip(blk - nctx, 0, rows - 1)
    rs = jnp.clip(rr - NA_KH // 2, 0, rows - NA_KH)
    s0 = ctx + rs * GRID_W
    i = lax.broadcasted_iota(jnp.int32, (bq, W), 0)
    j = lax.broadcasted_iota(jnp.int32, (bq, W), 1)
    kcol = j % GRID_W
    cs = jnp.clip(i - NA_KW // 2, 0, GRID_W - NA_KW)
    valid = lat & (kcol >= cs) & (kcol < cs + NA_KW)
    return s0, valid, rr - rs


HP = 2


def _attn_probs(q, kl, kc, sk, bias, valid):
    scale = HEAD ** -0.5
    nt_dims = (((1,), (1,)), ((), ()))
    sl = lax.dot_general(q, kl, nt_dims, preferred_element_type=F32) * scale
    if bias is not None:
        sl = sl + bias
    sl = jnp.where(valid, sl, NEG)
    sc = lax.dot_general(q, kc, nt_dims, preferred_element_type=F32) * scale
    m = jnp.maximum(jnp.maximum(jnp.max(sl, axis=-1, keepdims=True), jnp.max(sc, axis=-1, keepdims=True)), sk)
    el, ec, es = jnp.exp(sl - m), jnp.exp(sc - m), jnp.exp(sk - m)
    inv = 1.0 / (jnp.sum(el, axis=-1, keepdims=True) + jnp.sum(ec, axis=-1, keepdims=True) + es)
    return el * inv, ec * inv, es * inv


def _attn_specs(kind, n_q, n_kv, t_pad):
    bq = 128 if kind == "swa" else GRID_W
    rep = n_q // n_kv
    assert n_q % HP == 0 and HP % rep == 0
    kvw = HP // rep
    qspec = pl.BlockSpec((bq, HP * HEAD), lambda g, b: (b, g))
    kvspec = pl.BlockSpec((t_pad, kvw * HEAD), lambda g, b: (0, g))
    specs = [qspec, kvspec, kvspec, pl.BlockSpec(memory_space=pltpu.SMEM)]
    return bq, rep, qspec, kvspec, specs


def _bias_spec(ctx, seq):
    W = NA_KH * GRID_W

    def idx(g, b):
        rows = seq // GRID_W
        rr = jnp.clip(b - ctx // GRID_W, 0, rows - 1)
        return (g, rr - jnp.clip(rr - NA_KH // 2, 0, rows - NA_KH), 0, 0)

    return pl.BlockSpec((HP, None, GRID_W, W), idx)


def _attn_loads(kind, blk, q_ref, k_ref, v_ref, sink_ref, bias_ref, rep, ctx, seq):
    bq, W = (128, 384) if kind == "swa" else (GRID_W, NA_KH * GRID_W)
    g = pl.program_id(0)
    s0, valid, _ = _attn_geometry(kind, blk, ctx, seq)
    s0 = pl.multiple_of(s0, GRID_W)
    heads = []
    for j in range(HP):
        kv = slice((j // rep) * HEAD, (j // rep + 1) * HEAD)
        heads.append((q_ref[:, j * HEAD:(j + 1) * HEAD], k_ref[pl.ds(s0, W), kv], k_ref[0:ctx, kv], v_ref[pl.ds(s0, W), kv],
                      v_ref[0:ctx, kv], sink_ref[g * HP + j], bias_ref[j] if bias_ref is not None else None))
    return s0, W, valid, heads


def _attn_fwd(name, kind, q, k, v, sink, bias, ctx, seq, dst, head0):
    t_pad = q.shape[0]
    T = ctx + seq
    n_q, n_kv = q.shape[1] // HEAD, k.shape[1] // HEAD
    bq, rep, qspec, kvspec, specs = _attn_specs(kind, n_q, n_kv, t_pad)
    assert head0 % HP == 0

    def body(q_ref, k_ref, v_ref, sink_ref, *rest):
        bias_ref = rest[0] if bias is not None else None
        o_ref = rest[-1]
        _, _, valid, heads = _attn_loads(kind, pl.program_id(1), q_ref, k_ref, v_ref, sink_ref, bias_ref, rep, ctx, seq)
        outs = []
        for qv, kl, kc, vl, vc, sk, bv in heads:
            p_l, p_c, _ = _attn_probs(qv, kl, kc, sk, bv, valid)
            o = jnp.dot(p_l.astype(BF16), vl, preferred_element_type=F32) + jnp.dot(p_c.astype(BF16), vc, preferred_element_type=F32)
            outs.append(o.astype(o_ref.dtype))
        o_ref[...] = jnp.concatenate(outs, axis=-1)

    ins = [q, k, v, sink] + ([bias] if bias is not None else []) + [dst]
    if bias is not None:
        specs = specs + [_bias_spec(ctx, seq)]
    specs = specs + [pl.BlockSpec(memory_space=pl.ANY)]
    return pl.pallas_call(
        body, name=name, grid=(n_q // HP, T // bq), in_specs=specs,
        out_specs=pl.BlockSpec((bq, HP * HEAD), lambda g, b: (b, head0 // HP + g)),
        out_shape=jax.ShapeDtypeStruct(dst.shape, dst.dtype), input_output_aliases={len(ins) - 1: 0},
        compiler_params=_cparams(("parallel", "arbitrary")), **_CALL_KW)(*ins)


def _attn_bwd(name, kind, q, k, v, sink, bias, do, do_head0, ctx, seq):
    t_pad = q.shape[0]
    T = ctx + seq
    n_q, n_kv = q.shape[1] // HEAD, k.shape[1] // HEAD
    bq, rep, qspec, kvspec, specs = _attn_specs(kind, n_q, n_kv, t_pad)
    assert do_head0 % HP == 0
    scale = HEAD ** -0.5
    tn_dims = (((0,), (0,)), ((), ()))
    nt_dims = (((1,), (1,)), ((), ()))
    bdot = functools.partial(lax.dot_general, preferred_element_type=F32)

    def body(q_ref, k_ref, v_ref, sink_ref, *rest):
        if bias is not None:
            bias_ref, do_ref, dq_ref, dk_ref, dv_ref, dsk_ref, db_ref = rest
        else:
            bias_ref, db_ref = None, None
            do_ref, dq_ref, dk_ref, dv_ref, dsk_ref = rest
        blk = pl.program_id(1)
        s0, W, valid, heads = _attn_loads(kind, blk, q_ref, k_ref, v_ref, sink_ref, bias_ref, rep, ctx, seq)
        dos = [do_ref[:, j * HEAD:(j + 1) * HEAD] for j in range(HP)]

        @pl.when(blk == 0)
        def _():
            dk_ref[...] = jnp.zeros_like(dk_ref)
            dv_ref[...] = jnp.zeros_like(dv_ref)
            dsk_ref[...] = jnp.zeros_like(dsk_ref)

        if bias is not None:
            _, _, pat = _attn_geometry(kind, blk, ctx, seq)
            _, _, pat_prev = _attn_geometry(kind, jnp.maximum(blk - 1, 0), ctx, seq)

            @pl.when((blk == 0) | (pat != pat_prev))
            def _():
                db_ref[...] = jnp.zeros_like(db_ref)

        res = []
        for (qv, kl, kc, vl, vc, sk, bv), dov in zip(heads, dos):
            p_l, p_c, p_s = _attn_probs(qv, kl, kc, sk, bv, valid)
            dob = dov.astype(BF16)
            pl_b, pc_b = p_l.astype(BF16), p_c.astype(BF16)
            o = jnp.dot(pl_b, vl, preferred_element_type=F32) + jnp.dot(pc_b, vc, preferred_element_type=F32)
            delta = jnp.sum(dov * o, axis=-1, keepdims=True)
            ds_l = p_l * (bdot(dob, vl, nt_dims) - delta)
            ds_c = p_c * (bdot(dob, vc, nt_dims) - delta)
            dsl_b, dsc_b = ds_l.astype(BF16), ds_c.astype(BF16)
            dq = (jnp.dot(dsl_b, kl, preferred_element_type=F32) + jnp.dot(dsc_b, kc, preferred_element_type=F32)) * scale
            res.append((dq, bdot(dsl_b, qv, tn_dims) * scale, bdot(pl_b, dob, tn_dims), bdot(dsc_b, qv, tn_dims) * scale,
                        bdot(pc_b, dob, tn_dims), jnp.sum(-p_s * delta, axis=0, keepdims=True), ds_l))
        dq_ref[...] = jnp.concatenate([r[0] for r in res], axis=-1)
        for j, (_, dkl, dvl, dkc, dvc, dsk, ds_l) in enumerate(res):
            kv = slice((j // rep) * HEAD, (j // rep + 1) * HEAD)
            dk_ref[pl.ds(s0, W), kv] += dkl
            dv_ref[pl.ds(s0, W), kv] += dvl
            dk_ref[0:ctx, kv] += dkc
            dv_ref[0:ctx, kv] += dvc
            dsk_ref[j] += jnp.broadcast_to(dsk, (8, HEAD))
            if bias is not None:
                db_ref[j] += ds_l

    ins = [q, k, v, sink] + ([bias] if bias is not None else []) + [do]
    in_specs = specs + ([_bias_spec(ctx, seq)] if bias is not None else []) + [
        pl.BlockSpec((bq, HP * HEAD), lambda g, b: (b, do_head0 // HP + g))]
    out_specs = [qspec, kvspec, kvspec, pl.BlockSpec((HP, 8, HEAD), lambda g, b: (g, 0, 0))]
    out_shape = [jax.ShapeDtypeStruct((T, n_q * HEAD), F32), jax.ShapeDtypeStruct((t_pad, n_kv * HEAD), F32),
                 jax.ShapeDtypeStruct((t_pad, n_kv * HEAD), F32), jax.ShapeDtypeStruct((n_q, 8, HEAD), F32)]
    if bias is not None:
        out_specs.append(_bias_spec(ctx, seq))
        out_shape.append(jax.ShapeDtypeStruct(bias.shape, F32))
    res = pl.pallas_call(
        body, name=name, grid=(n_q // HP, T // bq), in_specs=in_specs, out_specs=out_specs, out_shape=out_shape,
        compiler_params=_cparams(("arbitrary", "arbitrary")), **_CALL_KW)(*ins)
    return res if bias is not None else list(res) + [None]


HALO = 8


def _halo_specs(width, col0, T, ctx_tiles):
    per = TOK // HALO
    main = pl.BlockSpec((TOK, width), lambda jc, i: (i, col0 + jc))
    prev = pl.BlockSpec((HALO, width), lambda jc, i: (jnp.maximum(i * per - 1, 0), col0 + jc))
    nxt = pl.BlockSpec((HALO, width), lambda jc, i: (jnp.minimum((i + 1) * per, T // HALO - 1), col0 + jc))
    return main, prev, nxt


def _with_halo(i, nt, ctx_tiles, prev, main, nxt):
    has_prev = (i != 0) & (i != ctx_tiles)
    has_next = (i != ctx_tiles - 1) & (i != nt - 1)
    return jnp.concatenate([jnp.where(has_prev, prev, 0.0), main, jnp.where(has_next, nxt, 0.0)], axis=0)


def _shifted(ext, s):
    n = ext.shape[0]
    return pltpu.roll(ext, (-s) % n, 0)[HALO:HALO + TOK]


def _conv_fwd(name, p, col0, conv_w, ctx_tiles):
    T = p.shape[0]
    nt = T // TOK
    ncol = 3
    Wc = conv_w.shape[1] // ncol
    pad = (DN_CONV - 1) // 2

    def body(m_ref, p_ref, n_ref, w_ref, o_ref):
        i = pl.program_id(1)
        ext = _with_halo(i, nt, ctx_tiles, p_ref[...].astype(F32), m_ref[...].astype(F32), n_ref[...].astype(F32))
        acc = jnp.zeros((TOK, Wc), F32)
        for j in range(DN_CONV):
            acc = acc + w_ref[j:j + 1, :] * _shifted(ext, j - pad)
        o_ref[...] = acc

    main, prev, nxt = _halo_specs(Wc, col0, T, ctx_tiles)
    return pl.pallas_call(
        body, name=name, grid=(ncol, nt), in_specs=[main, prev, nxt, pl.BlockSpec((DN_CONV, Wc), lambda jc, i: (0, jc))],
        out_specs=pl.BlockSpec((TOK, Wc), lambda jc, i: (i, jc)), out_shape=jax.ShapeDtypeStruct((T, ncol * Wc), F32),
        compiler_params=_cparams(("parallel", "parallel")), **_CALL_KW)(p, p, p, conv_w)


def _conv_bwd(name, p, col0, conv_w, dpre, ctx_tiles, dst):
    T = p.shape[0]
    nt = T // TOK
    ncol = 3
    Wc = conv_w.shape[1] // ncol
    pad = (DN_CONV - 1) // 2

    def body(m_ref, p_ref, n_ref, dm_ref, dp_ref, dn_ref, w_ref, dst_ref, dx_ref, dw_ref):
        i = pl.program_id(1)
        ext_x = _with_halo(i, nt, ctx_tiles, p_ref[...].astype(F32), m_ref[...].astype(F32), n_ref[...].astype(F32))
        ext_d = _with_halo(i, nt, ctx_tiles, dp_ref[...], dm_ref[...], dn_ref[...])
        dmain = dm_ref[...]

        @pl.when(i == 0)
        def _():
            dw_ref[...] = jnp.zeros_like(dw_ref)

        acc = jnp.zeros((TOK, Wc), F32)
        for j in range(DN_CONV):
            acc = acc + w_ref[j:j + 1, :] * _shifted(ext_d, pad - j)
            dw_ref[j:j + 1, :] += jnp.sum(dmain * _shifted(ext_x, j - pad), axis=0, keepdims=True)
        dx_ref[...] = acc.astype(BF16)

    main, prev, nxt = _halo_specs(Wc, col0, T, ctx_tiles)
    dmain, dprev, dnxt = _halo_specs(Wc, 0, T, ctx_tiles)
    return pl.pallas_call(
        body, name=name, grid=(ncol, nt),
        in_specs=[main, prev, nxt, dmain, dprev, dnxt, pl.BlockSpec((DN_CONV, Wc), lambda jc, i: (0, jc)),
                  pl.BlockSpec(memory_space=pl.ANY)],
        out_specs=[pl.BlockSpec((TOK, Wc), lambda jc, i: (i, col0 + jc)), pl.BlockSpec((8, Wc), lambda jc, i: (0, jc))],
        out_shape=[jax.ShapeDtypeStruct(dst.shape, dst.dtype), jax.ShapeDtypeStruct((8, ncol * Wc), F32)],
        input_output_aliases={7: 0},
        compiler_params=_cparams(("parallel", "arbitrary")), **_CALL_KW)(p, p, p, dpre, dpre, dpre, conv_w, dst)


def _softplus(x):
    return jnp.maximum(x, 0.0) + jnp.log(1.0 + jnp.exp(-jnp.abs(x)))


def _gdn_point(name, pre, dab, a_log, dt_bias, n_heads):
    T = pre.shape[0]
    Wd = n_heads * HEAD
    ng = 2 * n_heads

    def body(pre_ref, ab_ref, al_ref, dt_ref, q_ref, k_ref, v_ref, la_ref, be_ref):
        for h in range(n_heads):
            for part, ref in enumerate((q_ref, k_ref, v_ref)):
                xv = pre_ref[:, part * Wd + h * HEAD:part * Wd + (h + 1) * HEAD]
                s = xv * _sigmoid(xv)
                if part < 2:
                    s = s * lax.rsqrt(jnp.sum(s * s, axis=-1, keepdims=True) + EPS) * (HEAD ** -0.5 if part == 0 else 1.0)
                ref[:, h * HEAD:(h + 1) * HEAD] = s
        ab = ab_ref[...].astype(F32)
        lane = lax.broadcasted_iota(jnp.int32, ab.shape, 1)
        la_ref[...] = jnp.where(lane < ng, -jnp.exp(al_ref[...]) * _softplus(ab + dt_ref[...]), 0.0)
        be_ref[...] = jnp.where(lane < ng, _sigmoid(pltpu.roll(ab, HEAD - ng, 1)), 0.0)

    row = lambda w: pl.BlockSpec((TOK, w), lambda i: (i, 0))
    one = pl.BlockSpec((1, HEAD), lambda i: (0, 0))
    return pl.pallas_call(
        body, name=name, grid=(T // TOK,), in_specs=[row(3 * Wd), row(HEAD), one, one],
        out_specs=[row(Wd), row(Wd), row(Wd), row(HEAD), row(HEAD)],
        out_shape=[jax.ShapeDtypeStruct((T, Wd), F32)] * 3 + [jax.ShapeDtypeStruct((T, HEAD), F32)] * 2,
        compiler_params=_cparams(("parallel",)), **_CALL_KW)(pre, dab, a_log, dt_bias)


def _gdn_point_bwd(name, pre, dab, a_log, dt_bias, n_heads, dq, dk, dv, dla, dbe):
    T = pre.shape[0]
    Wd = n_heads * HEAD
    ng = 2 * n_heads

    def body(pre_ref, ab_ref, al_ref, dt_ref, dq_ref, dk_ref, dv_ref, dla_ref, dbe_ref, dpre_ref, dab_ref, dal_ref, ddt_ref):
        i = pl.program_id(0)

        @pl.when(i == 0)
        def _():
            dal_ref[...] = jnp.zeros_like(dal_ref)
            ddt_ref[...] = jnp.zeros_like(ddt_ref)

        for h in range(n_heads):
            for part, ref in enumerate((dq_ref, dk_ref, dv_ref)):
                cols = slice(part * Wd + h * HEAD, part * Wd + (h + 1) * HEAD)
                xv = pre_ref[:, cols]
                sg = _sigmoid(xv)
                s = xv * sg
                dy = ref[0, :, h * HEAD:(h + 1) * HEAD] + ref[1, :, h * HEAD:(h + 1) * HEAD]
                if part < 2:
                    c0 = HEAD ** -0.5 if part == 0 else 1.0
                    r = lax.rsqrt(jnp.sum(s * s, axis=-1, keepdims=True) + EPS)
                    ds = c0 * (r * dy - s * (r * r * r) * jnp.sum(dy * s, axis=-1, keepdims=True))
                else:
                    ds = dy
                dpre_ref[:, cols] = ds * (sg * (1.0 + xv * (1.0 - sg)))
        ab = ab_ref[...].astype(F32)
        lane = lax.broadcasted_iota(jnp.int32, ab.shape, 1)
        ea = jnp.exp(al_ref[...])
        z = ab + dt_ref[...]
        dlav = jnp.where(lane < ng, dla_ref[0] + dla_ref[1], 0.0)
        da = dlav * (-ea) * _sigmoid(z)
        dal_ref[...] += jnp.sum(dlav * (-ea) * _softplus(z), axis=0, keepdims=True)
        ddt_ref[...] += jnp.sum(da, axis=0, keepdims=True)
        be = _sigmoid(pltpu.roll(ab, HEAD - ng, 1))
        db = jnp.where(lane < ng, (dbe_ref[0] + dbe_ref[1]) * be * (1.0 - be), 0.0)
        dab_ref[...] = (da + pltpu.roll(db, ng, 1)).astype(BF16)

    row = lambda w: pl.BlockSpec((TOK, w), lambda i: (i, 0))
    two = lambda w: pl.BlockSpec((2, TOK, w), lambda i: (0, i, 0))
    one = pl.BlockSpec((1, HEAD), lambda i: (0, 0))
    return pl.pallas_call(
        body, name=name, grid=(T // TOK,),
        in_specs=[row(3 * Wd), row(HEAD), one, one, two(Wd), two(Wd), two(Wd), two(HEAD), two(HEAD)],
        out_specs=[row(3 * Wd), row(HEAD), one, one],
        out_shape=[jax.ShapeDtypeStruct((T, 3 * Wd), F32), jax.ShapeDtypeStruct((T, HEAD), BF16),
                   jax.ShapeDtypeStruct((1, HEAD), F32), jax.ShapeDtypeStruct((1, HEAD), F32)],
        compiler_params=_cparams(("arbitrary",)), **_CALL_KW)(pre, dab, a_log, dt_bias, dq, dk, dv, dla, dbe)


_NN, _NT, _TN = "nn", "nt", "tn"
_DIMS = {"nn": (((1,), (0,)), ((), ())), "nt": (((1,), (1,)), ((), ())), "tn": (((0,), (0,)), ((), ()))}
_BDIMS = {"nn": (((2,), (1,)), ((0,), (0,))), "nt": (((2,), (2,)), ((0,), (0,))), "tn": (((1,), (1,)), ((0,), (0,)))}


def _dims(a, kind):
    return _BDIMS[kind] if a.ndim == 3 else _DIMS[kind]


def _mm3_raw(a, b, kind=_NN):
    ah, bh = a.astype(BF16), b.astype(BF16)
    al, bl = (a - ah.astype(F32)).astype(BF16), (b - bh.astype(F32)).astype(BF16)
    d = functools.partial(lax.dot_general, dimension_numbers=_dims(a, kind), preferred_element_type=F32)
    return d(ah, bh) + (d(ah, bl) + d(al, bh))


@jax.custom_vjp
def _mm3(a, b):
    return _mm3_raw(a, b)


def _mm3_fwd(a, b):
    return _mm3_raw(a, b), (a, b)


def _mm3_bwd(res, g):
    a, b = res
    return _mm3_raw(g, b, _NT), _mm3_raw(a, g, _TN)


_mm3.defvjp(_mm3_fwd, _mm3_bwd)


def _bdot_raw(a, b, kind):
    return lax.dot_general(a.astype(BF16), b.astype(BF16), _dims(a, kind), preferred_element_type=F32)


@functools.partial(jax.custom_vjp, nondiff_argnums=(2,))
def _bdot(a, b, kind=_NN):
    return _bdot_raw(a, b, kind)


def _bdot_fwd(a, b, kind):
    return _bdot_raw(a, b, kind), (a, b)


def _bdot_bwd(kind, res, g):
    a, b = res
    if kind == "nn":
        return _bdot_raw(g, b, "nt"), _bdot_raw(a, g, "tn")
    if kind == "nt":
        return _bdot_raw(g, b, "nn"), _bdot_raw(g, a, "tn")
    return _bdot_raw(b, g, "nt"), _bdot_raw(a, g, "nn")


_bdot.defvjp(_bdot_fwd, _bdot_bwd)


def _chunk_masks(rev):
    C = DN_CHUNK
    ii = lax.broadcasted_iota(jnp.int32, (C, C), 0)
    jj = lax.broadcasted_iota(jnp.int32, (C, C), 1)
    diff = jnp.where(rev, jj - ii, ii - jj)
    incl = diff >= 0
    strict = diff > 0
    rowsel = (lax.broadcasted_iota(jnp.int32, (C, 1), 0) == jnp.where(rev, 0, C - 1)).astype(F32)
    return incl, strict, rowsel, (ii == jj).astype(F32)


def _head_stack(ref, n_heads):
    return jnp.stack([ref[:, h * HEAD:(h + 1) * HEAD] for h in range(n_heads)])


def _gate_views(g, gt, be, d, n_heads):
    lane = lax.broadcasted_iota(jnp.int32, (1, HEAD), 1)
    sub = lax.broadcasted_iota(jnp.int32, (HEAD, 1), 0)
    sels = [(lane == d * n_heads + h).astype(F32) for h in range(n_heads)]
    selts = [(sub == d * n_heads + h).astype(F32) for h in range(n_heads)]
    g_col = jnp.stack([jnp.sum(g * s, axis=1, keepdims=True) for s in sels])
    b_col = jnp.stack([jnp.sum(be * s, axis=1, keepdims=True) for s in sels])
    g_row = jnp.stack([jnp.sum(gt * s, axis=0, keepdims=True) for s in selts])
    return g_col, g_row, b_col, sels, selts


def _chunk_decay(g_col, g_row, incl):
    return jnp.where(incl, jnp.exp(jnp.where(incl, g_col - g_row, 0.0)), 0.0)


def _chunk_lower(k, g_col, g_row, b_col, incl, strict):
    return jnp.where(strict, _bdot(k * b_col, k, _NT) * _chunk_decay(g_col, g_row, incl), 0.0)


def _chunk_inverse(low, eye):
    m = -low
    x = eye + m
    p = m
    for _ in range(int(math.log2(DN_CHUNK)) - 1):
        p = _mm3(p, p)
        x = x + _mm3(x, p)
    return x


def _chunk_step(q, k, v, g_col, g_row, b_col, S, X, incl, rowsel):
    decay = _chunk_decay(g_col, g_row, incl)
    eg = jnp.exp(g_col)
    u = _mm3(X, v * b_col)
    w = _mm3(X, k * (b_col * eg))
    intra = _bdot(q, k, _NT) * decay
    g_last = jnp.sum(g_col * rowsel, axis=1, keepdims=True)
    v_new = u - _bdot(w, S)
    o = _bdot(q * eg, S) + _bdot(intra, v_new)
    S_new = S * jnp.exp(g_last) + _bdot(k * jnp.exp(g_last - g_col), v_new, _TN)
    return o, S_new


def _scan_index(ctx_chunks, n_chunks):
    def idx(d, n):
        return jnp.where(d == 0, n, jnp.where(n < ctx_chunks, ctx_chunks - 1 - n, n_chunks + ctx_chunks - 1 - n))
    return idx


def _cumsum_mats(rev):
    C = DN_CHUNK
    ii = lax.broadcasted_iota(jnp.int32, (C, C), 0)
    jj = lax.broadcasted_iota(jnp.int32, (C, C), 1)
    return jnp.where(jnp.where(rev, jj - ii, ii - jj) >= 0, 1.0, 0.0).astype(F32)


def _hosted(comm, n_in, n_out, refs):
    n_ci, n_co = (len(comm.ins), len(comm.out_shapes)) if comm is not None else (0, 0)
    ins, cin = refs[:n_in], refs[n_in:n_in + n_ci]
    outs, cout = refs[n_in + n_ci:n_in + n_ci + n_out], refs[n_in + n_ci + n_out:n_in + n_ci + n_out + n_co]
    rest = refs[n_in + n_ci + n_out + n_co:]
    n_sem = 3 if comm is not None else 0
    return ins, outs, rest[:len(rest) - n_sem], (cin, cout, rest[len(rest) - n_sem:])


def _hosted_start(comm, cref, first):
    if comm is not None:
        @pl.when(first)
        def _():
            for cp in comm.copies(*cref):
                cp.start()


def _hosted_wait(comm, cref, last):
    if comm is not None:
        @pl.when(last)
        def _():
            for cp in comm.copies(*cref):
                cp.wait()


def _gdn_scan(name, q, k, v, la, be, n_heads, ctx, comm=None):
    T, Wd = q.shape
    C = DN_CHUNK
    nch = T // C
    cidx = _scan_index(ctx // C, nch)

    def body(*refs):
        (q_ref, k_ref, v_ref, la_ref, be_ref), (o_ref, s_ref, x_ref), (state,), cref = _hosted(comm, 5, 3, refs)
        d, n = pl.program_id(0), pl.program_id(1)
        rev = d == 1
        _hosted_start(comm, cref, (d == 0) & (n == 0))

        @pl.when(n == 0)
        def _():
            state[...] = jnp.zeros_like(state)

        incl, strict, rowsel, eye = _chunk_masks(rev)
        g = jnp.dot(_cumsum_mats(rev), la_ref[...], precision=lax.Precision.HIGHEST, preferred_element_type=F32)
        g_col, g_row, b_col, _, _ = _gate_views(g, g.T, be_ref[...], d, n_heads)
        qs, ks, vs = _head_stack(q_ref, n_heads), _head_stack(k_ref, n_heads), _head_stack(v_ref, n_heads)
        S = state[...]
        X = _chunk_inverse(_chunk_lower(ks, g_col, g_row, b_col, incl, strict), eye)
        o, S_new = _chunk_step(qs, ks, vs, g_col, g_row, b_col, S, X, incl, rowsel)
        s_ref[...] = S
        x_ref[...] = X
        state[...] = S_new
        for h in range(n_heads):
            o_ref[:, h * HEAD:(h + 1) * HEAD] = o[h]
        _hosted_wait(comm, cref, (d == 1) & (n == nch - 1))

    tok = lambda w: pl.BlockSpec((C, w), lambda d, n: (cidx(d, n), 0))
    any_spec = pl.BlockSpec(memory_space=pl.ANY)
    c_ins, c_sds, c_sems = (comm.ins, comm.out_sds(), comm.sem_shapes()) if comm is not None else ([], [], [])
    res = pl.pallas_call(
        body, name=name, grid=(2, nch), in_specs=[tok(Wd), tok(Wd), tok(Wd), tok(HEAD), tok(HEAD)] + [any_spec] * len(c_ins),
        out_specs=[pl.BlockSpec((None, C, Wd), lambda d, n: (d, cidx(d, n), 0)),
                   pl.BlockSpec((None, None, n_heads, HEAD, HEAD), lambda d, n: (d, n, 0, 0, 0)),
                   pl.BlockSpec((None, None, n_heads, C, C), lambda d, n: (d, n, 0, 0, 0))] + [any_spec] * len(c_sds),
        out_shape=[jax.ShapeDtypeStruct((2, T, Wd), F32), jax.ShapeDtypeStruct((2, nch, n_heads, HEAD, HEAD), F32),
                   jax.ShapeDtypeStruct((2, nch, n_heads, C, C), F32)] + c_sds,
        scratch_shapes=[pltpu.VMEM((n_heads, HEAD, HEAD), F32)] + c_sems,
        compiler_params=_cparams(("arbitrary", "arbitrary"), has_side_effects=comm is not None), **_CALL_KW)(q, k, v, la, be, *c_ins)
    return res[0], res[1], res[2], list(res[3:])


def _gdn_scan_bwd(name, q, k, v, la, be, states, invs, do, n_heads, ctx, comm=None):
    T, Wd = q.shape
    C = DN_CHUNK
    nch = T // C
    cidx = _scan_index(ctx // C, nch)

    def body(*refs):
        ins, outs, (dstate,), cref = _hosted(comm, 8, 5, refs)
        q_ref, k_ref, v_ref, la_ref, be_ref, s_ref, x_ref, do_ref = ins
        dq_ref, dk_ref, dv_ref, dla_ref, dbe_ref = outs
        d, n = pl.program_id(0), pl.program_id(1)
        rev = d == 1
        _hosted_start(comm, cref, (d == 0) & (n == 0))

        @pl.when(n == 0)
        def _():
            dstate[...] = jnp.zeros_like(dstate)

        incl, strict, rowsel, eye = _chunk_masks(rev)
        tri = _cumsum_mats(rev)
        g = jnp.dot(tri, la_ref[...], precision=lax.Precision.HIGHEST, preferred_element_type=F32)
        g_col, g_row, b_col, sels, selts = _gate_views(g, g.T, be_ref[...], d, n_heads)
        qs, ks, vs = _head_stack(q_ref, n_heads), _head_stack(k_ref, n_heads), _head_stack(v_ref, n_heads)
        dos = _head_stack(do_ref, n_heads)
        S, X = s_ref[...], x_ref[...]
        step = functools.partial(_chunk_step, incl=incl, rowsel=rowsel)
        _, vjp_step = jax.vjp(step, qs, ks, vs, g_col, g_row, b_col, S, X)
        dq, dk1, dv_, dgc1, dgr1, dbc1, dS, dX = vjp_step((dos, dstate[...]))
        dlow = -_mm3_raw(_mm3_raw(X, dX, _TN), X, _NT)
        low_fn = functools.partial(_chunk_lower, incl=incl, strict=strict)
        _, vjp_low = jax.vjp(low_fn, ks, g_col, g_row, b_col)
        dk2, dgc2, dgr2, dbc2 = vjp_low(dlow)
        dk = dk1 + dk2
        dstate[...] = dS
        dgc, dgr, dbc = dgc1 + dgc2, dgr1 + dgr2, dbc1 + dbc2
        dg = jnp.zeros((C, HEAD), F32)
        dgt = jnp.zeros((HEAD, C), F32)
        dbe = jnp.zeros((C, HEAD), F32)
        for h in range(n_heads):
            cols = slice(h * HEAD, (h + 1) * HEAD)
            dq_ref[:, cols], dk_ref[:, cols], dv_ref[:, cols] = dq[h], dk[h], dv_[h]
            dg = dg + dgc[h] * sels[h]
            dgt = dgt + selts[h] * dgr[h]
            dbe = dbe + dbc[h] * sels[h]
        dg = dg + dgt.T
        dla_ref[...] = lax.dot_general(tri, dg, _DIMS["tn"], precision=lax.Precision.HIGHEST, preferred_element_type=F32)
        dbe_ref[...] = dbe
        _hosted_wait(comm, cref, (d == 1) & (n == nch - 1))

    rn = lambda d, n: cidx(d, nch - 1 - n)
    tok = lambda w: pl.BlockSpec((C, w), lambda d, n: (rn(d, n), 0))
    otok = lambda w: pl.BlockSpec((None, C, w), lambda d, n: (d, rn(d, n), 0))
    any_spec = pl.BlockSpec(memory_space=pl.ANY)
    c_ins, c_sds, c_sems = (comm.ins, comm.out_sds(), comm.sem_shapes()) if comm is not None else ([], [], [])
    res = pl.pallas_call(
        body, name=name, grid=(2, nch),
        in_specs=[tok(Wd), tok(Wd), tok(Wd), tok(HEAD), tok(HEAD),
                  pl.BlockSpec((None, None, n_heads, HEAD, HEAD), lambda d, n: (d, nch - 1 - n, 0, 0, 0)),
                  pl.BlockSpec((None, None, n_heads, C, C), lambda d, n: (d, nch - 1 - n, 0, 0, 0)), tok(Wd)] + [any_spec] * len(c_ins),
        out_specs=[otok(Wd), otok(Wd), otok(Wd), otok(HEAD), otok(HEAD)] + [any_spec] * len(c_sds),
        out_shape=[jax.ShapeDtypeStruct((2, T, Wd), F32)] * 3 + [jax.ShapeDtypeStruct((2, T, HEAD), F32)] * 2 + c_sds,
        scratch_shapes=[pltpu.VMEM((n_heads, HEAD, HEAD), F32)] + c_sems,
        compiler_params=_cparams(("arbitrary", "arbitrary"), has_side_effects=comm is not None), **_CALL_KW)(
            q, k, v, la, be, states, invs, do, *c_ins)
    return tuple(res[:5]) + (list(res[5:]),)


def _gated_norm(name, o2, p, zblk, g, n_heads, width):
    _, T, Wd = o2.shape

    def body(o_ref, z_ref, g_ref, y_ref):
        for h in range(n_heads):
            cols = slice(h * HEAD, (h + 1) * HEAD)
            ov = o_ref[0, :, cols] + o_ref[1, :, cols]
            zv = z_ref[:, cols].astype(F32)
            y = ov * lax.rsqrt(jnp.mean(ov * ov, axis=-1, keepdims=True) + EPS) * g_ref[...]
            y_ref[:, cols] = (y * (zv * _sigmoid(zv))).astype(BF16)

    return pl.pallas_call(
        body, name=name, grid=(T // TOK,),
        in_specs=[pl.BlockSpec((2, TOK, Wd), lambda i: (0, i, 0)), pl.BlockSpec((TOK, Wd), lambda i: (i, zblk)),
                  pl.BlockSpec((1, HEAD), lambda i: (0, 0))],
        out_specs=pl.BlockSpec((TOK, Wd), lambda i: (i, 0)), out_shape=jax.ShapeDtypeStruct((T, width), BF16),
        compiler_params=_cparams(("parallel",)), **_CALL_KW)(o2, p, g)


def _gated_norm_bwd(name, o2, p, zblk, g, n_heads, dmix, dblk):
    _, T, Wd = o2.shape

    def body(o_ref, z_ref, g_ref, dy_ref, do_ref, dz_ref, dg_ref):
        i = pl.program_id(0)

        @pl.when(i == 0)
        def _():
            dg_ref[...] = jnp.zeros_like(dg_ref)

        dg = jnp.zeros((1, HEAD), F32)
        for h in range(n_heads):
            cols = slice(h * HEAD, (h + 1) * HEAD)
            ov = o_ref[0, :, cols] + o_ref[1, :, cols]
            zv = z_ref[:, cols].astype(F32)
            dy = dy_ref[:, cols].astype(F32)
            r = lax.rsqrt(jnp.mean(ov * ov, axis=-1, keepdims=True) + EPS)
            on = ov * r
            sg = _sigmoid(zv)
            sz = zv * sg
            dz_ref[:, cols] = (dy * (on * g_ref[...]) * (sg * (1.0 + zv * (1.0 - sg)))).astype(BF16)
            dyn = dy * sz
            dg = dg + jnp.sum(dyn * on, axis=0, keepdims=True)
            u = dyn * g_ref[...]
            do_ref[:, cols] = r * (u - on * jnp.mean(u * on, axis=-1, keepdims=True))
        dg_ref[...] += dg

    row = pl.BlockSpec((TOK, Wd), lambda i: (i, 0))
    one = pl.BlockSpec((1, HEAD), lambda i: (0, 0))
    return pl.pallas_call(
        body, name=name, grid=(T // TOK,),
        in_specs=[pl.BlockSpec((2, TOK, Wd), lambda i: (0, i, 0)), pl.BlockSpec((TOK, Wd), lambda i: (i, zblk)), one,
                  pl.BlockSpec((TOK, Wd), lambda i: (i, dblk))],
        out_specs=[row, pl.BlockSpec((TOK, Wd), lambda i: (i, zblk)), one],
        out_shape=[jax.ShapeDtypeStruct((T, Wd), F32), jax.ShapeDtypeStruct(p.shape, BF16), jax.ShapeDtypeStruct((1, HEAD), F32)],
        compiler_params=_cparams(("arbitrary",)), **_CALL_KW)(o2, p, g, dmix)


class _Dims:
    def __init__(self, D, seq, ctx, ffn):
        self.D, self.seq, self.ctx, self.ffn = D, seq, ctx, ffn
        self.T = seq + ctx
        self.t_pad = -(-(self.T + 128) // TOK) * TOK
        self.ctx_tiles = ctx // TOK
        nh = D // HEAD
        self.swa_h, self.kv_h, self.dn_h = nh // 4, nh // 8, nh // 2
        self.na_h = nh - self.swa_h - self.dn_h
        self.swa_q, self.swa_kv, self.Wd, self.na = self.swa_h * HEAD, self.kv_h * HEAD, self.dn_h * HEAD, self.na_h * HEAD
        self.n_ab = 4 * self.dn_h
        self.o_ab = self.swa_q + 2 * self.swa_kv + 4 * self.Wd
        self.n_in = self.o_ab + self.n_ab + 3 * self.na
        self.n_main = self.n_in - self.n_ab
        assert ctx % TOK == 0 and seq % TOK == 0 and self.swa_q == 2 * self.swa_kv == self.na and 2 * self.na == self.Wd


def _rope_tables(dm):
    t = jnp.arange(dm.t_pad, dtype=jnp.int32) - dm.ctx
    lat = (t >= 0) & (t < dm.seq)
    row = (t // GRID_W).astype(F32)
    col = (t % GRID_W).astype(F32)
    n_freq = HEAD // 4
    inv = ROPE_THETA ** (-jnp.arange(n_freq, dtype=F32) / n_freq)
    ang = jnp.concatenate([row[:, None] * inv, row[:, None] * inv, col[:, None] * inv, col[:, None] * inv], axis=-1)
    ang = jnp.where(lat[:, None], ang, 0.0)
    return jnp.cos(ang), jnp.sin(ang)


def _bias_indices():
    o = np.arange(NA_KH)[:, None]
    jr = np.arange(NA_KH)[None, :]
    idx_r = jr - o + (NA_KH - 1)
    cols = np.arange(GRID_W)
    idx_c = np.clip(cols[None, :] - cols[:, None], -(NA_KW - 1), NA_KW - 1) + (NA_KW - 1)
    return idx_r, idx_c


def _bias_onehots():
    idx_r, idx_c = _bias_indices()
    sel_r = (idx_r.reshape(-1)[:, None] == np.arange(2 * NA_KH)[None, :]).astype(np.float32)
    sel_c = (np.arange(HEAD)[:, None] == idx_c.reshape(-1)[None, :]).astype(np.float32)
    return jnp.asarray(sel_r), jnp.asarray(sel_c)


def _rpb_pad(rpb):
    return jnp.pad(rpb, ((0, 0), (0, 2 * NA_KH - rpb.shape[1]), (0, HEAD - rpb.shape[2])))


def _bias_table(name, rpb):
    H = rpb.shape[0]
    sel_r, sel_c = _bias_onehots()
    hi = lax.Precision.HIGHEST

    def body(r_ref, sr_ref, sc_ref, o_ref):
        t = jnp.dot(sr_ref[...], r_ref[...], precision=hi, preferred_element_type=F32)
        o_ref[...] = jnp.dot(t, sc_ref[...], precision=hi, preferred_element_type=F32)

    n_r, n_c = sel_r.shape[0], sel_c.shape[1]
    tab = pl.pallas_call(
        body, name=name, grid=(H,),
        in_specs=[pl.BlockSpec((None, 2 * NA_KH, HEAD), lambda h: (h, 0, 0)), pl.BlockSpec(sel_r.shape, lambda h: (0, 0)),
                  pl.BlockSpec(sel_c.shape, lambda h: (0, 0))],
        out_specs=pl.BlockSpec((None, n_r, n_c), lambda h: (h, 0, 0)), out_shape=jax.ShapeDtypeStruct((H, n_r, n_c), F32),
        compiler_params=_cparams(("parallel",)), **_CALL_KW)(_rpb_pad(rpb), sel_r, sel_c)
    tab = tab.reshape(H, NA_KH, NA_KH, GRID_W, GRID_W).transpose(0, 1, 3, 2, 4)
    return tab.reshape(H, NA_KH, GRID_W, NA_KH * GRID_W)


def _bias_table_bwd(name, dbias, rpb_shape):
    H = dbias.shape[0]
    sel_r, sel_c = _bias_onehots()
    hi = lax.Precision.HIGHEST
    d = dbias.reshape(H, NA_KH, GRID_W, NA_KH, GRID_W).transpose(0, 1, 3, 2, 4).reshape(H, NA_KH * NA_KH, GRID_W * GRID_W)

    def body(d_ref, sr_ref, sc_ref, o_ref):
        dt = lax.dot_general(d_ref[...], sc_ref[...], _DIMS["nt"], precision=hi, preferred_element_type=F32)
        o_ref[...] = lax.dot_general(sr_ref[...], dt, _DIMS["tn"], precision=hi, preferred_element_type=F32)

    out = pl.pallas_call(
        body, name=name, grid=(H,),
        in_specs=[pl.BlockSpec((None,) + d.shape[1:], lambda h: (h, 0, 0)), pl.BlockSpec(sel_r.shape, lambda h: (0, 0)),
                  pl.BlockSpec(sel_c.shape, lambda h: (0, 0))],
        out_specs=pl.BlockSpec((None, 2 * NA_KH, HEAD), lambda h: (h, 0, 0)),
        out_shape=jax.ShapeDtypeStruct((H, 2 * NA_KH, HEAD), F32),
        compiler_params=_cparams(("parallel",)), **_CALL_KW)(d, sel_r, sel_c)
    return out[:, :rpb_shape[1], :rpb_shape[2]]


def _lane_row(v):
    v = v.reshape(-1)
    return jnp.pad(v, (0, HEAD - v.shape[0])).reshape(1, HEAD)


def _layer_weights(dm, l, chip, gathered, own):
    g_in, g_out, g_gate, g_up, g_down = [
        [jnp.where(chip == k, o[l], g[k]) for k in range(N_CHIPS)] for g, o in zip(gathered, own)]
    g_out, g_down = jnp.stack(g_out), jnp.stack(g_down)
    w_in = jnp.concatenate(g_in, axis=1)
    w_main = jnp.concatenate([w_in[:, :dm.o_ab], w_in[:, dm.o_ab + dm.n_ab:]], axis=1)
    w_ab = jnp.pad(w_in[:, dm.o_ab:dm.o_ab + dm.n_ab], ((0, 0), (0, HEAD - dm.n_ab)))
    w_out = g_out.reshape(dm.D, dm.D)
    w_out = jnp.concatenate([w_out[dm.swa_q:dm.swa_q + dm.Wd], w_out[:dm.swa_q], w_out[dm.swa_q + dm.Wd:]], axis=0)
    w_gu = jnp.concatenate(g_gate + g_up, axis=1)
    w_down = g_down.reshape(dm.ffn, dm.D)
    return dict(main=w_main, ab=w_ab, out=w_out, gu=w_gu, down=w_down)


def _layer_fwd(dm, x, W, sp, modv, cos, sin, comm=None):
    ct = dm.ctx_tiles
    h = _norm_mod("norm1", x, sp["norm1_g"], modv, 0, ct)
    P = _matmul("in_proj", h, W["main"], "nn")
    Pab = _matmul("in_proj_ab", h, W["ab"], "nn", tn=HEAD)
    one = jnp.ones((1, HEAD), F32)
    qa = _head_prep("swa_q_prep", P, 0, dm.swa_h, sp["swa_q_g"], cos, sin, dm.t_pad, True, True)
    ka = _head_prep("swa_k_prep", P, 2, dm.kv_h, sp["swa_k_g"], cos, sin, dm.t_pad, True, True)
    va = _head_prep("swa_v_prep", P, 3, dm.kv_h, one, cos, sin, dm.t_pad, False, False)
    qn = _head_prep("na_q_prep", P, 10, dm.na_h, sp["na_q_g"], cos, sin, dm.t_pad, True, False)
    kn = _head_prep("na_k_prep", P, 11, dm.na_h, sp["na_k_g"], cos, sin, dm.t_pad, True, False)
    vn = _head_prep("na_v_prep", P, 12, dm.na_h, one, cos, sin, dm.t_pad, False, False)
    no_sink = jnp.full((dm.na_h,), NEG, F32)
    bias = _bias_table("na_bias", sp["na_rpb"])
    pre = _conv_fwd("dn_conv", P, 1, sp["dn_conv_w"], ct)
    a_row, dt_row = _lane_row(sp["dn_A_log"]), _lane_row(sp["dn_dt_bias"])
    qh, kh, vh, la, be = _gdn_point("dn_point", pre, Pab, a_row, dt_row, dm.dn_h)
    o2, states, invs, hosted = _gdn_scan("dn_scan", qh, kh, vh, la, be, dm.dn_h, dm.ctx, comm)
    mix = _gated_norm("dn_out_norm", o2, P, 4, sp["dn_out_g"], dm.dn_h, dm.D)
    mix = _attn_fwd("swa_fwd", "swa", qa, ka, va, sp["swa_sink"], None, dm.ctx, dm.seq, mix, dm.Wd // HEAD)
    mix = _attn_fwd("na_fwd", "na", qn, kn, vn, no_sink, bias, dm.ctx, dm.seq, mix, (dm.Wd + dm.swa_q) // HEAD)
    ao = _matmul("out_proj", mix, W["out"], "nn")
    x1 = _resid_gate("resid1", x, ao, modv, 2, ct)
    h2 = _norm_mod("norm2", x1, sp["norm2_g"], modv, 3, ct)
    gu = _matmul("ffn_gate_up", h2, W["gu"], "nn")
    act = _swiglu("ffn_act", gu)
    fo = _matmul("ffn_down", act, W["down"], "nn")
    x2 = _resid_gate("resid2", x1, fo, modv, 5, ct)
    res = dict(x=x, h=h, P=P, Pab=Pab, qa=qa, ka=ka, va=va, qn=qn, kn=kn, vn=vn, bias=bias, no_sink=no_sink, pre=pre,
               a_row=a_row, dt_row=dt_row, qh=qh, kh=kh, vh=vh, la=la, be=be, o2=o2, states=states, invs=invs, mix=mix,
               ao=ao, x1=x1, h2=h2, gu=gu, act=act, fo=fo)
    return x2, res, hosted


def _layer_bwd(dm, dx2, W, sp, modv, cos, sin, r, host=None):
    ct = dm.ctx_tiles
    T, D = dm.T, dm.D
    one = jnp.ones((1, HEAD), F32)
    dfo, dgate2 = _resid_gate_bwd("resid2_bwd", dx2, r["fo"], modv, 5, ct)
    dact = _matmul("ffn_down_dx", dfo, W["down"], "nt")
    dw_down = _matmul("ffn_down_dw", r["act"], dfo, "tn")
    dgu = _swiglu_bwd("ffn_act_bwd", r["gu"], dact)
    dh2 = _matmul("ffn_gate_up_dx", dgu, W["gu"], "nt")
    dw_gu = _matmul("ffn_gate_up_dw", r["h2"], dgu, "tn")
    zero = jnp.zeros((T, D), F32)
    dx1, dn2g, dsh2, dsc2 = _norm_mod_bwd("norm2_bwd", r["x1"], sp["norm2_g"], modv, 3, dh2, zero, dx2, ct)
    dao, dgate1 = _resid_gate_bwd("resid1_bwd", dx1, r["ao"], modv, 2, ct)
    dmix = _matmul("out_proj_dx", dao, W["out"], "nt")
    dw_out = _matmul("out_proj_dw", r["mix"], dao, "tn")
    do_, dP, d_out_g = _gated_norm_bwd("dn_out_norm_bwd", r["o2"], r["P"], 4, sp["dn_out_g"], dm.dn_h, dmix, 0)
    comm = host(dict(out=dw_out, gu=dw_gu, down=dw_down)) if host is not None else None
    dq2, dk2, dv2, dla2, dbe2, hosted = _gdn_scan_bwd("dn_scan_bwd", r["qh"], r["kh"], r["vh"], r["la"], r["be"], r["states"],
                                                      r["invs"], do_, dm.dn_h, dm.ctx, comm)
    dpre, dPab, d_alog, d_dtb = _gdn_point_bwd("dn_point_bwd", r["pre"], r["Pab"], r["a_row"], r["dt_row"], dm.dn_h,
                                               dq2, dk2, dv2, dla2, dbe2)
    dP, d_conv = _conv_bwd("dn_conv_bwd", r["P"], 1, sp["dn_conv_w"], dpre, ct, dP)
    dqa, dka, dva, dsink, _ = _attn_bwd("swa_bwd", "swa", r["qa"], r["ka"], r["va"], sp["swa_sink"], None, dmix,
                                        dm.Wd // HEAD, dm.ctx, dm.seq)
    dP, d_swa_q_g = _head_prep_bwd("swa_q_prep_bwd", r["P"], 0, dm.swa_h, sp["swa_q_g"], cos, sin, dqa, True, True, dP)
    dP, d_swa_k_g = _head_prep_bwd("swa_k_prep_bwd", r["P"], 2, dm.kv_h, sp["swa_k_g"], cos, sin, dka, True, True, dP)
    dP, _ = _head_prep_bwd("swa_v_prep_bwd", r["P"], 3, dm.kv_h, one, cos, sin, dva, False, False, dP)
    dqn, dkn, dvn, _, dbias = _attn_bwd("na_bwd", "na", r["qn"], r["kn"], r["vn"], r["no_sink"], r["bias"], dmix,
                                        (dm.Wd + dm.swa_q) // HEAD, dm.ctx, dm.seq)
    dP, d_na_q_g = _head_prep_bwd("na_q_prep_bwd", r["P"], 10, dm.na_h, sp["na_q_g"], cos, sin, dqn, True, False, dP)
    dP, d_na_k_g = _head_prep_bwd("na_k_prep_bwd", r["P"], 11, dm.na_h, sp["na_k_g"], cos, sin, dkn, True, False, dP)
    dP, _ = _head_prep_bwd("na_v_prep_bwd", r["P"], 12, dm.na_h, one, cos, sin, dvn, False, False, dP)
    d_rpb = _bias_table_bwd("na_bias_bwd", dbias, sp["na_rpb"].shape)
    dw_main = _matmul("in_proj_dw", r["h"], dP, "tn")
    dw_ab = _matmul("in_proj_ab_dw", r["h"], dPab, "tn", tn=HEAD)
    dh = _matmul("in_proj_dx", dP, W["main"], "nt")
    dh_b = _matmul("in_proj_ab_dx", dPab, W["ab"], "nt")
    dx, dn1g, dsh1, dsc1 = _norm_mod_bwd("norm1_bwd", r["x"], sp["norm1_g"], modv, 0, dh, dh_b, dx1, ct)
    dmodv = jnp.concatenate([dsh1, dsc1, dgate1, dsh2, dsc2, dgate2], axis=1)
    big = dict(main=dw_main, ab=dw_ab, out=dw_out, gu=dw_gu, down=dw_down)
    small = dict(norm1_g=dn1g[0], norm2_g=dn2g[0], swa_q_g=d_swa_q_g[0], swa_k_g=d_swa_k_g[0], swa_sink=dsink[:, 0, 0],
                 dn_conv_w=d_conv[:DN_CONV], dn_A_log=d_alog[0, :2 * dm.dn_h].reshape(2, dm.dn_h),
                 dn_dt_bias=d_dtb[0, :2 * dm.dn_h].reshape(2, dm.dn_h), dn_out_g=d_out_g[0], na_q_g=d_na_q_g[0],
                 na_k_g=d_na_k_g[0], na_rpb=d_rpb)
    return dx, big, small, dmodv, hosted


def _cols(w):
    return w.reshape(w.shape[0], N_CHIPS, -1).transpose(1, 0, 2)


def _rows(w):
    return w.reshape(N_CHIPS, -1, w.shape[1])


def _chunks_in(dm, bigs):
    g = [_cols(jnp.concatenate([b["main"][:, :dm.o_ab], b["ab"][:, :dm.n_ab], b["main"][:, dm.o_ab:]], axis=1)) for b in bigs]
    return jnp.stack(g, axis=1).astype(BF16)


def _chunks_rest(dm, bigs):
    g_out = [_rows(jnp.concatenate([b["out"][dm.Wd:dm.Wd + dm.swa_q], b["out"][:dm.Wd], b["out"][dm.Wd + dm.swa_q:]], axis=0))
             for b in bigs]
    g_gate = [_cols(b["gu"][:, :dm.ffn]) for b in bigs]
    g_up = [_cols(b["gu"][:, dm.ffn:]) for b in bigs]
    g_down = [_rows(b["down"]) for b in bigs]
    return [jnp.stack(g, axis=1).astype(BF16) for g in (g_out, g_gate, g_up, g_down)]


SMALL = ("norm1_g", "norm2_g", "swa_q_g", "swa_k_g", "swa_sink", "dn_conv_w", "dn_A_log", "dn_dt_bias", "dn_out_g",
         "na_q_g", "na_k_g", "na_rpb")


def _pack(arrs):
    flat = jnp.concatenate([a.reshape(-1).astype(F32) for a in arrs])
    n = flat.shape[0]
    rows = -(-n // (8 * HEAD)) * 8
    return jnp.pad(flat, (0, rows * HEAD - n)).reshape(rows, HEAD)


def _unpack(packed, like):
    flat = packed.reshape(-1)
    out, o = [], 0
    for a in like:
        out.append(flat[o:o + a.size].reshape(a.shape))
        o += a.size
    return out


def _sum_devices(name, g, which):
    _, R, _ = g.shape

    def body(g_ref, o_ref):
        acc = g_ref[which[0]]
        for b in which[1:]:
            acc = acc + g_ref[b]
        o_ref[...] = acc

    return pl.pallas_call(
        body, name=name, grid=(R // 8,), in_specs=[pl.BlockSpec((N_DEV, 8, HEAD), lambda i: (0, i, 0))],
        out_specs=pl.BlockSpec((8, HEAD), lambda i: (i, 0)), out_shape=jax.ShapeDtypeStruct((R, HEAD), F32),
        compiler_params=_cparams(("parallel",)), **_CALL_KW)(g)


def _silu_rows(name, c_rows):
    return _elementwise(name, lambda c: (c * _sigmoid(c),), [c_rows], [BF16])[0]


def _ada_cotangent(name, dm_all, b_ada_shape):
    _, L, _, N6 = dm_all.shape
    tn = _pick(N6, (1024, 512, 256, 128))

    def body(d_ref, o_ref, b_ref):
        csum = d_ref[0, 0:1, :]
        for b in range(1, N_DEV):
            csum = csum + d_ref[b, 0:1, :]
        tot = csum
        for b in range(N_DEV):
            o_ref[b:b + 1, :] = d_ref[b, 1:2, :]
            tot = tot + d_ref[b, 1:2, :]
        first = lax.broadcasted_iota(jnp.int32, (8, tn), 0) == 0
        o_ref[N_DEV:, :] = jnp.where(first, jnp.broadcast_to(csum, (8, tn)), 0.0)
        b_ref[...] = jnp.broadcast_to(tot, (8, tn))

    return pl.pallas_call(
        body, name=name, grid=(L, N6 // tn), in_specs=[pl.BlockSpec((N_DEV, None, 2, tn), lambda l, j: (0, l, 0, j))],
        out_specs=[pl.BlockSpec((None, 16, tn), lambda l, j: (l, 0, j)), pl.BlockSpec((None, 8, tn), lambda l, j: (l, 0, j))],
        out_shape=[jax.ShapeDtypeStruct((L, 16, N6), F32), jax.ShapeDtypeStruct((L, 8, N6), F32)],
        compiler_params=_cparams(("parallel", "parallel")), **_CALL_KW)(dm_all)


def kernel(x, c, ctx, c_ctx, w_ada, b_ada, norm1_g, norm2_g, w_in, swa_q_g, swa_k_g, swa_sink, dn_conv_w, dn_A_log, dn_dt_bias, dn_out_g, na_q_g, na_k_g, na_rpb, w_out, w_gate, w_up, w_down, loss_target, m_c_ctx, m_w_ada, m_b_ada, m_norm1_g, m_norm2_g, m_w_in, m_swa_q_g, m_swa_k_g, m_swa_sink, m_dn_conv_w, m_dn_A_log, m_dn_dt_bias, m_dn_out_g, m_na_q_g, m_na_k_g, m_na_rpb, m_w_out, m_w_gate, m_w_up, m_w_down, v_c_ctx, v_w_ada, v_b_ada, v_norm1_g, v_norm2_g, v_w_in, v_swa_q_g, v_swa_k_g, v_swa_sink, v_dn_conv_w, v_dn_A_log, v_dn_dt_bias, v_dn_out_g, v_na_q_g, v_na_k_g, v_na_rpb, v_w_out, v_w_gate, v_w_up, v_w_down):
    L = w_in.shape[0]
    D, seq, n_ctx = x.shape[-1], x.shape[1], ctx.shape[1]
    dm = _Dims(D, seq, n_ctx, w_gate.shape[-1] * N_CHIPS)
    xi, yi, ci = _axes()
    chip = 2 * xi + yi
    dev = 4 * xi + 2 * yi + ci
    n6 = 6 * D
    n6s = n6 // N_CHIPS

    shards = [_elementwise(f"cast_{n}", lambda w: (w,), [w], [BF16])[0]
              for n, w in (("w_in", w_in), ("w_out", w_out), ("w_gate", w_gate), ("w_up", w_up), ("w_down", w_down))]
    assert L == 2
    gathered0 = _gather_d2d("gather_w0_d2d", _exchange("gather_w0_ici", _gather_ici_comm(shards, 0)))
    Ws = [_layer_weights(dm, 0, chip, gathered0, shards), None]
    conv_all = _allgather_small("gather_conv_w", _pack([dn_conv_w]))
    conv_full = jnp.concatenate([_unpack(conv_all[2 * k], [dn_conv_w])[0] for k in range(N_CHIPS)], axis=-1)

    c_all = _allgather_small("gather_c", _pack([c]))
    c_rows = jnp.concatenate([c_all[:, :D // HEAD].reshape(N_DEV, D), c_ctx[None], jnp.zeros((16 - N_DEV - 1, D), F32)], axis=0)
    a_rows = _silu_rows("ada_silu", c_rows)
    b_sh = lax.dynamic_slice_in_dim(b_ada, chip * n6s, n6s, axis=1)
    mod_sh = [_matmul(f"ada_mod{l}", a_rows, w_ada[l], "nn", tm=16) for l in range(L)]
    mod_all = _allgather_small("gather_mod", _pack(mod_sh))
    mods = []
    for l in range(L):
        per_chip = [_unpack(mod_all[2 * k], mod_sh)[l] for k in range(N_CHIPS)]
        mods.append(jnp.concatenate(per_chip, axis=1))
    modvs = []
    for l in range(L):
        rows = jnp.stack([mods[l][N_DEV], lax.dynamic_index_in_dim(mods[l], dev, 0, keepdims=False)])
        modvs.append(_elementwise(f"ada_bias{l}", lambda m, b: (m + b,), [rows, jnp.broadcast_to(b_ada[l][None], (2, n6))], [F32])[0]
                     .reshape(2, 6, D))

    cos, sin = _rope_tables(dm)
    sps = [dict(norm1_g=norm1_g[l][None], norm2_g=norm2_g[l][None], swa_q_g=swa_q_g[l][None], swa_k_g=swa_k_g[l][None],
                swa_sink=swa_sink[l], dn_conv_w=conv_full[l], dn_A_log=dn_A_log[l], dn_dt_bias=dn_dt_bias[l],
                dn_out_g=dn_out_g[l][None], na_q_g=na_q_g[l][None], na_k_g=na_k_g[l][None], na_rpb=na_rpb[l]) for l in range(L)]
    xs = jnp.concatenate([ctx[0], x[0]], axis=0)
    ress = [None] * L
    xs, ress[0], gathered1 = _layer_fwd(dm, xs, Ws[0], sps[0], modvs[0], cos, sin, _gather_ici_comm(shards, 1))
    Ws[1] = _layer_weights(dm, 1, chip, _gather_d2d("gather_w1_d2d", gathered1), shards)
    xs, ress[1], _ = _layer_fwd(dm, xs, Ws[1], sps[1], modvs[1], cos, sin)
    loss_blk, dxs = _loss_and_grad("loss", xs, loss_target[0], dm.ctx_tiles)
    loss = lax.psum(loss_blk[0, 0], ("x", "y", "c"))

    bigs, smalls, dmodvs = [None] * L, [None] * L, [None] * L
    pairs = {}

    def host1(big1):
        pairs["a"] = _reduce_pre("a", _chunks_rest(dm, [big1]))
        return _reduce_ici_comm(pairs["a"])

    def host0(big0):
        pairs["b"] = _reduce_pre("b", [_chunks_in(dm, [bigs[1]])] + _chunks_rest(dm, [big0]))
        return _reduce_ici_comm(pairs["b"])

    dxs, bigs[1], smalls[1], dmodvs[1], got_a = _layer_bwd(dm, dxs, Ws[1], sps[1], modvs[1], cos, sin, ress[1], host1)
    dxs, bigs[0], smalls[0], dmodvs[0], got_b = _layer_bwd(dm, dxs, Ws[0], sps[0], modvs[0], cos, sin, ress[0], host0)
    rest1 = _reduce_post("a", pairs["a"], got_a)
    g_in1, *rest0 = _reduce_post("b", pairs["b"], got_b)
    pair_c = _reduce_pre("c", [_chunks_in(dm, [bigs[0]])])
    (g_in0,) = _reduce_post("c", pair_c, _exchange("reduce_c_ici", _reduce_ici_comm(pair_c)))
    g_in = jnp.concatenate([g_in0, g_in1], axis=0)
    g_out, g_gate, g_up, g_down = [jnp.concatenate([a0, a1], axis=0) for a0, a1 in zip(rest0, rest1)]
    grad_x = dxs[n_ctx:][None]

    dm_mine = jnp.stack([d.reshape(2, n6) for d in dmodvs])
    dm_all = _allgather_small("gather_dmod", _pack([dm_mine]))
    dm_all = jnp.stack([_unpack(dm_all[b], [dm_mine])[0] for b in range(N_DEV)])
    dm_rows, d_b_ada = _ada_cotangent("ada_cot", dm_all, b_ada.shape)
    dm_sh = lax.dynamic_slice_in_dim(dm_rows, chip * n6s, n6s, axis=2).astype(BF16)
    g_w_ada = jnp.stack([_matmul(f"ada_dw{l}", a_rows, dm_sh[l], "tn") for l in range(L)])
    dc_part = [_matmul(f"ada_dc{l}", dm_sh[l], w_ada[l], "nt", tm=16) for l in range(L)]
    dc_mine = dc_part[0][N_DEV]
    for l in range(1, L):
        dc_mine = dc_mine + dc_part[l][N_DEV]

    small_list = [jnp.stack([smalls[l][n] for l in range(L)]) for n in SMALL]
    sm_all = _allgather_small("gather_small", _pack(small_list + [dc_mine]))
    sm_sum = _sum_devices("sum_small", sm_all, tuple(range(N_DEV)))
    dc_sum = _sum_devices("sum_dc", sm_all, tuple(range(0, N_DEV, 2)))
    g_small = dict(zip(SMALL, _unpack(sm_sum, small_list)))
    dcs = _unpack(dc_sum, small_list + [dc_mine])[-1]
    def silu_bwd(d, cc):
        s = _sigmoid(cc)
        return (d * (s * (1.0 + cc * (1.0 - s))),)

    g_c_ctx = _elementwise("c_ctx_silu_bwd", silu_bwd, [dcs.reshape(-1, HEAD), c_ctx.reshape(-1, HEAD)], [F32])[0]
    wd3 = dn_conv_w.shape[-1]
    g_small["dn_conv_w"] = lax.dynamic_slice_in_dim(g_small["dn_conv_w"], chip * wd3, wd3, axis=2)

    grads = dict(g_small, c_ctx=g_c_ctx, w_ada=g_w_ada, b_ada=d_b_ada[:, 0], w_in=g_in, w_out=g_out, w_gate=g_gate, w_up=g_up,
                 w_down=g_down)
    weights = dict(c_ctx=c_ctx, w_ada=w_ada, b_ada=b_ada, norm1_g=norm1_g, norm2_g=norm2_g, w_in=w_in, swa_q_g=swa_q_g,
                   swa_k_g=swa_k_g, swa_sink=swa_sink, dn_conv_w=dn_conv_w, dn_A_log=dn_A_log, dn_dt_bias=dn_dt_bias,
                   dn_out_g=dn_out_g, na_q_g=na_q_g, na_k_g=na_k_g, na_rpb=na_rpb, w_out=w_out, w_gate=w_gate, w_up=w_up,
                   w_down=w_down)
    ms = dict(c_ctx=m_c_ctx, w_ada=m_w_ada, b_ada=m_b_ada, norm1_g=m_norm1_g, norm2_g=m_norm2_g, w_in=m_w_in, swa_q_g=m_swa_q_g,
              swa_k_g=m_swa_k_g, swa_sink=m_swa_sink, dn_conv_w=m_dn_conv_w, dn_A_log=m_dn_A_log, dn_dt_bias=m_dn_dt_bias,
              dn_out_g=m_dn_out_g, na_q_g=m_na_q_g, na_k_g=m_na_k_g, na_rpb=m_na_rpb, w_out=m_w_out, w_gate=m_w_gate, w_up=m_w_up,
              w_down=m_w_down)
    vs = dict(c_ctx=v_c_ctx, w_ada=v_w_ada, b_ada=v_b_ada, norm1_g=v_norm1_g, norm2_g=v_norm2_g, w_in=v_w_in, swa_q_g=v_swa_q_g,
              swa_k_g=v_swa_k_g, swa_sink=v_swa_sink, dn_conv_w=v_dn_conv_w, dn_A_log=v_dn_A_log, dn_dt_bias=v_dn_dt_bias,
              dn_out_g=v_dn_out_g, na_q_g=v_na_q_g, na_k_g=v_na_k_g, na_rpb=v_na_rpb, w_out=v_w_out, w_gate=v_w_gate, w_up=v_w_up,
              w_down=v_w_down)
    order = ("c_ctx", "w_ada", "b_ada", "norm1_g", "norm2_g", "w_in", "swa_q_g", "swa_k_g", "swa_sink", "dn_conv_w", "dn_A_log",
             "dn_dt_bias", "dn_out_g", "na_q_g", "na_k_g", "na_rpb", "w_out", "w_gate", "w_up", "w_down")
    big_names = ("w_ada", "w_in", "w_out", "w_gate", "w_up", "w_down")
    grads = {n: grads[n].reshape(weights[n].shape) for n in order}
    delta, new_m, new_v = {}, {}, {}
    for n in big_names:
        delta[n], new_m[n], new_v[n] = _adamw(f"adamw_{n}", weights[n], grads[n], ms[n], vs[n])
    small_names = [n for n in order if n not in big_names]
    packed = [_pack([d[n] for n in small_names]) for d in (weights, grads, ms, vs)]
    outs = _adamw("adamw_small", *packed)
    like = [weights[n] for n in small_names]
    for d, o in zip((delta, new_m, new_v), outs):
        d.update(dict(zip(small_names, _unpack(o, like))))
    return (loss, grad_x, *[grads[n] for n in order], *[delta[n] for n in order], *[new_m[n] for n in order],
            *[new_v[n] for n in order])
```

```python
import functools
import math

import jax
import jax.numpy as jnp
import numpy as np
from jax import lax
from jax.experimental import pallas as pl
from jax.experimental.pallas import tpu as pltpu

F32, BF16 = jnp.float32, jnp.bfloat16
MESH = pl.DeviceIdType.MESH

GRID_W = 64
HEAD = 128
SWA_WINDOW = 128
DN_CONV = 5
DN_CHUNK = 64
NA_KH, NA_KW = 8, 16
ROPE_THETA = 10000.0
EPS = 1e-6
ADAM_LR, ADAM_B1, ADAM_B2, ADAM_EPS, ADAM_WD, ADAM_STEP = 0.001, 0.9, 0.999, 1e-08, 0.01, 10
N_CHIPS = 4
N_DEV = 8
TOK = 256
VMEM_LIMIT = 56 * 2 ** 20

_CALL_KW = {}


def _cparams(sem=None, **kw):
    if sem is not None:
        kw["dimension_semantics"] = sem
    return pltpu.CompilerParams(vmem_limit_bytes=VMEM_LIMIT, **kw)


def _pick(n, cands):
    for cnd in cands:
        if n % cnd == 0:
            return cnd
    raise ValueError(f"no tile for {n} in {cands}")


def _axes():
    return lax.axis_index("x"), lax.axis_index("y"), lax.axis_index("c")


def _matmul(name, a, b, kind, out_dtype=F32, tm=None, tn=None, tk=None, comm=None):
    if kind == "nn":
        (M, K), (K2, N) = a.shape, b.shape
    elif kind == "nt":
        (M, K), (N, K2) = a.shape, b.shape
    else:
        (K, M), (K2, N) = a.shape, b.shape
    assert K == K2, (name, a.shape, b.shape)
    tm = tm or _pick(M, (1024, 512, 256, 128) if kind == "tn" else (1088, 1024, 704, 512, 256, 128, 64, 32, 16, 8))
    tn = tn or _pick(N, (512, 256, 128))
    tk = tk or (K if K <= 4352 else _pick(K, (3328, 2816, 2048, 1024, 512)))
    nk = K // tk
    dims = {"nn": (((1,), (0,)), ((), ())), "nt": (((1,), (1,)), ((), ())), "tn": (((0,), (0,)), ((), ()))}[kind]

    ni, nj = M // tm, N // tn

    def body(*refs):
        (a_ref, b_ref), (o_ref,), scr, cref = _hosted(comm, 2, 1, refs)
        i, j, k = pl.program_id(0), pl.program_id(1), pl.program_id(2)
        _hosted_start(comm, cref, (i == 0) & (j == 0) & (k == 0))
        part = lax.dot_general(a_ref[...].astype(BF16), b_ref[...].astype(BF16), dims, preferred_element_type=F32)
        if nk == 1:
            o_ref[...] = part.astype(o_ref.dtype)
        else:
            acc = scr[0]

            @pl.when(k == 0)
            def _():
                acc[...] = part

            @pl.when(k > 0)
            def _():
                acc[...] += part

            @pl.when(k == nk - 1)
            def _():
                o_ref[...] = acc[...].astype(o_ref.dtype)
        _hosted_wait(comm, cref, (i == ni - 1) & (j == nj - 1) & (k == nk - 1))

    a_spec = {"nn": pl.BlockSpec((tm, tk), lambda i, j, k: (i, k)), "nt": pl.BlockSpec((tm, tk), lambda i, j, k: (i, k)),
              "tn": pl.BlockSpec((tk, tm), lambda i, j, k: (k, i))}[kind]
    b_spec = {"nn": pl.BlockSpec((tk, tn), lambda i, j, k: (k, j)), "nt": pl.BlockSpec((tn, tk), lambda i, j, k: (j, k)),
              "tn": pl.BlockSpec((tk, tn), lambda i, j, k: (k, j))}[kind]
    any_spec = pl.BlockSpec(memory_space=pl.ANY)
    c_ins, c_sds, c_sems = (comm.ins, comm.out_sds(), comm.sem_shapes()) if comm is not None else ([], [], [])
    sem = ("parallel", "parallel", "arbitrary") if comm is None else ("arbitrary",) * 3
    res = pl.pallas_call(
        body, name=name, grid=(ni, nj, nk), in_specs=[a_spec, b_spec] + [any_spec] * len(c_ins),
        out_specs=[pl.BlockSpec((tm, tn), lambda i, j, k: (i, j))] + [any_spec] * len(c_sds),
        out_shape=[jax.ShapeDtypeStruct((M, N), out_dtype)] + c_sds,
        scratch_shapes=([pltpu.VMEM((tm, tn), F32)] if nk > 1 else []) + c_sems,
        compiler_params=_cparams(sem, has_side_effects=comm is not None), **_CALL_KW)(a, b, *c_ins)
    return res[0] if comm is None else (res[0], list(res[1:]))


class _Comm:
    def __init__(self, ins, out_shapes, plan, n_local, n_remote, aliases=None):
        self.ins, self.out_shapes, self.plan = list(ins), list(out_shapes), plan
        self.n_local, self.n_remote, self.aliases = n_local, n_remote, aliases or {}

    def sem_shapes(self):
        return [pltpu.SemaphoreType.DMA((max(self.n_remote, 1),)), pltpu.SemaphoreType.DMA((max(self.n_remote, 1),)),
                pltpu.SemaphoreType.DMA((max(self.n_local, 1),))]

    def out_sds(self):
        return [jax.ShapeDtypeStruct(s, d) for s, d in self.out_shapes]

    def copies(self, in_refs, out_refs, sems):
        send_sems, recv_sems, loc_sems = sems
        x, y, c = _axes()
        local, remote = self.plan(x, y, c, in_refs, out_refs)
        assert len(local) == self.n_local and len(remote) == self.n_remote, (len(local), len(remote))
        lcs = [pltpu.make_async_copy(s, d, loc_sems.at[i]) for i, (s, d) in enumerate(local)]
        rcs = [pltpu.make_async_remote_copy(src_ref=s, dst_ref=d, send_sem=send_sems.at[i], recv_sem=recv_sems.at[i],
                                            device_id=dev, device_id_type=MESH) for i, (s, d, dev) in enumerate(remote)]
        return lcs + rcs


def _exchange(name, comm):
    n_in, n_out = len(comm.ins), len(comm.out_shapes)

    def body(*refs):
        cps = comm.copies(refs[:n_in], refs[n_in:n_in + n_out], refs[n_in + n_out:])
        for cp in cps:
            cp.start()
        for cp in cps:
            cp.wait()

    any_spec = pl.BlockSpec(memory_space=pl.ANY)
    return pl.pallas_call(
        body, name=name, in_specs=[any_spec] * n_in, out_specs=[any_spec] * n_out, out_shape=comm.out_sds(),
        scratch_shapes=comm.sem_shapes(), input_output_aliases=comm.aliases,
        compiler_params=pltpu.CompilerParams(has_side_effects=True), **_CALL_KW)(*comm.ins)


def _chip_of(k):
    return k // 2, k % 2


def _gather_ici_comm(shards, layer):
    def plan(x, y, c, ins, outs):
        me = 2 * x + y
        remote = []
        for w, g in zip(ins, outs):
            half = w.shape[1] // 2
            rows = pl.ds(c * half, half)
            for j in (1, 2, 3):
                px, py = _chip_of(me ^ j)
                remote.append((w.at[layer, rows], g.at[me, rows], (px, py, c)))
        return [], remote

    return _Comm(shards, [((N_CHIPS,) + w.shape[1:], w.dtype) for w in shards], plan, 0, 3 * len(shards))


def _gather_d2d(name, gath):
    def plan(x, y, c, ins, outs):
        me = 2 * x + y
        remote = []
        for g in outs:
            half = g.shape[1] // 2
            rows = pl.ds(c * half, half)
            for j in (1, 2, 3):
                remote.append((g.at[me ^ j, rows], g.at[me ^ j, rows], (x, y, 1 - c)))
        return [], remote

    n = len(gath)
    return _exchange(name, _Comm(gath, [(g.shape, g.dtype) for g in gath], plan, 0, 3 * n, aliases={i: i for i in range(n)}))


def _elementwise(name, fn, ins, out_dtypes, block_rows=None, n_out=None):
    shape = ins[0].shape
    lead, (R, C) = shape[:-2], shape[-2:]
    budget = (16 * 2 ** 20) // (8 * (len(ins) + len(out_dtypes)) * (-(-C // 128) * 128))
    br = block_rows or _pick(R, [r for r in (512, 256, 128, 352, 64, 32, 16, 8) if r <= max(budget, 8)] + [R])
    nl = len(lead)

    def body(*refs):
        outs = fn(*[r[...] for r in refs[:len(ins)]])
        for r, o in zip(refs[len(ins):], outs):
            r[...] = o.astype(r.dtype)

    blk = (None,) * nl + (br, C)
    spec = pl.BlockSpec(blk, lambda *g: tuple(g[:nl]) + (g[nl], 0))
    return pl.pallas_call(
        body, name=name, grid=tuple(lead) + (R // br,), in_specs=[spec] * len(ins), out_specs=[spec] * len(out_dtypes),
        out_shape=[jax.ShapeDtypeStruct(shape, d) for d in out_dtypes],
        compiler_params=_cparams(("parallel",) * (nl + 1)), **_CALL_KW)(*ins)


def _reduce_pre(tag, parts):
    n = len(parts)

    def plan_a(x, y, c, ins, outs):
        remote = []
        for p, r in zip(ins, outs):
            half = p.shape[2] // 2
            remote.append((p.at[:, :, pl.ds((1 - c) * half, half)], r, (x, y, 1 - c)))
        return [], remote

    halves = [((p.shape[0], p.shape[1], p.shape[2] // 2, p.shape[3]), p.dtype) for p in parts]
    got = _exchange(f"reduce_{tag}_d2d", _Comm(parts, halves, plan_a, 0, n))

    c = lax.axis_index("c")
    pair = []
    for idx, (p, r) in enumerate(zip(parts, got)):
        half = p.shape[2] // 2
        br = _pick(half, (512, 256, 352, 128, 64, 32, 16))
        nb = half // br

        def body(c_ref, p_ref, r_ref, o_ref):
            o_ref[...] = (p_ref[...].astype(F32) + r_ref[...].astype(F32)).astype(o_ref.dtype)

        blk = (None, None, br, p.shape[3])
        pair.append(pl.pallas_call(
            body, name=f"reduce_{tag}_pair{idx}",
            grid_spec=pltpu.PrefetchScalarGridSpec(
                num_scalar_prefetch=1, grid=(N_CHIPS, p.shape[1], nb),
                in_specs=[pl.BlockSpec(blk, lambda k, l, i, cr, nb=nb: (k, l, cr[0] * nb + i, 0)),
                          pl.BlockSpec(blk, lambda k, l, i, cr: (k, l, i, 0))],
                out_specs=pl.BlockSpec(blk, lambda k, l, i, cr: (k, l, i, 0))),
            out_shape=jax.ShapeDtypeStruct(r.shape, BF16),
            compiler_params=_cparams(("parallel",) * 3), **_CALL_KW)(jnp.reshape(c, (1,)).astype(jnp.int32), p, r))
    return pair


def _reduce_ici_comm(pair):
    def plan_b(x, y, c, ins, outs):
        me = 2 * x + y
        remote = []
        for p, r in zip(ins, outs):
            for j in (1, 2, 3):
                px, py = _chip_of(me ^ j)
                remote.append((p.at[me ^ j], r.at[me], (px, py, c)))
        return [], remote

    return _Comm(pair, [(p.shape, p.dtype) for p in pair], plan_b, 0, 3 * len(pair))


def _reduce_post(tag, pair, got):
    n = len(pair)
    c = lax.axis_index("c")
    me_chip = 2 * lax.axis_index("x") + lax.axis_index("y")

    sums = []
    for idx, r in enumerate(got):
        _, L, half, C = r.shape
        br = _pick(half, (512, 256, 352, 128, 64, 32, 16))
        nb = half // br

        def body(c_ref, p_ref, r_ref, o_ref):
            me = c_ref[1]
            acc = None
            for k in range(N_CHIPS):
                term = jnp.where(me == k, p_ref[k], r_ref[k]).astype(F32)
                acc = term if acc is None else acc + term
            o_ref[...] = acc

        blk4 = pl.BlockSpec((N_CHIPS, None, br, C), lambda l, i, cr: (0, l, i, 0))
        sums.append(pl.pallas_call(
            body, name=f"reduce_{tag}_sum{idx}",
            grid_spec=pltpu.PrefetchScalarGridSpec(
                num_scalar_prefetch=1, grid=(L, nb), in_specs=[blk4, blk4],
                out_specs=pl.BlockSpec((None, br, C), lambda l, i, cr, nb=nb: (l, cr[0] * nb + i, 0))),
            out_shape=jax.ShapeDtypeStruct((L, 2 * half, C), F32),
            compiler_params=_cparams(("parallel",) * 2), **_CALL_KW)(jnp.stack([c, me_chip]).astype(jnp.int32), pair[idx], r))

    def plan_c(x, y, c, ins, outs):
        remote = []
        for f in outs:
            half = f.shape[1] // 2
            rows = pl.ds(c * half, half)
            remote.append((f.at[:, rows], f.at[:, rows], (x, y, 1 - c)))
        return [], remote

    return _exchange(f"reduce_{tag}_bcast", _Comm(sums, [(s.shape, F32) for s in sums], plan_c, 0, n, aliases={i: i for i in range(n)}))


def _adamw_math(w, g, m, v):
    m = ADAM_B1 * m + (1.0 - ADAM_B1) * g
    v = ADAM_B2 * v + (1.0 - ADAM_B2) * (g * g)
    m_hat = m / (1.0 - ADAM_B1 ** ADAM_STEP)
    v_hat = v / (1.0 - ADAM_B2 ** ADAM_STEP)
    delta = -ADAM_LR * (m_hat / (jnp.sqrt(v_hat) + ADAM_EPS) + ADAM_WD * w)
    return delta, m, v


def _adamw(name, w, g, m, v):
    return _elementwise(name, _adamw_math, [w, g, m, v], [F32, F32, F32])


def _allgather_small(name, v):
    def plan(x, y, c, ins, outs):
        me = 4 * x + 2 * y + c
        (src,), (dst,) = ins, outs
        remote = []
        for j in range(1, N_DEV):
            p = me ^ j
            remote.append((src, dst.at[me], (p // 4, (p // 2) % 2, p % 2)))
        return [(src, dst.at[me])], remote

    return _exchange(name, _Comm([v], [((N_DEV,) + v.shape, v.dtype)], plan, 1, N_DEV - 1))[0]


def _seg_spec(rows, D, ctx_tiles):
    return pl.BlockSpec((None, rows, D), lambda i: (jnp.minimum(i // ctx_tiles, 1), 0, 0))


def _norm_mod(name, x, g, modv, r0, ctx_tiles):
    T, D = x.shape

    def body(x_ref, g_ref, m_ref, o_ref):
        xv = x_ref[...]
        r = lax.rsqrt(jnp.mean(xv * xv, axis=-1, keepdims=True) + EPS)
        y = xv * r * g_ref[...]
        o_ref[...] = (y * (1.0 + m_ref[r0 + 1:r0 + 2, :]) + m_ref[r0:r0 + 1, :]).astype(BF16)

    row = pl.BlockSpec((TOK, D), lambda i: (i, 0))
    return pl.pallas_call(
        body, name=name, grid=(T // TOK,), in_specs=[row, pl.BlockSpec((1, D), lambda i: (0, 0)), _seg_spec(6, D, ctx_tiles)],
        out_specs=row, out_shape=jax.ShapeDtypeStruct((T, D), BF16), compiler_params=_cparams(("parallel",)), **_CALL_KW)(x, g, modv)


def _norm_mod_bwd(name, x, g, modv, r0, dh, dh_b, dres, ctx_tiles):
    T, D = x.shape

    def body(x_ref, g_ref, m_ref, dh_ref, dhb_ref, dres_ref, dx_ref, dg_ref, dsh_ref, dsc_ref):
        i = pl.program_id(0)
        xv = x_ref[...]
        r = lax.rsqrt(jnp.mean(xv * xv, axis=-1, keepdims=True) + EPS)
        xn = xv * r
        y = xn * g_ref[...]
        dhv = dh_ref[...] + dhb_ref[...]

        @pl.when(i == 0)
        def _():
            dg_ref[...] = jnp.zeros_like(dg_ref)

        @pl.when((i == 0) | (i == ctx_tiles))
        def _():
            dsh_ref[...] = jnp.zeros_like(dsh_ref)
            dsc_ref[...] = jnp.zeros_like(dsc_ref)

        dsh_ref[...] += jnp.sum(dhv, axis=0, keepdims=True)
        dsc_ref[...] += jnp.sum(dhv * y, axis=0, keepdims=True)
        dy = dhv * (1.0 + m_ref[r0 + 1:r0 + 2, :])
        dg_ref[...] += jnp.sum(dy * xn, axis=0, keepdims=True)
        u = dy * g_ref[...]
        dx_ref[...] = dres_ref[...] + r * (u - xn * jnp.mean(u * xn, axis=-1, keepdims=True))

    row = pl.BlockSpec((TOK, D), lambda i: (i, 0))
    one = pl.BlockSpec((1, D), lambda i: (0, 0))
    return pl.pallas_call(
        body, name=name, grid=(T // TOK,), in_specs=[row, one, _seg_spec(6, D, ctx_tiles), row, row, row],
        out_specs=[row, one, _seg_spec(1, D, ctx_tiles), _seg_spec(1, D, ctx_tiles)],
        out_shape=[jax.ShapeDtypeStruct((T, D), F32), jax.ShapeDtypeStruct((1, D), F32),
                   jax.ShapeDtypeStruct((2, 1, D), F32), jax.ShapeDtypeStruct((2, 1, D), F32)],
        compiler_params=_cparams(("arbitrary",)), **_CALL_KW)(x, g, modv, dh, dh_b, dres)


def _resid_gate(name, x, y, modv, r, ctx_tiles):
    T, D = x.shape

    def body(x_ref, y_ref, m_ref, o_ref):
        o_ref[...] = x_ref[...] + m_ref[r:r + 1, :] * y_ref[...]

    row = pl.BlockSpec((TOK, D), lambda i: (i, 0))
    return pl.pallas_call(
        body, name=name, grid=(T // TOK,), in_specs=[row, row, _seg_spec(6, D, ctx_tiles)], out_specs=row,
        out_shape=jax.ShapeDtypeStruct((T, D), F32), compiler_params=_cparams(("parallel",)), **_CALL_KW)(x, y, modv)


def _resid_gate_bwd(name, dx, y, modv, r, ctx_tiles):
    T, D = dx.shape

    def body(dx_ref, y_ref, m_ref, dy_ref, dgt_ref):
        i = pl.program_id(0)

        @pl.when((i == 0) | (i == ctx_tiles))
        def _():
            dgt_ref[...] = jnp.zeros_like(dgt_ref)

        dxv = dx_ref[...]
        dgt_ref[...] += jnp.sum(dxv * y_ref[...], axis=0, keepdims=True)
        dy_ref[...] = (dxv * m_ref[r:r + 1, :]).astype(BF16)

    row = pl.BlockSpec((TOK, D), lambda i: (i, 0))
    return pl.pallas_call(
        body, name=name, grid=(T // TOK,), in_specs=[row, row, _seg_spec(6, D, ctx_tiles)],
        out_specs=[row, _seg_spec(1, D, ctx_tiles)],
        out_shape=[jax.ShapeDtypeStruct((T, D), BF16), jax.ShapeDtypeStruct((2, 1, D), F32)],
        compiler_params=_cparams(("arbitrary",)), **_CALL_KW)(dx, y, modv)


def _sigmoid(x):
    return 1.0 / (1.0 + jnp.exp(-x))


SWI_ROWS = 128


def _swiglu(name, gu):
    T, F2 = gu.shape
    F = F2 // 2

    def body(gu_ref, o_ref):
        g, u = gu_ref[:, :F], gu_ref[:, F:]
        o_ref[...] = ((g * _sigmoid(g)) * u).astype(BF16)

    return pl.pallas_call(
        body, name=name, grid=(T // SWI_ROWS,), in_specs=[pl.BlockSpec((SWI_ROWS, F2), lambda i: (i, 0))],
        out_specs=pl.BlockSpec((SWI_ROWS, F), lambda i: (i, 0)), out_shape=jax.ShapeDtypeStruct((T, F), BF16),
        compiler_params=_cparams(("parallel",)), **_CALL_KW)(gu)


def _swiglu_bwd(name, gu, dact):
    T, F2 = gu.shape
    F = F2 // 2

    def body(gu_ref, d_ref, o_ref):
        g, u, d = gu_ref[:, :F], gu_ref[:, F:], d_ref[...]
        s = _sigmoid(g)
        o_ref[:, :F] = (d * u * (s * (1.0 + g * (1.0 - s)))).astype(BF16)
        o_ref[:, F:] = (d * (g * s)).astype(BF16)

    return pl.pallas_call(
        body, name=name, grid=(T // SWI_ROWS,),
        in_specs=[pl.BlockSpec((SWI_ROWS, F2), lambda i: (i, 0)), pl.BlockSpec((SWI_ROWS, F), lambda i: (i, 0))],
        out_specs=pl.BlockSpec((SWI_ROWS, F2), lambda i: (i, 0)), out_shape=jax.ShapeDtypeStruct((T, F2), BF16),
        compiler_params=_cparams(("parallel",)), **_CALL_KW)(gu, dact)


def _loss_and_grad(name, y, target, ctx_tiles):
    T, D = y.shape

    def body(y_ref, t_ref, l_ref, dy_ref):
        i = pl.program_id(0)

        @pl.when(i == 0)
        def _():
            l_ref[...] = jnp.zeros_like(l_ref)

        lat = i >= ctx_tiles
        e = jnp.where(lat, y_ref[...] - t_ref[...], 0.0)
        dy_ref[...] = e * (1.0 / D)
        l_ref[...] += 0.5 * jnp.sum(jnp.sum(e * e, axis=-1, keepdims=True) * (1.0 / D), axis=0, keepdims=True)

    row = pl.BlockSpec((TOK, D), lambda i: (i, 0))
    return pl.pallas_call(
        body, name=name, grid=(T // TOK,),
        in_specs=[row, pl.BlockSpec((TOK, D), lambda i: (jnp.maximum(i - ctx_tiles, 0), 0))],
        out_specs=[pl.BlockSpec((8, 128), lambda i: (0, 0)), row],
        out_shape=[jax.ShapeDtypeStruct((8, 128), F32), jax.ShapeDtypeStruct((T, D), F32)],
        compiler_params=_cparams(("arbitrary",)), **_CALL_KW)(y, target)


def _rot_half(x):
    lane = lax.broadcasted_iota(jnp.int32, x.shape, 1)
    return jnp.where((lane % 64) < 32, -pltpu.roll(x, 96, 1), pltpu.roll(x, 32, 1))


def _head_prep(name, src, col_blk, n_heads, g, cos, sin, t_pad, norm, rope):
    T = src.shape[0]
    W = n_heads * HEAD
    nt = T // TOK

    def body(s_ref, g_ref, cos_ref, sin_ref, o_ref):
        i = pl.program_id(0)
        outs = []
        for h in range(n_heads):
            xv = s_ref[:, h * HEAD:(h + 1) * HEAD].astype(F32)
            if norm:
                xv = xv * lax.rsqrt(jnp.mean(xv * xv, axis=-1, keepdims=True) + EPS) * g_ref[...]
            if rope:
                xv = xv * cos_ref[...] + _rot_half(xv) * sin_ref[...]
            outs.append(jnp.where(i < nt, xv, 0.0).astype(BF16))
        o_ref[...] = jnp.concatenate(outs, axis=-1) if n_heads > 1 else outs[0]

    tab = pl.BlockSpec((TOK, HEAD), lambda i: (i, 0))
    return pl.pallas_call(
        body, name=name, grid=(t_pad // TOK,),
        in_specs=[pl.BlockSpec((TOK, W), lambda i: (jnp.minimum(i, nt - 1), col_blk)), pl.BlockSpec((1, HEAD), lambda i: (0, 0)), tab, tab],
        out_specs=pl.BlockSpec((TOK, W), lambda i: (i, 0)), out_shape=jax.ShapeDtypeStruct((t_pad, W), BF16),
        compiler_params=_cparams(("parallel",)), **_CALL_KW)(src, g, cos, sin)


def _head_prep_bwd(name, src, col_blk, n_heads, g, cos, sin, dout, norm, rope, dst):
    T = src.shape[0]
    W = n_heads * HEAD

    def body(s_ref, g_ref, cos_ref, sin_ref, d_ref, dst_ref, ds_ref, dg_ref):
        i = pl.program_id(0)

        @pl.when(i == 0)
        def _():
            dg_ref[...] = jnp.zeros_like(dg_ref)

        outs = []
        dg = jnp.zeros((1, HEAD), F32)
        for h in range(n_heads):
            dz = d_ref[:, h * HEAD:(h + 1) * HEAD]
            if rope:
                dz = dz * cos_ref[...] - _rot_half(dz * sin_ref[...])
            if norm:
                xv = s_ref[:, h * HEAD:(h + 1) * HEAD].astype(F32)
                r = lax.rsqrt(jnp.mean(xv * xv, axis=-1, keepdims=True) + EPS)
                xn = xv * r
                dg = dg + jnp.sum(dz * xn, axis=0, keepdims=True)
                u = dz * g_ref[...]
                dz = r * (u - xn * jnp.mean(u * xn, axis=-1, keepdims=True))
            outs.append(dz.astype(BF16))
        dg_ref[...] += dg
        ds_ref[...] = jnp.concatenate(outs, axis=-1) if n_heads > 1 else outs[0]

    tab = pl.BlockSpec((TOK, HEAD), lambda i: (i, 0))
    col = pl.BlockSpec((TOK, W), lambda i: (i, col_blk))
    one = pl.BlockSpec((1, HEAD), lambda i: (0, 0))
    return pl.pallas_call(
        body, name=name, grid=(T // TOK,),
        in_specs=[col, one, tab, tab, pl.BlockSpec((TOK, W), lambda i: (i, 0)), pl.BlockSpec(memory_space=pl.ANY)],
        out_specs=[col, one], out_shape=[jax.ShapeDtypeStruct(dst.shape, dst.dtype), jax.ShapeDtypeStruct((1, HEAD), F32)],
        input_output_aliases={5: 0}, compiler_params=_cparams(("arbitrary",)), **_CALL_KW)(src, g, cos, sin, dout, dst)


NEG = -1e30


def _attn_geometry(kind, blk, ctx, seq):
    if kind == "swa":
        bq, W = 128, 384
        nctx = ctx // bq
        lat = blk >= nctx
        n = blk - nctx
        s0 = jnp.where(lat, ctx + (n - 1) * bq, 0)
        i = lax.broadcasted_iota(jnp.int32, (bq, W), 0)
        j = lax.broadcasted_iota(jnp.int32, (bq, W), 1)
        kpos = (n - 1) * bq + j
        rel = j - bq - i
        valid = lat & (rel <= SWA_WINDOW) & (rel >= -SWA_WINDOW) & (kpos >= 0) & (kpos < seq)
        return s0, valid, 0
    bq, W = GRID_W, NA_KH * GRID_W
    nctx = ctx // bq
    rows = seq // GRID_W
    lat = blk >= nctx
    rr = jnp.clip(blk - nctx, 0, rows - 1)
    rs = jnp.clip(rr - NA_KH // 2, 0, rows - NA_KH)
    s0 = ctx + rs * GRID_W
    i = lax.broadcasted_iota(jnp.int32, (bq, W), 0)
    j = lax.broadcasted_iota(jnp.int32, (bq, W), 1)
    kcol = j % GRID_W
    cs = jnp.clip(i - NA_KW // 2, 0, GRID_W - NA_KW)
    valid = lat & (kcol >= cs) & (kcol < cs + NA_KW)
    return s0, valid, rr - rs


HP = 2


def _attn_probs(q, kl, kc, sk, bias, valid):
    scale = HEAD ** -0.5
    nt_dims = (((1,), (1,)), ((), ()))
    sl = lax.dot_general(q, kl, nt_dims, preferred_element_type=F32) * scale
    if bias is not None:
        sl = sl + bias
    sl = jnp.where(valid, sl, NEG)
    sc = lax.dot_general(q, kc, nt_dims, preferred_element_type=F32) * scale
    m = jnp.maximum(jnp.maximum(jnp.max(sl, axis=-1, keepdims=True), jnp.max(sc, axis=-1, keepdims=True)), sk)
    el, ec, es = jnp.exp(sl - m), jnp.exp(sc - m), jnp.exp(sk - m)
    inv = 1.0 / (jnp.sum(el, axis=-1, keepdims=True) + jnp.sum(ec, axis=-1, keepdims=True) + es)
    return el * inv, ec * inv, es * inv


def _attn_specs(kind, n_q, n_kv, t_pad):
    bq = 128 if kind == "swa" else GRID_W
    rep = n_q // n_kv
    assert n_q % HP == 0 and HP % rep == 0
    kvw = HP // rep
    qspec = pl.BlockSpec((bq, HP * HEAD), lambda g, b: (b, g))
    kvspec = pl.BlockSpec((t_pad, kvw * HEAD), lambda g, b: (0, g))
    specs = [qspec, kvspec, kvspec, pl.BlockSpec(memory_space=pltpu.SMEM)]
    return bq, rep, qspec, kvspec, specs


def _bias_spec(ctx, seq):
    W = NA_KH * GRID_W

    def idx(g, b):
        rows = seq // GRID_W
        rr = jnp.clip(b - ctx // GRID_W, 0, rows - 1)
        return (g, rr - jnp.clip(rr - NA_KH // 2, 0, rows - NA_KH), 0, 0)

    return pl.BlockSpec((HP, None, GRID_W, W), idx)


def _attn_loads(kind, blk, q_ref, k_ref, v_ref, sink_ref, bias_ref, rep, ctx, seq):
    bq, W = (128, 384) if kind == "swa" else (GRID_W, NA_KH * GRID_W)
    g = pl.program_id(0)
    s0, valid, _ = _attn_geometry(kind, blk, ctx, seq)
    s0 = pl.multiple_of(s0, GRID_W)
    heads = []
    for j in range(HP):
        kv = slice((j // rep) * HEAD, (j // rep + 1) * HEAD)
        heads.append((q_ref[:, j * HEAD:(j + 1) * HEAD], k_ref[pl.ds(s0, W), kv], k_ref[0:ctx, kv], v_ref[pl.ds(s0, W), kv],
                      v_ref[0:ctx, kv], sink_ref[g * HP + j], bias_ref[j] if bias_ref is not None else None))
    return s0, W, valid, heads


def _attn_fwd(name, kind, q, k, v, sink, bias, ctx, seq, dst, head0, comm=None):
    t_pad = q.shape[0]
    T = ctx + seq
    n_q, n_kv = q.shape[1] // HEAD, k.shape[1] // HEAD
    bq, rep, qspec, kvspec, specs = _attn_specs(kind, n_q, n_kv, t_pad)
    assert head0 % HP == 0
    ins = [q, k, v, sink] + ([bias] if bias is not None else []) + [dst]
    ng, nb = n_q // HP, T // bq

    def body(*refs):
        mine, (o_ref,), _, cref = _hosted(comm, len(ins), 1, refs)
        q_ref, k_ref, v_ref, sink_ref = mine[:4]
        bias_ref = mine[4] if bias is not None else None
        g, blk = pl.program_id(0), pl.program_id(1)
        _hosted_start(comm, cref, (g == 0) & (blk == 0))
        _, _, valid, heads = _attn_loads(kind, blk, q_ref, k_ref, v_ref, sink_ref, bias_ref, rep, ctx, seq)
        outs = []
        for qv, kl, kc, vl, vc, sk, bv in heads:
            p_l, p_c, _ = _attn_probs(qv, kl, kc, sk, bv, valid)
            o = jnp.dot(p_l.astype(BF16), vl, preferred_element_type=F32) + jnp.dot(p_c.astype(BF16), vc, preferred_element_type=F32)
            outs.append(o.astype(o_ref.dtype))
        o_ref[...] = jnp.concatenate(outs, axis=-1)
        _hosted_wait(comm, cref, (g == ng - 1) & (blk == nb - 1))

    if bias is not None:
        specs = specs + [_bias_spec(ctx, seq)]
    any_spec = pl.BlockSpec(memory_space=pl.ANY)
    c_ins, c_sds, c_sems = (comm.ins, comm.out_sds(), comm.sem_shapes()) if comm is not None else ([], [], [])
    res = pl.pallas_call(
        body, name=name, grid=(ng, nb), in_specs=specs + [any_spec] * (1 + len(c_ins)),
        out_specs=[pl.BlockSpec((bq, HP * HEAD), lambda g, b: (b, head0 // HP + g))] + [any_spec] * len(c_sds),
        out_shape=[jax.ShapeDtypeStruct(dst.shape, dst.dtype)] + c_sds, input_output_aliases={len(ins) - 1: 0},
        scratch_shapes=c_sems,
        compiler_params=_cparams(("arbitrary", "arbitrary"), has_side_effects=comm is not None), **_CALL_KW)(*ins, *c_ins)
    return res[0], list(res[1:])


def _attn_bwd(name, kind, q, k, v, sink, bias, do, do_head0, ctx, seq):
    t_pad = q.shape[0]
    T = ctx + seq
    n_q, n_kv = q.shape[1] // HEAD, k.shape[1] // HEAD
    bq, rep, qspec, kvspec, specs = _attn_specs(kind, n_q, n_kv, t_pad)
    assert do_head0 % HP == 0
    scale = HEAD ** -0.5
    tn_dims = (((0,), (0,)), ((), ()))
    nt_dims = (((1,), (1,)), ((), ()))
    bdot = functools.partial(lax.dot_general, preferred_element_type=F32)

    def body(q_ref, k_ref, v_ref, sink_ref, *rest):
        if bias is not None:
            bias_ref, do_ref, dq_ref, dk_ref, dv_ref, dsk_ref, db_ref = rest
        else:
            bias_ref, db_ref = None, None
            do_ref, dq_ref, dk_ref, dv_ref, dsk_ref = rest
        blk = pl.program_id(1)
        s0, W, valid, heads = _attn_loads(kind, blk, q_ref, k_ref, v_ref, sink_ref, bias_ref, rep, ctx, seq)
        dos = [do_ref[:, j * HEAD:(j + 1) * HEAD] for j in range(HP)]

        @pl.when(blk == 0)
        def _():
            dk_ref[...] = jnp.zeros_like(dk_ref)
            dv_ref[...] = jnp.zeros_like(dv_ref)
            dsk_ref[...] = jnp.zeros_like(dsk_ref)

        if bias is not None:
            _, _, pat = _attn_geometry(kind, blk, ctx, seq)
            _, _, pat_prev = _attn_geometry(kind, jnp.maximum(blk - 1, 0), ctx, seq)

            @pl.when((blk == 0) | (pat != pat_prev))
            def _():
                db_ref[...] = jnp.zeros_like(db_ref)

        res = []
        for (qv, kl, kc, vl, vc, sk, bv), dov in zip(heads, dos):
            p_l, p_c, p_s = _attn_probs(qv, kl, kc, sk, bv, valid)
            dob = dov.astype(BF16)
            pl_b, pc_b = p_l.astype(BF16), p_c.astype(BF16)
            o = jnp.dot(pl_b, vl, preferred_element_type=F32) + jnp.dot(pc_b, vc, preferred_element_type=F32)
            delta = jnp.sum(dov * o, axis=-1, keepdims=True)
            ds_l = p_l * (bdot(dob, vl, nt_dims) - delta)
            ds_c = p_c * (bdot(dob, vc, nt_dims) - delta)
            dsl_b, dsc_b = ds_l.astype(BF16), ds_c.astype(BF16)
            dq = (jnp.dot(dsl_b, kl, preferred_element_type=F32) + jnp.dot(dsc_b, kc, preferred_element_type=F32)) * scale
            res.append((dq, bdot(dsl_b, qv, tn_dims) * scale, bdot(pl_b, dob, tn_dims), bdot(dsc_b, qv, tn_dims) * scale,
                        bdot(pc_b, dob, tn_dims), jnp.sum(-p_s * delta, axis=0, keepdims=True), ds_l))
        dq_ref[...] = jnp.concatenate([r[0] for r in res], axis=-1)
        for j, (_, dkl, dvl, dkc, dvc, dsk, ds_l) in enumerate(res):
            kv = slice((j // rep) * HEAD, (j // rep + 1) * HEAD)
            dk_ref[pl.ds(s0, W), kv] += dkl
            dv_ref[pl.ds(s0, W), kv] += dvl
            dk_ref[0:ctx, kv] += dkc
            dv_ref[0:ctx, kv] += dvc
            dsk_ref[j] += jnp.broadcast_to(dsk, (8, HEAD))
            if bias is not None:
                db_ref[j] += ds_l

    ins = [q, k, v, sink] + ([bias] if bias is not None else []) + [do]
    in_specs = specs + ([_bias_spec(ctx, seq)] if bias is not None else []) + [
        pl.BlockSpec((bq, HP * HEAD), lambda g, b: (b, do_head0 // HP + g))]
    out_specs = [qspec, kvspec, kvspec, pl.BlockSpec((HP, 8, HEAD), lambda g, b: (g, 0, 0))]
    out_shape = [jax.ShapeDtypeStruct((T, n_q * HEAD), F32), jax.ShapeDtypeStruct((t_pad, n_kv * HEAD), F32),
                 jax.ShapeDtypeStruct((t_pad, n_kv * HEAD), F32), jax.ShapeDtypeStruct((n_q, 8, HEAD), F32)]
    if bias is not None:
        out_specs.append(_bias_spec(ctx, seq))
        out_shape.append(jax.ShapeDtypeStruct(bias.shape, F32))
    res = pl.pallas_call(
        body, name=name, grid=(n_q // HP, T // bq), in_specs=in_specs, out_specs=out_specs, out_shape=out_shape,
        compiler_params=_cparams(("arbitrary", "arbitrary")), **_CALL_KW)(*ins)
    return res if bias is not None else list(res) + [None]


HALO = 8


def _halo_specs(width, col0, T, ctx_tiles):
    per = TOK // HALO
    main = pl.BlockSpec((TOK, width), lambda jc, i: (i, col0 + jc))
    prev = pl.BlockSpec((HALO, width), lambda jc, i: (jnp.maximum(i * per - 1, 0), col0 + jc))
    nxt = pl.BlockSpec((HALO, width), lambda jc, i: (jnp.minimum((i + 1) * per, T // HALO - 1), col0 + jc))
    return main, prev, nxt


def _with_halo(i, nt, ctx_tiles, prev, main, nxt):
    has_prev = (i != 0) & (i != ctx_tiles)
    has_next = (i != ctx_tiles - 1) & (i != nt - 1)
    return jnp.concatenate([jnp.where(has_prev, prev, 0.0), main, jnp.where(has_next, nxt, 0.0)], axis=0)


def _shifted(ext, s):
    n = ext.shape[0]
    return pltpu.roll(ext, (-s) % n, 0)[HALO:HALO + TOK]


def _conv_fwd(name, p, col0, conv_w, ctx_tiles):
    T = p.shape[0]
    nt = T // TOK
    ncol = 3
    Wc = conv_w.shape[1] // ncol
    pad = (DN_CONV - 1) // 2

    def body(m_ref, p_ref, n_ref, w_ref, o_ref):
        i = pl.program_id(1)
        ext = _with_halo(i, nt, ctx_tiles, p_ref[...].astype(F32), m_ref[...].astype(F32), n_ref[...].astype(F32))
        acc = jnp.zeros((TOK, Wc), F32)
        for j in range(DN_CONV):
            acc = acc + w_ref[j:j + 1, :] * _shifted(ext, j - pad)
        o_ref[...] = acc

    main, prev, nxt = _halo_specs(Wc, col0, T, ctx_tiles)
    return pl.pallas_call(
        body, name=name, grid=(ncol, nt), in_specs=[main, prev, nxt, pl.BlockSpec((DN_CONV, Wc), lambda jc, i: (0, jc))],
        out_specs=pl.BlockSpec((TOK, Wc), lambda jc, i: (i, jc)), out_shape=jax.ShapeDtypeStruct((T, ncol * Wc), F32),
        compiler_params=_cparams(("parallel", "parallel")), **_CALL_KW)(p, p, p, conv_w)


def _conv_bwd(name, p, col0, conv_w, dpre, ctx_tiles, dst):
    T = p.shape[0]
    nt = T // TOK
    ncol = 3
    Wc = conv_w.shape[1] // ncol
    pad = (DN_CONV - 1) // 2

    def body(m_ref, p_ref, n_ref, dm_ref, dp_ref, dn_ref, w_ref, dst_ref, dx_ref, dw_ref):
        i = pl.program_id(1)
        ext_x = _with_halo(i, nt, ctx_tiles, p_ref[...].astype(F32), m_ref[...].astype(F32), n_ref[...].astype(F32))
        ext_d = _with_halo(i, nt, ctx_tiles, dp_ref[...], dm_ref[...], dn_ref[...])
        dmain = dm_ref[...]

        @pl.when(i == 0)
        def _():
            dw_ref[...] = jnp.zeros_like(dw_ref)

        acc = jnp.zeros((TOK, Wc), F32)
        for j in range(DN_CONV):
            acc = acc + w_ref[j:j + 1, :] * _shifted(ext_d, pad - j)
            dw_ref[j:j + 1, :] += jnp.sum(dmain * _shifted(ext_x, j - pad), axis=0, keepdims=True)
        dx_ref[...] = acc.astype(BF16)

    main, prev, nxt = _halo_specs(Wc, col0, T, ctx_tiles)
    dmain, dprev, dnxt = _halo_specs(Wc, 0, T, ctx_tiles)
    return pl.pallas_call(
        body, name=name, grid=(ncol, nt),
        in_specs=[main, prev, nxt, dmain, dprev, dnxt, pl.BlockSpec((DN_CONV, Wc), lambda jc, i: (0, jc)),
                  pl.BlockSpec(memory_space=pl.ANY)],
        out_specs=[pl.BlockSpec((TOK, Wc), lambda jc, i: (i, col0 + jc)), pl.BlockSpec((8, Wc), lambda jc, i: (0, jc))],
        out_shape=[jax.ShapeDtypeStruct(dst.shape, dst.dtype), jax.ShapeDtypeStruct((8, ncol * Wc), F32)],
        input_output_aliases={7: 0},
        compiler_params=_cparams(("parallel", "arbitrary")), **_CALL_KW)(p, p, p, dpre, dpre, dpre, conv_w, dst)


def _softplus(x):
    return jnp.maximum(x, 0.0) + jnp.log(1.0 + jnp.exp(-jnp.abs(x)))


def _gdn_point(name, pre, dab, a_log, dt_bias, n_heads):
    T = pre.shape[0]
    Wd = n_heads * HEAD
    ng = 2 * n_heads

    def body(pre_ref, ab_ref, al_ref, dt_ref, q_ref, k_ref, v_ref, la_ref, be_ref):
        for h in range(n_heads):
            for part, ref in enumerate((q_ref, k_ref, v_ref)):
                xv = pre_ref[:, part * Wd + h * HEAD:part * Wd + (h + 1) * HEAD]
                s = xv * _sigmoid(xv)
                if part < 2:
                    s = s * lax.rsqrt(jnp.sum(s * s, axis=-1, keepdims=True) + EPS) * (HEAD ** -0.5 if part == 0 else 1.0)
                ref[:, h * HEAD:(h + 1) * HEAD] = s
        ab = ab_ref[...].astype(F32)
        lane = lax.broadcasted_iota(jnp.int32, ab.shape, 1)
        la_ref[...] = jnp.where(lane < ng, -jnp.exp(al_ref[...]) * _softplus(ab + dt_ref[...]), 0.0)
        be_ref[...] = jnp.where(lane < ng, _sigmoid(pltpu.roll(ab, HEAD - ng, 1)), 0.0)

    row = lambda w: pl.BlockSpec((TOK, w), lambda i: (i, 0))
    one = pl.BlockSpec((1, HEAD), lambda i: (0, 0))
    return pl.pallas_call(
        body, name=name, grid=(T // TOK,), in_specs=[row(3 * Wd), row(HEAD), one, one],
        out_specs=[row(Wd), row(Wd), row(Wd), row(HEAD), row(HEAD)],
        out_shape=[jax.ShapeDtypeStruct((T, Wd), F32)] * 3 + [jax.ShapeDtypeStruct((T, HEAD), F32)] * 2,
        compiler_params=_cparams(("parallel",)), **_CALL_KW)(pre, dab, a_log, dt_bias)


def _gdn_point_bwd(name, pre, dab, a_log, dt_bias, n_heads, dq, dk, dv, dla, dbe):
    T = pre.shape[0]
    Wd = n_heads * HEAD
    ng = 2 * n_heads

    def body(pre_ref, ab_ref, al_ref, dt_ref, dq_ref, dk_ref, dv_ref, dla_ref, dbe_ref, dpre_ref, dab_ref, dal_ref, ddt_ref):
        i = pl.program_id(0)

        @pl.when(i == 0)
        def _():
            dal_ref[...] = jnp.zeros_like(dal_ref)
            ddt_ref[...] = jnp.zeros_like(ddt_ref)

        for h in range(n_heads):
            for part, ref in enumerate((dq_ref, dk_ref, dv_ref)):
                cols = slice(part * Wd + h * HEAD, part * Wd + (h + 1) * HEAD)
                xv = pre_ref[:, cols]
                sg = _sigmoid(xv)
                s = xv * sg
                dy = ref[0, :, h * HEAD:(h + 1) * HEAD] + ref[1, :, h * HEAD:(h + 1) * HEAD]
                if part < 2:
                    c0 = HEAD ** -0.5 if part == 0 else 1.0
                    r = lax.rsqrt(jnp.sum(s * s, axis=-1, keepdims=True) + EPS)
                    ds = c0 * (r * dy - s * (r * r * r) * jnp.sum(dy * s, axis=-1, keepdims=True))
                else:
                    ds = dy
                dpre_ref[:, cols] = ds * (sg * (1.0 + xv * (1.0 - sg)))
        ab = ab_ref[...].astype(F32)
        lane = lax.broadcasted_iota(jnp.int32, ab.shape, 1)
        ea = jnp.exp(al_ref[...])
        z = ab + dt_ref[...]
        dlav = jnp.where(lane < ng, dla_ref[0] + dla_ref[1], 0.0)
        da = dlav * (-ea) * _sigmoid(z)
        dal_ref[...] += jnp.sum(dlav * (-ea) * _softplus(z), axis=0, keepdims=True)
        ddt_ref[...] += jnp.sum(da, axis=0, keepdims=True)
        be = _sigmoid(pltpu.roll(ab, HEAD - ng, 1))
        db = jnp.where(lane < ng, (dbe_ref[0] + dbe_ref[1]) * be * (1.0 - be), 0.0)
        dab_ref[...] = (da + pltpu.roll(db, ng, 1)).astype(BF16)

    row = lambda w: pl.BlockSpec((TOK, w), lambda i: (i, 0))
    two = lambda w: pl.BlockSpec((2, TOK, w), lambda i: (0, i, 0))
    one = pl.BlockSpec((1, HEAD), lambda i: (0, 0))
    return pl.pallas_call(
        body, name=name, grid=(T // TOK,),
        in_specs=[row(3 * Wd), row(HEAD), one, one, two(Wd), two(Wd), two(Wd), two(HEAD), two(HEAD)],
        out_specs=[row(3 * Wd), row(HEAD), one, one],
        out_shape=[jax.ShapeDtypeStruct((T, 3 * Wd), F32), jax.ShapeDtypeStruct((T, HEAD), BF16),
                   jax.ShapeDtypeStruct((1, HEAD), F32), jax.ShapeDtypeStruct((1, HEAD), F32)],
        compiler_params=_cparams(("arbitrary",)), **_CALL_KW)(pre, dab, a_log, dt_bias, dq, dk, dv, dla, dbe)


_NN, _NT, _TN = "nn", "nt", "tn"
_DIMS = {"nn": (((1,), (0,)), ((), ())), "nt": (((1,), (1,)), ((), ())), "tn": (((0,), (0,)), ((), ()))}
_BDIMS = {"nn": (((2,), (1,)), ((0,), (0,))), "nt": (((2,), (2,)), ((0,), (0,))), "tn": (((1,), (1,)), ((0,), (0,)))}


def _dims(a, kind):
    return _BDIMS[kind] if a.ndim == 3 else _DIMS[kind]


def _mm3_raw(a, b, kind=_NN):
    ah, bh = a.astype(BF16), b.astype(BF16)
    al, bl = (a - ah.astype(F32)).astype(BF16), (b - bh.astype(F32)).astype(BF16)
    d = functools.partial(lax.dot_general, dimension_numbers=_dims(a, kind), preferred_element_type=F32)
    return d(ah, bh) + (d(ah, bl) + d(al, bh))


@jax.custom_vjp
def _mm3(a, b):
    return _mm3_raw(a, b)


def _mm3_fwd(a, b):
    return _mm3_raw(a, b), (a, b)


def _mm3_bwd(res, g):
    a, b = res
    return _mm3_raw(g, b, _NT), _mm3_raw(a, g, _TN)


_mm3.defvjp(_mm3_fwd, _mm3_bwd)


def _bdot_raw(a, b, kind):
    return lax.dot_general(a.astype(BF16), b.astype(BF16), _dims(a, kind), preferred_element_type=F32)


@functools.partial(jax.custom_vjp, nondiff_argnums=(2,))
def _bdot(a, b, kind=_NN):
    return _bdot_raw(a, b, kind)


def _bdot_fwd(a, b, kind):
    return _bdot_raw(a, b, kind), (a, b)


def _bdot_bwd(kind, res, g):
    a, b = res
    if kind == "nn":
        return _bdot_raw(g, b, "nt"), _bdot_raw(a, g, "tn")
    if kind == "nt":
        return _bdot_raw(g, b, "nn"), _bdot_raw(g, a, "tn")
    return _bdot_raw(b, g, "nt"), _bdot_raw(a, g, "nn")


_bdot.defvjp(_bdot_fwd, _bdot_bwd)


def _chunk_masks(rev):
    C = DN_CHUNK
    ii = lax.broadcasted_iota(jnp.int32, (C, C), 0)
    jj = lax.broadcasted_iota(jnp.int32, (C, C), 1)
    diff = jnp.where(rev, jj - ii, ii - jj)
    incl = diff >= 0
    strict = diff > 0
    rowsel = (lax.broadcasted_iota(jnp.int32, (C, 1), 0) == jnp.where(rev, 0, C - 1)).astype(F32)
    return incl, strict, rowsel, (ii == jj).astype(F32)


def _head_stack(ref, n_heads):
    return jnp.stack([ref[:, h * HEAD:(h + 1) * HEAD] for h in range(n_heads)])


def _gate_views(g, gt, be, d, n_heads):
    lane = lax.broadcasted_iota(jnp.int32, (1, HEAD), 1)
    sub = lax.broadcasted_iota(jnp.int32, (HEAD, 1), 0)
    sels = [(lane == d * n_heads + h).astype(F32) for h in range(n_heads)]
    selts = [(sub == d * n_heads + h).astype(F32) for h in range(n_heads)]
    g_col = jnp.stack([jnp.sum(g * s, axis=1, keepdims=True) for s in sels])
    b_col = jnp.stack([jnp.sum(be * s, axis=1, keepdims=True) for s in sels])
    g_row = jnp.stack([jnp.sum(gt * s, axis=0, keepdims=True) for s in selts])
    return g_col, g_row, b_col, sels, selts


def _chunk_decay(g_col, g_row, incl):
    return jnp.where(incl, jnp.exp(jnp.where(incl, g_col - g_row, 0.0)), 0.0)


def _chunk_lower(k, g_col, g_row, b_col, incl, strict):
    return jnp.where(strict, _bdot(k * b_col, k, _NT) * _chunk_decay(g_col, g_row, incl), 0.0)


def _chunk_inverse(low, eye):
    m = -low
    x = eye + m
    p = m
    for _ in range(int(math.log2(DN_CHUNK)) - 1):
        p = _mm3(p, p)
        x = x + _mm3(x, p)
    return x


def _chunk_step(q, k, v, g_col, g_row, b_col, S, X, incl, rowsel):
    decay = _chunk_decay(g_col, g_row, incl)
    eg = jnp.exp(g_col)
    u = _mm3(X, v * b_col)
    w = _mm3(X, k * (b_col * eg))
    intra = _bdot(q, k, _NT) * decay
    g_last = jnp.sum(g_col * rowsel, axis=1, keepdims=True)
    v_new = u - _bdot(w, S)
    o = _bdot(q * eg, S) + _bdot(intra, v_new)
    S_new = S * jnp.exp(g_last) + _bdot(k * jnp.exp(g_last - g_col), v_new, _TN)
    return o, S_new


def _scan_index(ctx_chunks, n_chunks):
    def idx(d, n):
        return jnp.where(d == 0, n, jnp.where(n < ctx_chunks, ctx_chunks - 1 - n, n_chunks + ctx_chunks - 1 - n))
    return idx


def _cumsum_mats(rev):
    C = DN_CHUNK
    ii = lax.broadcasted_iota(jnp.int32, (C, C), 0)
    jj = lax.broadcasted_iota(jnp.int32, (C, C), 1)
    return jnp.where(jnp.where(rev, jj - ii, ii - jj) >= 0, 1.0, 0.0).astype(F32)


def _hosted(comm, n_in, n_out, refs):
    n_ci, n_co = (len(comm.ins), len(comm.out_shapes)) if comm is not None else (0, 0)
    ins, cin = refs[:n_in], refs[n_in:n_in + n_ci]
    outs, cout = refs[n_in + n_ci:n_in + n_ci + n_out], refs[n_in + n_ci + n_out:n_in + n_ci + n_out + n_co]
    rest = refs[n_in + n_ci + n_out + n_co:]
    n_sem = 3 if comm is not None else 0
    return ins, outs, rest[:len(rest) - n_sem], (cin, cout, rest[len(rest) - n_sem:])


def _hosted_start(comm, cref, first):
    if comm is not None:
        @pl.when(first)
        def _():
            for cp in comm.copies(*cref):
                cp.start()


def _hosted_wait(comm, cref, last):
    if comm is not None:
        @pl.when(last)
        def _():
            for cp in comm.copies(*cref):
                cp.wait()


def _gdn_scan(name, q, k, v, la, be, n_heads, ctx, comm=None):
    T, Wd = q.shape
    C = DN_CHUNK
    nch = T // C
    cidx = _scan_index(ctx // C, nch)

    def body(*refs):
        (q_ref, k_ref, v_ref, la_ref, be_ref), (o_ref, s_ref, x_ref), (state,), cref = _hosted(comm, 5, 3, refs)
        d, n = pl.program_id(0), pl.program_id(1)
        rev = d == 1
        _hosted_start(comm, cref, (d == 0) & (n == 0))

        @pl.when(n == 0)
        def _():
            state[...] = jnp.zeros_like(state)

        incl, strict, rowsel, eye = _chunk_masks(rev)
        g = jnp.dot(_cumsum_mats(rev), la_ref[...], precision=lax.Precision.HIGHEST, preferred_element_type=F32)
        g_col, g_row, b_col, _, _ = _gate_views(g, g.T, be_ref[...], d, n_heads)
        qs, ks, vs = _head_stack(q_ref, n_heads), _head_stack(k_ref, n_heads), _head_stack(v_ref, n_heads)
        S = state[...]
        X = _chunk_inverse(_chunk_lower(ks, g_col, g_row, b_col, incl, strict), eye)
        o, S_new = _chunk_step(qs, ks, vs, g_col, g_row, b_col, S, X, incl, rowsel)
        s_ref[...] = S
        x_ref[...] = X
        state[...] = S_new
        for h in range(n_heads):
            o_ref[:, h * HEAD:(h + 1) * HEAD] = o[h]
        _hosted_wait(comm, cref, (d == 1) & (n == nch - 1))

    tok = lambda w: pl.BlockSpec((C, w), lambda d, n: (cidx(d, n), 0))
    any_spec = pl.BlockSpec(memory_space=pl.ANY)
    c_ins, c_sds, c_sems = (comm.ins, comm.out_sds(), comm.sem_shapes()) if comm is not None else ([], [], [])
    res = pl.pallas_call(
        body, name=name, grid=(2, nch), in_specs=[tok(Wd), tok(Wd), tok(Wd), tok(HEAD), tok(HEAD)] + [any_spec] * len(c_ins),
        out_specs=[pl.BlockSpec((None, C, Wd), lambda d, n: (d, cidx(d, n), 0)),
                   pl.BlockSpec((None, None, n_heads, HEAD, HEAD), lambda d, n: (d, n, 0, 0, 0)),
                   pl.BlockSpec((None, None, n_heads, C, C), lambda d, n: (d, n, 0, 0, 0))] + [any_spec] * len(c_sds),
        out_shape=[jax.ShapeDtypeStruct((2, T, Wd), F32), jax.ShapeDtypeStruct((2, nch, n_heads, HEAD, HEAD), F32),
                   jax.ShapeDtypeStruct((2, nch, n_heads, C, C), F32)] + c_sds,
        scratch_shapes=[pltpu.VMEM((n_heads, HEAD, HEAD), F32)] + c_sems,
        compiler_params=_cparams(("arbitrary", "arbitrary"), has_side_effects=comm is not None), **_CALL_KW)(q, k, v, la, be, *c_ins)
    return res[0], res[1], res[2], list(res[3:])


def _gdn_scan_bwd(name, q, k, v, la, be, states, invs, do, n_heads, ctx, comm=None):
    T, Wd = q.shape
    C = DN_CHUNK
    nch = T // C
    cidx = _scan_index(ctx // C, nch)

    def body(*refs):
        ins, outs, (dstate,), cref = _hosted(comm, 8, 5, refs)
        q_ref, k_ref, v_ref, la_ref, be_ref, s_ref, x_ref, do_ref = ins
        dq_ref, dk_ref, dv_ref, dla_ref, dbe_ref = outs
        d, n = pl.program_id(0), pl.program_id(1)
        rev = d == 1
        _hosted_start(comm, cref, (d == 0) & (n == 0))

        @pl.when(n == 0)
        def _():
            dstate[...] = jnp.zeros_like(dstate)

        incl, strict, rowsel, eye = _chunk_masks(rev)
        tri = _cumsum_mats(rev)
        g = jnp.dot(tri, la_ref[...], precision=lax.Precision.HIGHEST, preferred_element_type=F32)
        g_col, g_row, b_col, sels, selts = _gate_views(g, g.T, be_ref[...], d, n_heads)
        qs, ks, vs = _head_stack(q_ref, n_heads), _head_stack(k_ref, n_heads), _head_stack(v_ref, n_heads)
        dos = _head_stack(do_ref, n_heads)
        S, X = s_ref[...], x_ref[...]
        step = functools.partial(_chunk_step, incl=incl, rowsel=rowsel)
        _, vjp_step = jax.vjp(step, qs, ks, vs, g_col, g_row, b_col, S, X)
        dq, dk1, dv_, dgc1, dgr1, dbc1, dS, dX = vjp_step((dos, dstate[...]))
        dlow = -_mm3_raw(_mm3_raw(X, dX, _TN), X, _NT)
        low_fn = functools.partial(_chunk_lower, incl=incl, strict=strict)
        _, vjp_low = jax.vjp(low_fn, ks, g_col, g_row, b_col)
        dk2, dgc2, dgr2, dbc2 = vjp_low(dlow)
        dk = dk1 + dk2
        dstate[...] = dS
        dgc, dgr, dbc = dgc1 + dgc2, dgr1 + dgr2, dbc1 + dbc2
        dg = jnp.zeros((C, HEAD), F32)
        dgt = jnp.zeros((HEAD, C), F32)
        dbe = jnp.zeros((C, HEAD), F32)
        for h in range(n_heads):
            cols = slice(h * HEAD, (h + 1) * HEAD)
            dq_ref[:, cols], dk_ref[:, cols], dv_ref[:, cols] = dq[h], dk[h], dv_[h]
            dg = dg + dgc[h] * sels[h]
            dgt = dgt + selts[h] * dgr[h]
            dbe = dbe + dbc[h] * sels[h]
        dg = dg + dgt.T
        dla_ref[...] = lax.dot_general(tri, dg, _DIMS["tn"], precision=lax.Precision.HIGHEST, preferred_element_type=F32)
        dbe_ref[...] = dbe
        _hosted_wait(comm, cref, (d == 1) & (n == nch - 1))

    rn = lambda d, n: cidx(d, nch - 1 - n)
    tok = lambda w: pl.BlockSpec((C, w), lambda d, n: (rn(d, n), 0))
    otok = lambda w: pl.BlockSpec((None, C, w), lambda d, n: (d, rn(d, n), 0))
    any_spec = pl.BlockSpec(memory_space=pl.ANY)
    c_ins, c_sds, c_sems = (comm.ins, comm.out_sds(), comm.sem_shapes()) if comm is not None else ([], [], [])
    res = pl.pallas_call(
        body, name=name, grid=(2, nch),
        in_specs=[tok(Wd), tok(Wd), tok(Wd), tok(HEAD), tok(HEAD),
                  pl.BlockSpec((None, None, n_heads, HEAD, HEAD), lambda d, n: (d, nch - 1 - n, 0, 0, 0)),
                  pl.BlockSpec((None, None, n_heads, C, C), lambda d, n: (d, nch - 1 - n, 0, 0, 0)), tok(Wd)] + [any_spec] * len(c_ins),
        out_specs=[otok(Wd), otok(Wd), otok(Wd), otok(HEAD), otok(HEAD)] + [any_spec] * len(c_sds),
        out_shape=[jax.ShapeDtypeStruct((2, T, Wd), F32)] * 3 + [jax.ShapeDtypeStruct((2, T, HEAD), F32)] * 2 + c_sds,
        scratch_shapes=[pltpu.VMEM((n_heads, HEAD, HEAD), F32)] + c_sems,
        compiler_params=_cparams(("arbitrary", "arbitrary"), has_side_effects=comm is not None), **_CALL_KW)(
            q, k, v, la, be, states, invs, do, *c_ins)
    return tuple(res[:5]) + (list(res[5:]),)


def _gated_norm(name, o2, p, zblk, g, n_heads, width):
    _, T, Wd = o2.shape

    def body(o_ref, z_ref, g_ref, y_ref):
        for h in range(n_heads):
            cols = slice(h * HEAD, (h + 1) * HEAD)
            ov = o_ref[0, :, cols] + o_ref[1, :, cols]
            zv = z_ref[:, cols].astype(F32)
            y = ov * lax.rsqrt(jnp.mean(ov * ov, axis=-1, keepdims=True) + EPS) * g_ref[...]
            y_ref[:, cols] = (y * (zv * _sigmoid(zv))).astype(BF16)

    return pl.pallas_call(
        body, name=name, grid=(T // TOK,),
        in_specs=[pl.BlockSpec((2, TOK, Wd), lambda i: (0, i, 0)), pl.BlockSpec((TOK, Wd), lambda i: (i, zblk)),
                  pl.BlockSpec((1, HEAD), lambda i: (0, 0))],
        out_specs=pl.BlockSpec((TOK, Wd), lambda i: (i, 0)), out_shape=jax.ShapeDtypeStruct((T, width), BF16),
        compiler_params=_cparams(("parallel",)), **_CALL_KW)(o2, p, g)


def _gated_norm_bwd(name, o2, p, zblk, g, n_heads, dmix, dblk):
    _, T, Wd = o2.shape

    def body(o_ref, z_ref, g_ref, dy_ref, do_ref, dz_ref, dg_ref):
        i = pl.program_id(0)

        @pl.when(i == 0)
        def _():
            dg_ref[...] = jnp.zeros_like(dg_ref)

        dg = jnp.zeros((1, HEAD), F32)
        for h in range(n_heads):
            cols = slice(h * HEAD, (h + 1) * HEAD)
            ov = o_ref[0, :, cols] + o_ref[1, :, cols]
            zv = z_ref[:, cols].astype(F32)
            dy = dy_ref[:, cols].astype(F32)
            r = lax.rsqrt(jnp.mean(ov * ov, axis=-1, keepdims=True) + EPS)
            on = ov * r
            sg = _sigmoid(zv)
            sz = zv * sg
            dz_ref[:, cols] = (dy * (on * g_ref[...]) * (sg * (1.0 + zv * (1.0 - sg)))).astype(BF16)
            dyn = dy * sz
            dg = dg + jnp.sum(dyn * on, axis=0, keepdims=True)
            u = dyn * g_ref[...]
            do_ref[:, cols] = r * (u - on * jnp.mean(u * on, axis=-1, keepdims=True))
        dg_ref[...] += dg

    row = pl.BlockSpec((TOK, Wd), lambda i: (i, 0))
    one = pl.BlockSpec((1, HEAD), lambda i: (0, 0))
    return pl.pallas_call(
        body, name=name, grid=(T // TOK,),
        in_specs=[pl.BlockSpec((2, TOK, Wd), lambda i: (0, i, 0)), pl.BlockSpec((TOK, Wd), lambda i: (i, zblk)), one,
                  pl.BlockSpec((TOK, Wd), lambda i: (i, dblk))],
        out_specs=[row, pl.BlockSpec((TOK, Wd), lambda i: (i, zblk)), one],
        out_shape=[jax.ShapeDtypeStruct((T, Wd), F32), jax.ShapeDtypeStruct(p.shape, BF16), jax.ShapeDtypeStruct((1, HEAD), F32)],
        compiler_params=_cparams(("arbitrary",)), **_CALL_KW)(o2, p, g, dmix)


class _Dims:
    def __init__(self, D, seq, ctx, ffn):
        self.D, self.seq, self.ctx, self.ffn = D, seq, ctx, ffn
        self.T = seq + ctx
        self.t_pad = -(-(self.T + 128) // TOK) * TOK
        self.ctx_tiles = ctx // TOK
        nh = D // HEAD
        self.swa_h, self.kv_h, self.dn_h = nh // 4, nh // 8, nh // 2
        self.na_h = nh - self.swa_h - self.dn_h
        self.swa_q, self.swa_kv, self.Wd, self.na = self.swa_h * HEAD, self.kv_h * HEAD, self.dn_h * HEAD, self.na_h * HEAD
        self.n_ab = 4 * self.dn_h
        self.o_ab = self.swa_q + 2 * self.swa_kv + 4 * self.Wd
        self.n_in = self.o_ab + self.n_ab + 3 * self.na
        self.n_main = self.n_in - self.n_ab
        assert ctx % TOK == 0 and seq % TOK == 0 and self.swa_q == 2 * self.swa_kv == self.na and 2 * self.na == self.Wd


def _rope_tables(dm):
    t = jnp.arange(dm.t_pad, dtype=jnp.int32) - dm.ctx
    lat = (t >= 0) & (t < dm.seq)
    row = (t // GRID_W).astype(F32)
    col = (t % GRID_W).astype(F32)
    n_freq = HEAD // 4
    inv = ROPE_THETA ** (-jnp.arange(n_freq, dtype=F32) / n_freq)
    ang = jnp.concatenate([row[:, None] * inv, row[:, None] * inv, col[:, None] * inv, col[:, None] * inv], axis=-1)
    ang = jnp.where(lat[:, None], ang, 0.0)
    return jnp.cos(ang), jnp.sin(ang)


def _bias_indices():
    o = np.arange(NA_KH)[:, None]
    jr = np.arange(NA_KH)[None, :]
    idx_r = jr - o + (NA_KH - 1)
    cols = np.arange(GRID_W)
    idx_c = np.clip(cols[None, :] - cols[:, None], -(NA_KW - 1), NA_KW - 1) + (NA_KW - 1)
    return idx_r, idx_c


def _bias_onehots():
    idx_r, idx_c = _bias_indices()
    sel_r = (idx_r.reshape(-1)[:, None] == np.arange(2 * NA_KH)[None, :]).astype(np.float32)
    sel_c = (np.arange(HEAD)[:, None] == idx_c.reshape(-1)[None, :]).astype(np.float32)
    return jnp.asarray(sel_r), jnp.asarray(sel_c)


def _rpb_pad(rpb):
    return jnp.pad(rpb, ((0, 0), (0, 2 * NA_KH - rpb.shape[1]), (0, HEAD - rpb.shape[2])))


def _bias_table(name, rpb):
    H = rpb.shape[0]
    sel_r, sel_c = _bias_onehots()
    hi = lax.Precision.HIGHEST

    def body(r_ref, sr_ref, sc_ref, o_ref):
        t = jnp.dot(sr_ref[...], r_ref[...], precision=hi, preferred_element_type=F32)
        o_ref[...] = jnp.dot(t, sc_ref[...], precision=hi, preferred_element_type=F32)

    n_r, n_c = sel_r.shape[0], sel_c.shape[1]
    tab = pl.pallas_call(
        body, name=name, grid=(H,),
        in_specs=[pl.BlockSpec((None, 2 * NA_KH, HEAD), lambda h: (h, 0, 0)), pl.BlockSpec(sel_r.shape, lambda h: (0, 0)),
                  pl.BlockSpec(sel_c.shape, lambda h: (0, 0))],
        out_specs=pl.BlockSpec((None, n_r, n_c), lambda h: (h, 0, 0)), out_shape=jax.ShapeDtypeStruct((H, n_r, n_c), F32),
        compiler_params=_cparams(("parallel",)), **_CALL_KW)(_rpb_pad(rpb), sel_r, sel_c)
    tab = tab.reshape(H, NA_KH, NA_KH, GRID_W, GRID_W).transpose(0, 1, 3, 2, 4)
    return tab.reshape(H, NA_KH, GRID_W, NA_KH * GRID_W)


def _bias_table_bwd(name, dbias, rpb_shape):
    H = dbias.shape[0]
    sel_r, sel_c = _bias_onehots()
    hi = lax.Precision.HIGHEST
    d = dbias.reshape(H, NA_KH, GRID_W, NA_KH, GRID_W).transpose(0, 1, 3, 2, 4).reshape(H, NA_KH * NA_KH, GRID_W * GRID_W)

    def body(d_ref, sr_ref, sc_ref, o_ref):
        dt = lax.dot_general(d_ref[...], sc_ref[...], _DIMS["nt"], precision=hi, preferred_element_type=F32)
        o_ref[...] = lax.dot_general(sr_ref[...], dt, _DIMS["tn"], precision=hi, preferred_element_type=F32)

    out = pl.pallas_call(
        body, name=name, grid=(H,),
        in_specs=[pl.BlockSpec((None,) + d.shape[1:], lambda h: (h, 0, 0)), pl.BlockSpec(sel_r.shape, lambda h: (0, 0)),
                  pl.BlockSpec(sel_c.shape, lambda h: (0, 0))],
        out_specs=pl.BlockSpec((None, 2 * NA_KH, HEAD), lambda h: (h, 0, 0)),
        out_shape=jax.ShapeDtypeStruct((H, 2 * NA_KH, HEAD), F32),
        compiler_params=_cparams(("parallel",)), **_CALL_KW)(d, sel_r, sel_c)
    return out[:, :rpb_shape[1], :rpb_shape[2]]


def _lane_row(v):
    v = v.reshape(-1)
    return jnp.pad(v, (0, HEAD - v.shape[0])).reshape(1, HEAD)


def _chunks_of(l, chip, gathered, own):
    return [jnp.where(chip == k, own[l], gathered[k]) for k in range(N_CHIPS)]


def _weights_in(dm, l, chip, g_in, own_in):
    w_in = jnp.concatenate(_chunks_of(l, chip, g_in, own_in), axis=1)
    w_main = jnp.concatenate([w_in[:, :dm.o_ab], w_in[:, dm.o_ab + dm.n_ab:]], axis=1)
    w_ab = jnp.pad(w_in[:, dm.o_ab:dm.o_ab + dm.n_ab], ((0, 0), (0, HEAD - dm.n_ab)))
    return dict(main=w_main, ab=w_ab)


def _weights_rest(dm, l, chip, gathered, own):
    g_out, g_gate, g_up, g_down = [_chunks_of(l, chip, g, o) for g, o in zip(gathered, own)]
    w_out = jnp.stack(g_out).reshape(dm.D, dm.D)
    w_out = jnp.concatenate([w_out[dm.swa_q:dm.swa_q + dm.Wd], w_out[:dm.swa_q], w_out[dm.swa_q + dm.Wd:]], axis=0)
    return dict(out=w_out, gu=jnp.concatenate(g_gate + g_up, axis=1), down=jnp.stack(g_down).reshape(dm.ffn, dm.D))


def _layer_fwd(dm, x, W, sp, modv, cos, sin, hosts=None, late_weights=None):
    ct = dm.ctx_tiles
    hosts = hosts or {}
    got = {}
    h = _norm_mod("norm1", x, sp["norm1_g"], modv, 0, ct)
    P = _matmul("in_proj", h, W["main"], "nn")
    Pab = _matmul("in_proj_ab", h, W["ab"], "nn", tn=HEAD)
    one = jnp.ones((1, HEAD), F32)
    qa = _head_prep("swa_q_prep", P, 0, dm.swa_h, sp["swa_q_g"], cos, sin, dm.t_pad, True, True)
    ka = _head_prep("swa_k_prep", P, 2, dm.kv_h, sp["swa_k_g"], cos, sin, dm.t_pad, True, True)
    va = _head_prep("swa_v_prep", P, 3, dm.kv_h, one, cos, sin, dm.t_pad, False, False)
    qn = _head_prep("na_q_prep", P, 10, dm.na_h, sp["na_q_g"], cos, sin, dm.t_pad, True, False)
    kn = _head_prep("na_k_prep", P, 11, dm.na_h, sp["na_k_g"], cos, sin, dm.t_pad, True, False)
    vn = _head_prep("na_v_prep", P, 12, dm.na_h, one, cos, sin, dm.t_pad, False, False)
    no_sink = jnp.full((dm.na_h,), NEG, F32)
    bias = _bias_table("na_bias", sp["na_rpb"])
    pre = _conv_fwd("dn_conv", P, 1, sp["dn_conv_w"], ct)
    a_row, dt_row = _lane_row(sp["dn_A_log"]), _lane_row(sp["dn_dt_bias"])
    qh, kh, vh, la, be = _gdn_point("dn_point", pre, Pab, a_row, dt_row, dm.dn_h)
    o2, states, invs, got["scan"] = _gdn_scan("dn_scan", qh, kh, vh, la, be, dm.dn_h, dm.ctx, hosts.get("scan"))
    if late_weights is not None:
        W = dict(W, **late_weights(got["scan"]))
    mix = _gated_norm("dn_out_norm", o2, P, 4, sp["dn_out_g"], dm.dn_h, dm.D)
    mix, _ = _attn_fwd("swa_fwd", "swa", qa, ka, va, sp["swa_sink"], None, dm.ctx, dm.seq, mix, dm.Wd // HEAD)
    mix, got["na"] = _attn_fwd("na_fwd", "na", qn, kn, vn, no_sink, bias, dm.ctx, dm.seq, mix, (dm.Wd + dm.swa_q) // HEAD,
                               hosts.get("na"))
    ao = _matmul("out_proj", mix, W["out"], "nn")
    x1 = _resid_gate("resid1", x, ao, modv, 2, ct)
    h2 = _norm_mod("norm2", x1, sp["norm2_g"], modv, 3, ct)
    gu = _matmul("ffn_gate_up", h2, W["gu"], "nn", comm=hosts.get("gu"))
    if hosts.get("gu") is not None:
        gu, got["gu"] = gu
    act = _swiglu("ffn_act", gu)
    fo = _matmul("ffn_down", act, W["down"], "nn", comm=hosts.get("down"))
    if hosts.get("down") is not None:
        fo, got["down"] = fo
    x2 = _resid_gate("resid2", x1, fo, modv, 5, ct)
    res = dict(x=x, h=h, P=P, Pab=Pab, qa=qa, ka=ka, va=va, qn=qn, kn=kn, vn=vn, bias=bias, no_sink=no_sink, pre=pre,
               a_row=a_row, dt_row=dt_row, qh=qh, kh=kh, vh=vh, la=la, be=be, o2=o2, states=states, invs=invs, mix=mix,
               ao=ao, x1=x1, h2=h2, gu=gu, act=act, fo=fo)
    return x2, res, got, W


def _layer_bwd(dm, dx2, W, sp, modv, cos, sin, r, host=None):
    ct = dm.ctx_tiles
    T, D = dm.T, dm.D
    one = jnp.ones((1, HEAD), F32)
    dfo, dgate2 = _resid_gate_bwd("resid2_bwd", dx2, r["fo"], modv, 5, ct)
    dact = _matmul("ffn_down_dx", dfo, W["down"], "nt")
    dw_down = _matmul("ffn_down_dw", r["act"], dfo, "tn")
    dgu = _swiglu_bwd("ffn_act_bwd", r["gu"], dact)
    dh2 = _matmul("ffn_gate_up_dx", dgu, W["gu"], "nt")
    dw_gu = _matmul("ffn_gate_up_dw", r["h2"], dgu, "tn")
    zero = jnp.zeros((T, D), F32)
    dx1, dn2g, dsh2, dsc2 = _norm_mod_bwd("norm2_bwd", r["x1"], sp["norm2_g"], modv, 3, dh2, zero, dx2, ct)
    dao, dgate1 = _resid_gate_bwd("resid1_bwd", dx1, r["ao"], modv, 2, ct)
    dmix = _matmul("out_proj_dx", dao, W["out"], "nt")
    dw_out = _matmul("out_proj_dw", r["mix"], dao, "tn")
    do_, dP, d_out_g = _gated_norm_bwd("dn_out_norm_bwd", r["o2"], r["P"], 4, sp["dn_out_g"], dm.dn_h, dmix, 0)
    comm = host(dict(out=dw_out, gu=dw_gu, down=dw_down)) if host is not None else None
    dq2, dk2, dv2, dla2, dbe2, hosted = _gdn_scan_bwd("dn_scan_bwd", r["qh"], r["kh"], r["vh"], r["la"], r["be"], r["states"],
                                                      r["invs"], do_, dm.dn_h, dm.ctx, comm)
    dpre, dPab, d_alog, d_dtb = _gdn_point_bwd("dn_point_bwd", r["pre"], r["Pab"], r["a_row"], r["dt_row"], dm.dn_h,
                                               dq2, dk2, dv2, dla2, dbe2)
    dP, d_conv = _conv_bwd("dn_conv_bwd", r["P"], 1, sp["dn_conv_w"], dpre, ct, dP)
    dqa, dka, dva, dsink, _ = _attn_bwd("swa_bwd", "swa", r["qa"], r["ka"], r["va"], sp["swa_sink"], None, dmix,
                                        dm.Wd // HEAD, dm.ctx, dm.seq)
    dP, d_swa_q_g = _head_prep_bwd("swa_q_prep_bwd", r["P"], 0, dm.swa_h, sp["swa_q_g"], cos, sin, dqa, True, True, dP)
    dP, d_swa_k_g = _head_prep_bwd("swa_k_prep_bwd", r["P"], 2, dm.kv_h, sp["swa_k_g"], cos, sin, dka, True, True, dP)
    dP, _ = _head_prep_bwd("swa_v_prep_bwd", r["P"], 3, dm.kv_h, one, cos, sin, dva, False, False, dP)
    dqn, dkn, dvn, _, dbias = _attn_bwd("na_bwd", "na", r["qn"], r["kn"], r["vn"], r["no_sink"], r["bias"], dmix,
                                        (dm.Wd + dm.swa_q) // HEAD, dm.ctx, dm.seq)
    dP, d_na_q_g = _head_prep_bwd("na_q_prep_bwd", r["P"], 10, dm.na_h, sp["na_q_g"], cos, sin, dqn, True, False, dP)
    dP, d_na_k_g = _head_prep_bwd("na_k_prep_bwd", r["P"], 11, dm.na_h, sp["na_k_g"], cos, sin, dkn, True, False, dP)
    dP, _ = _head_prep_bwd("na_v_prep_bwd", r["P"], 12, dm.na_h, one, cos, sin, dvn, False, False, dP)
    d_rpb = _bias_table_bwd("na_bias_bwd", dbias, sp["na_rpb"].shape)
    dw_main = _matmul("in_proj_dw", r["h"], dP, "tn")
    dw_ab = _matmul("in_proj_ab_dw", r["h"], dPab, "tn", tn=HEAD)
    dh = _matmul("in_proj_dx", dP, W["main"], "nt")
    dh_b = _matmul("in_proj_ab_dx", dPab, W["ab"], "nt")
    dx, dn1g, dsh1, dsc1 = _norm_mod_bwd("norm1_bwd", r["x"], sp["norm1_g"], modv, 0, dh, dh_b, dx1, ct)
    dmodv = jnp.concatenate([dsh1, dsc1, dgate1, dsh2, dsc2, dgate2], axis=1)
    big = dict(main=dw_main, ab=dw_ab, out=dw_out, gu=dw_gu, down=dw_down)
    small = dict(norm1_g=dn1g[0], norm2_g=dn2g[0], swa_q_g=d_swa_q_g[0], swa_k_g=d_swa_k_g[0], swa_sink=dsink[:, 0, 0],
                 dn_conv_w=d_conv[:DN_CONV], dn_A_log=d_alog[0, :2 * dm.dn_h].reshape(2, dm.dn_h),
                 dn_dt_bias=d_dtb[0, :2 * dm.dn_h].reshape(2, dm.dn_h), dn_out_g=d_out_g[0], na_q_g=d_na_q_g[0],
                 na_k_g=d_na_k_g[0], na_rpb=d_rpb)
    return dx, big, small, dmodv, hosted


def _cols(w):
    return w.reshape(w.shape[0], N_CHIPS, -1).transpose(1, 0, 2)


def _rows(w):
    return w.reshape(N_CHIPS, -1, w.shape[1])


def _chunks_in(dm, bigs):
    g = [_cols(jnp.concatenate([b["main"][:, :dm.o_ab], b["ab"][:, :dm.n_ab], b["main"][:, dm.o_ab:]], axis=1)) for b in bigs]
    return jnp.stack(g, axis=1).astype(BF16)


def _chunks_rest(dm, bigs):
    g_out = [_rows(jnp.concatenate([b["out"][dm.Wd:dm.Wd + dm.swa_q], b["out"][:dm.Wd], b["out"][dm.Wd + dm.swa_q:]], axis=0))
             for b in bigs]
    g_gate = [_cols(b["gu"][:, :dm.ffn]) for b in bigs]
    g_up = [_cols(b["gu"][:, dm.ffn:]) for b in bigs]
    g_down = [_rows(b["down"]) for b in bigs]
    return [jnp.stack(g, axis=1).astype(BF16) for g in (g_out, g_gate, g_up, g_down)]


SMALL = ("norm1_g", "norm2_g", "swa_q_g", "swa_k_g", "swa_sink", "dn_conv_w", "dn_A_log", "dn_dt_bias", "dn_out_g",
         "na_q_g", "na_k_g", "na_rpb")


def _pack(arrs):
    flat = jnp.concatenate([a.reshape(-1).astype(F32) for a in arrs])
    n = flat.shape[0]
    rows = -(-n // (8 * HEAD)) * 8
    return jnp.pad(flat, (0, rows * HEAD - n)).reshape(rows, HEAD)


def _unpack(packed, like):
    flat = packed.reshape(-1)
    out, o = [], 0
    for a in like:
        out.append(flat[o:o + a.size].reshape(a.shape))
        o += a.size
    return out


def _sum_devices(name, g, which):
    _, R, _ = g.shape

    def body(g_ref, o_ref):
        acc = g_ref[which[0]]
        for b in which[1:]:
            acc = acc + g_ref[b]
        o_ref[...] = acc

    return pl.pallas_call(
        body, name=name, grid=(R // 8,), in_specs=[pl.BlockSpec((N_DEV, 8, HEAD), lambda i: (0, i, 0))],
        out_specs=pl.BlockSpec((8, HEAD), lambda i: (i, 0)), out_shape=jax.ShapeDtypeStruct((R, HEAD), F32),
        compiler_params=_cparams(("parallel",)), **_CALL_KW)(g)


def _silu_rows(name, c_rows):
    return _elementwise(name, lambda c: (c * _sigmoid(c),), [c_rows], [BF16])[0]


def _ada_cotangent(name, dm_all, b_ada_shape):
    _, L, _, N6 = dm_all.shape
    tn = _pick(N6, (1024, 512, 256, 128))

    def body(d_ref, o_ref, b_ref):
        csum = d_ref[0, 0:1, :]
        for b in range(1, N_DEV):
            csum = csum + d_ref[b, 0:1, :]
        tot = csum
        for b in range(N_DEV):
            o_ref[b:b + 1, :] = d_ref[b, 1:2, :]
            tot = tot + d_ref[b, 1:2, :]
        first = lax.broadcasted_iota(jnp.int32, (8, tn), 0) == 0
        o_ref[N_DEV:, :] = jnp.where(first, jnp.broadcast_to(csum, (8, tn)), 0.0)
        b_ref[...] = jnp.broadcast_to(tot, (8, tn))

    return pl.pallas_call(
        body, name=name, grid=(L, N6 // tn), in_specs=[pl.BlockSpec((N_DEV, None, 2, tn), lambda l, j: (0, l, 0, j))],
        out_specs=[pl.BlockSpec((None, 16, tn), lambda l, j: (l, 0, j)), pl.BlockSpec((None, 8, tn), lambda l, j: (l, 0, j))],
        out_shape=[jax.ShapeDtypeStruct((L, 16, N6), F32), jax.ShapeDtypeStruct((L, 8, N6), F32)],
        compiler_params=_cparams(("parallel", "parallel")), **_CALL_KW)(dm_all)


def kernel(x, c, ctx, c_ctx, w_ada, b_ada, norm1_g, norm2_g, w_in, swa_q_g, swa_k_g, swa_sink, dn_conv_w, dn_A_log, dn_dt_bias, dn_out_g, na_q_g, na_k_g, na_rpb, w_out, w_gate, w_up, w_down, loss_target, m_c_ctx, m_w_ada, m_b_ada, m_norm1_g, m_norm2_g, m_w_in, m_swa_q_g, m_swa_k_g, m_swa_sink, m_dn_conv_w, m_dn_A_log, m_dn_dt_bias, m_dn_out_g, m_na_q_g, m_na_k_g, m_na_rpb, m_w_out, m_w_gate, m_w_up, m_w_down, v_c_ctx, v_w_ada, v_b_ada, v_norm1_g, v_norm2_g, v_w_in, v_swa_q_g, v_swa_k_g, v_swa_sink, v_dn_conv_w, v_dn_A_log, v_dn_dt_bias, v_dn_out_g, v_na_q_g, v_na_k_g, v_na_rpb, v_w_out, v_w_gate, v_w_up, v_w_down):
    L = w_in.shape[0]
    D, seq, n_ctx = x.shape[-1], x.shape[1], ctx.shape[1]
    dm = _Dims(D, seq, n_ctx, w_gate.shape[-1] * N_CHIPS)
    xi, yi, ci = _axes()
    chip = 2 * xi + yi
    dev = 4 * xi + 2 * yi + ci
    n6 = 6 * D
    n6s = n6 // N_CHIPS

    shards = [_elementwise(f"cast_{n}", lambda w: (w,), [w], [BF16])[0]
              for n, w in (("w_in", w_in), ("w_out", w_out), ("w_gate", w_gate), ("w_up", w_up), ("w_down", w_down))]
    assert L == 2
    g_in0 = _gather_d2d("gather_w0a_d2d", _exchange("gather_w0a_ici", _gather_ici_comm(shards[:1], 0)))
    Ws = [_weights_in(dm, 0, chip, g_in0[0], shards[0]), None]
    conv_all = _allgather_small("gather_conv_w", _pack([dn_conv_w]))
    conv_full = jnp.concatenate([_unpack(conv_all[2 * k], [dn_conv_w])[0] for k in range(N_CHIPS)], axis=-1)

    c_all = _allgather_small("gather_c", _pack([c]))
    c_rows = jnp.concatenate([c_all[:, :D // HEAD].reshape(N_DEV, D), c_ctx[None], jnp.zeros((16 - N_DEV - 1, D), F32)], axis=0)
    a_rows = _silu_rows("ada_silu", c_rows)
    b_sh = lax.dynamic_slice_in_dim(b_ada, chip * n6s, n6s, axis=1)
    mod_sh = [_matmul(f"ada_mod{l}", a_rows, w_ada[l], "nn", tm=16) for l in range(L)]
    mod_all = _allgather_small("gather_mod", _pack(mod_sh))
    mods = []
    for l in range(L):
        per_chip = [_unpack(mod_all[2 * k], mod_sh)[l] for k in range(N_CHIPS)]
        mods.append(jnp.concatenate(per_chip, axis=1))
    modvs = []
    for l in range(L):
        rows = jnp.stack([mods[l][N_DEV], lax.dynamic_index_in_dim(mods[l], dev, 0, keepdims=False)])
        modvs.append(_elementwise(f"ada_bias{l}", lambda m, b: (m + b,), [rows, jnp.broadcast_to(b_ada[l][None], (2, n6))], [F32])[0]
                     .reshape(2, 6, D))

    cos, sin = _rope_tables(dm)
    sps = [dict(norm1_g=norm1_g[l][None], norm2_g=norm2_g[l][None], swa_q_g=swa_q_g[l][None], swa_k_g=swa_k_g[l][None],
                swa_sink=swa_sink[l], dn_conv_w=conv_full[l], dn_A_log=dn_A_log[l], dn_dt_bias=dn_dt_bias[l],
                dn_out_g=dn_out_g[l][None], na_q_g=na_q_g[l][None], na_k_g=na_k_g[l][None], na_rpb=na_rpb[l]) for l in range(L)]
    xs = jnp.concatenate([ctx[0], x[0]], axis=0)
    ress = [None] * L
    w_in_s, w_out_s, w_gate_s, w_up_s, w_down_s = shards
    hosts0 = dict(scan=_gather_ici_comm(shards[1:], 0),
                  na=_gather_ici_comm([w_in_s], 1), gu=_gather_ici_comm([w_gate_s, w_up_s], 1),
                  down=_gather_ici_comm([w_out_s, w_down_s], 1))

    def late0(arrived):
        return _weights_rest(dm, 0, chip, _gather_d2d("gather_w0b_d2d", arrived), shards[1:])

    xs, ress[0], got0, Ws[0] = _layer_fwd(dm, xs, Ws[0], sps[0], modvs[0], cos, sin, hosts0, late0)
    g1 = _gather_d2d("gather_w1_d2d", got0["na"] + [got0["down"][0]] + got0["gu"] + [got0["down"][1]])
    Ws[1] = dict(_weights_in(dm, 1, chip, g1[0], w_in_s), **_weights_rest(dm, 1, chip, g1[1:], shards[1:]))
    xs, ress[1], _, _ = _layer_fwd(dm, xs, Ws[1], sps[1], modvs[1], cos, sin)
    loss_blk, dxs = _loss_and_grad("loss", xs, loss_target[0], dm.ctx_tiles)
    loss = lax.psum(loss_blk[0, 0], ("x", "y", "c"))

    bigs, smalls, dmodvs = [None] * L, [None] * L, [None] * L
    pairs = {}

    def host1(big1):
        pairs["a"] = _reduce_pre("a", _chunks_rest(dm, [big1]))
        return _reduce_ici_comm(pairs["a"])

    def host0(big0):
        pairs["b"] = _reduce_pre("b", [_chunks_in(dm, [bigs[1]])] + _chunks_rest(dm, [big0]))
        return _reduce_ici_comm(pairs["b"])

    dxs, bigs[1], smalls[1], dmodvs[1], got_a = _layer_bwd(dm, dxs, Ws[1], sps[1], modvs[1], cos, sin, ress[1], host1)
    dxs, bigs[0], smalls[0], dmodvs[0], got_b = _layer_bwd(dm, dxs, Ws[0], sps[0], modvs[0], cos, sin, ress[0], host0)
    rest1 = _reduce_post("a", pairs["a"], got_a)
    g_in1, *rest0 = _reduce_post("b", pairs["b"], got_b)
    pair_c = _reduce_pre("c", [_chunks_in(dm, [bigs[0]])])
    (g_in0,) = _reduce_post("c", pair_c, _exchange("reduce_c_ici", _reduce_ici_comm(pair_c)))
    g_in = jnp.concatenate([g_in0, g_in1], axis=0)
    g_out, g_gate, g_up, g_down = [jnp.concatenate([a0, a1], axis=0) for a0, a1 in zip(rest0, rest1)]
    grad_x = dxs[n_ctx:][None]

    dm_mine = jnp.stack([d.reshape(2, n6) for d in dmodvs])
    dm_all = _allgather_small("gather_dmod", _pack([dm_mine]))
    dm_all = jnp.stack([_unpack(dm_all[b], [dm_mine])[0] for b in range(N_DEV)])
    dm_rows, d_b_ada = _ada_cotangent("ada_cot", dm_all, b_ada.shape)
    dm_sh = lax.dynamic_slice_in_dim(dm_rows, chip * n6s, n6s, axis=2).astype(BF16)
    g_w_ada = jnp.stack([_matmul(f"ada_dw{l}", a_rows, dm_sh[l], "tn") for l in range(L)])
    dc_part = [_matmul(f"ada_dc{l}", dm_sh[l], w_ada[l], "nt", tm=16) for l in range(L)]
    dc_mine = dc_part[0][N_DEV]
    for l in range(1, L):
        dc_mine = dc_mine + dc_part[l][N_DEV]

    small_list = [jnp.stack([smalls[l][n] for l in range(L)]) for n in SMALL]
    sm_all = _allgather_small("gather_small", _pack(small_list + [dc_mine]))
    sm_sum = _sum_devices("sum_small", sm_all, tuple(range(N_DEV)))
    dc_sum = _sum_devices("sum_dc", sm_all, tuple(range(0, N_DEV, 2)))
    g_small = dict(zip(SMALL, _unpack(sm_sum, small_list)))
    dcs = _unpack(dc_sum, small_list + [dc_mine])[-1]
    def silu_bwd(d, cc):
        s = _sigmoid(cc)
        return (d * (s * (1.0 + cc * (1.0 - s))),)

    g_c_ctx = _elementwise("c_ctx_silu_bwd", silu_bwd, [dcs.reshape(-1, HEAD), c_ctx.reshape(-1, HEAD)], [F32])[0]
    wd3 = dn_conv_w.shape[-1]
    g_small["dn_conv_w"] = lax.dynamic_slice_in_dim(g_small["dn_conv_w"], chip * wd3, wd3, axis=2)

    grads = dict(g_small, c_ctx=g_c_ctx, w_ada=g_w_ada, b_ada=d_b_ada[:, 0], w_in=g_in, w_out=g_out, w_gate=g_gate, w_up=g_up,
                 w_down=g_down)
    weights = dict(c_ctx=c_ctx, w_ada=w_ada, b_ada=b_ada, norm1_g=norm1_g, norm2_g=norm2_g, w_in=w_in, swa_q_g=swa_q_g,
                   swa_k_g=swa_k_g, swa_sink=swa_sink, dn_conv_w=dn_conv_w, dn_A_log=dn_A_log, dn_dt_bias=dn_dt_bias,
                   dn_out_g=dn_out_g, na_q_g=na_q_g, na_k_g=na_k_g, na_rpb=na_rpb, w_out=w_out, w_gate=w_gate, w_up=w_up,
                   w_down=w_down)
    ms = dict(c_ctx=m_c_ctx, w_ada=m_w_ada, b_ada=m_b_ada, norm1_g=m_norm1_g, norm2_g=m_norm2_g, w_in=m_w_in, swa_q_g=m_swa_q_g,
              swa_k_g=m_swa_k_g, swa_sink=m_swa_sink, dn_conv_w=m_dn_conv_w, dn_A_log=m_dn_A_log, dn_dt_bias=m_dn_dt_bias,
              dn_out_g=m_dn_out_g, na_q_g=m_na_q_g, na_k_g=m_na_k_g, na_rpb=m_na_rpb, w_out=m_w_out, w_gate=m_w_gate, w_up=m_w_up,
              w_down=m_w_down)
    vs = dict(c_ctx=v_c_ctx, w_ada=v_w_ada, b_ada=v_b_ada, norm1_g=v_norm1_g, norm2_g=v_norm2_g, w_in=v_w_in, swa_q_g=v_swa_q_g,
              swa_k_g=v_swa_k_g, swa_sink=v_swa_sink, dn_conv_w=v_dn_conv_w, dn_A_log=v_dn_A_log, dn_dt_bias=v_dn_dt_bias,
              dn_out_g=v_dn_out_g, na_q_g=v_na_q_g, na_k_g=v_na_k_g, na_rpb=v_na_rpb, w_out=v_w_out, w_gate=v_w_gate, w_up=v_w_up,
              w_down=v_w_down)
    order = ("c_ctx", "w_ada", "b_ada", "norm1_g", "norm2_g", "w_in", "swa_q_g", "swa_k_g", "swa_sink", "dn_conv_w", "dn_A_log",
             "dn_dt_bias", "dn_out_g", "na_q_g", "na_k_g", "na_rpb", "w_out", "w_gate", "w_up", "w_down")
    big_names = ("w_ada", "w_in", "w_out", "w_gate", "w_up", "w_down")
    grads = {n: grads[n].reshape(weights[n].shape) for n in order}
    delta, new_m, new_v = {}, {}, {}
    for n in big_names:
        delta[n], new_m[n], new_v[n] = _adamw(f"adamw_{n}", weights[n], grads[n], ms[n], vs[n])
    small_names = [n for n in order if n not in big_names]
    packed = [_pack([d[n] for n in small_names]) for d in (weights, grads, ms, vs)]
    outs = _adamw("adamw_small", *packed)
    like = [weights[n] for n in small_names]
    for d, o in zip((delta, new_m, new_v), outs):
        d.update(dict(zip(small_names, _unpack(o, like))))
    return (loss, grad_x, *[grads[n] for n in order], *[delta[n] for n in order], *[new_m[n] for n in order],
            *[new_v[n] for n in order])
```

```python
import functools
import math

import jax
import jax.numpy as jnp
import numpy as np
from jax import lax
from jax.experimental import pallas as pl
from jax.experimental.pallas import tpu as pltpu

F32, BF16 = jnp.float32, jnp.bfloat16
MESH = pl.DeviceIdType.MESH

GRID_W = 64
HEAD = 128
SWA_WINDOW = 128
DN_CONV = 5
DN_CHUNK = 64
NA_KH, NA_KW = 8, 16
ROPE_THETA = 10000.0
EPS = 1e-6
ADAM_LR, ADAM_B1, ADAM_B2, ADAM_EPS, ADAM_WD, ADAM_STEP = 0.001, 0.9, 0.999, 1e-08, 0.01, 10
N_CHIPS = 4
N_DEV = 8
TOK = 256
VMEM_LIMIT = 56 * 2 ** 20
MATMUL_VMEM = 40 * 2 ** 20

_CALL_KW = {}


def _cparams(sem=None, **kw):
    if sem is not None:
        kw["dimension_semantics"] = sem
    return pltpu.CompilerParams(vmem_limit_bytes=VMEM_LIMIT, **kw)


def _pick(n, cands):
    for cnd in cands:
        if n % cnd == 0:
            return cnd
    raise ValueError(f"no tile for {n} in {cands}")


def _axes():
    return lax.axis_index("x"), lax.axis_index("y"), lax.axis_index("c")


def _matmul(name, a, b, kind, out_dtype=F32, tm=None, tn=None, tk=None, comm=None):
    if kind == "nn":
        (M, K), (K2, N) = a.shape, b.shape
    elif kind == "nt":
        (M, K), (N, K2) = a.shape, b.shape
    else:
        (K, M), (K2, N) = a.shape, b.shape
    assert K == K2, (name, a.shape, b.shape)
    tm = tm or _pick(M, (1024, 512, 256, 128) if kind == "tn" else (1088, 1024, 704, 512, 256, 128, 64, 32, 16, 8))

    def vmem(tn_, tk_):
        acc = tm * tn_ * 4 if tk_ < K else 0
        return 2 * (tm * tk_ * a.dtype.itemsize + tk_ * tn_ * b.dtype.itemsize + tm * tn_ * jnp.dtype(out_dtype).itemsize) + acc

    if tn is None or tk is None:
        cands = [(n_, k_) for k_ in ((tk,) if tk else (K, 3328, 2816, 2048, 1024, 512)) if K % k_ == 0
                 for n_ in ((tn,) if tn else (1024, 512, 256, 128)) if N % n_ == 0]
        tn, tk = next((c for c in cands if vmem(*c) <= MATMUL_VMEM), cands[-1])
    nk = K // tk
    dims = {"nn": (((1,), (0,)), ((), ())), "nt": (((1,), (1,)), ((), ())), "tn": (((0,), (0,)), ((), ()))}[kind]

    ni, nj = M // tm, N // tn

    def body(*refs):
        (a_ref, b_ref), (o_ref,), scr, cref = _hosted(comm, 2, 1, refs)
        i, j, k = pl.program_id(0), pl.program_id(1), pl.program_id(2)
        _hosted_start(comm, cref, (i == 0) & (j == 0) & (k == 0))
        part = lax.dot_general(a_ref[...].astype(BF16), b_ref[...].astype(BF16), dims, preferred_element_type=F32)
        if nk == 1:
            o_ref[...] = part.astype(o_ref.dtype)
        else:
            acc = scr[0]

            @pl.when(k == 0)
            def _():
                acc[...] = part

            @pl.when(k > 0)
            def _():
                acc[...] += part

            @pl.when(k == nk - 1)
            def _():
                o_ref[...] = acc[...].astype(o_ref.dtype)
        _hosted_wait(comm, cref, (i == ni - 1) & (j == nj - 1) & (k == nk - 1))

    a_spec = {"nn": pl.BlockSpec((tm, tk), lambda i, j, k: (i, k)), "nt": pl.BlockSpec((tm, tk), lambda i, j, k: (i, k)),
              "tn": pl.BlockSpec((tk, tm), lambda i, j, k: (k, i))}[kind]
    b_spec = {"nn": pl.BlockSpec((tk, tn), lambda i, j, k: (k, j)), "nt": pl.BlockSpec((tn, tk), lambda i, j, k: (j, k)),
              "tn": pl.BlockSpec((tk, tn), lambda i, j, k: (k, j))}[kind]
    any_spec = pl.BlockSpec(memory_space=pl.ANY)
    c_ins, c_sds, c_sems = (comm.ins, comm.out_sds(), comm.sem_shapes()) if comm is not None else ([], [], [])
    sem = ("parallel", "parallel", "arbitrary") if comm is None else ("arbitrary",) * 3
    res = pl.pallas_call(
        body, name=name, grid=(ni, nj, nk), in_specs=[a_spec, b_spec] + [any_spec] * len(c_ins),
        out_specs=[pl.BlockSpec((tm, tn), lambda i, j, k: (i, j))] + [any_spec] * len(c_sds),
        out_shape=[jax.ShapeDtypeStruct((M, N), out_dtype)] + c_sds,
        scratch_shapes=([pltpu.VMEM((tm, tn), F32)] if nk > 1 else []) + c_sems,
        compiler_params=_cparams(sem, has_side_effects=comm is not None), **_CALL_KW)(a, b, *c_ins)
    return res[0] if comm is None else (res[0], list(res[1:]))


class _Comm:
    def __init__(self, ins, out_shapes, plan, n_local, n_remote, aliases=None):
        self.ins, self.out_shapes, self.plan = list(ins), list(out_shapes), plan
        self.n_local, self.n_remote, self.aliases = n_local, n_remote, aliases or {}

    def sem_shapes(self):
        return [pltpu.SemaphoreType.DMA((max(self.n_remote, 1),)), pltpu.SemaphoreType.DMA((max(self.n_remote, 1),)),
                pltpu.SemaphoreType.DMA((max(self.n_local, 1),))]

    def out_sds(self):
        return [jax.ShapeDtypeStruct(s, d) for s, d in self.out_shapes]

    def copies(self, in_refs, out_refs, sems):
        send_sems, recv_sems, loc_sems = sems
        x, y, c = _axes()
        local, remote = self.plan(x, y, c, in_refs, out_refs)
        assert len(local) == self.n_local and len(remote) == self.n_remote, (len(local), len(remote))
        lcs = [pltpu.make_async_copy(s, d, loc_sems.at[i]) for i, (s, d) in enumerate(local)]
        rcs = [pltpu.make_async_remote_copy(src_ref=s, dst_ref=d, send_sem=send_sems.at[i], recv_sem=recv_sems.at[i],
                                            device_id=dev, device_id_type=MESH) for i, (s, d, dev) in enumerate(remote)]
        return lcs + rcs


def _exchange(name, comm):
    n_in, n_out = len(comm.ins), len(comm.out_shapes)

    def body(*refs):
        cps = comm.copies(refs[:n_in], refs[n_in:n_in + n_out], refs[n_in + n_out:])
        for cp in cps:
            cp.start()
        for cp in cps:
            cp.wait()

    any_spec = pl.BlockSpec(memory_space=pl.ANY)
    return pl.pallas_call(
        body, name=name, in_specs=[any_spec] * n_in, out_specs=[any_spec] * n_out, out_shape=comm.out_sds(),
        scratch_shapes=comm.sem_shapes(), input_output_aliases=comm.aliases,
        compiler_params=pltpu.CompilerParams(has_side_effects=True), **_CALL_KW)(*comm.ins)


def _chip_of(k):
    return k // 2, k % 2


def _gather_ici_comm(shards, layer):
    def plan(x, y, c, ins, outs):
        me = 2 * x + y
        remote = []
        for w, g in zip(ins, outs):
            half = w.shape[1] // 2
            rows = pl.ds(c * half, half)
            for j in (1, 2, 3):
                px, py = _chip_of(me ^ j)
                remote.append((w.at[layer, rows], g.at[me, rows], (px, py, c)))
        return [], remote

    return _Comm(shards, [((N_CHIPS,) + w.shape[1:], w.dtype) for w in shards], plan, 0, 3 * len(shards))


def _gather_d2d(name, gath):
    def plan(x, y, c, ins, outs):
        me = 2 * x + y
        remote = []
        for g in outs:
            half = g.shape[1] // 2
            rows = pl.ds(c * half, half)
            for j in (1, 2, 3):
                remote.append((g.at[me ^ j, rows], g.at[me ^ j, rows], (x, y, 1 - c)))
        return [], remote

    n = len(gath)
    return _exchange(name, _Comm(gath, [(g.shape, g.dtype) for g in gath], plan, 0, 3 * n, aliases={i: i for i in range(n)}))


def _elementwise(name, fn, ins, out_dtypes, block_rows=None, n_out=None):
    shape = ins[0].shape
    lead, (R, C) = shape[:-2], shape[-2:]
    budget = (16 * 2 ** 20) // (8 * (len(ins) + len(out_dtypes)) * (-(-C // 128) * 128))
    br = block_rows or _pick(R, [r for r in (512, 256, 128, 352, 64, 32, 16, 8) if r <= max(budget, 8)] + [R])
    nl = len(lead)

    def body(*refs):
        outs = fn(*[r[...] for r in refs[:len(ins)]])
        for r, o in zip(refs[len(ins):], outs):
            r[...] = o.astype(r.dtype)

    blk = (None,) * nl + (br, C)
    spec = pl.BlockSpec(blk, lambda *g: tuple(g[:nl]) + (g[nl], 0))
    return pl.pallas_call(
        body, name=name, grid=tuple(lead) + (R // br,), in_specs=[spec] * len(ins), out_specs=[spec] * len(out_dtypes),
        out_shape=[jax.ShapeDtypeStruct(shape, d) for d in out_dtypes],
        compiler_params=_cparams(("parallel",) * (nl + 1)), **_CALL_KW)(*ins)


def _reduce_pre(tag, parts):
    n = len(parts)

    def plan_a(x, y, c, ins, outs):
        remote = []
        for p, r in zip(ins, outs):
            half = p.shape[2] // 2
            remote.append((p.at[:, :, pl.ds((1 - c) * half, half)], r, (x, y, 1 - c)))
        return [], remote

    halves = [((p.shape[0], p.shape[1], p.shape[2] // 2, p.shape[3]), p.dtype) for p in parts]
    got = _exchange(f"reduce_{tag}_d2d", _Comm(parts, halves, plan_a, 0, n))

    c = lax.axis_index("c")
    pair = []
    for idx, (p, r) in enumerate(zip(parts, got)):
        half = p.shape[2] // 2
        br = _pick(half, (512, 256, 352, 128, 64, 32, 16))
        nb = half // br

        def body(c_ref, p_ref, r_ref, o_ref):
            o_ref[...] = (p_ref[...].astype(F32) + r_ref[...].astype(F32)).astype(o_ref.dtype)

        blk = (None, None, br, p.shape[3])
        pair.append(pl.pallas_call(
            body, name=f"reduce_{tag}_pair{idx}",
            grid_spec=pltpu.PrefetchScalarGridSpec(
                num_scalar_prefetch=1, grid=(N_CHIPS, p.shape[1], nb),
                in_specs=[pl.BlockSpec(blk, lambda k, l, i, cr, nb=nb: (k, l, cr[0] * nb + i, 0)),
                          pl.BlockSpec(blk, lambda k, l, i, cr: (k, l, i, 0))],
                out_specs=pl.BlockSpec(blk, lambda k, l, i, cr: (k, l, i, 0))),
            out_shape=jax.ShapeDtypeStruct(r.shape, BF16),
            compiler_params=_cparams(("parallel",) * 3), **_CALL_KW)(jnp.reshape(c, (1,)).astype(jnp.int32), p, r))
    return pair


def _reduce_ici_comm(pair):
    def plan_b(x, y, c, ins, outs):
        me = 2 * x + y
        remote = []
        for p, r in zip(ins, outs):
            for j in (1, 2, 3):
                px, py = _chip_of(me ^ j)
                remote.append((p.at[me ^ j], r.at[me], (px, py, c)))
        return [], remote

    return _Comm(pair, [(p.shape, p.dtype) for p in pair], plan_b, 0, 3 * len(pair))


def _reduce_post(tag, pair, got):
    n = len(pair)
    c = lax.axis_index("c")
    me_chip = 2 * lax.axis_index("x") + lax.axis_index("y")

    sums = []
    for idx, r in enumerate(got):
        _, L, half, C = r.shape
        br = _pick(half, (512, 256, 352, 128, 64, 32, 16))
        nb = half // br

        def body(c_ref, p_ref, r_ref, o_ref):
            me = c_ref[1]
            acc = None
            for k in range(N_CHIPS):
                term = jnp.where(me == k, p_ref[k], r_ref[k]).astype(F32)
                acc = term if acc is None else acc + term
            o_ref[...] = acc

        blk4 = pl.BlockSpec((N_CHIPS, None, br, C), lambda l, i, cr: (0, l, i, 0))
        sums.append(pl.pallas_call(
            body, name=f"reduce_{tag}_sum{idx}",
            grid_spec=pltpu.PrefetchScalarGridSpec(
                num_scalar_prefetch=1, grid=(L, nb), in_specs=[blk4, blk4],
                out_specs=pl.BlockSpec((None, br, C), lambda l, i, cr, nb=nb: (l, cr[0] * nb + i, 0))),
            out_shape=jax.ShapeDtypeStruct((L, 2 * half, C), F32),
            compiler_params=_cparams(("parallel",) * 2), **_CALL_KW)(jnp.stack([c, me_chip]).astype(jnp.int32), pair[idx], r))

    def plan_c(x, y, c, ins, outs):
        remote = []
        for f in outs:
            half = f.shape[1] // 2
            rows = pl.ds(c * half, half)
            remote.append((f.at[:, rows], f.at[:, rows], (x, y, 1 - c)))
        return [], remote

    return _exchange(f"reduce_{tag}_bcast", _Comm(sums, [(s.shape, F32) for s in sums], plan_c, 0, n, aliases={i: i for i in range(n)}))


def _adamw_math(w, g, m, v):
    m = ADAM_B1 * m + (1.0 - ADAM_B1) * g
    v = ADAM_B2 * v + (1.0 - ADAM_B2) * (g * g)
    m_hat = m / (1.0 - ADAM_B1 ** ADAM_STEP)
    v_hat = v / (1.0 - ADAM_B2 ** ADAM_STEP)
    delta = -ADAM_LR * (m_hat / (jnp.sqrt(v_hat) + ADAM_EPS) + ADAM_WD * w)
    return delta, m, v


def _adamw(name, w, g, m, v):
    return _elementwise(name, _adamw_math, [w, g, m, v], [F32, F32, F32])


def _allgather_small(name, v):
    def plan(x, y, c, ins, outs):
        me = 4 * x + 2 * y + c
        (src,), (dst,) = ins, outs
        remote = []
        for j in range(1, N_DEV):
            p = me ^ j
            remote.append((src, dst.at[me], (p // 4, (p // 2) % 2, p % 2)))
        return [(src, dst.at[me])], remote

    return _exchange(name, _Comm([v], [((N_DEV,) + v.shape, v.dtype)], plan, 1, N_DEV - 1))[0]


def _seg_spec(rows, D, ctx_tiles):
    return pl.BlockSpec((None, rows, D), lambda i: (jnp.minimum(i // ctx_tiles, 1), 0, 0))


def _norm_mod(name, x, g, modv, r0, ctx_tiles):
    T, D = x.shape

    def body(x_ref, g_ref, m_ref, o_ref):
        xv = x_ref[...]
        r = lax.rsqrt(jnp.mean(xv * xv, axis=-1, keepdims=True) + EPS)
        y = xv * r * g_ref[...]
        o_ref[...] = (y * (1.0 + m_ref[r0 + 1:r0 + 2, :]) + m_ref[r0:r0 + 1, :]).astype(BF16)

    row = pl.BlockSpec((TOK, D), lambda i: (i, 0))
    return pl.pallas_call(
        body, name=name, grid=(T // TOK,), in_specs=[row, pl.BlockSpec((1, D), lambda i: (0, 0)), _seg_spec(6, D, ctx_tiles)],
        out_specs=row, out_shape=jax.ShapeDtypeStruct((T, D), BF16), compiler_params=_cparams(("parallel",)), **_CALL_KW)(x, g, modv)


def _norm_mod_bwd(name, x, g, modv, r0, dh, dh_b, dres, ctx_tiles):
    T, D = x.shape

    def body(x_ref, g_ref, m_ref, dh_ref, dhb_ref, dres_ref, dx_ref, dg_ref, dsh_ref, dsc_ref):
        i = pl.program_id(0)
        xv = x_ref[...]
        r = lax.rsqrt(jnp.mean(xv * xv, axis=-1, keepdims=True) + EPS)
        xn = xv * r
        y = xn * g_ref[...]
        dhv = dh_ref[...] + dhb_ref[...]

        @pl.when(i == 0)
        def _():
            dg_ref[...] = jnp.zeros_like(dg_ref)

        @pl.when((i == 0) | (i == ctx_tiles))
        def _():
            dsh_ref[...] = jnp.zeros_like(dsh_ref)
            dsc_ref[...] = jnp.zeros_like(dsc_ref)

        dsh_ref[...] += jnp.sum(dhv, axis=0, keepdims=True)
        dsc_ref[...] += jnp.sum(dhv * y, axis=0, keepdims=True)
        dy = dhv * (1.0 + m_ref[r0 + 1:r0 + 2, :])
        dg_ref[...] += jnp.sum(dy * xn, axis=0, keepdims=True)
        u = dy * g_ref[...]
        dx_ref[...] = dres_ref[...] + r * (u - xn * jnp.mean(u * xn, axis=-1, keepdims=True))

    row = pl.BlockSpec((TOK, D), lambda i: (i, 0))
    one = pl.BlockSpec((1, D), lambda i: (0, 0))
    return pl.pallas_call(
        body, name=name, grid=(T // TOK,), in_specs=[row, one, _seg_spec(6, D, ctx_tiles), row, row, row],
        out_specs=[row, one, _seg_spec(1, D, ctx_tiles), _seg_spec(1, D, ctx_tiles)],
        out_shape=[jax.ShapeDtypeStruct((T, D), F32), jax.ShapeDtypeStruct((1, D), F32),
                   jax.ShapeDtypeStruct((2, 1, D), F32), jax.ShapeDtypeStruct((2, 1, D), F32)],
        compiler_params=_cparams(("arbitrary",)), **_CALL_KW)(x, g, modv, dh, dh_b, dres)


def _resid_gate(name, x, y, modv, r, ctx_tiles):
    T, D = x.shape

    def body(x_ref, y_ref, m_ref, o_ref):
        o_ref[...] = x_ref[...] + m_ref[r:r + 1, :] * y_ref[...]

    row = pl.BlockSpec((TOK, D), lambda i: (i, 0))
    return pl.pallas_call(
        body, name=name, grid=(T // TOK,), in_specs=[row, row, _seg_spec(6, D, ctx_tiles)], out_specs=row,
        out_shape=jax.ShapeDtypeStruct((T, D), F32), compiler_params=_cparams(("parallel",)), **_CALL_KW)(x, y, modv)


def _resid_gate_bwd(name, dx, y, modv, r, ctx_tiles):
    T, D = dx.shape

    def body(dx_ref, y_ref, m_ref, dy_ref, dgt_ref):
        i = pl.program_id(0)

        @pl.when((i == 0) | (i == ctx_tiles))
        def _():
            dgt_ref[...] = jnp.zeros_like(dgt_ref)

        dxv = dx_ref[...]
        dgt_ref[...] += jnp.sum(dxv * y_ref[...], axis=0, keepdims=True)
        dy_ref[...] = (dxv * m_ref[r:r + 1, :]).astype(BF16)

    row = pl.BlockSpec((TOK, D), lambda i: (i, 0))
    return pl.pallas_call(
        body, name=name, grid=(T // TOK,), in_specs=[row, row, _seg_spec(6, D, ctx_tiles)],
        out_specs=[row, _seg_spec(1, D, ctx_tiles)],
        out_shape=[jax.ShapeDtypeStruct((T, D), BF16), jax.ShapeDtypeStruct((2, 1, D), F32)],
        compiler_params=_cparams(("arbitrary",)), **_CALL_KW)(dx, y, modv)


def _sigmoid(x):
    return 1.0 / (1.0 + jnp.exp(-x))


SWI_ROWS = 128


def _swiglu(name, gu):
    T, F2 = gu.shape
    F = F2 // 2

    def body(gu_ref, o_ref):
        g, u = gu_ref[:, :F], gu_ref[:, F:]
        o_ref[...] = ((g * _sigmoid(g)) * u).astype(BF16)

    return pl.pallas_call(
        body, name=name, grid=(T // SWI_ROWS,), in_specs=[pl.BlockSpec((SWI_ROWS, F2), lambda i: (i, 0))],
        out_specs=pl.BlockSpec((SWI_ROWS, F), lambda i: (i, 0)), out_shape=jax.ShapeDtypeStruct((T, F), BF16),
        compiler_params=_cparams(("parallel",)), **_CALL_KW)(gu)


def _swiglu_bwd(name, gu, dact):
    T, F2 = gu.shape
    F = F2 // 2

    def body(gu_ref, d_ref, o_ref):
        g, u, d = gu_ref[:, :F], gu_ref[:, F:], d_ref[...]
        s = _sigmoid(g)
        o_ref[:, :F] = (d * u * (s * (1.0 + g * (1.0 - s)))).astype(BF16)
        o_ref[:, F:] = (d * (g * s)).astype(BF16)

    return pl.pallas_call(
        body, name=name, grid=(T // SWI_ROWS,),
        in_specs=[pl.BlockSpec((SWI_ROWS, F2), lambda i: (i, 0)), pl.BlockSpec((SWI_ROWS, F), lambda i: (i, 0))],
        out_specs=pl.BlockSpec((SWI_ROWS, F2), lambda i: (i, 0)), out_shape=jax.ShapeDtypeStruct((T, F2), BF16),
        compiler_params=_cparams(("parallel",)), **_CALL_KW)(gu, dact)


def _loss_and_grad(name, y, target, ctx_tiles):
    T, D = y.shape

    def body(y_ref, t_ref, l_ref, dy_ref):
        i = pl.program_id(0)

        @pl.when(i == 0)
        def _():
            l_ref[...] = jnp.zeros_like(l_ref)

        lat = i >= ctx_tiles
        e = jnp.where(lat, y_ref[...] - t_ref[...], 0.0)
        dy_ref[...] = e * (1.0 / D)
        l_ref[...] += 0.5 * jnp.sum(jnp.sum(e * e, axis=-1, keepdims=True) * (1.0 / D), axis=0, keepdims=True)

    row = pl.BlockSpec((TOK, D), lambda i: (i, 0))
    return pl.pallas_call(
        body, name=name, grid=(T // TOK,),
        in_specs=[row, pl.BlockSpec((TOK, D), lambda i: (jnp.maximum(i - ctx_tiles, 0), 0))],
        out_specs=[pl.BlockSpec((8, 128), lambda i: (0, 0)), row],
        out_shape=[jax.ShapeDtypeStruct((8, 128), F32), jax.ShapeDtypeStruct((T, D), F32)],
        compiler_params=_cparams(("arbitrary",)), **_CALL_KW)(y, target)


def _rot_half(x):
    lane = lax.broadcasted_iota(jnp.int32, x.shape, 1)
    return jnp.where((lane % 64) < 32, -pltpu.roll(x, 96, 1), pltpu.roll(x, 32, 1))


def _head_prep(name, src, col_blk, n_heads, g, cos, sin, t_pad, norm, rope):
    T = src.shape[0]
    W = n_heads * HEAD
    nt = T // TOK

    def body(s_ref, g_ref, cos_ref, sin_ref, o_ref):
        i = pl.program_id(0)
        outs = []
        for h in range(n_heads):
            xv = s_ref[:, h * HEAD:(h + 1) * HEAD].astype(F32)
            if norm:
                xv = xv * lax.rsqrt(jnp.mean(xv * xv, axis=-1, keepdims=True) + EPS) * g_ref[...]
            if rope:
                xv = xv * cos_ref[...] + _rot_half(xv) * sin_ref[...]
            outs.append(jnp.where(i < nt, xv, 0.0).astype(BF16))
        o_ref[...] = jnp.concatenate(outs, axis=-1) if n_heads > 1 else outs[0]

    tab = pl.BlockSpec((TOK, HEAD), lambda i: (i, 0))
    return pl.pallas_call(
        body, name=name, grid=(t_pad // TOK,),
        in_specs=[pl.BlockSpec((TOK, W), lambda i: (jnp.minimum(i, nt - 1), col_blk)), pl.BlockSpec((1, HEAD), lambda i: (0, 0)), tab, tab],
        out_specs=pl.BlockSpec((TOK, W), lambda i: (i, 0)), out_shape=jax.ShapeDtypeStruct((t_pad, W), BF16),
        compiler_params=_cparams(("parallel",)), **_CALL_KW)(src, g, cos, sin)


def _head_prep_bwd(name, src, col_blk, n_heads, g, cos, sin, dout, norm, rope, dst):
    T = src.shape[0]
    W = n_heads * HEAD

    def body(s_ref, g_ref, cos_ref, sin_ref, d_ref, dst_ref, ds_ref, dg_ref):
        i = pl.program_id(0)

        @pl.when(i == 0)
        def _():
            dg_ref[...] = jnp.zeros_like(dg_ref)

        outs = []
        dg = jnp.zeros((1, HEAD), F32)
        for h in range(n_heads):
            dz = d_ref[:, h * HEAD:(h + 1) * HEAD]
            if rope:
                dz = dz * cos_ref[...] - _rot_half(dz * sin_ref[...])
            if norm:
                xv = s_ref[:, h * HEAD:(h + 1) * HEAD].astype(F32)
                r = lax.rsqrt(jnp.mean(xv * xv, axis=-1, keepdims=True) + EPS)
                xn = xv * r
                dg = dg + jnp.sum(dz * xn, axis=0, keepdims=True)
                u = dz * g_ref[...]
                dz = r * (u - xn * jnp.mean(u * xn, axis=-1, keepdims=True))
            outs.append(dz.astype(BF16))
        dg_ref[...] += dg
        ds_ref[...] = jnp.concatenate(outs, axis=-1) if n_heads > 1 else outs[0]

    tab = pl.BlockSpec((TOK, HEAD), lambda i: (i, 0))
    col = pl.BlockSpec((TOK, W), lambda i: (i, col_blk))
    one = pl.BlockSpec((1, HEAD), lambda i: (0, 0))
    return pl.pallas_call(
        body, name=name, grid=(T // TOK,),
        in_specs=[col, one, tab, tab, pl.BlockSpec((TOK, W), lambda i: (i, 0)), pl.BlockSpec(memory_space=pl.ANY)],
        out_specs=[col, one], out_shape=[jax.ShapeDtypeStruct(dst.shape, dst.dtype), jax.ShapeDtypeStruct((1, HEAD), F32)],
        input_output_aliases={5: 0}, compiler_params=_cparams(("arbitrary",)), **_CALL_KW)(src, g, cos, sin, dout, dst)


NEG = -1e30


def _attn_geometry(kind, blk, ctx, seq):
    if kind == "swa":
        bq, W = 128, 384
        nctx = ctx // bq
        lat = blk >= nctx
        n = blk - nctx
        s0 = jnp.where(lat, ctx + (n - 1) * bq, 0)
        i = lax.broadcasted_iota(jnp.int32, (bq, W), 0)
        j = lax.broadcasted_iota(jnp.int32, (bq, W), 1)
        kpos = (n - 1) * bq + j
        rel = j - bq - i
        valid = lat & (rel <= SWA_WINDOW) & (rel >= -SWA_WINDOW) & (kpos >= 0) & (kpos < seq)
        return s0, valid, 0
    bq, W = GRID_W, NA_KH * GRID_W
    nctx = ctx // bq
    rows = seq // GRID_W
    lat = blk >= nctx
    rr = jnp.clip(blk - nctx, 0, rows - 1)
    rs = jnp.clip(rr - NA_KH // 2, 0, rows - NA_KH)
    s0 = ctx + rs * GRID_W
    i = lax.broadcasted_iota(jnp.int32, (bq, W), 0)
    j = lax.broadcasted_iota(jnp.int32, (bq, W), 1)
    kcol = j % GRID_W
    cs = jnp.clip(i - NA_KW // 2, 0, GRID_W - NA_KW)
    valid = lat & (kcol >= cs) & (kcol < cs + NA_KW)
    return s0, valid, rr - rs


HP = 2


def _attn_probs(q, kl, kc, sk, bias, valid):
    scale = HEAD ** -0.5
    nt_dims = (((1,), (1,)), ((), ()))
    sl = lax.dot_general(q, kl, nt_dims, preferred_element_type=F32) * scale
    if bias is not None:
        sl = sl + bias
    sl = jnp.where(valid, sl, NEG)
    sc = lax.dot_general(q, kc, nt_dims, preferred_element_type=F32) * scale
    m = jnp.maximum(jnp.maximum(jnp.max(sl, axis=-1, keepdims=True), jnp.max(sc, axis=-1, keepdims=True)), sk)
    el, ec, es = jnp.exp(sl - m), jnp.exp(sc - m), jnp.exp(sk - m)
    inv = 1.0 / (jnp.sum(el, axis=-1, keepdims=True) + jnp.sum(ec, axis=-1, keepdims=True) + es)
    return el * inv, ec * inv, es * inv


def _attn_specs(kind, n_q, n_kv, t_pad):
    bq = 128 if kind == "swa" else GRID_W
    rep = n_q // n_kv
    assert n_q % HP == 0 and HP % rep == 0
    kvw = HP // rep
    qspec = pl.BlockSpec((bq, HP * HEAD), lambda g, b: (b, g))
    kvspec = pl.BlockSpec((t_pad, kvw * HEAD), lambda g, b: (0, g))
    specs = [qspec, kvspec, kvspec, pl.BlockSpec(memory_space=pltpu.SMEM)]
    return bq, rep, qspec, kvspec, specs


def _bias_spec(ctx, seq):
    W = NA_KH * GRID_W

    def idx(g, b):
        rows = seq // GRID_W
        rr = jnp.clip(b - ctx // GRID_W, 0, rows - 1)
        return (g, rr - jnp.clip(rr - NA_KH // 2, 0, rows - NA_KH), 0, 0)

    return pl.BlockSpec((HP, None, GRID_W, W), idx)


def _attn_loads(kind, blk, q_ref, k_ref, v_ref, sink_ref, bias_ref, rep, ctx, seq):
    bq, W = (128, 384) if kind == "swa" else (GRID_W, NA_KH * GRID_W)
    g = pl.program_id(0)
    s0, valid, _ = _attn_geometry(kind, blk, ctx, seq)
    s0 = pl.multiple_of(s0, GRID_W)
    heads = []
    for j in range(HP):
        kv = slice((j // rep) * HEAD, (j // rep + 1) * HEAD)
        heads.append((q_ref[:, j * HEAD:(j + 1) * HEAD], k_ref[pl.ds(s0, W), kv], k_ref[0:ctx, kv], v_ref[pl.ds(s0, W), kv],
                      v_ref[0:ctx, kv], sink_ref[g * HP + j], bias_ref[j] if bias_ref is not None else None))
    return s0, W, valid, heads


def _attn_fwd(name, kind, q, k, v, sink, bias, ctx, seq, dst, head0, comm=None):
    t_pad = q.shape[0]
    T = ctx + seq
    n_q, n_kv = q.shape[1] // HEAD, k.shape[1] // HEAD
    bq, rep, qspec, kvspec, specs = _attn_specs(kind, n_q, n_kv, t_pad)
    assert head0 % HP == 0
    ins = [q, k, v, sink] + ([bias] if bias is not None else []) + [dst]
    ng, nb = n_q // HP, T // bq

    def body(*refs):
        mine, (o_ref,), _, cref = _hosted(comm, len(ins), 1, refs)
        q_ref, k_ref, v_ref, sink_ref = mine[:4]
        bias_ref = mine[4] if bias is not None else None
        g, blk = pl.program_id(0), pl.program_id(1)
        _hosted_start(comm, cref, (g == 0) & (blk == 0))
        _, _, valid, heads = _attn_loads(kind, blk, q_ref, k_ref, v_ref, sink_ref, bias_ref, rep, ctx, seq)
        outs = []
        for qv, kl, kc, vl, vc, sk, bv in heads:
            p_l, p_c, _ = _attn_probs(qv, kl, kc, sk, bv, valid)
            o = jnp.dot(p_l.astype(BF16), vl, preferred_element_type=F32) + jnp.dot(p_c.astype(BF16), vc, preferred_element_type=F32)
            outs.append(o.astype(o_ref.dtype))
        o_ref[...] = jnp.concatenate(outs, axis=-1)
        _hosted_wait(comm, cref, (g == ng - 1) & (blk == nb - 1))

    if bias is not None:
        specs = specs + [_bias_spec(ctx, seq)]
    any_spec = pl.BlockSpec(memory_space=pl.ANY)
    c_ins, c_sds, c_sems = (comm.ins, comm.out_sds(), comm.sem_shapes()) if comm is not None else ([], [], [])
    res = pl.pallas_call(
        body, name=name, grid=(ng, nb), in_specs=specs + [any_spec] * (1 + len(c_ins)),
        out_specs=[pl.BlockSpec((bq, HP * HEAD), lambda g, b: (b, head0 // HP + g))] + [any_spec] * len(c_sds),
        out_shape=[jax.ShapeDtypeStruct(dst.shape, dst.dtype)] + c_sds, input_output_aliases={len(ins) - 1: 0},
        scratch_shapes=c_sems,
        compiler_params=_cparams(("arbitrary", "arbitrary"), has_side_effects=comm is not None), **_CALL_KW)(*ins, *c_ins)
    return res[0], list(res[1:])


def _attn_bwd(name, kind, q, k, v, sink, bias, do, do_head0, ctx, seq):
    t_pad = q.shape[0]
    T = ctx + seq
    n_q, n_kv = q.shape[1] // HEAD, k.shape[1] // HEAD
    bq, rep, qspec, kvspec, specs = _attn_specs(kind, n_q, n_kv, t_pad)
    assert do_head0 % HP == 0
    scale = HEAD ** -0.5
    tn_dims = (((0,), (0,)), ((), ()))
    nt_dims = (((1,), (1,)), ((), ()))
    bdot = functools.partial(lax.dot_general, preferred_element_type=F32)

    def body(q_ref, k_ref, v_ref, sink_ref, *rest):
        if bias is not None:
            bias_ref, do_ref, dq_ref, dk_ref, dv_ref, dsk_ref, db_ref = rest
        else:
            bias_ref, db_ref = None, None
            do_ref, dq_ref, dk_ref, dv_ref, dsk_ref = rest
        blk = pl.program_id(1)
        s0, W, valid, heads = _attn_loads(kind, blk, q_ref, k_ref, v_ref, sink_ref, bias_ref, rep, ctx, seq)
        dos = [do_ref[:, j * HEAD:(j + 1) * HEAD] for j in range(HP)]

        @pl.when(blk == 0)
        def _():
            dk_ref[...] = jnp.zeros_like(dk_ref)
            dv_ref[...] = jnp.zeros_like(dv_ref)
            dsk_ref[...] = jnp.zeros_like(dsk_ref)

        if bias is not None:
            _, _, pat = _attn_geometry(kind, blk, ctx, seq)
            _, _, pat_prev = _attn_geometry(kind, jnp.maximum(blk - 1, 0), ctx, seq)

            @pl.when((blk == 0) | (pat != pat_prev))
            def _():
                db_ref[...] = jnp.zeros_like(db_ref)

        res = []
        for (qv, kl, kc, vl, vc, sk, bv), dov in zip(heads, dos):
            p_l, p_c, p_s = _attn_probs(qv, kl, kc, sk, bv, valid)
            dob = dov.astype(BF16)
            pl_b, pc_b = p_l.astype(BF16), p_c.astype(BF16)
            o = jnp.dot(pl_b, vl, preferred_element_type=F32) + jnp.dot(pc_b, vc, preferred_element_type=F32)
            delta = jnp.sum(dov * o, axis=-1, keepdims=True)
            ds_l = p_l * (bdot(dob, vl, nt_dims) - delta)
            ds_c = p_c * (bdot(dob, vc, nt_dims) - delta)
            dsl_b, dsc_b = ds_l.astype(BF16), ds_c.astype(BF16)
            dq = (jnp.dot(dsl_b, kl, preferred_element_type=F32) + jnp.dot(dsc_b, kc, preferred_element_type=F32)) * scale
            res.append((dq, bdot(dsl_b, qv, tn_dims) * scale, bdot(pl_b, dob, tn_dims), bdot(dsc_b, qv, tn_dims) * scale,
                        bdot(pc_b, dob, tn_dims), jnp.sum(-p_s * delta, axis=0, keepdims=True), ds_l))
        dq_ref[...] = jnp.concatenate([r[0] for r in res], axis=-1)
        for j, (_, dkl, dvl, dkc, dvc, dsk, ds_l) in enumerate(res):
            kv = slice((j // rep) * HEAD, (j // rep + 1) * HEAD)
            dk_ref[pl.ds(s0, W), kv] += dkl
            dv_ref[pl.ds(s0, W), kv] += dvl
            dk_ref[0:ctx, kv] += dkc
            dv_ref[0:ctx, kv] += dvc
            dsk_ref[j] += jnp.broadcast_to(dsk, (8, HEAD))
            if bias is not None:
                db_ref[j] += ds_l

    ins = [q, k, v, sink] + ([bias] if bias is not None else []) + [do]
    in_specs = specs + ([_bias_spec(ctx, seq)] if bias is not None else []) + [
        pl.BlockSpec((bq, HP * HEAD), lambda g, b: (b, do_head0 // HP + g))]
    out_specs = [qspec, kvspec, kvspec, pl.BlockSpec((HP, 8, HEAD), lambda g, b: (g, 0, 0))]
    out_shape = [jax.ShapeDtypeStruct((T, n_q * HEAD), F32), jax.ShapeDtypeStruct((t_pad, n_kv * HEAD), F32),
                 jax.ShapeDtypeStruct((t_pad, n_kv * HEAD), F32), jax.ShapeDtypeStruct((n_q, 8, HEAD), F32)]
    if bias is not None:
        out_specs.append(_bias_spec(ctx, seq))
        out_shape.append(jax.ShapeDtypeStruct(bias.shape, F32))
    res = pl.pallas_call(
        body, name=name, grid=(n_q // HP, T // bq), in_specs=in_specs, out_specs=out_specs, out_shape=out_shape,
        compiler_params=_cparams(("arbitrary", "arbitrary")), **_CALL_KW)(*ins)
    return res if bias is not None else list(res) + [None]


HALO = 8


def _halo_specs(width, col0, T, ctx_tiles):
    per = TOK // HALO
    main = pl.BlockSpec((TOK, width), lambda jc, i: (i, col0 + jc))
    prev = pl.BlockSpec((HALO, width), lambda jc, i: (jnp.maximum(i * per - 1, 0), col0 + jc))
    nxt = pl.BlockSpec((HALO, width), lambda jc, i: (jnp.minimum((i + 1) * per, T // HALO - 1), col0 + jc))
    return main, prev, nxt


def _with_halo(i, nt, ctx_tiles, prev, main, nxt):
    has_prev = (i != 0) & (i != ctx_tiles)
    has_next = (i != ctx_tiles - 1) & (i != nt - 1)
    return jnp.concatenate([jnp.where(has_prev, prev, 0.0), main, jnp.where(has_next, nxt, 0.0)], axis=0)


def _shifted(ext, s):
    n = ext.shape[0]
    return pltpu.roll(ext, (-s) % n, 0)[HALO:HALO + TOK]


def _conv_fwd(name, p, col0, conv_w, ctx_tiles):
    T = p.shape[0]
    nt = T // TOK
    ncol = 3
    Wc = conv_w.shape[1] // ncol
    pad = (DN_CONV - 1) // 2

    def body(m_ref, p_ref, n_ref, w_ref, o_ref):
        i = pl.program_id(1)
        ext = _with_halo(i, nt, ctx_tiles, p_ref[...].astype(F32), m_ref[...].astype(F32), n_ref[...].astype(F32))
        acc = jnp.zeros((TOK, Wc), F32)
        for j in range(DN_CONV):
            acc = acc + w_ref[j:j + 1, :] * _shifted(ext, j - pad)
        o_ref[...] = acc

    main, prev, nxt = _halo_specs(Wc, col0, T, ctx_tiles)
    return pl.pallas_call(
        body, name=name, grid=(ncol, nt), in_specs=[main, prev, nxt, pl.BlockSpec((DN_CONV, Wc), lambda jc, i: (0, jc))],
        out_specs=pl.BlockSpec((TOK, Wc), lambda jc, i: (i, jc)), out_shape=jax.ShapeDtypeStruct((T, ncol * Wc), F32),
        compiler_params=_cparams(("parallel", "parallel")), **_CALL_KW)(p, p, p, conv_w)


def _conv_bwd(name, p, col0, conv_w, dpre, ctx_tiles, dst):
    T = p.shape[0]
    nt = T // TOK
    ncol = 3
    Wc = conv_w.shape[1] // ncol
    pad = (DN_CONV - 1) // 2

    def body(m_ref, p_ref, n_ref, dm_ref, dp_ref, dn_ref, w_ref, dst_ref, dx_ref, dw_ref):
        i = pl.program_id(1)
        ext_x = _with_halo(i, nt, ctx_tiles, p_ref[...].astype(F32), m_ref[...].astype(F32), n_ref[...].astype(F32))
        ext_d = _with_halo(i, nt, ctx_tiles, dp_ref[...], dm_ref[...], dn_ref[...])
        dmain = dm_ref[...]

        @pl.when(i == 0)
        def _():
            dw_ref[...] = jnp.zeros_like(dw_ref)

        acc = jnp.zeros((TOK, Wc), F32)
        for j in range(DN_CONV):
            acc = acc + w_ref[j:j + 1, :] * _shifted(ext_d, pad - j)
            dw_ref[j:j + 1, :] += jnp.sum(dmain * _shifted(ext_x, j - pad), axis=0, keepdims=True)
        dx_ref[...] = acc.astype(BF16)

    main, prev, nxt = _halo_specs(Wc, col0, T, ctx_tiles)
    dmain, dprev, dnxt = _halo_specs(Wc, 0, T, ctx_tiles)
    return pl.pallas_call(
        body, name=name, grid=(ncol, nt),
        in_specs=[main, prev, nxt, dmain, dprev, dnxt, pl.BlockSpec((DN_CONV, Wc), lambda jc, i: (0, jc)),
                  pl.BlockSpec(memory_space=pl.ANY)],
        out_specs=[pl.BlockSpec((TOK, Wc), lambda jc, i: (i, col0 + jc)), pl.BlockSpec((8, Wc), lambda jc, i: (0, jc))],
        out_shape=[jax.ShapeDtypeStruct(dst.shape, dst.dtype), jax.ShapeDtypeStruct((8, ncol * Wc), F32)],
        input_output_aliases={7: 0},
        compiler_params=_cparams(("parallel", "arbitrary")), **_CALL_KW)(p, p, p, dpre, dpre, dpre, conv_w, dst)


def _softplus(x):
    return jnp.maximum(x, 0.0) + jnp.log(1.0 + jnp.exp(-jnp.abs(x)))


def _gdn_point(name, pre, dab, a_log, dt_bias, n_heads):
    T = pre.shape[0]
    Wd = n_heads * HEAD
    ng = 2 * n_heads

    def body(pre_ref, ab_ref, al_ref, dt_ref, q_ref, k_ref, v_ref, la_ref, be_ref):
        for h in range(n_heads):
            for part, ref in enumerate((q_ref, k_ref, v_ref)):
                xv = pre_ref[:, part * Wd + h * HEAD:part * Wd + (h + 1) * HEAD]
                s = xv * _sigmoid(xv)
                if part < 2:
                    s = s * lax.rsqrt(jnp.sum(s * s, axis=-1, keepdims=True) + EPS) * (HEAD ** -0.5 if part == 0 else 1.0)
                ref[:, h * HEAD:(h + 1) * HEAD] = s
        ab = ab_ref[...].astype(F32)
        lane = lax.broadcasted_iota(jnp.int32, ab.shape, 1)
        la_ref[...] = jnp.where(lane < ng, -jnp.exp(al_ref[...]) * _softplus(ab + dt_ref[...]), 0.0)
        be_ref[...] = jnp.where(lane < ng, _sigmoid(pltpu.roll(ab, HEAD - ng, 1)), 0.0)

    row = lambda w: pl.BlockSpec((TOK, w), lambda i: (i, 0))
    one = pl.BlockSpec((1, HEAD), lambda i: (0, 0))
    return pl.pallas_call(
        body, name=name, grid=(T // TOK,), in_specs=[row(3 * Wd), row(HEAD), one, one],
        out_specs=[row(Wd), row(Wd), row(Wd), row(HEAD), row(HEAD)],
        out_shape=[jax.ShapeDtypeStruct((T, Wd), F32)] * 3 + [jax.ShapeDtypeStruct((T, HEAD), F32)] * 2,
        compiler_params=_cparams(("parallel",)), **_CALL_KW)(pre, dab, a_log, dt_bias)


def _gdn_point_bwd(name, pre, dab, a_log, dt_bias, n_heads, dq, dk, dv, dla, dbe):
    T = pre.shape[0]
    Wd = n_heads * HEAD
    ng = 2 * n_heads

    def body(pre_ref, ab_ref, al_ref, dt_ref, dq_ref, dk_ref, dv_ref, dla_ref, dbe_ref, dpre_ref, dab_ref, dal_ref, ddt_ref):
        i = pl.program_id(0)

        @pl.when(i == 0)
        def _():
            dal_ref[...] = jnp.zeros_like(dal_ref)
            ddt_ref[...] = jnp.zeros_like(ddt_ref)

        for h in range(n_heads):
            for part, ref in enumerate((dq_ref, dk_ref, dv_ref)):
                cols = slice(part * Wd + h * HEAD, part * Wd + (h + 1) * HEAD)
                xv = pre_ref[:, cols]
                sg = _sigmoid(xv)
                s = xv * sg
                dy = ref[0, :, h * HEAD:(h + 1) * HEAD] + ref[1, :, h * HEAD:(h + 1) * HEAD]
                if part < 2:
                    c0 = HEAD ** -0.5 if part == 0 else 1.0
                    r = lax.rsqrt(jnp.sum(s * s, axis=-1, keepdims=True) + EPS)
                    ds = c0 * (r * dy - s * (r * r * r) * jnp.sum(dy * s, axis=-1, keepdims=True))
                else:
                    ds = dy
                dpre_ref[:, cols] = ds * (sg * (1.0 + xv * (1.0 - sg)))
        ab = ab_ref[...].astype(F32)
        lane = lax.broadcasted_iota(jnp.int32, ab.shape, 1)
        ea = jnp.exp(al_ref[...])
        z = ab + dt_ref[...]
        dlav = jnp.where(lane < ng, dla_ref[0] + dla_ref[1], 0.0)
        da = dlav * (-ea) * _sigmoid(z)
        dal_ref[...] += jnp.sum(dlav * (-ea) * _softplus(z), axis=0, keepdims=True)
        ddt_ref[...] += jnp.sum(da, axis=0, keepdims=True)
        be = _sigmoid(pltpu.roll(ab, HEAD - ng, 1))
        db = jnp.where(lane < ng, (dbe_ref[0] + dbe_ref[1]) * be * (1.0 - be), 0.0)
        dab_ref[...] = (da + pltpu.roll(db, ng, 1)).astype(BF16)

    row = lambda w: pl.BlockSpec((TOK, w), lambda i: (i, 0))
    two = lambda w: pl.BlockSpec((2, TOK, w), lambda i: (0, i, 0))
    one = pl.BlockSpec((1, HEAD), lambda i: (0, 0))
    return pl.pallas_call(
        body, name=name, grid=(T // TOK,),
        in_specs=[row(3 * Wd), row(HEAD), one, one, two(Wd), two(Wd), two(Wd), two(HEAD), two(HEAD)],
        out_specs=[row(3 * Wd), row(HEAD), one, one],
        out_shape=[jax.ShapeDtypeStruct((T, 3 * Wd), F32), jax.ShapeDtypeStruct((T, HEAD), BF16),
                   jax.ShapeDtypeStruct((1, HEAD), F32), jax.ShapeDtypeStruct((1, HEAD), F32)],
        compiler_params=_cparams(("arbitrary",)), **_CALL_KW)(pre, dab, a_log, dt_bias, dq, dk, dv, dla, dbe)


_NN, _NT, _TN = "nn", "nt", "tn"
_DIMS = {"nn": (((1,), (0,)), ((), ())), "nt": (((1,), (1,)), ((), ())), "tn": (((0,), (0,)), ((), ()))}
_BDIMS = {"nn": (((2,), (1,)), ((0,), (0,))), "nt": (((2,), (2,)), ((0,), (0,))), "tn": (((1,), (1,)), ((0,), (0,)))}


def _dims(a, kind):
    return _BDIMS[kind] if a.ndim == 3 else _DIMS[kind]


def _mm3_raw(a, b, kind=_NN):
    ah, bh = a.astype(BF16), b.astype(BF16)
    al, bl = (a - ah.astype(F32)).astype(BF16), (b - bh.astype(F32)).astype(BF16)
    d = functools.partial(lax.dot_general, dimension_numbers=_dims(a, kind), preferred_element_type=F32)
    return d(ah, bh) + (d(ah, bl) + d(al, bh))


@jax.custom_vjp
def _mm3(a, b):
    return _mm3_raw(a, b)


def _mm3_fwd(a, b):
    return _mm3_raw(a, b), (a, b)


def _mm3_bwd(res, g):
    a, b = res
    return _mm3_raw(g, b, _NT), _mm3_raw(a, g, _TN)


_mm3.defvjp(_mm3_fwd, _mm3_bwd)


def _bdot_raw(a, b, kind):
    return lax.dot_general(a.astype(BF16), b.astype(BF16), _dims(a, kind), preferred_element_type=F32)


@functools.partial(jax.custom_vjp, nondiff_argnums=(2,))
def _bdot(a, b, kind=_NN):
    return _bdot_raw(a, b, kind)


def _bdot_fwd(a, b, kind):
    return _bdot_raw(a, b, kind), (a, b)


def _bdot_bwd(kind, res, g):
    a, b = res
    if kind == "nn":
        return _bdot_raw(g, b, "nt"), _bdot_raw(a, g, "tn")
    if kind == "nt":
        return _bdot_raw(g, b, "nn"), _bdot_raw(g, a, "tn")
    return _bdot_raw(b, g, "nt"), _bdot_raw(a, g, "nn")


_bdot.defvjp(_bdot_fwd, _bdot_bwd)


def _chunk_masks(rev):
    C = DN_CHUNK
    ii = lax.broadcasted_iota(jnp.int32, (C, C), 0)
    jj = lax.broadcasted_iota(jnp.int32, (C, C), 1)
    diff = jnp.where(rev, jj - ii, ii - jj)
    incl = diff >= 0
    strict = diff > 0
    rowsel = (lax.broadcasted_iota(jnp.int32, (C, 1), 0) == jnp.where(rev, 0, C - 1)).astype(F32)
    return incl, strict, rowsel, (ii == jj).astype(F32)


def _head_stack(ref, n_heads, rows=slice(None)):
    return jnp.stack([ref[rows, h * HEAD:(h + 1) * HEAD] for h in range(n_heads)])


def _cat(parts):
    return jnp.concatenate(parts, axis=0)


def _gate_views(g, gt, be, d, n_heads):
    lane = lax.broadcasted_iota(jnp.int32, (1, HEAD), 1)
    sub = lax.broadcasted_iota(jnp.int32, (HEAD, 1), 0)
    sels = [(lane == d * n_heads + h).astype(F32) for h in range(n_heads)]
    selts = [(sub == d * n_heads + h).astype(F32) for h in range(n_heads)]
    g_col = jnp.stack([jnp.sum(g * s, axis=1, keepdims=True) for s in sels])
    b_col = jnp.stack([jnp.sum(be * s, axis=1, keepdims=True) for s in sels])
    g_row = jnp.stack([jnp.sum(gt * s, axis=0, keepdims=True) for s in selts])
    return g_col, g_row, b_col, sels, selts


def _chunk_decay(g_col, g_row, incl):
    return jnp.where(incl, jnp.exp(jnp.where(incl, g_col - g_row, 0.0)), 0.0)


def _chunk_lower(k, g_col, g_row, b_col, incl, strict):
    return jnp.where(strict, _bdot(k * b_col, k, _NT) * _chunk_decay(g_col, g_row, incl), 0.0)


def _chunk_inverse(low, eye):
    m = -low
    x = eye + m
    p = m
    for _ in range(int(math.log2(DN_CHUNK)) - 1):
        p = _mm3(p, p)
        x = x + _mm3(x, p)
    return x


def _chunk_step(q, k, v, g_col, g_row, b_col, S, X, incl, rowsel):
    decay = _chunk_decay(g_col, g_row, incl)
    eg = jnp.exp(g_col)
    u = _mm3(X, v * b_col)
    w = _mm3(X, k * (b_col * eg))
    intra = _bdot(q, k, _NT) * decay
    g_last = jnp.sum(g_col * rowsel, axis=1, keepdims=True)
    v_new = u - _bdot(w, S)
    o = _bdot(q * eg, S) + _bdot(intra, v_new)
    S_new = S * jnp.exp(g_last) + _bdot(k * jnp.exp(g_last - g_col), v_new, _TN)
    return o, S_new


def _scan_index(ctx_chunks, n_chunks):
    def idx(d, n):
        return jnp.where(d == 0, n, jnp.where(n < ctx_chunks, ctx_chunks - 1 - n, n_chunks + ctx_chunks - 1 - n))
    return idx


def _cumsum_mats(rev):
    C = DN_CHUNK
    ii = lax.broadcasted_iota(jnp.int32, (C, C), 0)
    jj = lax.broadcasted_iota(jnp.int32, (C, C), 1)
    return jnp.where(jnp.where(rev, jj - ii, ii - jj) >= 0, 1.0, 0.0).astype(F32)


def _hosted(comm, n_in, n_out, refs):
    n_ci, n_co = (len(comm.ins), len(comm.out_shapes)) if comm is not None else (0, 0)
    ins, cin = refs[:n_in], refs[n_in:n_in + n_ci]
    outs, cout = refs[n_in + n_ci:n_in + n_ci + n_out], refs[n_in + n_ci + n_out:n_in + n_ci + n_out + n_co]
    rest = refs[n_in + n_ci + n_out + n_co:]
    n_sem = 3 if comm is not None else 0
    return ins, outs, rest[:len(rest) - n_sem], (cin, cout, rest[len(rest) - n_sem:])


def _hosted_start(comm, cref, first):
    if comm is not None:
        @pl.when(first)
        def _():
            for cp in comm.copies(*cref):
                cp.start()


def _hosted_wait(comm, cref, last):
    if comm is not None:
        @pl.when(last)
        def _():
            for cp in comm.copies(*cref):
                cp.wait()


def _gdn_scan(name, q, k, v, la, be, n_heads, ctx, comm=None):
    T, Wd = q.shape
    C = DN_CHUNK
    nch = T // C
    npair = nch // 2
    assert nch % 2 == 0 and (ctx // C) % 2 == 0
    pidx = _scan_index(ctx // C // 2, npair)
    hi = lax.Precision.HIGHEST

    def body(*refs):
        (q_ref, k_ref, v_ref, la_ref, be_ref), (o_ref, s_ref, x_ref), (state,), cref = _hosted(comm, 5, 3, refs)
        d, n = pl.program_id(0), pl.program_id(1)
        rev = d == 1
        _hosted_start(comm, cref, (d == 0) & (n == 0))

        @pl.when(n == 0)
        def _():
            state[...] = jnp.zeros_like(state)

        incl, strict, rowsel, eye = _chunk_masks(rev)
        tri = _cumsum_mats(rev)
        offs = [pl.multiple_of(jnp.where(rev, C, 0), C), pl.multiple_of(jnp.where(rev, 0, C), C)]
        views, qkv = [], []
        for off in offs:
            rows = pl.ds(off, C)
            g = jnp.dot(tri, la_ref[rows, :], precision=hi, preferred_element_type=F32)
            views.append(_gate_views(g, g.T, be_ref[rows, :], d, n_heads)[:3])
            qkv.append(tuple(_head_stack(r, n_heads, rows) for r in (q_ref, k_ref, v_ref)))
        low = _chunk_lower(_cat([qkv[0][1], qkv[1][1]]), *[_cat([views[0][j], views[1][j]]) for j in range(3)], incl, strict)
        X = _chunk_inverse(low, eye)
        S = state[...]
        for i, off in enumerate(offs):
            Xi = X[i * n_heads:(i + 1) * n_heads]
            s_ref[i] = S
            x_ref[i] = Xi
            o, S = _chunk_step(*qkv[i], *views[i], S, Xi, incl, rowsel)
            for h in range(n_heads):
                o_ref[pl.ds(off, C), h * HEAD:(h + 1) * HEAD] = o[h]
        state[...] = S
        _hosted_wait(comm, cref, (d == 1) & (n == npair - 1))

    tok = lambda w: pl.BlockSpec((2 * C, w), lambda d, n: (pidx(d, n), 0))
    any_spec = pl.BlockSpec(memory_space=pl.ANY)
    c_ins, c_sds, c_sems = (comm.ins, comm.out_sds(), comm.sem_shapes()) if comm is not None else ([], [], [])
    res = pl.pallas_call(
        body, name=name, grid=(2, npair), in_specs=[tok(Wd), tok(Wd), tok(Wd), tok(HEAD), tok(HEAD)] + [any_spec] * len(c_ins),
        out_specs=[pl.BlockSpec((None, 2 * C, Wd), lambda d, n: (d, pidx(d, n), 0)),
                   pl.BlockSpec((None, 2, n_heads, HEAD, HEAD), lambda d, n: (d, n, 0, 0, 0)),
                   pl.BlockSpec((None, 2, n_heads, C, C), lambda d, n: (d, n, 0, 0, 0))] + [any_spec] * len(c_sds),
        out_shape=[jax.ShapeDtypeStruct((2, T, Wd), F32), jax.ShapeDtypeStruct((2, nch, n_heads, HEAD, HEAD), F32),
                   jax.ShapeDtypeStruct((2, nch, n_heads, C, C), F32)] + c_sds,
        scratch_shapes=[pltpu.VMEM((n_heads, HEAD, HEAD), F32)] + c_sems,
        compiler_params=_cparams(("arbitrary", "arbitrary"), has_side_effects=comm is not None), **_CALL_KW)(q, k, v, la, be, *c_ins)
    return res[0], res[1], res[2], list(res[3:])


def _gdn_scan_bwd(name, q, k, v, la, be, states, invs, do, n_heads, ctx, comm=None):
    T, Wd = q.shape
    C = DN_CHUNK
    nch = T // C
    npair = nch // 2
    pidx = _scan_index(ctx // C // 2, npair)
    hi = lax.Precision.HIGHEST

    def body(*refs):
        ins, outs, (dstate,), cref = _hosted(comm, 8, 5, refs)
        q_ref, k_ref, v_ref, la_ref, be_ref, s_ref, x_ref, do_ref = ins
        dq_ref, dk_ref, dv_ref, dla_ref, dbe_ref = outs
        d, n = pl.program_id(0), pl.program_id(1)
        rev = d == 1
        _hosted_start(comm, cref, (d == 0) & (n == 0))

        @pl.when(n == 0)
        def _():
            dstate[...] = jnp.zeros_like(dstate)

        incl, strict, rowsel, eye = _chunk_masks(rev)
        tri = _cumsum_mats(rev)
        offs = [pl.multiple_of(jnp.where(rev, C, 0), C), pl.multiple_of(jnp.where(rev, 0, C), C)]
        views, qkv, dos = [], [], []
        for off in offs:
            rows = pl.ds(off, C)
            g = jnp.dot(tri, la_ref[rows, :], precision=hi, preferred_element_type=F32)
            g_col, g_row, b_col, sels, selts = _gate_views(g, g.T, be_ref[rows, :], d, n_heads)
            views.append((g_col, g_row, b_col))
            qkv.append(tuple(_head_stack(r, n_heads, rows) for r in (q_ref, k_ref, v_ref)))
            dos.append(_head_stack(do_ref, n_heads, rows))
        step = functools.partial(_chunk_step, incl=incl, rowsel=rowsel)
        dS = dstate[...]
        part = [None, None]
        for i in (1, 0):
            _, vjp_step = jax.vjp(step, *qkv[i], *views[i], s_ref[i], x_ref[i])
            *part[i], dS, dX = vjp_step((dos[i], dS))
            part[i].append(dX)
        dstate[...] = dS
        X, dX = _cat([x_ref[0], x_ref[1]]), _cat([part[0][6], part[1][6]])
        dlow = -_mm3_raw(_mm3_raw(X, dX, _TN), X, _NT)
        low_fn = functools.partial(_chunk_lower, incl=incl, strict=strict)
        _, vjp_low = jax.vjp(low_fn, _cat([qkv[0][1], qkv[1][1]]), *[_cat([views[0][j], views[1][j]]) for j in range(3)])
        dk2, dgc2, dgr2, dbc2 = vjp_low(dlow)
        for i, off in enumerate(offs):
            rows = pl.ds(off, C)
            sl = slice(i * n_heads, (i + 1) * n_heads)
            dq, dk1, dv_, dgc1, dgr1, dbc1, _ = part[i]
            dk, dgc, dgr, dbc = dk1 + dk2[sl], dgc1 + dgc2[sl], dgr1 + dgr2[sl], dbc1 + dbc2[sl]
            dg = jnp.zeros((C, HEAD), F32)
            dgt = jnp.zeros((HEAD, C), F32)
            dbe = jnp.zeros((C, HEAD), F32)
            for h in range(n_heads):
                cols = slice(h * HEAD, (h + 1) * HEAD)
                dq_ref[rows, cols], dk_ref[rows, cols], dv_ref[rows, cols] = dq[h], dk[h], dv_[h]
                dg = dg + dgc[h] * sels[h]
                dgt = dgt + selts[h] * dgr[h]
                dbe = dbe + dbc[h] * sels[h]
            dla_ref[rows, :] = lax.dot_general(tri, dg + dgt.T, _DIMS["tn"], precision=hi, preferred_element_type=F32)
            dbe_ref[rows, :] = dbe
        _hosted_wait(comm, cref, (d == 1) & (n == npair - 1))

    rn = lambda d, n: pidx(d, npair - 1 - n)
    tok = lambda w: pl.BlockSpec((2 * C, w), lambda d, n: (rn(d, n), 0))
    otok = lambda w: pl.BlockSpec((None, 2 * C, w), lambda d, n: (d, rn(d, n), 0))
    any_spec = pl.BlockSpec(memory_space=pl.ANY)
    c_ins, c_sds, c_sems = (comm.ins, comm.out_sds(), comm.sem_shapes()) if comm is not None else ([], [], [])
    res = pl.pallas_call(
        body, name=name, grid=(2, npair),
        in_specs=[tok(Wd), tok(Wd), tok(Wd), tok(HEAD), tok(HEAD),
                  pl.BlockSpec((None, 2, n_heads, HEAD, HEAD), lambda d, n: (d, npair - 1 - n, 0, 0, 0)),
                  pl.BlockSpec((None, 2, n_heads, C, C), lambda d, n: (d, npair - 1 - n, 0, 0, 0)), tok(Wd)] + [any_spec] * len(c_ins),
        out_specs=[otok(Wd), otok(Wd), otok(Wd), otok(HEAD), otok(HEAD)] + [any_spec] * len(c_sds),
        out_shape=[jax.ShapeDtypeStruct((2, T, Wd), F32)] * 3 + [jax.ShapeDtypeStruct((2, T, HEAD), F32)] * 2 + c_sds,
        scratch_shapes=[pltpu.VMEM((n_heads, HEAD, HEAD), F32)] + c_sems,
        compiler_params=_cparams(("arbitrary", "arbitrary"), has_side_effects=comm is not None), **_CALL_KW)(
            q, k, v, la, be, states, invs, do, *c_ins)
    return tuple(res[:5]) + (list(res[5:]),)


def _gated_norm(name, o2, p, zblk, g, n_heads, width):
    _, T, Wd = o2.shape

    def body(o_ref, z_ref, g_ref, y_ref):
        for h in range(n_heads):
            cols = slice(h * HEAD, (h + 1) * HEAD)
            ov = o_ref[0, :, cols] + o_ref[1, :, cols]
            zv = z_ref[:, cols].astype(F32)
            y = ov * lax.rsqrt(jnp.mean(ov * ov, axis=-1, keepdims=True) + EPS) * g_ref[...]
            y_ref[:, cols] = (y * (zv * _sigmoid(zv))).astype(BF16)

    return pl.pallas_call(
        body, name=name, grid=(T // TOK,),
        in_specs=[pl.BlockSpec((2, TOK, Wd), lambda i: (0, i, 0)), pl.BlockSpec((TOK, Wd), lambda i: (i, zblk)),
                  pl.BlockSpec((1, HEAD), lambda i: (0, 0))],
        out_specs=pl.BlockSpec((TOK, Wd), lambda i: (i, 0)), out_shape=jax.ShapeDtypeStruct((T, width), BF16),
        compiler_params=_cparams(("parallel",)), **_CALL_KW)(o2, p, g)


def _gated_norm_bwd(name, o2, p, zblk, g, n_heads, dmix, dblk):
    _, T, Wd = o2.shape

    def body(o_ref, z_ref, g_ref, dy_ref, do_ref, dz_ref, dg_ref):
        i = pl.program_id(0)

        @pl.when(i == 0)
        def _():
            dg_ref[...] = jnp.zeros_like(dg_ref)

        dg = jnp.zeros((1, HEAD), F32)
        for h in range(n_heads):
            cols = slice(h * HEAD, (h + 1) * HEAD)
            ov = o_ref[0, :, cols] + o_ref[1, :, cols]
            zv = z_ref[:, cols].astype(F32)
            dy = dy_ref[:, cols].astype(F32)
            r = lax.rsqrt(jnp.mean(ov * ov, axis=-1, keepdims=True) + EPS)
            on = ov * r
            sg = _sigmoid(zv)
            sz = zv * sg
            dz_ref[:, cols] = (dy * (on * g_ref[...]) * (sg * (1.0 + zv * (1.0 - sg)))).astype(BF16)
            dyn = dy * sz
            dg = dg + jnp.sum(dyn * on, axis=0, keepdims=True)
            u = dyn * g_ref[...]
            do_ref[:, cols] = r * (u - on * jnp.mean(u * on, axis=-1, keepdims=True))
        dg_ref[...] += dg

    row = pl.BlockSpec((TOK, Wd), lambda i: (i, 0))
    one = pl.BlockSpec((1, HEAD), lambda i: (0, 0))
    return pl.pallas_call(
        body, name=name, grid=(T // TOK,),
        in_specs=[pl.BlockSpec((2, TOK, Wd), lambda i: (0, i, 0)), pl.BlockSpec((TOK, Wd), lambda i: (i, zblk)), one,
                  pl.BlockSpec((TOK, Wd), lambda i: (i, dblk))],
        out_specs=[row, pl.BlockSpec((TOK, Wd), lambda i: (i, zblk)), one],
        out_shape=[jax.ShapeDtypeStruct((T, Wd), F32), jax.ShapeDtypeStruct(p.shape, BF16), jax.ShapeDtypeStruct((1, HEAD), F32)],
        compiler_params=_cparams(("arbitrary",)), **_CALL_KW)(o2, p, g, dmix)


class _Dims:
    def __init__(self, D, seq, ctx, ffn):
        self.D, self.seq, self.ctx, self.ffn = D, seq, ctx, ffn
        self.T = seq + ctx
        self.t_pad = -(-(self.T + 128) // TOK) * TOK
        self.ctx_tiles = ctx // TOK
        nh = D // HEAD
        self.swa_h, self.kv_h, self.dn_h = nh // 4, nh // 8, nh // 2
        self.na_h = nh - self.swa_h - self.dn_h
        self.swa_q, self.swa_kv, self.Wd, self.na = self.swa_h * HEAD, self.kv_h * HEAD, self.dn_h * HEAD, self.na_h * HEAD
        self.n_ab = 4 * self.dn_h
        self.o_ab = self.swa_q + 2 * self.swa_kv + 4 * self.Wd
        self.n_in = self.o_ab + self.n_ab + 3 * self.na
        self.n_main = self.n_in - self.n_ab
        assert ctx % TOK == 0 and seq % TOK == 0 and self.swa_q == 2 * self.swa_kv == self.na and 2 * self.na == self.Wd


def _rope_tables(dm):
    t = jnp.arange(dm.t_pad, dtype=jnp.int32) - dm.ctx
    lat = (t >= 0) & (t < dm.seq)
    row = (t // GRID_W).astype(F32)
    col = (t % GRID_W).astype(F32)
    n_freq = HEAD // 4
    inv = ROPE_THETA ** (-jnp.arange(n_freq, dtype=F32) / n_freq)
    ang = jnp.concatenate([row[:, None] * inv, row[:, None] * inv, col[:, None] * inv, col[:, None] * inv], axis=-1)
    ang = jnp.where(lat[:, None], ang, 0.0)
    return jnp.cos(ang), jnp.sin(ang)


def _bias_indices():
    o = np.arange(NA_KH)[:, None]
    jr = np.arange(NA_KH)[None, :]
    idx_r = jr - o + (NA_KH - 1)
    cols = np.arange(GRID_W)
    idx_c = np.clip(cols[None, :] - cols[:, None], -(NA_KW - 1), NA_KW - 1) + (NA_KW - 1)
    return idx_r, idx_c


def _bias_onehots():
    idx_r, idx_c = _bias_indices()
    sel_r = (idx_r.reshape(-1)[:, None] == np.arange(2 * NA_KH)[None, :]).astype(np.float32)
    sel_c = (np.arange(HEAD)[:, None] == idx_c.reshape(-1)[None, :]).astype(np.float32)
    return jnp.asarray(sel_r), jnp.asarray(sel_c)


def _rpb_pad(rpb):
    return jnp.pad(rpb, ((0, 0), (0, 2 * NA_KH - rpb.shape[1]), (0, HEAD - rpb.shape[2])))


def _bias_table(name, rpb):
    H = rpb.shape[0]
    sel_r, sel_c = _bias_onehots()
    hi = lax.Precision.HIGHEST

    def body(r_ref, sr_ref, sc_ref, o_ref):
        t = jnp.dot(sr_ref[...], r_ref[...], precision=hi, preferred_element_type=F32)
        o_ref[...] = jnp.dot(t, sc_ref[...], precision=hi, preferred_element_type=F32)

    n_r, n_c = sel_r.shape[0], sel_c.shape[1]
    tab = pl.pallas_call(
        body, name=name, grid=(H,),
        in_specs=[pl.BlockSpec((None, 2 * NA_KH, HEAD), lambda h: (h, 0, 0)), pl.BlockSpec(sel_r.shape, lambda h: (0, 0)),
                  pl.BlockSpec(sel_c.shape, lambda h: (0, 0))],
        out_specs=pl.BlockSpec((None, n_r, n_c), lambda h: (h, 0, 0)), out_shape=jax.ShapeDtypeStruct((H, n_r, n_c), F32),
        compiler_params=_cparams(("parallel",)), **_CALL_KW)(_rpb_pad(rpb), sel_r, sel_c)
    tab = tab.reshape(H, NA_KH, NA_KH, GRID_W, GRID_W).transpose(0, 1, 3, 2, 4)
    return tab.reshape(H, NA_KH, GRID_W, NA_KH * GRID_W)


def _bias_table_bwd(name, dbias, rpb_shape):
    H = dbias.shape[0]
    sel_r, sel_c = _bias_onehots()
    hi = lax.Precision.HIGHEST
    d = dbias.reshape(H, NA_KH, GRID_W, NA_KH, GRID_W).transpose(0, 1, 3, 2, 4).reshape(H, NA_KH * NA_KH, GRID_W * GRID_W)

    def body(d_ref, sr_ref, sc_ref, o_ref):
        dt = lax.dot_general(d_ref[...], sc_ref[...], _DIMS["nt"], precision=hi, preferred_element_type=F32)
        o_ref[...] = lax.dot_general(sr_ref[...], dt, _DIMS["tn"], precision=hi, preferred_element_type=F32)

    out = pl.pallas_call(
        body, name=name, grid=(H,),
        in_specs=[pl.BlockSpec((None,) + d.shape[1:], lambda h: (h, 0, 0)), pl.BlockSpec(sel_r.shape, lambda h: (0, 0)),
                  pl.BlockSpec(sel_c.shape, lambda h: (0, 0))],
        out_specs=pl.BlockSpec((None, 2 * NA_KH, HEAD), lambda h: (h, 0, 0)),
        out_shape=jax.ShapeDtypeStruct((H, 2 * NA_KH, HEAD), F32),
        compiler_params=_cparams(("parallel",)), **_CALL_KW)(d, sel_r, sel_c)
    return out[:, :rpb_shape[1], :rpb_shape[2]]


def _lane_row(v):
    v = v.reshape(-1)
    return jnp.pad(v, (0, HEAD - v.shape[0])).reshape(1, HEAD)


def _chunks_of(l, chip, gathered, own):
    return [jnp.where(chip == k, own[l], gathered[k]) for k in range(N_CHIPS)]


def _weights_in(dm, l, chip, g_in, own_in):
    w_in = jnp.concatenate(_chunks_of(l, chip, g_in, own_in), axis=1)
    w_main = jnp.concatenate([w_in[:, :dm.o_ab], w_in[:, dm.o_ab + dm.n_ab:]], axis=1)
    w_ab = jnp.pad(w_in[:, dm.o_ab:dm.o_ab + dm.n_ab], ((0, 0), (0, HEAD - dm.n_ab)))
    return dict(main=w_main, ab=w_ab)


def _weights_rest(dm, l, chip, gathered, own):
    g_out, g_gate, g_up, g_down = [_chunks_of(l, chip, g, o) for g, o in zip(gathered, own)]
    w_out = jnp.stack(g_out).reshape(dm.D, dm.D)
    w_out = jnp.concatenate([w_out[dm.swa_q:dm.swa_q + dm.Wd], w_out[:dm.swa_q], w_out[dm.swa_q + dm.Wd:]], axis=0)
    return dict(out=w_out, gu=jnp.concatenate(g_gate + g_up, axis=1), down=jnp.stack(g_down).reshape(dm.ffn, dm.D))


def _layer_fwd(dm, x, W, sp, modv, cos, sin, hosts=None, late_weights=None):
    ct = dm.ctx_tiles
    hosts = hosts or {}
    got = {}
    h = _norm_mod("norm1", x, sp["norm1_g"], modv, 0, ct)
    P = _matmul("in_proj", h, W["main"], "nn")
    Pab = _matmul("in_proj_ab", h, W["ab"], "nn", tn=HEAD)
    one = jnp.ones((1, HEAD), F32)
    qa = _head_prep("swa_q_prep", P, 0, dm.swa_h, sp["swa_q_g"], cos, sin, dm.t_pad, True, True)
    ka = _head_prep("swa_k_prep", P, 2, dm.kv_h, sp["swa_k_g"], cos, sin, dm.t_pad, True, True)
    va = _head_prep("swa_v_prep", P, 3, dm.kv_h, one, cos, sin, dm.t_pad, False, False)
    qn = _head_prep("na_q_prep", P, 10, dm.na_h, sp["na_q_g"], cos, sin, dm.t_pad, True, False)
    kn = _head_prep("na_k_prep", P, 11, dm.na_h, sp["na_k_g"], cos, sin, dm.t_pad, True, False)
    vn = _head_prep("na_v_prep", P, 12, dm.na_h, one, cos, sin, dm.t_pad, False, False)
    no_sink = jnp.full((dm.na_h,), NEG, F32)
    bias = _bias_table("na_bias", sp["na_rpb"])
    pre = _conv_fwd("dn_conv", P, 1, sp["dn_conv_w"], ct)
    a_row, dt_row = _lane_row(sp["dn_A_log"]), _lane_row(sp["dn_dt_bias"])
    qh, kh, vh, la, be = _gdn_point("dn_point", pre, Pab, a_row, dt_row, dm.dn_h)
    o2, states, invs, got["scan"] = _gdn_scan("dn_scan", qh, kh, vh, la, be, dm.dn_h, dm.ctx, hosts.get("scan"))
    if late_weights is not None:
        W = dict(W, **late_weights(got["scan"]))
    mix = _gated_norm("dn_out_norm", o2, P, 4, sp["dn_out_g"], dm.dn_h, dm.D)
    mix, _ = _attn_fwd("swa_fwd", "swa", qa, ka, va, sp["swa_sink"], None, dm.ctx, dm.seq, mix, dm.Wd // HEAD)
    mix, got["na"] = _attn_fwd("na_fwd", "na", qn, kn, vn, no_sink, bias, dm.ctx, dm.seq, mix, (dm.Wd + dm.swa_q) // HEAD,
                               hosts.get("na"))
    ao = _matmul("out_proj", mix, W["out"], "nn")
    x1 = _resid_gate("resid1", x, ao, modv, 2, ct)
    h2 = _norm_mod("norm2", x1, sp["norm2_g"], modv, 3, ct)
    gu = _matmul("ffn_gate_up", h2, W["gu"], "nn", comm=hosts.get("gu"))
    if hosts.get("gu") is not None:
        gu, got["gu"] = gu
    act = _swiglu("ffn_act", gu)
    fo = _matmul("ffn_down", act, W["down"], "nn", comm=hosts.get("down"))
    if hosts.get("down") is not None:
        fo, got["down"] = fo
    x2 = _resid_gate("resid2", x1, fo, modv, 5, ct)
    res = dict(x=x, h=h, P=P, Pab=Pab, qa=qa, ka=ka, va=va, qn=qn, kn=kn, vn=vn, bias=bias, no_sink=no_sink, pre=pre,
               a_row=a_row, dt_row=dt_row, qh=qh, kh=kh, vh=vh, la=la, be=be, o2=o2, states=states, invs=invs, mix=mix,
               ao=ao, x1=x1, h2=h2, gu=gu, act=act, fo=fo)
    return x2, res, got, W


def _layer_bwd(dm, dx2, W, sp, modv, cos, sin, r, host=None):
    ct = dm.ctx_tiles
    T, D = dm.T, dm.D
    one = jnp.ones((1, HEAD), F32)
    dfo, dgate2 = _resid_gate_bwd("resid2_bwd", dx2, r["fo"], modv, 5, ct)
    dact = _matmul("ffn_down_dx", dfo, W["down"], "nt")
    dw_down = _matmul("ffn_down_dw", r["act"], dfo, "tn")
    dgu = _swiglu_bwd("ffn_act_bwd", r["gu"], dact)
    dh2 = _matmul("ffn_gate_up_dx", dgu, W["gu"], "nt")
    dw_gu = _matmul("ffn_gate_up_dw", r["h2"], dgu, "tn")
    zero = jnp.zeros((T, D), F32)
    dx1, dn2g, dsh2, dsc2 = _norm_mod_bwd("norm2_bwd", r["x1"], sp["norm2_g"], modv, 3, dh2, zero, dx2, ct)
    dao, dgate1 = _resid_gate_bwd("resid1_bwd", dx1, r["ao"], modv, 2, ct)
    dmix = _matmul("out_proj_dx", dao, W["out"], "nt")
    dw_out = _matmul("out_proj_dw", r["mix"], dao, "tn")
    do_, dP, d_out_g = _gated_norm_bwd("dn_out_norm_bwd", r["o2"], r["P"], 4, sp["dn_out_g"], dm.dn_h, dmix, 0)
    comm = host(dict(out=dw_out, gu=dw_gu, down=dw_down)) if host is not None else None
    dq2, dk2, dv2, dla2, dbe2, hosted = _gdn_scan_bwd("dn_scan_bwd", r["qh"], r["kh"], r["vh"], r["la"], r["be"], r["states"],
                                                      r["invs"], do_, dm.dn_h, dm.ctx, comm)
    dpre, dPab, d_alog, d_dtb = _gdn_point_bwd("dn_point_bwd", r["pre"], r["Pab"], r["a_row"], r["dt_row"], dm.dn_h,
                                               dq2, dk2, dv2, dla2, dbe2)
    dP, d_conv = _conv_bwd("dn_conv_bwd", r["P"], 1, sp["dn_conv_w"], dpre, ct, dP)
    dqa, dka, dva, dsink, _ = _attn_bwd("swa_bwd", "swa", r["qa"], r["ka"], r["va"], sp["swa_sink"], None, dmix,
                                        dm.Wd // HEAD, dm.ctx, dm.seq)
    dP, d_swa_q_g = _head_prep_bwd("swa_q_prep_bwd", r["P"], 0, dm.swa_h, sp["swa_q_g"], cos, sin, dqa, True, True, dP)
    dP, d_swa_k_g = _head_prep_bwd("swa_k_prep_bwd", r["P"], 2, dm.kv_h, sp["swa_k_g"], cos, sin, dka, True, True, dP)
    dP, _ = _head_prep_bwd("swa_v_prep_bwd", r["P"], 3, dm.kv_h, one, cos, sin, dva, False, False, dP)
    dqn, dkn, dvn, _, dbias = _attn_bwd("na_bwd", "na", r["qn"], r["kn"], r["vn"], r["no_sink"], r["bias"], dmix,
                                        (dm.Wd + dm.swa_q) // HEAD, dm.ctx, dm.seq)
    dP, d_na_q_g = _head_prep_bwd("na_q_prep_bwd", r["P"], 10, dm.na_h, sp["na_q_g"], cos, sin, dqn, True, False, dP)
    dP, d_na_k_g = _head_prep_bwd("na_k_prep_bwd", r["P"], 11, dm.na_h, sp["na_k_g"], cos, sin, dkn, True, False, dP)
    dP, _ = _head_prep_bwd("na_v_prep_bwd", r["P"], 12, dm.na_h, one, cos, sin, dvn, False, False, dP)
    d_rpb = _bias_table_bwd("na_bias_bwd", dbias, sp["na_rpb"].shape)
    dw_main = _matmul("in_proj_dw", r["h"], dP, "tn")
    dw_ab = _matmul("in_proj_ab_dw", r["h"], dPab, "tn", tn=HEAD)
    dh = _matmul("in_proj_dx", dP, W["main"], "nt")
    dh_b = _matmul("in_proj_ab_dx", dPab, W["ab"], "nt")
    dx, dn1g, dsh1, dsc1 = _norm_mod_bwd("norm1_bwd", r["x"], sp["norm1_g"], modv, 0, dh, dh_b, dx1, ct)
    dmodv = jnp.concatenate([dsh1, dsc1, dgate1, dsh2, dsc2, dgate2], axis=1)
    big = dict(main=dw_main, ab=dw_ab, out=dw_out, gu=dw_gu, down=dw_down)
    small = dict(norm1_g=dn1g[0], norm2_g=dn2g[0], swa_q_g=d_swa_q_g[0], swa_k_g=d_swa_k_g[0], swa_sink=dsink[:, 0, 0],
                 dn_conv_w=d_conv[:DN_CONV], dn_A_log=d_alog[0, :2 * dm.dn_h].reshape(2, dm.dn_h),
                 dn_dt_bias=d_dtb[0, :2 * dm.dn_h].reshape(2, dm.dn_h), dn_out_g=d_out_g[0], na_q_g=d_na_q_g[0],
                 na_k_g=d_na_k_g[0], na_rpb=d_rpb)
    return dx, big, small, dmodv, hosted


def _cols(w):
    return w.reshape(w.shape[0], N_CHIPS, -1).transpose(1, 0, 2)


def _rows(w):
    return w.reshape(N_CHIPS, -1, w.shape[1])


def _chunks_in(dm, bigs):
    g = [_cols(jnp.concatenate([b["main"][:, :dm.o_ab], b["ab"][:, :dm.n_ab], b["main"][:, dm.o_ab:]], axis=1)) for b in bigs]
    return jnp.stack(g, axis=1).astype(BF16)


def _chunks_rest(dm, bigs):
    g_out = [_rows(jnp.concatenate([b["out"][dm.Wd:dm.Wd + dm.swa_q], b["out"][:dm.Wd], b["out"][dm.Wd + dm.swa_q:]], axis=0))
             for b in bigs]
    g_gate = [_cols(b["gu"][:, :dm.ffn]) for b in bigs]
    g_up = [_cols(b["gu"][:, dm.ffn:]) for b in bigs]
    g_down = [_rows(b["down"]) for b in bigs]
    return [jnp.stack(g, axis=1).astype(BF16) for g in (g_out, g_gate, g_up, g_down)]


SMALL = ("norm1_g", "norm2_g", "swa_q_g", "swa_k_g", "swa_sink", "dn_conv_w", "dn_A_log", "dn_dt_bias", "dn_out_g",
         "na_q_g", "na_k_g", "na_rpb")


def _pack(arrs):
    flat = jnp.concatenate([a.reshape(-1).astype(F32) for a in arrs])
    n = flat.shape[0]
    rows = -(-n // (8 * HEAD)) * 8
    return jnp.pad(flat, (0, rows * HEAD - n)).reshape(rows, HEAD)


def _unpack(packed, like):
    flat = packed.reshape(-1)
    out, o = [], 0
    for a in like:
        out.append(flat[o:o + a.size].reshape(a.shape))
        o += a.size
    return out


def _sum_devices(name, g, which):
    _, R, _ = g.shape

    def body(g_ref, o_ref):
        acc = g_ref[which[0]]
        for b in which[1:]:
            acc = acc + g_ref[b]
        o_ref[...] = acc

    return pl.pallas_call(
        body, name=name, grid=(R // 8,), in_specs=[pl.BlockSpec((N_DEV, 8, HEAD), lambda i: (0, i, 0))],
        out_specs=pl.BlockSpec((8, HEAD), lambda i: (i, 0)), out_shape=jax.ShapeDtypeStruct((R, HEAD), F32),
        compiler_params=_cparams(("parallel",)), **_CALL_KW)(g)


def _silu_rows(name, c_rows):
    return _elementwise(name, lambda c: (c * _sigmoid(c),), [c_rows], [BF16])[0]


def _ada_cotangent(name, dm_all, b_ada_shape):
    _, L, _, N6 = dm_all.shape
    tn = _pick(N6, (1024, 512, 256, 128))

    def body(d_ref, o_ref, b_ref):
        csum = d_ref[0, 0:1, :]
        for b in range(1, N_DEV):
            csum = csum + d_ref[b, 0:1, :]
        tot = csum
        for b in range(N_DEV):
            o_ref[b:b + 1, :] = d_ref[b, 1:2, :]
            tot = tot + d_ref[b, 1:2, :]
        first = lax.broadcasted_iota(jnp.int32, (8, tn), 0) == 0
        o_ref[N_DEV:, :] = jnp.where(first, jnp.broadcast_to(csum, (8, tn)), 0.0)
        b_ref[...] = jnp.broadcast_to(tot, (8, tn))

    return pl.pallas_call(
        body, name=name, grid=(L, N6 // tn), in_specs=[pl.BlockSpec((N_DEV, None, 2, tn), lambda l, j: (0, l, 0, j))],
        out_specs=[pl.BlockSpec((None, 16, tn), lambda l, j: (l, 0, j)), pl.BlockSpec((None, 8, tn), lambda l, j: (l, 0, j))],
        out_shape=[jax.ShapeDtypeStruct((L, 16, N6), F32), jax.ShapeDtypeStruct((L, 8, N6), F32)],
        compiler_params=_cparams(("parallel", "parallel")), **_CALL_KW)(dm_all)


def kernel(x, c, ctx, c_ctx, w_ada, b_ada, norm1_g, norm2_g, w_in, swa_q_g, swa_k_g, swa_sink, dn_conv_w, dn_A_log, dn_dt_bias, dn_out_g, na_q_g, na_k_g, na_rpb, w_out, w_gate, w_up, w_down, loss_target, m_c_ctx, m_w_ada, m_b_ada, m_norm1_g, m_norm2_g, m_w_in, m_swa_q_g, m_swa_k_g, m_swa_sink, m_dn_conv_w, m_dn_A_log, m_dn_dt_bias, m_dn_out_g, m_na_q_g, m_na_k_g, m_na_rpb, m_w_out, m_w_gate, m_w_up, m_w_down, v_c_ctx, v_w_ada, v_b_ada, v_norm1_g, v_norm2_g, v_w_in, v_swa_q_g, v_swa_k_g, v_swa_sink, v_dn_conv_w, v_dn_A_log, v_dn_dt_bias, v_dn_out_g, v_na_q_g, v_na_k_g, v_na_rpb, v_w_out, v_w_gate, v_w_up, v_w_down):
    L = w_in.shape[0]
    D, seq, n_ctx = x.shape[-1], x.shape[1], ctx.shape[1]
    dm = _Dims(D, seq, n_ctx, w_gate.shape[-1] * N_CHIPS)
    xi, yi, ci = _axes()
    chip = 2 * xi + yi
    dev = 4 * xi + 2 * yi + ci
    n6 = 6 * D
    n6s = n6 // N_CHIPS

    shards = [_elementwise(f"cast_{n}", lambda w: (w,), [w], [BF16])[0]
              for n, w in (("w_in", w_in), ("w_out", w_out), ("w_gate", w_gate), ("w_up", w_up), ("w_down", w_down))]
    assert L == 2
    g_in0 = _gather_d2d("gather_w0a_d2d", _exchange("gather_w0a_ici", _gather_ici_comm(shards[:1], 0)))
    Ws = [_weights_in(dm, 0, chip, g_in0[0], shards[0]), None]
    conv_all = _allgather_small("gather_conv_w", _pack([dn_conv_w]))
    conv_full = jnp.concatenate([_unpack(conv_all[2 * k], [dn_conv_w])[0] for k in range(N_CHIPS)], axis=-1)

    c_all = _allgather_small("gather_c", _pack([c]))
    c_rows = jnp.concatenate([c_all[:, :D // HEAD].reshape(N_DEV, D), c_ctx[None], jnp.zeros((16 - N_DEV - 1, D), F32)], axis=0)
    a_rows = _silu_rows("ada_silu", c_rows)
    b_sh = lax.dynamic_slice_in_dim(b_ada, chip * n6s, n6s, axis=1)
    mod_sh = [_matmul(f"ada_mod{l}", a_rows, w_ada[l], "nn", tm=16) for l in range(L)]
    mod_all = _allgather_small("gather_mod", _pack(mod_sh))
    mods = []
    for l in range(L):
        per_chip = [_unpack(mod_all[2 * k], mod_sh)[l] for k in range(N_CHIPS)]
        mods.append(jnp.concatenate(per_chip, axis=1))
    modvs = []
    for l in range(L):
        rows = jnp.stack([mods[l][N_DEV], lax.dynamic_index_in_dim(mods[l], dev, 0, keepdims=False)])
        modvs.append(_elementwise(f"ada_bias{l}", lambda m, b: (m + b,), [rows, jnp.broadcast_to(b_ada[l][None], (2, n6))], [F32])[0]
                     .reshape(2, 6, D))

    cos, sin = _rope_tables(dm)
    sps = [dict(norm1_g=norm1_g[l][None], norm2_g=norm2_g[l][None], swa_q_g=swa_q_g[l][None], swa_k_g=swa_k_g[l][None],
                swa_sink=swa_sink[l], dn_conv_w=conv_full[l], dn_A_log=dn_A_log[l], dn_dt_bias=dn_dt_bias[l],
                dn_out_g=dn_out_g[l][None], na_q_g=na_q_g[l][None], na_k_g=na_k_g[l][None], na_rpb=na_rpb[l]) for l in range(L)]
    xs = jnp.concatenate([ctx[0], x[0]], axis=0)
    ress = [None] * L
    w_in_s, w_out_s, w_gate_s, w_up_s, w_down_s = shards
    hosts0 = dict(scan=_gather_ici_comm(shards[1:], 0),
                  na=_gather_ici_comm([w_in_s], 1), gu=_gather_ici_comm([w_gate_s, w_up_s], 1),
                  down=_gather_ici_comm([w_out_s, w_down_s], 1))

    def late0(arrived):
        return _weights_rest(dm, 0, chip, _gather_d2d("gather_w0b_d2d", arrived), shards[1:])

    xs, ress[0], got0, Ws[0] = _layer_fwd(dm, xs, Ws[0], sps[0], modvs[0], cos, sin, hosts0, late0)
    g1 = _gather_d2d("gather_w1_d2d", got0["na"] + [got0["down"][0]] + got0["gu"] + [got0["down"][1]])
    Ws[1] = dict(_weights_in(dm, 1, chip, g1[0], w_in_s), **_weights_rest(dm, 1, chip, g1[1:], shards[1:]))
    xs, ress[1], _, _ = _layer_fwd(dm, xs, Ws[1], sps[1], modvs[1], cos, sin)
    loss_blk, dxs = _loss_and_grad("loss", xs, loss_target[0], dm.ctx_tiles)
    loss = lax.psum(loss_blk[0, 0], ("x", "y", "c"))

    bigs, smalls, dmodvs = [None] * L, [None] * L, [None] * L
    pairs = {}

    def host1(big1):
        pairs["a"] = _reduce_pre("a", _chunks_rest(dm, [big1]))
        return _reduce_ici_comm(pairs["a"])

    def host0(big0):
        pairs["b"] = _reduce_pre("b", [_chunks_in(dm, [bigs[1]])] + _chunks_rest(dm, [big0]))
        return _reduce_ici_comm(pairs["b"])

    dxs, bigs[1], smalls[1], dmodvs[1], got_a = _layer_bwd(dm, dxs, Ws[1], sps[1], modvs[1], cos, sin, ress[1], host1)
    dxs, bigs[0], smalls[0], dmodvs[0], got_b = _layer_bwd(dm, dxs, Ws[0], sps[0], modvs[0], cos, sin, ress[0], host0)
    rest1 = _reduce_post("a", pairs["a"], got_a)
    g_in1, *rest0 = _reduce_post("b", pairs["b"], got_b)
    pair_c = _reduce_pre("c", [_chunks_in(dm, [bigs[0]])])
    (g_in0,) = _reduce_post("c", pair_c, _exchange("reduce_c_ici", _reduce_ici_comm(pair_c)))
    g_in = jnp.concatenate([g_in0, g_in1], axis=0)
    g_out, g_gate, g_up, g_down = [jnp.concatenate([a0, a1], axis=0) for a0, a1 in zip(rest0, rest1)]
    grad_x = dxs[n_ctx:][None]

    dm_mine = jnp.stack([d.reshape(2, n6) for d in dmodvs])
    dm_all = _allgather_small("gather_dmod", _pack([dm_mine]))
    dm_all = jnp.stack([_unpack(dm_all[b], [dm_mine])[0] for b in range(N_DEV)])
    dm_rows, d_b_ada = _ada_cotangent("ada_cot", dm_all, b_ada.shape)
    dm_sh = lax.dynamic_slice_in_dim(dm_rows, chip * n6s, n6s, axis=2).astype(BF16)
    g_w_ada = jnp.stack([_matmul(f"ada_dw{l}", a_rows, dm_sh[l], "tn") for l in range(L)])
    dc_part = [_matmul(f"ada_dc{l}", dm_sh[l], w_ada[l], "nt", tm=16) for l in range(L)]
    dc_mine = dc_part[0][N_DEV]
    for l in range(1, L):
        dc_mine = dc_mine + dc_part[l][N_DEV]

    small_list = [jnp.stack([smalls[l][n] for l in range(L)]) for n in SMALL]
    sm_all = _allgather_small("gather_small", _pack(small_list + [dc_mine]))
    sm_sum = _sum_devices("sum_small", sm_all, tuple(range(N_DEV)))
    dc_sum = _sum_devices("sum_dc", sm_all, tuple(range(0, N_DEV, 2)))
    g_small = dict(zip(SMALL, _unpack(sm_sum, small_list)))
    dcs = _unpack(dc_sum, small_list + [dc_mine])[-1]
    def silu_bwd(d, cc):
        s = _sigmoid(cc)
        return (d * (s * (1.0 + cc * (1.0 - s))),)

    g_c_ctx = _elementwise("c_ctx_silu_bwd", silu_bwd, [dcs.reshape(-1, HEAD), c_ctx.reshape(-1, HEAD)], [F32])[0]
    wd3 = dn_conv_w.shape[-1]
    g_small["dn_conv_w"] = lax.dynamic_slice_in_dim(g_small["dn_conv_w"], chip * wd3, wd3, axis=2)

    grads = dict(g_small, c_ctx=g_c_ctx, w_ada=g_w_ada, b_ada=d_b_ada[:, 0], w_in=g_in, w_out=g_out, w_gate=g_gate, w_up=g_up,
                 w_down=g_down)
    weights = dict(c_ctx=c_ctx, w_ada=w_ada, b_ada=b_ada, norm1_g=norm1_g, norm2_g=norm2_g, w_in=w_in, swa_q_g=swa_q_g,
                   swa_k_g=swa_k_g, swa_sink=swa_sink, dn_conv_w=dn_conv_w, dn_A_log=dn_A_log, dn_dt_bias=dn_dt_bias,
                   dn_out_g=dn_out_g, na_q_g=na_q_g, na_k_g=na_k_g, na_rpb=na_rpb, w_out=w_out, w_gate=w_gate, w_up=w_up,
                   w_down=w_down)
    ms = dict(c_ctx=m_c_ctx, w_ada=m_w_ada, b_ada=m_b_ada, norm1_g=m_norm1_g, norm2_g=m_norm2_g, w_in=m_w_in, swa_q_g=m_swa_q_g,
              swa_k_g=m_swa_k_g, swa_sink=m_swa_sink, dn_conv_w=m_dn_conv_w, dn_A_log=m_dn_A_log, dn_dt_bias=m_dn_dt_bias,
              dn_out_g=m_dn_out_g, na_q_g=m_na_q_g, na_k_g=m_na_k_g, na_rpb=m_na_rpb, w_out=m_w_out, w_gate=m_w_gate, w_up=m_w_up,
              w_down=m_w_down)
    vs = dict(c_ctx=v_c_ctx, w_ada=v_w_ada, b_ada=v_b_ada, norm1_g=v_norm1_g, norm2_g=v_norm2_g, w_in=v_w_in, swa_q_g=v_swa_q_g,
              swa_k_g=v_swa_k_g, swa_sink=v_swa_sink, dn_conv_w=v_dn_conv_w, dn_A_log=v_dn_A_log, dn_dt_bias=v_dn_dt_bias,
              dn_out_g=v_dn_out_g, na_q_g=v_na_q_g, na_k_g=v_na_k_g, na_rpb=v_na_rpb, w_out=v_w_out, w_gate=v_w_gate, w_up=v_w_up,
              w_down=v_w_down)
    order = ("c_ctx", "w_ada", "b_ada", "norm1_g", "norm2_g", "w_in", "swa_q_g", "swa_k_g", "swa_sink", "dn_conv_w", "dn_A_log",
             "dn_dt_bias", "dn_out_g", "na_q_g", "na_k_g", "na_rpb", "w_out", "w_gate", "w_up", "w_down")
    big_names = ("w_ada", "w_in", "w_out", "w_gate", "w_up", "w_down")
    grads = {n: grads[n].reshape(weights[n].shape) for n in order}
    delta, new_m, new_v = {}, {}, {}
    for n in big_names:
        delta[n], new_m[n], new_v[n] = _adamw(f"adamw_{n}", weights[n], grads[n], ms[n], vs[n])
    small_names = [n for n in order if n not in big_names]
    packed = [_pack([d[n] for n in small_names]) for d in (weights, grads, ms, vs)]
    outs = _adamw("adamw_small", *packed)
    like = [weights[n] for n in small_names]
    for d, o in zip((delta, new_m, new_v), outs):
        d.update(dict(zip(small_names, _unpack(o, like))))
    return (loss, grad_x, *[grads[n] for n in order], *[delta[n] for n in order], *[new_m[n] for n in order],
            *[new_v[n] for n in order])
```

```python
import functools
import math

import jax
import jax.numpy as jnp
import numpy as np
from jax import lax
from jax.experimental import pallas as pl
from jax.experimental.pallas import tpu as pltpu

F32, BF16 = jnp.float32, jnp.bfloat16
MESH = pl.DeviceIdType.MESH

GRID_W = 64
HEAD = 128
SWA_WINDOW = 128
DN_CONV = 5
DN_CHUNK = 64
NA_KH, NA_KW = 8, 16
ROPE_THETA = 10000.0
EPS = 1e-6
ADAM_LR, ADAM_B1, ADAM_B2, ADAM_EPS, ADAM_WD, ADAM_STEP = 0.001, 0.9, 0.999, 1e-08, 0.01, 10
N_CHIPS = 4
N_DEV = 8
TOK = 256
VMEM_LIMIT = 56 * 2 ** 20
MATMUL_VMEM = 40 * 2 ** 20

_CALL_KW = {}


def _cparams(sem=None, **kw):
    if sem is not None:
        kw["dimension_semantics"] = sem
    return pltpu.CompilerParams(vmem_limit_bytes=VMEM_LIMIT, **kw)


def _pick(n, cands):
    for cnd in cands:
        if n % cnd == 0:
            return cnd
    raise ValueError(f"no tile for {n} in {cands}")


def _axes():
    return lax.axis_index("x"), lax.axis_index("y"), lax.axis_index("c")


def _matmul(name, a, b, kind, out_dtype=F32, tm=None, tn=None, tk=None, comm=None):
    if kind == "nn":
        (M, K), (K2, N) = a.shape, b.shape
    elif kind == "nt":
        (M, K), (N, K2) = a.shape, b.shape
    else:
        (K, M), (K2, N) = a.shape, b.shape
    assert K == K2, (name, a.shape, b.shape)
    tm = tm or _pick(M, (1024, 512, 256, 128) if kind == "tn" else (1088, 1024, 704, 512, 256, 128, 64, 32, 16, 8))

    def vmem(tn_, tk_):
        acc = tm * tn_ * 4 if tk_ < K else 0
        return 2 * (tm * tk_ * a.dtype.itemsize + tk_ * tn_ * b.dtype.itemsize + tm * tn_ * jnp.dtype(out_dtype).itemsize) + acc

    if tn is None or tk is None:
        cands = [(n_, k_) for k_ in ((tk,) if tk else (K, 3328, 2816, 2048, 1024, 512)) if K % k_ == 0
                 for n_ in ((tn,) if tn else (1024, 512, 256, 128)) if N % n_ == 0]
        tn, tk = next((c for c in cands if vmem(*c) <= MATMUL_VMEM), cands[-1])
    nk = K // tk
    dims = {"nn": (((1,), (0,)), ((), ())), "nt": (((1,), (1,)), ((), ())), "tn": (((0,), (0,)), ((), ()))}[kind]

    ni, nj = M // tm, N // tn

    def body(*refs):
        (a_ref, b_ref), (o_ref,), scr, cref = _hosted(comm, 2, 1, refs)
        i, j, k = pl.program_id(0), pl.program_id(1), pl.program_id(2)
        _hosted_start(comm, cref, (i == 0) & (j == 0) & (k == 0))
        part = lax.dot_general(a_ref[...].astype(BF16), b_ref[...].astype(BF16), dims, preferred_element_type=F32)
        if nk == 1:
            o_ref[...] = part.astype(o_ref.dtype)
        else:
            acc = scr[0]

            @pl.when(k == 0)
            def _():
                acc[...] = part

            @pl.when(k > 0)
            def _():
                acc[...] += part

            @pl.when(k == nk - 1)
            def _():
                o_ref[...] = acc[...].astype(o_ref.dtype)
        _hosted_wait(comm, cref, (i == ni - 1) & (j == nj - 1) & (k == nk - 1))

    a_spec = {"nn": pl.BlockSpec((tm, tk), lambda i, j, k: (i, k)), "nt": pl.BlockSpec((tm, tk), lambda i, j, k: (i, k)),
              "tn": pl.BlockSpec((tk, tm), lambda i, j, k: (k, i))}[kind]
    b_spec = {"nn": pl.BlockSpec((tk, tn), lambda i, j, k: (k, j)), "nt": pl.BlockSpec((tn, tk), lambda i, j, k: (j, k)),
              "tn": pl.BlockSpec((tk, tn), lambda i, j, k: (k, j))}[kind]
    any_spec = pl.BlockSpec(memory_space=pl.ANY)
    c_ins, c_sds, c_sems = (comm.ins, comm.out_sds(), comm.sem_shapes()) if comm is not None else ([], [], [])
    sem = ("parallel", "parallel", "arbitrary") if comm is None else ("arbitrary",) * 3
    res = pl.pallas_call(
        body, name=name, grid=(ni, nj, nk), in_specs=[a_spec, b_spec] + [any_spec] * len(c_ins),
        out_specs=[pl.BlockSpec((tm, tn), lambda i, j, k: (i, j))] + [any_spec] * len(c_sds),
        out_shape=[jax.ShapeDtypeStruct((M, N), out_dtype)] + c_sds,
        scratch_shapes=([pltpu.VMEM((tm, tn), F32)] if nk > 1 else []) + c_sems,
        compiler_params=_cparams(sem, has_side_effects=comm is not None), **_CALL_KW)(a, b, *c_ins)
    return res[0] if comm is None else (res[0], list(res[1:]))


class _Comm:
    def __init__(self, ins, out_shapes, plan, n_local, n_remote, aliases=None):
        self.ins, self.out_shapes, self.plan = list(ins), list(out_shapes), plan
        self.n_local, self.n_remote, self.aliases = n_local, n_remote, aliases or {}

    def sem_shapes(self):
        return [pltpu.SemaphoreType.DMA((max(self.n_remote, 1),)), pltpu.SemaphoreType.DMA((max(self.n_remote, 1),)),
                pltpu.SemaphoreType.DMA((max(self.n_local, 1),))]

    def out_sds(self):
        return [jax.ShapeDtypeStruct(s, d) for s, d in self.out_shapes]

    def copies(self, in_refs, out_refs, sems):
        send_sems, recv_sems, loc_sems = sems
        x, y, c = _axes()
        local, remote = self.plan(x, y, c, in_refs, out_refs)
        assert len(local) == self.n_local and len(remote) == self.n_remote, (len(local), len(remote))
        lcs = [pltpu.make_async_copy(s, d, loc_sems.at[i]) for i, (s, d) in enumerate(local)]
        rcs = [pltpu.make_async_remote_copy(src_ref=s, dst_ref=d, send_sem=send_sems.at[i], recv_sem=recv_sems.at[i],
                                            device_id=dev, device_id_type=MESH) for i, (s, d, dev) in enumerate(remote)]
        return lcs + rcs


def _exchange(name, comm):
    n_in, n_out = len(comm.ins), len(comm.out_shapes)

    def body(*refs):
        cps = comm.copies(refs[:n_in], refs[n_in:n_in + n_out], refs[n_in + n_out:])
        for cp in cps:
            cp.start()
        for cp in cps:
            cp.wait()

    any_spec = pl.BlockSpec(memory_space=pl.ANY)
    return pl.pallas_call(
        body, name=name, in_specs=[any_spec] * n_in, out_specs=[any_spec] * n_out, out_shape=comm.out_sds(),
        scratch_shapes=comm.sem_shapes(), input_output_aliases=comm.aliases,
        compiler_params=pltpu.CompilerParams(has_side_effects=True), **_CALL_KW)(*comm.ins)


def _chip_of(k):
    return k // 2, k % 2


def _gather_ici_comm(shards, layer):
    def plan(x, y, c, ins, outs):
        me = 2 * x + y
        remote = []
        for w, g in zip(ins, outs):
            half = w.shape[1] // 2
            rows = pl.ds(c * half, half)
            for j in (1, 2, 3):
                px, py = _chip_of(me ^ j)
                remote.append((w.at[layer, rows], g.at[me, rows], (px, py, c)))
        return [], remote

    return _Comm(shards, [((N_CHIPS,) + w.shape[1:], w.dtype) for w in shards], plan, 0, 3 * len(shards))


def _gather_d2d(name, gath):
    def plan(x, y, c, ins, outs):
        me = 2 * x + y
        remote = []
        for g in outs:
            half = g.shape[1] // 2
            rows = pl.ds(c * half, half)
            for j in (1, 2, 3):
                remote.append((g.at[me ^ j, rows], g.at[me ^ j, rows], (x, y, 1 - c)))
        return [], remote

    n = len(gath)
    return _exchange(name, _Comm(gath, [(g.shape, g.dtype) for g in gath], plan, 0, 3 * n, aliases={i: i for i in range(n)}))


def _elementwise(name, fn, ins, out_dtypes, block_rows=None, n_out=None):
    shape = ins[0].shape
    lead, (R, C) = shape[:-2], shape[-2:]
    budget = (16 * 2 ** 20) // (8 * (len(ins) + len(out_dtypes)) * (-(-C // 128) * 128))
    br = block_rows or _pick(R, [r for r in (512, 256, 128, 352, 64, 32, 16, 8) if r <= max(budget, 8)] + [R])
    nl = len(lead)

    def body(*refs):
        outs = fn(*[r[...] for r in refs[:len(ins)]])
        for r, o in zip(refs[len(ins):], outs):
            r[...] = o.astype(r.dtype)

    blk = (None,) * nl + (br, C)
    spec = pl.BlockSpec(blk, lambda *g: tuple(g[:nl]) + (g[nl], 0))
    return pl.pallas_call(
        body, name=name, grid=tuple(lead) + (R // br,), in_specs=[spec] * len(ins), out_specs=[spec] * len(out_dtypes),
        out_shape=[jax.ShapeDtypeStruct(shape, d) for d in out_dtypes],
        compiler_params=_cparams(("parallel",) * (nl + 1)), **_CALL_KW)(*ins)


def _reduce_pre(tag, parts):
    n = len(parts)

    def plan_a(x, y, c, ins, outs):
        remote = []
        for p, r in zip(ins, outs):
            half = p.shape[2] // 2
            remote.append((p.at[:, :, pl.ds((1 - c) * half, half)], r, (x, y, 1 - c)))
        return [], remote

    halves = [((p.shape[0], p.shape[1], p.shape[2] // 2, p.shape[3]), p.dtype) for p in parts]
    got = _exchange(f"reduce_{tag}_d2d", _Comm(parts, halves, plan_a, 0, n))

    c = lax.axis_index("c")
    pair = []
    for idx, (p, r) in enumerate(zip(parts, got)):
        half = p.shape[2] // 2
        br = _pick(half, (512, 256, 352, 128, 64, 32, 16))
        nb = half // br

        def body(c_ref, p_ref, r_ref, o_ref):
            o_ref[...] = (p_ref[...].astype(F32) + r_ref[...].astype(F32)).astype(o_ref.dtype)

        blk = (None, None, br, p.shape[3])
        pair.append(pl.pallas_call(
            body, name=f"reduce_{tag}_pair{idx}",
            grid_spec=pltpu.PrefetchScalarGridSpec(
                num_scalar_prefetch=1, grid=(N_CHIPS, p.shape[1], nb),
                in_specs=[pl.BlockSpec(blk, lambda k, l, i, cr, nb=nb: (k, l, cr[0] * nb + i, 0)),
                          pl.BlockSpec(blk, lambda k, l, i, cr: (k, l, i, 0))],
                out_specs=pl.BlockSpec(blk, lambda k, l, i, cr: (k, l, i, 0))),
            out_shape=jax.ShapeDtypeStruct(r.shape, BF16),
            compiler_params=_cparams(("parallel",) * 3), **_CALL_KW)(jnp.reshape(c, (1,)).astype(jnp.int32), p, r))
    return pair


def _reduce_ici_comm(pair):
    def plan_b(x, y, c, ins, outs):
        me = 2 * x + y
        remote = []
        for p, r in zip(ins, outs):
            for j in (1, 2, 3):
                px, py = _chip_of(me ^ j)
                remote.append((p.at[me ^ j], r.at[me], (px, py, c)))
        return [], remote

    return _Comm(pair, [(p.shape, p.dtype) for p in pair], plan_b, 0, 3 * len(pair))


def _reduce_post(tag, pair, got):
    n = len(pair)
    c = lax.axis_index("c")
    me_chip = 2 * lax.axis_index("x") + lax.axis_index("y")

    sums = []
    for idx, r in enumerate(got):
        _, L, half, C = r.shape
        br = _pick(half, (512, 256, 352, 128, 64, 32, 16))
        nb = half // br

        def body(c_ref, p_ref, r_ref, o_ref):
            me = c_ref[1]
            acc = None
            for k in range(N_CHIPS):
                term = jnp.where(me == k, p_ref[k], r_ref[k]).astype(F32)
                acc = term if acc is None else acc + term
            o_ref[...] = acc

        blk4 = pl.BlockSpec((N_CHIPS, None, br, C), lambda l, i, cr: (0, l, i, 0))
        sums.append(pl.pallas_call(
            body, name=f"reduce_{tag}_sum{idx}",
            grid_spec=pltpu.PrefetchScalarGridSpec(
                num_scalar_prefetch=1, grid=(L, nb), in_specs=[blk4, blk4],
                out_specs=pl.BlockSpec((None, br, C), lambda l, i, cr, nb=nb: (l, cr[0] * nb + i, 0))),
            out_shape=jax.ShapeDtypeStruct((L, 2 * half, C), F32),
            compiler_params=_cparams(("parallel",) * 2), **_CALL_KW)(jnp.stack([c, me_chip]).astype(jnp.int32), pair[idx], r))

    def plan_c(x, y, c, ins, outs):
        remote = []
        for f in outs:
            half = f.shape[1] // 2
            rows = pl.ds(c * half, half)
            remote.append((f.at[:, rows], f.at[:, rows], (x, y, 1 - c)))
        return [], remote

    return _exchange(f"reduce_{tag}_bcast", _Comm(sums, [(s.shape, F32) for s in sums], plan_c, 0, n, aliases={i: i for i in range(n)}))


def _adamw_math(w, g, m, v):
    m = ADAM_B1 * m + (1.0 - ADAM_B1) * g
    v = ADAM_B2 * v + (1.0 - ADAM_B2) * (g * g)
    m_hat = m / (1.0 - ADAM_B1 ** ADAM_STEP)
    v_hat = v / (1.0 - ADAM_B2 ** ADAM_STEP)
    delta = -ADAM_LR * (m_hat / (jnp.sqrt(v_hat) + ADAM_EPS) + ADAM_WD * w)
    return delta, m, v


def _adamw(name, w, g, m, v):
    return _elementwise(name, _adamw_math, [w, g, m, v], [F32, F32, F32])


def _allgather_small(name, v):
    def plan(x, y, c, ins, outs):
        me = 4 * x + 2 * y + c
        (src,), (dst,) = ins, outs
        remote = []
        for j in range(1, N_DEV):
            p = me ^ j
            remote.append((src, dst.at[me], (p // 4, (p // 2) % 2, p % 2)))
        return [(src, dst.at[me])], remote

    return _exchange(name, _Comm([v], [((N_DEV,) + v.shape, v.dtype)], plan, 1, N_DEV - 1))[0]


def _seg_spec(rows, D, ctx_tiles):
    return pl.BlockSpec((None, rows, D), lambda i: (jnp.minimum(i // ctx_tiles, 1), 0, 0))


def _norm_mod(name, x, g, modv, r0, ctx_tiles):
    T, D = x.shape

    def body(x_ref, g_ref, m_ref, o_ref):
        xv = x_ref[...]
        r = lax.rsqrt(jnp.mean(xv * xv, axis=-1, keepdims=True) + EPS)
        y = xv * r * g_ref[...]
        o_ref[...] = (y * (1.0 + m_ref[r0 + 1:r0 + 2, :]) + m_ref[r0:r0 + 1, :]).astype(BF16)

    row = pl.BlockSpec((TOK, D), lambda i: (i, 0))
    return pl.pallas_call(
        body, name=name, grid=(T // TOK,), in_specs=[row, pl.BlockSpec((1, D), lambda i: (0, 0)), _seg_spec(6, D, ctx_tiles)],
        out_specs=row, out_shape=jax.ShapeDtypeStruct((T, D), BF16), compiler_params=_cparams(("parallel",)), **_CALL_KW)(x, g, modv)


def _norm_mod_bwd(name, x, g, modv, r0, dh, dh_b, dres, ctx_tiles):
    T, D = x.shape

    def body(x_ref, g_ref, m_ref, dh_ref, dhb_ref, dres_ref, dx_ref, dg_ref, dsh_ref, dsc_ref):
        i = pl.program_id(0)
        xv = x_ref[...]
        r = lax.rsqrt(jnp.mean(xv * xv, axis=-1, keepdims=True) + EPS)
        xn = xv * r
        y = xn * g_ref[...]
        dhv = dh_ref[...] + dhb_ref[...]

        @pl.when(i == 0)
        def _():
            dg_ref[...] = jnp.zeros_like(dg_ref)

        @pl.when((i == 0) | (i == ctx_tiles))
        def _():
            dsh_ref[...] = jnp.zeros_like(dsh_ref)
            dsc_ref[...] = jnp.zeros_like(dsc_ref)

        dsh_ref[...] += jnp.sum(dhv, axis=0, keepdims=True)
        dsc_ref[...] += jnp.sum(dhv * y, axis=0, keepdims=True)
        dy = dhv * (1.0 + m_ref[r0 + 1:r0 + 2, :])
        dg_ref[...] += jnp.sum(dy * xn, axis=0, keepdims=True)
        u = dy * g_ref[...]
        dx_ref[...] = dres_ref[...] + r * (u - xn * jnp.mean(u * xn, axis=-1, keepdims=True))

    row = pl.BlockSpec((TOK, D), lambda i: (i, 0))
    one = pl.BlockSpec((1, D), lambda i: (0, 0))
    return pl.pallas_call(
        body, name=name, grid=(T // TOK,), in_specs=[row, one, _seg_spec(6, D, ctx_tiles), row, row, row],
        out_specs=[row, one, _seg_spec(1, D, ctx_tiles), _seg_spec(1, D, ctx_tiles)],
        out_shape=[jax.ShapeDtypeStruct((T, D), F32), jax.ShapeDtypeStruct((1, D), F32),
                   jax.ShapeDtypeStruct((2, 1, D), F32), jax.ShapeDtypeStruct((2, 1, D), F32)],
        compiler_params=_cparams(("arbitrary",)), **_CALL_KW)(x, g, modv, dh, dh_b, dres)


def _resid_gate(name, x, y, modv, r, ctx_tiles):
    T, D = x.shape

    def body(x_ref, y_ref, m_ref, o_ref):
        o_ref[...] = x_ref[...] + m_ref[r:r + 1, :] * y_ref[...]

    row = pl.BlockSpec((TOK, D), lambda i: (i, 0))
    return pl.pallas_call(
        body, name=name, grid=(T // TOK,), in_specs=[row, row, _seg_spec(6, D, ctx_tiles)], out_specs=row,
        out_shape=jax.ShapeDtypeStruct((T, D), F32), compiler_params=_cparams(("parallel",)), **_CALL_KW)(x, y, modv)


def _resid_gate_bwd(name, dx, y, modv, r, ctx_tiles):
    T, D = dx.shape

    def body(dx_ref, y_ref, m_ref, dy_ref, dgt_ref):
        i = pl.program_id(0)

        @pl.when((i == 0) | (i == ctx_tiles))
        def _():
            dgt_ref[...] = jnp.zeros_like(dgt_ref)

        dxv = dx_ref[...]
        dgt_ref[...] += jnp.sum(dxv * y_ref[...], axis=0, keepdims=True)
        dy_ref[...] = (dxv * m_ref[r:r + 1, :]).astype(BF16)

    row = pl.BlockSpec((TOK, D), lambda i: (i, 0))
    return pl.pallas_call(
        body, name=name, grid=(T // TOK,), in_specs=[row, row, _seg_spec(6, D, ctx_tiles)],
        out_specs=[row, _seg_spec(1, D, ctx_tiles)],
        out_shape=[jax.ShapeDtypeStruct((T, D), BF16), jax.ShapeDtypeStruct((2, 1, D), F32)],
        compiler_params=_cparams(("arbitrary",)), **_CALL_KW)(dx, y, modv)


def _sigmoid(x):
    return 1.0 / (1.0 + jnp.exp(-x))


SWI_ROWS = 128


def _swiglu(name, gu):
    T, F2 = gu.shape
    F = F2 // 2

    def body(gu_ref, o_ref):
        g, u = gu_ref[:, :F], gu_ref[:, F:]
        o_ref[...] = ((g * _sigmoid(g)) * u).astype(BF16)

    return pl.pallas_call(
        body, name=name, grid=(T // SWI_ROWS,), in_specs=[pl.BlockSpec((SWI_ROWS, F2), lambda i: (i, 0))],
        out_specs=pl.BlockSpec((SWI_ROWS, F), lambda i: (i, 0)), out_shape=jax.ShapeDtypeStruct((T, F), BF16),
        compiler_params=_cparams(("parallel",)), **_CALL_KW)(gu)


def _swiglu_bwd(name, gu, dact):
    T, F2 = gu.shape
    F = F2 // 2

    def body(gu_ref, d_ref, o_ref):
        g, u, d = gu_ref[:, :F], gu_ref[:, F:], d_ref[...]
        s = _sigmoid(g)
        o_ref[:, :F] = (d * u * (s * (1.0 + g * (1.0 - s)))).astype(BF16)
        o_ref[:, F:] = (d * (g * s)).astype(BF16)

    return pl.pallas_call(
        body, name=name, grid=(T // SWI_ROWS,),
        in_specs=[pl.BlockSpec((SWI_ROWS, F2), lambda i: (i, 0)), pl.BlockSpec((SWI_ROWS, F), lambda i: (i, 0))],
        out_specs=pl.BlockSpec((SWI_ROWS, F2), lambda i: (i, 0)), out_shape=jax.ShapeDtypeStruct((T, F2), BF16),
        compiler_params=_cparams(("parallel",)), **_CALL_KW)(gu, dact)


def _loss_and_grad(name, y, target, ctx_tiles):
    T, D = y.shape

    def body(y_ref, t_ref, l_ref, dy_ref):
        i = pl.program_id(0)

        @pl.when(i == 0)
        def _():
            l_ref[...] = jnp.zeros_like(l_ref)

        lat = i >= ctx_tiles
        e = jnp.where(lat, y_ref[...] - t_ref[...], 0.0)
        dy_ref[...] = e * (1.0 / D)
        l_ref[...] += 0.5 * jnp.sum(jnp.sum(e * e, axis=-1, keepdims=True) * (1.0 / D), axis=0, keepdims=True)

    row = pl.BlockSpec((TOK, D), lambda i: (i, 0))
    return pl.pallas_call(
        body, name=name, grid=(T // TOK,),
        in_specs=[row, pl.BlockSpec((TOK, D), lambda i: (jnp.maximum(i - ctx_tiles, 0), 0))],
        out_specs=[pl.BlockSpec((8, 128), lambda i: (0, 0)), row],
        out_shape=[jax.ShapeDtypeStruct((8, 128), F32), jax.ShapeDtypeStruct((T, D), F32)],
        compiler_params=_cparams(("arbitrary",)), **_CALL_KW)(y, target)


def _rot_half(x):
    lane = lax.broadcasted_iota(jnp.int32, x.shape, 1)
    return jnp.where((lane % 64) < 32, -pltpu.roll(x, 96, 1), pltpu.roll(x, 32, 1))


def _head_prep(name, src, col_blk, n_heads, g, cos, sin, t_pad, norm, rope):
    T = src.shape[0]
    W = n_heads * HEAD
    nt = T // TOK

    def body(s_ref, g_ref, cos_ref, sin_ref, o_ref):
        i = pl.program_id(0)
        outs = []
        for h in range(n_heads):
            xv = s_ref[:, h * HEAD:(h + 1) * HEAD].astype(F32)
            if norm:
                xv = xv * lax.rsqrt(jnp.mean(xv * xv, axis=-1, keepdims=True) + EPS) * g_ref[...]
            if rope:
                xv = xv * cos_ref[...] + _rot_half(xv) * sin_ref[...]
            outs.append(jnp.where(i < nt, xv, 0.0).astype(BF16))
        o_ref[...] = jnp.concatenate(outs, axis=-1) if n_heads > 1 else outs[0]

    tab = pl.BlockSpec((TOK, HEAD), lambda i: (i, 0))
    return pl.pallas_call(
        body, name=name, grid=(t_pad // TOK,),
        in_specs=[pl.BlockSpec((TOK, W), lambda i: (jnp.minimum(i, nt - 1), col_blk)), pl.BlockSpec((1, HEAD), lambda i: (0, 0)), tab, tab],
        out_specs=pl.BlockSpec((TOK, W), lambda i: (i, 0)), out_shape=jax.ShapeDtypeStruct((t_pad, W), BF16),
        compiler_params=_cparams(("parallel",)), **_CALL_KW)(src, g, cos, sin)


def _head_prep_bwd(name, src, col_blk, n_heads, g, cos, sin, dout, norm, rope, dst):
    T = src.shape[0]
    W = n_heads * HEAD

    def body(s_ref, g_ref, cos_ref, sin_ref, d_ref, dst_ref, ds_ref, dg_ref):
        i = pl.program_id(0)

        @pl.when(i == 0)
        def _():
            dg_ref[...] = jnp.zeros_like(dg_ref)

        outs = []
        dg = jnp.zeros((1, HEAD), F32)
        for h in range(n_heads):
            dz = d_ref[:, h * HEAD:(h + 1) * HEAD]
            if rope:
                dz = dz * cos_ref[...] - _rot_half(dz * sin_ref[...])
            if norm:
                xv = s_ref[:, h * HEAD:(h + 1) * HEAD].astype(F32)
                r = lax.rsqrt(jnp.mean(xv * xv, axis=-1, keepdims=True) + EPS)
                xn = xv * r
                dg = dg + jnp.sum(dz * xn, axis=0, keepdims=True)
                u = dz * g_ref[...]
                dz = r * (u - xn * jnp.mean(u * xn, axis=-1, keepdims=True))
            outs.append(dz.astype(BF16))
        dg_ref[...] += dg
        ds_ref[...] = jnp.concatenate(outs, axis=-1) if n_heads > 1 else outs[0]

    tab = pl.BlockSpec((TOK, HEAD), lambda i: (i, 0))
    col = pl.BlockSpec((TOK, W), lambda i: (i, col_blk))
    one = pl.BlockSpec((1, HEAD), lambda i: (0, 0))
    return pl.pallas_call(
        body, name=name, grid=(T // TOK,),
        in_specs=[col, one, tab, tab, pl.BlockSpec((TOK, W), lambda i: (i, 0)), pl.BlockSpec(memory_space=pl.ANY)],
        out_specs=[col, one], out_shape=[jax.ShapeDtypeStruct(dst.shape, dst.dtype), jax.ShapeDtypeStruct((1, HEAD), F32)],
        input_output_aliases={5: 0}, compiler_params=_cparams(("arbitrary",)), **_CALL_KW)(src, g, cos, sin, dout, dst)


NEG = -1e30


def _attn_geometry(kind, blk, ctx, seq):
    if kind == "swa":
        bq, W = 128, 384
        nctx = ctx // bq
        lat = blk >= nctx
        n = blk - nctx
        s0 = jnp.where(lat, ctx + (n - 1) * bq, 0)
        i = lax.broadcasted_iota(jnp.int32, (bq, W), 0)
        j = lax.broadcasted_iota(jnp.int32, (bq, W), 1)
        kpos = (n - 1) * bq + j
        rel = j - bq - i
        valid = lat & (rel <= SWA_WINDOW) & (rel >= -SWA_WINDOW) & (kpos >= 0) & (kpos < seq)
        return s0, valid, 0
    bq, W = GRID_W, NA_KH * GRID_W
    nctx = ctx // bq
    rows = seq // GRID_W
    lat = blk >= nctx
    rr = jnp.clip(blk - nctx, 0, rows - 1)
    rs = jnp.clip(rr - NA_KH // 2, 0, rows - NA_KH)
    s0 = ctx + rs * GRID_W
    i = lax.broadcasted_iota(jnp.int32, (bq, W), 0)
    j = lax.broadcasted_iota(jnp.int32, (bq, W), 1)
    kcol = j % GRID_W
    cs = jnp.clip(i - NA_KW // 2, 0, GRID_W - NA_KW)
    valid = lat & (kcol >= cs) & (kcol < cs + NA_KW)
    return s0, valid, rr - rs


HP = 2


def _attn_probs(q, kl, kc, sk, bias, valid):
    scale = HEAD ** -0.5
    nt_dims = (((1,), (1,)), ((), ()))
    sl = lax.dot_general(q, kl, nt_dims, preferred_element_type=F32) * scale
    if bias is not None:
        sl = sl + bias
    sl = jnp.where(valid, sl, NEG)
    sc = lax.dot_general(q, kc, nt_dims, preferred_element_type=F32) * scale
    m = jnp.maximum(jnp.maximum(jnp.max(sl, axis=-1, keepdims=True), jnp.max(sc, axis=-1, keepdims=True)), sk)
    el, ec, es = jnp.exp(sl - m), jnp.exp(sc - m), jnp.exp(sk - m)
    inv = 1.0 / (jnp.sum(el, axis=-1, keepdims=True) + jnp.sum(ec, axis=-1, keepdims=True) + es)
    return el * inv, ec * inv, es * inv


def _attn_specs(kind, n_q, n_kv, t_pad):
    bq = 128 if kind == "swa" else GRID_W
    rep = n_q // n_kv
    assert n_q % HP == 0 and HP % rep == 0
    kvw = HP // rep
    qspec = pl.BlockSpec((bq, HP * HEAD), lambda g, b: (b, g))
    kvspec = pl.BlockSpec((t_pad, kvw * HEAD), lambda g, b: (0, g))
    specs = [qspec, kvspec, kvspec, pl.BlockSpec(memory_space=pltpu.SMEM)]
    return bq, rep, qspec, kvspec, specs


def _bias_spec(ctx, seq):
    W = NA_KH * GRID_W

    def idx(g, b):
        rows = seq // GRID_W
        rr = jnp.clip(b - ctx // GRID_W, 0, rows - 1)
        return (g, rr - jnp.clip(rr - NA_KH // 2, 0, rows - NA_KH), 0, 0)

    return pl.BlockSpec((HP, None, GRID_W, W), idx)


def _attn_loads(kind, blk, q_ref, k_ref, v_ref, sink_ref, bias_ref, rep, ctx, seq):
    bq, W = (128, 384) if kind == "swa" else (GRID_W, NA_KH * GRID_W)
    g = pl.program_id(0)
    s0, valid, _ = _attn_geometry(kind, blk, ctx, seq)
    s0 = pl.multiple_of(s0, GRID_W)
    heads = []
    for j in range(HP):
        kv = slice((j // rep) * HEAD, (j // rep + 1) * HEAD)
        heads.append((q_ref[:, j * HEAD:(j + 1) * HEAD], k_ref[pl.ds(s0, W), kv], k_ref[0:ctx, kv], v_ref[pl.ds(s0, W), kv],
                      v_ref[0:ctx, kv], sink_ref[g * HP + j], bias_ref[j] if bias_ref is not None else None))
    return s0, W, valid, heads


def _attn_fwd(name, kind, q, k, v, sink, bias, ctx, seq, dst, head0, comm=None):
    t_pad = q.shape[0]
    T = ctx + seq
    n_q, n_kv = q.shape[1] // HEAD, k.shape[1] // HEAD
    bq, rep, qspec, kvspec, specs = _attn_specs(kind, n_q, n_kv, t_pad)
    assert head0 % HP == 0
    ins = [q, k, v, sink] + ([bias] if bias is not None else []) + [dst]
    ng, nb = n_q // HP, T // bq

    def body(*refs):
        mine, (o_ref,), _, cref = _hosted(comm, len(ins), 1, refs)
        q_ref, k_ref, v_ref, sink_ref = mine[:4]
        bias_ref = mine[4] if bias is not None else None
        g, blk = pl.program_id(0), pl.program_id(1)
        _hosted_start(comm, cref, (g == 0) & (blk == 0))
        _, _, valid, heads = _attn_loads(kind, blk, q_ref, k_ref, v_ref, sink_ref, bias_ref, rep, ctx, seq)
        outs = []
        for qv, kl, kc, vl, vc, sk, bv in heads:
            p_l, p_c, _ = _attn_probs(qv, kl, kc, sk, bv, valid)
            o = jnp.dot(p_l.astype(BF16), vl, preferred_element_type=F32) + jnp.dot(p_c.astype(BF16), vc, preferred_element_type=F32)
            outs.append(o.astype(o_ref.dtype))
        o_ref[...] = jnp.concatenate(outs, axis=-1)
        _hosted_wait(comm, cref, (g == ng - 1) & (blk == nb - 1))

    if bias is not None:
        specs = specs + [_bias_spec(ctx, seq)]
    any_spec = pl.BlockSpec(memory_space=pl.ANY)
    c_ins, c_sds, c_sems = (comm.ins, comm.out_sds(), comm.sem_shapes()) if comm is not None else ([], [], [])
    res = pl.pallas_call(
        body, name=name, grid=(ng, nb), in_specs=specs + [any_spec] * (1 + len(c_ins)),
        out_specs=[pl.BlockSpec((bq, HP * HEAD), lambda g, b: (b, head0 // HP + g))] + [any_spec] * len(c_sds),
        out_shape=[jax.ShapeDtypeStruct(dst.shape, dst.dtype)] + c_sds, input_output_aliases={len(ins) - 1: 0},
        scratch_shapes=c_sems,
        compiler_params=_cparams(("arbitrary", "arbitrary"), has_side_effects=comm is not None), **_CALL_KW)(*ins, *c_ins)
    return res[0], list(res[1:])


def _attn_bwd(name, kind, q, k, v, sink, bias, do, do_head0, ctx, seq):
    t_pad = q.shape[0]
    T = ctx + seq
    n_q, n_kv = q.shape[1] // HEAD, k.shape[1] // HEAD
    bq, rep, qspec, kvspec, specs = _attn_specs(kind, n_q, n_kv, t_pad)
    assert do_head0 % HP == 0
    scale = HEAD ** -0.5
    tn_dims = (((0,), (0,)), ((), ()))
    nt_dims = (((1,), (1,)), ((), ()))
    bdot = functools.partial(lax.dot_general, preferred_element_type=F32)

    def body(q_ref, k_ref, v_ref, sink_ref, *rest):
        if bias is not None:
            bias_ref, do_ref, dq_ref, dk_ref, dv_ref, dsk_ref, db_ref = rest
        else:
            bias_ref, db_ref = None, None
            do_ref, dq_ref, dk_ref, dv_ref, dsk_ref = rest
        blk = pl.program_id(1)
        s0, W, valid, heads = _attn_loads(kind, blk, q_ref, k_ref, v_ref, sink_ref, bias_ref, rep, ctx, seq)
        dos = [do_ref[:, j * HEAD:(j + 1) * HEAD] for j in range(HP)]

        @pl.when(blk == 0)
        def _():
            dk_ref[...] = jnp.zeros_like(dk_ref)
            dv_ref[...] = jnp.zeros_like(dv_ref)
            dsk_ref[...] = jnp.zeros_like(dsk_ref)

        if bias is not None:
            _, _, pat = _attn_geometry(kind, blk, ctx, seq)
            _, _, pat_prev = _attn_geometry(kind, jnp.maximum(blk - 1, 0), ctx, seq)

            @pl.when((blk == 0) | (pat != pat_prev))
            def _():
                db_ref[...] = jnp.zeros_like(db_ref)

        res = []
        for (qv, kl, kc, vl, vc, sk, bv), dov in zip(heads, dos):
            p_l, p_c, p_s = _attn_probs(qv, kl, kc, sk, bv, valid)
            dob = dov.astype(BF16)
            pl_b, pc_b = p_l.astype(BF16), p_c.astype(BF16)
            o = jnp.dot(pl_b, vl, preferred_element_type=F32) + jnp.dot(pc_b, vc, preferred_element_type=F32)
            delta = jnp.sum(dov * o, axis=-1, keepdims=True)
            ds_l = p_l * (bdot(dob, vl, nt_dims) - delta)
            ds_c = p_c * (bdot(dob, vc, nt_dims) - delta)
            dsl_b, dsc_b = ds_l.astype(BF16), ds_c.astype(BF16)
            dq = (jnp.dot(dsl_b, kl, preferred_element_type=F32) + jnp.dot(dsc_b, kc, preferred_element_type=F32)) * scale
            res.append((dq, bdot(dsl_b, qv, tn_dims) * scale, bdot(pl_b, dob, tn_dims), bdot(dsc_b, qv, tn_dims) * scale,
                        bdot(pc_b, dob, tn_dims), jnp.sum(-p_s * delta, axis=0, keepdims=True), ds_l))
        dq_ref[...] = jnp.concatenate([r[0] for r in res], axis=-1)
        for j, (_, dkl, dvl, dkc, dvc, dsk, ds_l) in enumerate(res):
            kv = slice((j // rep) * HEAD, (j // rep + 1) * HEAD)
            dk_ref[pl.ds(s0, W), kv] += dkl
            dv_ref[pl.ds(s0, W), kv] += dvl
            dk_ref[0:ctx, kv] += dkc
            dv_ref[0:ctx, kv] += dvc
            dsk_ref[j] += jnp.broadcast_to(dsk, (8, HEAD))
            if bias is not None:
                db_ref[j] += ds_l

    ins = [q, k, v, sink] + ([bias] if bias is not None else []) + [do]
    in_specs = specs + ([_bias_spec(ctx, seq)] if bias is not None else []) + [
        pl.BlockSpec((bq, HP * HEAD), lambda g, b: (b, do_head0 // HP + g))]
    out_specs = [qspec, kvspec, kvspec, pl.BlockSpec((HP, 8, HEAD), lambda g, b: (g, 0, 0))]
    out_shape = [jax.ShapeDtypeStruct((T, n_q * HEAD), F32), jax.ShapeDtypeStruct((t_pad, n_kv * HEAD), F32),
                 jax.ShapeDtypeStruct((t_pad, n_kv * HEAD), F32), jax.ShapeDtypeStruct((n_q, 8, HEAD), F32)]
    if bias is not None:
        out_specs.append(_bias_spec(ctx, seq))
        out_shape.append(jax.ShapeDtypeStruct(bias.shape, F32))
    res = pl.pallas_call(
        body, name=name, grid=(n_q // HP, T // bq), in_specs=in_specs, out_specs=out_specs, out_shape=out_shape,
        compiler_params=_cparams(("arbitrary", "arbitrary")), **_CALL_KW)(*ins)
    return res if bias is not None else list(res) + [None]


HALO = 8


def _halo_specs(width, col0, T, ctx_tiles):
    per = TOK // HALO
    main = pl.BlockSpec((TOK, width), lambda jc, i: (i, col0 + jc))
    prev = pl.BlockSpec((HALO, width), lambda jc, i: (jnp.maximum(i * per - 1, 0), col0 + jc))
    nxt = pl.BlockSpec((HALO, width), lambda jc, i: (jnp.minimum((i + 1) * per, T // HALO - 1), col0 + jc))
    return main, prev, nxt


def _with_halo(i, nt, ctx_tiles, prev, main, nxt):
    has_prev = (i != 0) & (i != ctx_tiles)
    has_next = (i != ctx_tiles - 1) & (i != nt - 1)
    return jnp.concatenate([jnp.where(has_prev, prev, 0.0), main, jnp.where(has_next, nxt, 0.0)], axis=0)


def _shifted(ext, s):
    n = ext.shape[0]
    return pltpu.roll(ext, (-s) % n, 0)[HALO:HALO + TOK]


def _conv_fwd(name, p, col0, conv_w, ctx_tiles):
    T = p.shape[0]
    nt = T // TOK
    ncol = 3
    Wc = conv_w.shape[1] // ncol
    pad = (DN_CONV - 1) // 2

    def body(m_ref, p_ref, n_ref, w_ref, o_ref):
        i = pl.program_id(1)
        ext = _with_halo(i, nt, ctx_tiles, p_ref[...].astype(F32), m_ref[...].astype(F32), n_ref[...].astype(F32))
        acc = jnp.zeros((TOK, Wc), F32)
        for j in range(DN_CONV):
            acc = acc + w_ref[j:j + 1, :] * _shifted(ext, j - pad)
        o_ref[...] = acc

    main, prev, nxt = _halo_specs(Wc, col0, T, ctx_tiles)
    return pl.pallas_call(
        body, name=name, grid=(ncol, nt), in_specs=[main, prev, nxt, pl.BlockSpec((DN_CONV, Wc), lambda jc, i: (0, jc))],
        out_specs=pl.BlockSpec((TOK, Wc), lambda jc, i: (i, jc)), out_shape=jax.ShapeDtypeStruct((T, ncol * Wc), F32),
        compiler_params=_cparams(("parallel", "parallel")), **_CALL_KW)(p, p, p, conv_w)


def _conv_bwd(name, p, col0, conv_w, dpre, ctx_tiles, dst):
    T = p.shape[0]
    nt = T // TOK
    ncol = 3
    Wc = conv_w.shape[1] // ncol
    pad = (DN_CONV - 1) // 2

    def body(m_ref, p_ref, n_ref, dm_ref, dp_ref, dn_ref, w_ref, dst_ref, dx_ref, dw_ref):
        i = pl.program_id(1)
        ext_x = _with_halo(i, nt, ctx_tiles, p_ref[...].astype(F32), m_ref[...].astype(F32), n_ref[...].astype(F32))
        ext_d = _with_halo(i, nt, ctx_tiles, dp_ref[...], dm_ref[...], dn_ref[...])
        dmain = dm_ref[...]

        @pl.when(i == 0)
        def _():
            dw_ref[...] = jnp.zeros_like(dw_ref)

        acc = jnp.zeros((TOK, Wc), F32)
        for j in range(DN_CONV):
            acc = acc + w_ref[j:j + 1, :] * _shifted(ext_d, pad - j)
            dw_ref[j:j + 1, :] += jnp.sum(dmain * _shifted(ext_x, j - pad), axis=0, keepdims=True)
        dx_ref[...] = acc.astype(BF16)

    main, prev, nxt = _halo_specs(Wc, col0, T, ctx_tiles)
    dmain, dprev, dnxt = _halo_specs(Wc, 0, T, ctx_tiles)
    return pl.pallas_call(
        body, name=name, grid=(ncol, nt),
        in_specs=[main, prev, nxt, dmain, dprev, dnxt, pl.BlockSpec((DN_CONV, Wc), lambda jc, i: (0, jc)),
                  pl.BlockSpec(memory_space=pl.ANY)],
        out_specs=[pl.BlockSpec((TOK, Wc), lambda jc, i: (i, col0 + jc)), pl.BlockSpec((8, Wc), lambda jc, i: (0, jc))],
        out_shape=[jax.ShapeDtypeStruct(dst.shape, dst.dtype), jax.ShapeDtypeStruct((8, ncol * Wc), F32)],
        input_output_aliases={7: 0},
        compiler_params=_cparams(("parallel", "arbitrary")), **_CALL_KW)(p, p, p, dpre, dpre, dpre, conv_w, dst)


def _softplus(x):
    return jnp.maximum(x, 0.0) + jnp.log(1.0 + jnp.exp(-jnp.abs(x)))


def _gdn_point(name, pre, dab, a_log, dt_bias, n_heads):
    T = pre.shape[0]
    Wd = n_heads * HEAD
    ng = 2 * n_heads

    def body(pre_ref, ab_ref, al_ref, dt_ref, q_ref, k_ref, v_ref, la_ref, be_ref):
        for h in range(n_heads):
            for part, ref in enumerate((q_ref, k_ref, v_ref)):
                xv = pre_ref[:, part * Wd + h * HEAD:part * Wd + (h + 1) * HEAD]
                s = xv * _sigmoid(xv)
                if part < 2:
                    s = s * lax.rsqrt(jnp.sum(s * s, axis=-1, keepdims=True) + EPS) * (HEAD ** -0.5 if part == 0 else 1.0)
                ref[:, h * HEAD:(h + 1) * HEAD] = s
        ab = ab_ref[...].astype(F32)
        lane = lax.broadcasted_iota(jnp.int32, ab.shape, 1)
        la_ref[...] = jnp.where(lane < ng, -jnp.exp(al_ref[...]) * _softplus(ab + dt_ref[...]), 0.0)
        be_ref[...] = jnp.where(lane < ng, _sigmoid(pltpu.roll(ab, HEAD - ng, 1)), 0.0)

    row = lambda w: pl.BlockSpec((TOK, w), lambda i: (i, 0))
    one = pl.BlockSpec((1, HEAD), lambda i: (0, 0))
    return pl.pallas_call(
        body, name=name, grid=(T // TOK,), in_specs=[row(3 * Wd), row(HEAD), one, one],
        out_specs=[row(Wd), row(Wd), row(Wd), row(HEAD), row(HEAD)],
        out_shape=[jax.ShapeDtypeStruct((T, Wd), F32)] * 3 + [jax.ShapeDtypeStruct((T, HEAD), F32)] * 2,
        compiler_params=_cparams(("parallel",)), **_CALL_KW)(pre, dab, a_log, dt_bias)


def _gdn_point_bwd(name, pre, dab, a_log, dt_bias, n_heads, dq, dk, dv, dla, dbe):
    T = pre.shape[0]
    Wd = n_heads * HEAD
    ng = 2 * n_heads

    def body(pre_ref, ab_ref, al_ref, dt_ref, dq_ref, dk_ref, dv_ref, dla_ref, dbe_ref, dpre_ref, dab_ref, dal_ref, ddt_ref):
        i = pl.program_id(0)

        @pl.when(i == 0)
        def _():
            dal_ref[...] = jnp.zeros_like(dal_ref)
            ddt_ref[...] = jnp.zeros_like(ddt_ref)

        for h in range(n_heads):
            for part, ref in enumerate((dq_ref, dk_ref, dv_ref)):
                cols = slice(part * Wd + h * HEAD, part * Wd + (h + 1) * HEAD)
                xv = pre_ref[:, cols]
                sg = _sigmoid(xv)
                s = xv * sg
                dy = ref[0, :, h * HEAD:(h + 1) * HEAD] + ref[1, :, h * HEAD:(h + 1) * HEAD]
                if part < 2:
                    c0 = HEAD ** -0.5 if part == 0 else 1.0
                    r = lax.rsqrt(jnp.sum(s * s, axis=-1, keepdims=True) + EPS)
                    ds = c0 * (r * dy - s * (r * r * r) * jnp.sum(dy * s, axis=-1, keepdims=True))
                else:
                    ds = dy
                dpre_ref[:, cols] = ds * (sg * (1.0 + xv * (1.0 - sg)))
        ab = ab_ref[...].astype(F32)
        lane = lax.broadcasted_iota(jnp.int32, ab.shape, 1)
        ea = jnp.exp(al_ref[...])
        z = ab + dt_ref[...]
        dlav = jnp.where(lane < ng, dla_ref[0] + dla_ref[1], 0.0)
        da = dlav * (-ea) * _sigmoid(z)
        dal_ref[...] += jnp.sum(dlav * (-ea) * _softplus(z), axis=0, keepdims=True)
        ddt_ref[...] += jnp.sum(da, axis=0, keepdims=True)
        be = _sigmoid(pltpu.roll(ab, HEAD - ng, 1))
        db = jnp.where(lane < ng, (dbe_ref[0] + dbe_ref[1]) * be * (1.0 - be), 0.0)
        dab_ref[...] = (da + pltpu.roll(db, ng, 1)).astype(BF16)

    row = lambda w: pl.BlockSpec((TOK, w), lambda i: (i, 0))
    two = lambda w: pl.BlockSpec((2, TOK, w), lambda i: (0, i, 0))
    one = pl.BlockSpec((1, HEAD), lambda i: (0, 0))
    return pl.pallas_call(
        body, name=name, grid=(T // TOK,),
        in_specs=[row(3 * Wd), row(HEAD), one, one, two(Wd), two(Wd), two(Wd), two(HEAD), two(HEAD)],
        out_specs=[row(3 * Wd), row(HEAD), one, one],
        out_shape=[jax.ShapeDtypeStruct((T, 3 * Wd), F32), jax.ShapeDtypeStruct((T, HEAD), BF16),
                   jax.ShapeDtypeStruct((1, HEAD), F32), jax.ShapeDtypeStruct((1, HEAD), F32)],
        compiler_params=_cparams(("arbitrary",)), **_CALL_KW)(pre, dab, a_log, dt_bias, dq, dk, dv, dla, dbe)


_NN, _NT, _TN = "nn", "nt", "tn"
_DIMS = {"nn": (((1,), (0,)), ((), ())), "nt": (((1,), (1,)), ((), ())), "tn": (((0,), (0,)), ((), ()))}
_BDIMS = {"nn": (((2,), (1,)), ((0,), (0,))), "nt": (((2,), (2,)), ((0,), (0,))), "tn": (((1,), (1,)), ((0,), (0,)))}


def _dims(a, kind):
    return _BDIMS[kind] if a.ndim == 3 else _DIMS[kind]


def _mm3_raw(a, b, kind=_NN):
    ah, bh = a.astype(BF16), b.astype(BF16)
    al, bl = (a - ah.astype(F32)).astype(BF16), (b - bh.astype(F32)).astype(BF16)
    d = functools.partial(lax.dot_general, dimension_numbers=_dims(a, kind), preferred_element_type=F32)
    return d(ah, bh) + (d(ah, bl) + d(al, bh))


@jax.custom_vjp
def _mm3(a, b):
    return _mm3_raw(a, b)


def _mm3_fwd(a, b):
    return _mm3_raw(a, b), (a, b)


def _mm3_bwd(res, g):
    a, b = res
    return _mm3_raw(g, b, _NT), _mm3_raw(a, g, _TN)


_mm3.defvjp(_mm3_fwd, _mm3_bwd)


def _bdot_raw(a, b, kind):
    return lax.dot_general(a.astype(BF16), b.astype(BF16), _dims(a, kind), preferred_element_type=F32)


@functools.partial(jax.custom_vjp, nondiff_argnums=(2,))
def _bdot(a, b, kind=_NN):
    return _bdot_raw(a, b, kind)


def _bdot_fwd(a, b, kind):
    return _bdot_raw(a, b, kind), (a, b)


def _bdot_bwd(kind, res, g):
    a, b = res
    if kind == "nn":
        return _bdot_raw(g, b, "nt"), _bdot_raw(a, g, "tn")
    if kind == "nt":
        return _bdot_raw(g, b, "nn"), _bdot_raw(g, a, "tn")
    return _bdot_raw(b, g, "nt"), _bdot_raw(a, g, "nn")


_bdot.defvjp(_bdot_fwd, _bdot_bwd)


def _chunk_masks(rev):
    C = DN_CHUNK
    ii = lax.broadcasted_iota(jnp.int32, (C, C), 0)
    jj = lax.broadcasted_iota(jnp.int32, (C, C), 1)
    diff = jnp.where(rev, jj - ii, ii - jj)
    incl = diff >= 0
    strict = diff > 0
    rowsel = (lax.broadcasted_iota(jnp.int32, (C, 1), 0) == jnp.where(rev, 0, C - 1)).astype(F32)
    return incl, strict, rowsel, (ii == jj).astype(F32)


def _head_stack(ref, n_heads, rows=slice(None)):
    return jnp.stack([ref[rows, h * HEAD:(h + 1) * HEAD] for h in range(n_heads)])


def _cat(parts):
    return jnp.concatenate(parts, axis=0)


def _gate_views(g, gt, be, d, n_heads):
    lane = lax.broadcasted_iota(jnp.int32, (1, HEAD), 1)
    sub = lax.broadcasted_iota(jnp.int32, (HEAD, 1), 0)
    sels = [(lane == d * n_heads + h).astype(F32) for h in range(n_heads)]
    selts = [(sub == d * n_heads + h).astype(F32) for h in range(n_heads)]
    g_col = jnp.stack([jnp.sum(g * s, axis=1, keepdims=True) for s in sels])
    b_col = jnp.stack([jnp.sum(be * s, axis=1, keepdims=True) for s in sels])
    g_row = jnp.stack([jnp.sum(gt * s, axis=0, keepdims=True) for s in selts])
    return g_col, g_row, b_col, sels, selts


def _chunk_decay(g_col, g_row, incl):
    return jnp.where(incl, jnp.exp(jnp.where(incl, g_col - g_row, 0.0)), 0.0)


def _chunk_lower(k, g_col, g_row, b_col, incl, strict):
    return jnp.where(strict, _bdot(k * b_col, k, _NT) * _chunk_decay(g_col, g_row, incl), 0.0)


def _chunk_inverse(low, eye):
    m = -low
    x = eye + m
    p = m
    for _ in range(int(math.log2(DN_CHUNK)) - 1):
        p = _mm3(p, p)
        x = x + _mm3(x, p)
    return x


def _chunk_step(q, k, v, g_col, g_row, b_col, S, X, incl, rowsel):
    decay = _chunk_decay(g_col, g_row, incl)
    eg = jnp.exp(g_col)
    u = _mm3(X, v * b_col)
    w = _mm3(X, k * (b_col * eg))
    intra = _bdot(q, k, _NT) * decay
    g_last = jnp.sum(g_col * rowsel, axis=1, keepdims=True)
    v_new = u - _bdot(w, S)
    o = _bdot(q * eg, S) + _bdot(intra, v_new)
    S_new = S * jnp.exp(g_last) + _bdot(k * jnp.exp(g_last - g_col), v_new, _TN)
    return o, S_new


def _scan_index(ctx_chunks, n_chunks):
    def idx(d, n):
        return jnp.where(d == 0, n, jnp.where(n < ctx_chunks, ctx_chunks - 1 - n, n_chunks + ctx_chunks - 1 - n))
    return idx


def _cumsum_mats(rev):
    C = DN_CHUNK
    ii = lax.broadcasted_iota(jnp.int32, (C, C), 0)
    jj = lax.broadcasted_iota(jnp.int32, (C, C), 1)
    return jnp.where(jnp.where(rev, jj - ii, ii - jj) >= 0, 1.0, 0.0).astype(F32)


def _hosted(comm, n_in, n_out, refs):
    n_ci, n_co = (len(comm.ins), len(comm.out_shapes)) if comm is not None else (0, 0)
    ins, cin = refs[:n_in], refs[n_in:n_in + n_ci]
    outs, cout = refs[n_in + n_ci:n_in + n_ci + n_out], refs[n_in + n_ci + n_out:n_in + n_ci + n_out + n_co]
    rest = refs[n_in + n_ci + n_out + n_co:]
    n_sem = 3 if comm is not None else 0
    return ins, outs, rest[:len(rest) - n_sem], (cin, cout, rest[len(rest) - n_sem:])


def _hosted_start(comm, cref, first):
    if comm is not None:
        @pl.when(first)
        def _():
            for cp in comm.copies(*cref):
                cp.start()


def _hosted_wait(comm, cref, last):
    if comm is not None:
        @pl.when(last)
        def _():
            for cp in comm.copies(*cref):
                cp.wait()


def _gdn_scan(name, q, k, v, la, be, n_heads, ctx, comm=None):
    T, Wd = q.shape
    C = DN_CHUNK
    nch = T // C
    npair = nch // 2
    assert nch % 2 == 0 and (ctx // C) % 2 == 0
    pidx = _scan_index(ctx // C // 2, npair)
    hi = lax.Precision.HIGHEST

    def body(*refs):
        (q_ref, k_ref, v_ref, la_ref, be_ref), (o_ref, s_ref, x_ref), (state,), cref = _hosted(comm, 5, 3, refs)
        d, n = pl.program_id(0), pl.program_id(1)
        rev = d == 1
        _hosted_start(comm, cref, (d == 0) & (n == 0))

        @pl.when(n == 0)
        def _():
            state[...] = jnp.zeros_like(state)

        incl, strict, rowsel, eye = _chunk_masks(rev)
        tri = _cumsum_mats(rev)
        offs = [pl.multiple_of(jnp.where(rev, C, 0), C), pl.multiple_of(jnp.where(rev, 0, C), C)]
        views, qkv = [], []
        for off in offs:
            rows = pl.ds(off, C)
            g = jnp.dot(tri, la_ref[rows, :], precision=hi, preferred_element_type=F32)
            views.append(_gate_views(g, g.T, be_ref[rows, :], d, n_heads)[:3])
            qkv.append(tuple(_head_stack(r, n_heads, rows) for r in (q_ref, k_ref, v_ref)))
        low = _chunk_lower(_cat([qkv[0][1], qkv[1][1]]), *[_cat([views[0][j], views[1][j]]) for j in range(3)], incl, strict)
        X = _chunk_inverse(low, eye)
        S = state[...]
        for i, off in enumerate(offs):
            Xi = X[i * n_heads:(i + 1) * n_heads]
            s_ref[i] = S
            x_ref[i] = Xi
            o, S = _chunk_step(*qkv[i], *views[i], S, Xi, incl, rowsel)
            for h in range(n_heads):
                o_ref[pl.ds(off, C), h * HEAD:(h + 1) * HEAD] = o[h]
        state[...] = S
        _hosted_wait(comm, cref, (d == 1) & (n == npair - 1))

    tok = lambda w: pl.BlockSpec((2 * C, w), lambda d, n: (pidx(d, n), 0))
    any_spec = pl.BlockSpec(memory_space=pl.ANY)
    c_ins, c_sds, c_sems = (comm.ins, comm.out_sds(), comm.sem_shapes()) if comm is not None else ([], [], [])
    res = pl.pallas_call(
        body, name=name, grid=(2, npair), in_specs=[tok(Wd), tok(Wd), tok(Wd), tok(HEAD), tok(HEAD)] + [any_spec] * len(c_ins),
        out_specs=[pl.BlockSpec((None, 2 * C, Wd), lambda d, n: (d, pidx(d, n), 0)),
                   pl.BlockSpec((None, 2, n_heads, HEAD, HEAD), lambda d, n: (d, n, 0, 0, 0)),
                   pl.BlockSpec((None, 2, n_heads, C, C), lambda d, n: (d, n, 0, 0, 0))] + [any_spec] * len(c_sds),
        out_shape=[jax.ShapeDtypeStruct((2, T, Wd), F32), jax.ShapeDtypeStruct((2, nch, n_heads, HEAD, HEAD), F32),
                   jax.ShapeDtypeStruct((2, nch, n_heads, C, C), F32)] + c_sds,
        scratch_shapes=[pltpu.VMEM((n_heads, HEAD, HEAD), F32)] + c_sems,
        compiler_params=_cparams(("arbitrary", "arbitrary"), has_side_effects=comm is not None), **_CALL_KW)(q, k, v, la, be, *c_ins)
    return res[0], res[1], res[2], list(res[3:])


def _gdn_scan_bwd(name, q, k, v, la, be, states, invs, do, n_heads, ctx, comm=None):
    T, Wd = q.shape
    C = DN_CHUNK
    nch = T // C
    npair = nch // 2
    pidx = _scan_index(ctx // C // 2, npair)
    hi = lax.Precision.HIGHEST

    def body(*refs):
        ins, outs, (dstate,), cref = _hosted(comm, 8, 5, refs)
        q_ref, k_ref, v_ref, la_ref, be_ref, s_ref, x_ref, do_ref = ins
        dq_ref, dk_ref, dv_ref, dla_ref, dbe_ref = outs
        d, n = pl.program_id(0), pl.program_id(1)
        rev = d == 1
        _hosted_start(comm, cref, (d == 0) & (n == 0))

        @pl.when(n == 0)
        def _():
            dstate[...] = jnp.zeros_like(dstate)

        incl, strict, rowsel, eye = _chunk_masks(rev)
        tri = _cumsum_mats(rev)
        offs = [pl.multiple_of(jnp.where(rev, C, 0), C), pl.multiple_of(jnp.where(rev, 0, C), C)]
        views, qkv, dos = [], [], []
        for off in offs:
            rows = pl.ds(off, C)
            g = jnp.dot(tri, la_ref[rows, :], precision=hi, preferred_element_type=F32)
            g_col, g_row, b_col, sels, selts = _gate_views(g, g.T, be_ref[rows, :], d, n_heads)
            views.append((g_col, g_row, b_col))
            qkv.append(tuple(_head_stack(r, n_heads, rows) for r in (q_ref, k_ref, v_ref)))
            dos.append(_head_stack(do_ref, n_heads, rows))
        step = functools.partial(_chunk_step, incl=incl, rowsel=rowsel)
        dS = dstate[...]
        part = [None, None]
        for i in (1, 0):
            _, vjp_step = jax.vjp(step, *qkv[i], *views[i], s_ref[i], x_ref[i])
            *part[i], dS, dX = vjp_step((dos[i], dS))
            part[i].append(dX)
        dstate[...] = dS
        X, dX = _cat([x_ref[0], x_ref[1]]), _cat([part[0][6], part[1][6]])
        dlow = -_mm3_raw(_mm3_raw(X, dX, _TN), X, _NT)
        low_fn = functools.partial(_chunk_lower, incl=incl, strict=strict)
        _, vjp_low = jax.vjp(low_fn, _cat([qkv[0][1], qkv[1][1]]), *[_cat([views[0][j], views[1][j]]) for j in range(3)])
        dk2, dgc2, dgr2, dbc2 = vjp_low(dlow)
        for i, off in enumerate(offs):
            rows = pl.ds(off, C)
            sl = slice(i * n_heads, (i + 1) * n_heads)
            dq, dk1, dv_, dgc1, dgr1, dbc1, _ = part[i]
            dk, dgc, dgr, dbc = dk1 + dk2[sl], dgc1 + dgc2[sl], dgr1 + dgr2[sl], dbc1 + dbc2[sl]
            dg = jnp.zeros((C, HEAD), F32)
            dgt = jnp.zeros((HEAD, C), F32)
            dbe = jnp.zeros((C, HEAD), F32)
            for h in range(n_heads):
                cols = slice(h * HEAD, (h + 1) * HEAD)
                dq_ref[rows, cols], dk_ref[rows, cols], dv_ref[rows, cols] = dq[h], dk[h], dv_[h]
                dg = dg + dgc[h] * sels[h]
                dgt = dgt + selts[h] * dgr[h]
                dbe = dbe + dbc[h] * sels[h]
            dla_ref[rows, :] = lax.dot_general(tri, dg + dgt.T, _DIMS["tn"], precision=hi, preferred_element_type=F32)
            dbe_ref[rows, :] = dbe
        _hosted_wait(comm, cref, (d == 1) & (n == npair - 1))

    rn = lambda d, n: pidx(d, npair - 1 - n)
    tok = lambda w: pl.BlockSpec((2 * C, w), lambda d, n: (rn(d, n), 0))
    otok = lambda w: pl.BlockSpec((None, 2 * C, w), lambda d, n: (d, rn(d, n), 0))
    any_spec = pl.BlockSpec(memory_space=pl.ANY)
    c_ins, c_sds, c_sems = (comm.ins, comm.out_sds(), comm.sem_shapes()) if comm is not None else ([], [], [])
    res = pl.pallas_call(
        body, name=name, grid=(2, npair),
        in_specs=[tok(Wd), tok(Wd), tok(Wd), tok(HEAD), tok(HEAD),
                  pl.BlockSpec((None, 2, n_heads, HEAD, HEAD), lambda d, n: (d, npair - 1 - n, 0, 0, 0)),
                  pl.BlockSpec((None, 2, n_heads, C, C), lambda d, n: (d, npair - 1 - n, 0, 0, 0)), tok(Wd)] + [any_spec] * len(c_ins),
        out_specs=[otok(Wd), otok(Wd), otok(Wd), otok(HEAD), otok(HEAD)] + [any_spec] * len(c_sds),
        out_shape=[jax.ShapeDtypeStruct((2, T, Wd), F32)] * 3 + [jax.ShapeDtypeStruct((2, T, HEAD), F32)] * 2 + c_sds,
        scratch_shapes=[pltpu.VMEM((n_heads, HEAD, HEAD), F32)] + c_sems,
        compiler_params=_cparams(("arbitrary", "arbitrary"), has_side_effects=comm is not None), **_CALL_KW)(
            q, k, v, la, be, states, invs, do, *c_ins)
    return tuple(res[:5]) + (list(res[5:]),)


def _gated_norm(name, o2, p, zblk, g, n_heads, width):
    _, T, Wd = o2.shape

    def body(o_ref, z_ref, g_ref, y_ref):
        for h in range(n_heads):
            cols = slice(h * HEAD, (h + 1) * HEAD)
            ov = o_ref[0, :, cols] + o_ref[1, :, cols]
            zv = z_ref[:, cols].astype(F32)
            y = ov * lax.rsqrt(jnp.mean(ov * ov, axis=-1, keepdims=True) + EPS) * g_ref[...]
            y_ref[:, cols] = (y * (zv * _sigmoid(zv))).astype(BF16)

    return pl.pallas_call(
        body, name=name, grid=(T // TOK,),
        in_specs=[pl.BlockSpec((2, TOK, Wd), lambda i: (0, i, 0)), pl.BlockSpec((TOK, Wd), lambda i: (i, zblk)),
                  pl.BlockSpec((1, HEAD), lambda i: (0, 0))],
        out_specs=pl.BlockSpec((TOK, Wd), lambda i: (i, 0)), out_shape=jax.ShapeDtypeStruct((T, width), BF16),
        compiler_params=_cparams(("parallel",)), **_CALL_KW)(o2, p, g)


def _gated_norm_bwd(name, o2, p, zblk, g, n_heads, dmix, dblk):
    _, T, Wd = o2.shape

    def body(o_ref, z_ref, g_ref, dy_ref, do_ref, dz_ref, dg_ref):
        i = pl.program_id(0)

        @pl.when(i == 0)
        def _():
            dg_ref[...] = jnp.zeros_like(dg_ref)

        dg = jnp.zeros((1, HEAD), F32)
        for h in range(n_heads):
            cols = slice(h * HEAD, (h + 1) * HEAD)
            ov = o_ref[0, :, cols] + o_ref[1, :, cols]
            zv = z_ref[:, cols].astype(F32)
            dy = dy_ref[:, cols].astype(F32)
            r = lax.rsqrt(jnp.mean(ov * ov, axis=-1, keepdims=True) + EPS)
            on = ov * r
            sg = _sigmoid(zv)
            sz = zv * sg
            dz_ref[:, cols] = (dy * (on * g_ref[...]) * (sg * (1.0 + zv * (1.0 - sg)))).astype(BF16)
            dyn = dy * sz
            dg = dg + jnp.sum(dyn * on, axis=0, keepdims=True)
            u = dyn * g_ref[...]
            do_ref[:, cols] = r * (u - on * jnp.mean(u * on, axis=-1, keepdims=True))
        dg_ref[...] += dg

    row = pl.BlockSpec((TOK, Wd), lambda i: (i, 0))
    one = pl.BlockSpec((1, HEAD), lambda i: (0, 0))
    return pl.pallas_call(
        body, name=name, grid=(T // TOK,),
        in_specs=[pl.BlockSpec((2, TOK, Wd), lambda i: (0, i, 0)), pl.BlockSpec((TOK, Wd), lambda i: (i, zblk)), one,
                  pl.BlockSpec((TOK, Wd), lambda i: (i, dblk))],
        out_specs=[row, pl.BlockSpec((TOK, Wd), lambda i: (i, zblk)), one],
        out_shape=[jax.ShapeDtypeStruct((T, Wd), F32), jax.ShapeDtypeStruct(p.shape, BF16), jax.ShapeDtypeStruct((1, HEAD), F32)],
        compiler_params=_cparams(("arbitrary",)), **_CALL_KW)(o2, p, g, dmix)


class _Dims:
    def __init__(self, D, seq, ctx, ffn):
        self.D, self.seq, self.ctx, self.ffn = D, seq, ctx, ffn
        self.T = seq + ctx
        self.t_pad = -(-(self.T + 128) // TOK) * TOK
        self.ctx_tiles = ctx // TOK
        nh = D // HEAD
        self.swa_h, self.kv_h, self.dn_h = nh // 4, nh // 8, nh // 2
        self.na_h = nh - self.swa_h - self.dn_h
        self.swa_q, self.swa_kv, self.Wd, self.na = self.swa_h * HEAD, self.kv_h * HEAD, self.dn_h * HEAD, self.na_h * HEAD
        self.n_ab = 4 * self.dn_h
        self.o_ab = self.swa_q + 2 * self.swa_kv + 4 * self.Wd
        self.n_in = self.o_ab + self.n_ab + 3 * self.na
        self.n_main = self.n_in - self.n_ab
        assert ctx % TOK == 0 and seq % TOK == 0 and self.swa_q == 2 * self.swa_kv == self.na and 2 * self.na == self.Wd


def _rope_tables(dm):
    t = jnp.arange(dm.t_pad, dtype=jnp.int32) - dm.ctx
    lat = (t >= 0) & (t < dm.seq)
    row = (t // GRID_W).astype(F32)
    col = (t % GRID_W).astype(F32)
    n_freq = HEAD // 4
    inv = ROPE_THETA ** (-jnp.arange(n_freq, dtype=F32) / n_freq)
    ang = jnp.concatenate([row[:, None] * inv, row[:, None] * inv, col[:, None] * inv, col[:, None] * inv], axis=-1)
    ang = jnp.where(lat[:, None], ang, 0.0)
    return jnp.cos(ang), jnp.sin(ang)


def _bias_indices():
    o = np.arange(NA_KH)[:, None]
    jr = np.arange(NA_KH)[None, :]
    idx_r = jr - o + (NA_KH - 1)
    cols = np.arange(GRID_W)
    idx_c = np.clip(cols[None, :] - cols[:, None], -(NA_KW - 1), NA_KW - 1) + (NA_KW - 1)
    return idx_r, idx_c


def _bias_onehots():
    idx_r, idx_c = _bias_indices()
    sel_r = (idx_r.reshape(-1)[:, None] == np.arange(2 * NA_KH)[None, :]).astype(np.float32)
    sel_c = (np.arange(HEAD)[:, None] == idx_c.reshape(-1)[None, :]).astype(np.float32)
    return jnp.asarray(sel_r), jnp.asarray(sel_c)


def _rpb_pad(rpb):
    return jnp.pad(rpb, ((0, 0), (0, 2 * NA_KH - rpb.shape[1]), (0, HEAD - rpb.shape[2])))


def _bias_table(name, rpb):
    H = rpb.shape[0]
    sel_r, sel_c = _bias_onehots()
    hi = lax.Precision.HIGHEST

    def body(r_ref, sr_ref, sc_ref, o_ref):
        t = jnp.dot(sr_ref[...], r_ref[...], precision=hi, preferred_element_type=F32)
        o_ref[...] = jnp.dot(t, sc_ref[...], precision=hi, preferred_element_type=F32)

    n_r, n_c = sel_r.shape[0], sel_c.shape[1]
    tab = pl.pallas_call(
        body, name=name, grid=(H,),
        in_specs=[pl.BlockSpec((None, 2 * NA_KH, HEAD), lambda h: (h, 0, 0)), pl.BlockSpec(sel_r.shape, lambda h: (0, 0)),
                  pl.BlockSpec(sel_c.shape, lambda h: (0, 0))],
        out_specs=pl.BlockSpec((None, n_r, n_c), lambda h: (h, 0, 0)), out_shape=jax.ShapeDtypeStruct((H, n_r, n_c), F32),
        compiler_params=_cparams(("parallel",)), **_CALL_KW)(_rpb_pad(rpb), sel_r, sel_c)
    tab = tab.reshape(H, NA_KH, NA_KH, GRID_W, GRID_W).transpose(0, 1, 3, 2, 4)
    return tab.reshape(H, NA_KH, GRID_W, NA_KH * GRID_W)


def _bias_table_bwd(name, dbias, rpb_shape):
    H = dbias.shape[0]
    sel_r, sel_c = _bias_onehots()
    hi = lax.Precision.HIGHEST
    d = dbias.reshape(H, NA_KH, GRID_W, NA_KH, GRID_W).transpose(0, 1, 3, 2, 4).reshape(H, NA_KH * NA_KH, GRID_W * GRID_W)

    def body(d_ref, sr_ref, sc_ref, o_ref):
        dt = lax.dot_general(d_ref[...], sc_ref[...], _DIMS["nt"], precision=hi, preferred_element_type=F32)
        o_ref[...] = lax.dot_general(sr_ref[...], dt, _DIMS["tn"], precision=hi, preferred_element_type=F32)

    out = pl.pallas_call(
        body, name=name, grid=(H,),
        in_specs=[pl.BlockSpec((None,) + d.shape[1:], lambda h: (h, 0, 0)), pl.BlockSpec(sel_r.shape, lambda h: (0, 0)),
                  pl.BlockSpec(sel_c.shape, lambda h: (0, 0))],
        out_specs=pl.BlockSpec((None, 2 * NA_KH, HEAD), lambda h: (h, 0, 0)),
        out_shape=jax.ShapeDtypeStruct((H, 2 * NA_KH, HEAD), F32),
        compiler_params=_cparams(("parallel",)), **_CALL_KW)(d, sel_r, sel_c)
    return out[:, :rpb_shape[1], :rpb_shape[2]]


def _lane_row(v):
    v = v.reshape(-1)
    return jnp.pad(v, (0, HEAD - v.shape[0])).reshape(1, HEAD)


def _chunks_of(l, chip, gathered, own):
    return [jnp.where(chip == k, own[l], gathered[k]) for k in range(N_CHIPS)]


def _weights_in(dm, l, chip, g_in, own_in):
    w_in = jnp.concatenate(_chunks_of(l, chip, g_in, own_in), axis=1)
    w_main = jnp.concatenate([w_in[:, :dm.o_ab], w_in[:, dm.o_ab + dm.n_ab:]], axis=1)
    w_ab = jnp.pad(w_in[:, dm.o_ab:dm.o_ab + dm.n_ab], ((0, 0), (0, HEAD - dm.n_ab)))
    return dict(main=w_main, ab=w_ab)


def _weights_rest(dm, l, chip, gathered, own):
    g_out, g_gate, g_up, g_down = [_chunks_of(l, chip, g, o) for g, o in zip(gathered, own)]
    w_out = jnp.stack(g_out).reshape(dm.D, dm.D)
    w_out = jnp.concatenate([w_out[dm.swa_q:dm.swa_q + dm.Wd], w_out[:dm.swa_q], w_out[dm.swa_q + dm.Wd:]], axis=0)
    return dict(out=w_out, gu=jnp.concatenate(g_gate + g_up, axis=1), down=jnp.stack(g_down).reshape(dm.ffn, dm.D))


def _layer_fwd(dm, x, W, sp, modv, cos, sin, hosts=None, late_weights=None):
    ct = dm.ctx_tiles
    hosts = hosts or {}
    got = {}
    h = _norm_mod("norm1", x, sp["norm1_g"], modv, 0, ct)
    P = _matmul("in_proj", h, W["main"], "nn")
    Pab = _matmul("in_proj_ab", h, W["ab"], "nn", tn=HEAD)
    one = jnp.ones((1, HEAD), F32)
    qa = _head_prep("swa_q_prep", P, 0, dm.swa_h, sp["swa_q_g"], cos, sin, dm.t_pad, True, True)
    ka = _head_prep("swa_k_prep", P, 2, dm.kv_h, sp["swa_k_g"], cos, sin, dm.t_pad, True, True)
    va = _head_prep("swa_v_prep", P, 3, dm.kv_h, one, cos, sin, dm.t_pad, False, False)
    qn = _head_prep("na_q_prep", P, 10, dm.na_h, sp["na_q_g"], cos, sin, dm.t_pad, True, False)
    kn = _head_prep("na_k_prep", P, 11, dm.na_h, sp["na_k_g"], cos, sin, dm.t_pad, True, False)
    vn = _head_prep("na_v_prep", P, 12, dm.na_h, one, cos, sin, dm.t_pad, False, False)
    no_sink = jnp.full((dm.na_h,), NEG, F32)
    bias = _bias_table("na_bias", sp["na_rpb"])
    pre = _conv_fwd("dn_conv", P, 1, sp["dn_conv_w"], ct)
    a_row, dt_row = _lane_row(sp["dn_A_log"]), _lane_row(sp["dn_dt_bias"])
    qh, kh, vh, la, be = _gdn_point("dn_point", pre, Pab, a_row, dt_row, dm.dn_h)
    o2, states, invs, got["scan"] = _gdn_scan("dn_scan", qh, kh, vh, la, be, dm.dn_h, dm.ctx, hosts.get("scan"))
    if late_weights is not None:
        W = dict(W, **late_weights(got["scan"]))
    mix = _gated_norm("dn_out_norm", o2, P, 4, sp["dn_out_g"], dm.dn_h, dm.D)
    mix, _ = _attn_fwd("swa_fwd", "swa", qa, ka, va, sp["swa_sink"], None, dm.ctx, dm.seq, mix, dm.Wd // HEAD)
    mix, got["na"] = _attn_fwd("na_fwd", "na", qn, kn, vn, no_sink, bias, dm.ctx, dm.seq, mix, (dm.Wd + dm.swa_q) // HEAD,
                               hosts.get("na"))
    ao = _matmul("out_proj", mix, W["out"], "nn")
    x1 = _resid_gate("resid1", x, ao, modv, 2, ct)
    h2 = _norm_mod("norm2", x1, sp["norm2_g"], modv, 3, ct)
    gu = _matmul("ffn_gate_up", h2, W["gu"], "nn", comm=hosts.get("gu"))
    if hosts.get("gu") is not None:
        gu, got["gu"] = gu
    act = _swiglu("ffn_act", gu)
    fo = _matmul("ffn_down", act, W["down"], "nn", comm=hosts.get("down"))
    if hosts.get("down") is not None:
        fo, got["down"] = fo
    x2 = _resid_gate("resid2", x1, fo, modv, 5, ct)
    res = dict(x=x, h=h, P=P, Pab=Pab, qa=qa, ka=ka, va=va, qn=qn, kn=kn, vn=vn, bias=bias, no_sink=no_sink, pre=pre,
               a_row=a_row, dt_row=dt_row, qh=qh, kh=kh, vh=vh, la=la, be=be, o2=o2, states=states, invs=invs, mix=mix,
               ao=ao, x1=x1, h2=h2, gu=gu, act=act, fo=fo)
    return x2, res, got, W


def _layer_bwd(dm, dx2, W, sp, modv, cos, sin, r, host, host_in):
    ct = dm.ctx_tiles
    T, D = dm.T, dm.D
    one = jnp.ones((1, HEAD), F32)
    dfo, dgate2 = _resid_gate_bwd("resid2_bwd", dx2, r["fo"], modv, 5, ct)
    dact = _matmul("ffn_down_dx", dfo, W["down"], "nt")
    dw_down = _matmul("ffn_down_dw", r["act"], dfo, "tn")
    dgu = _swiglu_bwd("ffn_act_bwd", r["gu"], dact)
    dh2 = _matmul("ffn_gate_up_dx", dgu, W["gu"], "nt")
    dw_gu = _matmul("ffn_gate_up_dw", r["h2"], dgu, "tn")
    zero = jnp.zeros((T, D), F32)
    dx1, dn2g, dsh2, dsc2 = _norm_mod_bwd("norm2_bwd", r["x1"], sp["norm2_g"], modv, 3, dh2, zero, dx2, ct)
    dao, dgate1 = _resid_gate_bwd("resid1_bwd", dx1, r["ao"], modv, 2, ct)
    dmix = _matmul("out_proj_dx", dao, W["out"], "nt")
    dw_out = _matmul("out_proj_dw", r["mix"], dao, "tn")
    do_, dP, d_out_g = _gated_norm_bwd("dn_out_norm_bwd", r["o2"], r["P"], 4, sp["dn_out_g"], dm.dn_h, dmix, 0)
    comm = host(dict(out=dw_out, gu=dw_gu, down=dw_down)) if host is not None else None
    dq2, dk2, dv2, dla2, dbe2, hosted = _gdn_scan_bwd("dn_scan_bwd", r["qh"], r["kh"], r["vh"], r["la"], r["be"], r["states"],
                                                      r["invs"], do_, dm.dn_h, dm.ctx, comm)
    dpre, dPab, d_alog, d_dtb = _gdn_point_bwd("dn_point_bwd", r["pre"], r["Pab"], r["a_row"], r["dt_row"], dm.dn_h,
                                               dq2, dk2, dv2, dla2, dbe2)
    dP, d_conv = _conv_bwd("dn_conv_bwd", r["P"], 1, sp["dn_conv_w"], dpre, ct, dP)
    dqa, dka, dva, dsink, _ = _attn_bwd("swa_bwd", "swa", r["qa"], r["ka"], r["va"], sp["swa_sink"], None, dmix,
                                        dm.Wd // HEAD, dm.ctx, dm.seq)
    dP, d_swa_q_g = _head_prep_bwd("swa_q_prep_bwd", r["P"], 0, dm.swa_h, sp["swa_q_g"], cos, sin, dqa, True, True, dP)
    dP, d_swa_k_g = _head_prep_bwd("swa_k_prep_bwd", r["P"], 2, dm.kv_h, sp["swa_k_g"], cos, sin, dka, True, True, dP)
    dP, _ = _head_prep_bwd("swa_v_prep_bwd", r["P"], 3, dm.kv_h, one, cos, sin, dva, False, False, dP)
    dqn, dkn, dvn, _, dbias = _attn_bwd("na_bwd", "na", r["qn"], r["kn"], r["vn"], r["no_sink"], r["bias"], dmix,
                                        (dm.Wd + dm.swa_q) // HEAD, dm.ctx, dm.seq)
    dP, d_na_q_g = _head_prep_bwd("na_q_prep_bwd", r["P"], 10, dm.na_h, sp["na_q_g"], cos, sin, dqn, True, False, dP)
    dP, d_na_k_g = _head_prep_bwd("na_k_prep_bwd", r["P"], 11, dm.na_h, sp["na_k_g"], cos, sin, dkn, True, False, dP)
    dP, _ = _head_prep_bwd("na_v_prep_bwd", r["P"], 12, dm.na_h, one, cos, sin, dvn, False, False, dP)
    d_rpb = _bias_table_bwd("na_bias_bwd", dbias, sp["na_rpb"].shape)
    dw_main = _matmul("in_proj_dw", r["h"], dP, "tn")
    dw_ab = _matmul("in_proj_ab_dw", r["h"], dPab, "tn", tn=HEAD)
    dh, hosted_in = _matmul("in_proj_dx", dP, W["main"], "nt", comm=host_in(dict(main=dw_main, ab=dw_ab)))
    dh_b = _matmul("in_proj_ab_dx", dPab, W["ab"], "nt")
    dx, dn1g, dsh1, dsc1 = _norm_mod_bwd("norm1_bwd", r["x"], sp["norm1_g"], modv, 0, dh, dh_b, dx1, ct)
    dmodv = jnp.concatenate([dsh1, dsc1, dgate1, dsh2, dsc2, dgate2], axis=1)
    big = dict(main=dw_main, ab=dw_ab, out=dw_out, gu=dw_gu, down=dw_down)
    small = dict(norm1_g=dn1g[0], norm2_g=dn2g[0], swa_q_g=d_swa_q_g[0], swa_k_g=d_swa_k_g[0], swa_sink=dsink[:, 0, 0],
                 dn_conv_w=d_conv[:DN_CONV], dn_A_log=d_alog[0, :2 * dm.dn_h].reshape(2, dm.dn_h),
                 dn_dt_bias=d_dtb[0, :2 * dm.dn_h].reshape(2, dm.dn_h), dn_out_g=d_out_g[0], na_q_g=d_na_q_g[0],
                 na_k_g=d_na_k_g[0], na_rpb=d_rpb)
    return dx, big, small, dmodv, hosted, hosted_in


def _cols(w):
    return w.reshape(w.shape[0], N_CHIPS, -1).transpose(1, 0, 2)


def _rows(w):
    return w.reshape(N_CHIPS, -1, w.shape[1])


def _chunks_in(dm, bigs):
    g = [_cols(jnp.concatenate([b["main"][:, :dm.o_ab], b["ab"][:, :dm.n_ab], b["main"][:, dm.o_ab:]], axis=1)) for b in bigs]
    return jnp.stack(g, axis=1).astype(BF16)


def _chunks_rest(dm, bigs):
    g_out = [_rows(jnp.concatenate([b["out"][dm.Wd:dm.Wd + dm.swa_q], b["out"][:dm.Wd], b["out"][dm.Wd + dm.swa_q:]], axis=0))
             for b in bigs]
    g_gate = [_cols(b["gu"][:, :dm.ffn]) for b in bigs]
    g_up = [_cols(b["gu"][:, dm.ffn:]) for b in bigs]
    g_down = [_rows(b["down"]) for b in bigs]
    return [jnp.stack(g, axis=1).astype(BF16) for g in (g_out, g_gate, g_up, g_down)]


SMALL = ("norm1_g", "norm2_g", "swa_q_g", "swa_k_g", "swa_sink", "dn_conv_w", "dn_A_log", "dn_dt_bias", "dn_out_g",
         "na_q_g", "na_k_g", "na_rpb")


def _pack(arrs):
    flat = jnp.concatenate([a.reshape(-1).astype(F32) for a in arrs])
    n = flat.shape[0]
    rows = -(-n // (8 * HEAD)) * 8
    return jnp.pad(flat, (0, rows * HEAD - n)).reshape(rows, HEAD)


def _unpack(packed, like):
    flat = packed.reshape(-1)
    out, o = [], 0
    for a in like:
        out.append(flat[o:o + a.size].reshape(a.shape))
        o += a.size
    return out


def _sum_devices(name, g, which):
    _, R, _ = g.shape

    def body(g_ref, o_ref):
        acc = g_ref[which[0]]
        for b in which[1:]:
            acc = acc + g_ref[b]
        o_ref[...] = acc

    return pl.pallas_call(
        body, name=name, grid=(R // 8,), in_specs=[pl.BlockSpec((N_DEV, 8, HEAD), lambda i: (0, i, 0))],
        out_specs=pl.BlockSpec((8, HEAD), lambda i: (i, 0)), out_shape=jax.ShapeDtypeStruct((R, HEAD), F32),
        compiler_params=_cparams(("parallel",)), **_CALL_KW)(g)


def _silu_rows(name, c_rows):
    return _elementwise(name, lambda c: (c * _sigmoid(c),), [c_rows], [BF16])[0]


def _ada_cotangent(name, dm_all, b_ada_shape):
    _, L, _, N6 = dm_all.shape
    tn = _pick(N6, (1024, 512, 256, 128))

    def body(d_ref, o_ref, b_ref):
        csum = d_ref[0, 0:1, :]
        for b in range(1, N_DEV):
            csum = csum + d_ref[b, 0:1, :]
        tot = csum
        for b in range(N_DEV):
            o_ref[b:b + 1, :] = d_ref[b, 1:2, :]
            tot = tot + d_ref[b, 1:2, :]
        first = lax.broadcasted_iota(jnp.int32, (8, tn), 0) == 0
        o_ref[N_DEV:, :] = jnp.where(first, jnp.broadcast_to(csum, (8, tn)), 0.0)
        b_ref[...] = jnp.broadcast_to(tot, (8, tn))

    return pl.pallas_call(
        body, name=name, grid=(L, N6 // tn), in_specs=[pl.BlockSpec((N_DEV, None, 2, tn), lambda l, j: (0, l, 0, j))],
        out_specs=[pl.BlockSpec((None, 16, tn), lambda l, j: (l, 0, j)), pl.BlockSpec((None, 8, tn), lambda l, j: (l, 0, j))],
        out_shape=[jax.ShapeDtypeStruct((L, 16, N6), F32), jax.ShapeDtypeStruct((L, 8, N6), F32)],
        compiler_params=_cparams(("parallel", "parallel")), **_CALL_KW)(dm_all)


def kernel(x, c, ctx, c_ctx, w_ada, b_ada, norm1_g, norm2_g, w_in, swa_q_g, swa_k_g, swa_sink, dn_conv_w, dn_A_log, dn_dt_bias, dn_out_g, na_q_g, na_k_g, na_rpb, w_out, w_gate, w_up, w_down, loss_target, m_c_ctx, m_w_ada, m_b_ada, m_norm1_g, m_norm2_g, m_w_in, m_swa_q_g, m_swa_k_g, m_swa_sink, m_dn_conv_w, m_dn_A_log, m_dn_dt_bias, m_dn_out_g, m_na_q_g, m_na_k_g, m_na_rpb, m_w_out, m_w_gate, m_w_up, m_w_down, v_c_ctx, v_w_ada, v_b_ada, v_norm1_g, v_norm2_g, v_w_in, v_swa_q_g, v_swa_k_g, v_swa_sink, v_dn_conv_w, v_dn_A_log, v_dn_dt_bias, v_dn_out_g, v_na_q_g, v_na_k_g, v_na_rpb, v_w_out, v_w_gate, v_w_up, v_w_down):
    L = w_in.shape[0]
    D, seq, n_ctx = x.shape[-1], x.shape[1], ctx.shape[1]
    dm = _Dims(D, seq, n_ctx, w_gate.shape[-1] * N_CHIPS)
    xi, yi, ci = _axes()
    chip = 2 * xi + yi
    dev = 4 * xi + 2 * yi + ci
    n6 = 6 * D
    n6s = n6 // N_CHIPS

    shards = [_elementwise(f"cast_{n}", lambda w: (w,), [w], [BF16])[0]
              for n, w in (("w_in", w_in), ("w_out", w_out), ("w_gate", w_gate), ("w_up", w_up), ("w_down", w_down))]
    assert L == 2
    g_in0 = _gather_d2d("gather_w0a_d2d", _exchange("gather_w0a_ici", _gather_ici_comm(shards[:1], 0)))
    Ws = [_weights_in(dm, 0, chip, g_in0[0], shards[0]), None]
    conv_all = _allgather_small("gather_conv_w", _pack([dn_conv_w]))
    conv_full = jnp.concatenate([_unpack(conv_all[2 * k], [dn_conv_w])[0] for k in range(N_CHIPS)], axis=-1)

    c_all = _allgather_small("gather_c", _pack([c]))
    c_rows = jnp.concatenate([c_all[:, :D // HEAD].reshape(N_DEV, D), c_ctx[None], jnp.zeros((16 - N_DEV - 1, D), F32)], axis=0)
    a_rows = _silu_rows("ada_silu", c_rows)
    b_sh = lax.dynamic_slice_in_dim(b_ada, chip * n6s, n6s, axis=1)
    mod_sh = [_matmul(f"ada_mod{l}", a_rows, w_ada[l], "nn", tm=16) for l in range(L)]
    mod_all = _allgather_small("gather_mod", _pack(mod_sh))
    mods = []
    for l in range(L):
        per_chip = [_unpack(mod_all[2 * k], mod_sh)[l] for k in range(N_CHIPS)]
        mods.append(jnp.concatenate(per_chip, axis=1))
    modvs = []
    for l in range(L):
        rows = jnp.stack([mods[l][N_DEV], lax.dynamic_index_in_dim(mods[l], dev, 0, keepdims=False)])
        modvs.append(_elementwise(f"ada_bias{l}", lambda m, b: (m + b,), [rows, jnp.broadcast_to(b_ada[l][None], (2, n6))], [F32])[0]
                     .reshape(2, 6, D))

    cos, sin = _rope_tables(dm)
    sps = [dict(norm1_g=norm1_g[l][None], norm2_g=norm2_g[l][None], swa_q_g=swa_q_g[l][None], swa_k_g=swa_k_g[l][None],
                swa_sink=swa_sink[l], dn_conv_w=conv_full[l], dn_A_log=dn_A_log[l], dn_dt_bias=dn_dt_bias[l],
                dn_out_g=dn_out_g[l][None], na_q_g=na_q_g[l][None], na_k_g=na_k_g[l][None], na_rpb=na_rpb[l]) for l in range(L)]
    xs = jnp.concatenate([ctx[0], x[0]], axis=0)
    ress = [None] * L
    w_in_s, w_out_s, w_gate_s, w_up_s, w_down_s = shards
    hosts0 = dict(scan=_gather_ici_comm(shards[1:], 0),
                  na=_gather_ici_comm([w_in_s], 1), gu=_gather_ici_comm([w_gate_s, w_up_s], 1),
                  down=_gather_ici_comm([w_out_s, w_down_s], 1))

    def late0(arrived):
        return _weights_rest(dm, 0, chip, _gather_d2d("gather_w0b_d2d", arrived), shards[1:])

    xs, ress[0], got0, Ws[0] = _layer_fwd(dm, xs, Ws[0], sps[0], modvs[0], cos, sin, hosts0, late0)
    g1 = _gather_d2d("gather_w1_d2d", got0["na"] + [got0["down"][0]] + got0["gu"] + [got0["down"][1]])
    Ws[1] = dict(_weights_in(dm, 1, chip, g1[0], w_in_s), **_weights_rest(dm, 1, chip, g1[1:], shards[1:]))
    xs, ress[1], _, _ = _layer_fwd(dm, xs, Ws[1], sps[1], modvs[1], cos, sin)
    loss_blk, dxs = _loss_and_grad("loss", xs, loss_target[0], dm.ctx_tiles)
    loss = lax.psum(loss_blk[0, 0], ("x", "y", "c"))

    bigs, smalls, dmodvs = [None] * L, [None] * L, [None] * L
    pairs, reduced = {}, {}

    def hosts_of(l):
        def host(big):
            pairs["r", l] = _reduce_pre(f"rest{l}", _chunks_rest(dm, [big]))
            return _reduce_ici_comm(pairs["r", l])

        def host_in(big):
            pairs["i", l] = _reduce_pre(f"in{l}", [_chunks_in(dm, [big])])
            return _reduce_ici_comm(pairs["i", l])

        return host, host_in

    for l in reversed(range(L)):
        dxs, bigs[l], smalls[l], dmodvs[l], got_r, got_i = _layer_bwd(dm, dxs, Ws[l], sps[l], modvs[l], cos, sin, ress[l], *hosts_of(l))
        reduced["r", l] = _reduce_post(f"rest{l}", pairs["r", l], got_r)
        reduced["i", l] = _reduce_post(f"in{l}", pairs["i", l], got_i)
    g_in = jnp.concatenate([reduced["i", l][0] for l in range(L)], axis=0)
    g_out, g_gate, g_up, g_down = [jnp.concatenate([reduced["r", l][j] for l in range(L)], axis=0) for j in range(4)]
    grad_x = dxs[n_ctx:][None]

    dm_mine = jnp.stack([d.reshape(2, n6) for d in dmodvs])
    dm_all = _allgather_small("gather_dmod", _pack([dm_mine]))
    dm_all = jnp.stack([_unpack(dm_all[b], [dm_mine])[0] for b in range(N_DEV)])
    dm_rows, d_b_ada = _ada_cotangent("ada_cot", dm_all, b_ada.shape)
    dm_sh = lax.dynamic_slice_in_dim(dm_rows, chip * n6s, n6s, axis=2).astype(BF16)
    g_w_ada = jnp.stack([_matmul(f"ada_dw{l}", a_rows, dm_sh[l], "tn") for l in range(L)])
    dc_part = [_matmul(f"ada_dc{l}", dm_sh[l], w_ada[l], "nt", tm=16) for l in range(L)]
    dc_mine = dc_part[0][N_DEV]
    for l in range(1, L):
        dc_mine = dc_mine + dc_part[l][N_DEV]

    small_list = [jnp.stack([smalls[l][n] for l in range(L)]) for n in SMALL]
    sm_all = _allgather_small("gather_small", _pack(small_list + [dc_mine]))
    sm_sum = _sum_devices("sum_small", sm_all, tuple(range(N_DEV)))
    dc_sum = _sum_devices("sum_dc", sm_all, tuple(range(0, N_DEV, 2)))
    g_small = dict(zip(SMALL, _unpack(sm_sum, small_list)))
    dcs = _unpack(dc_sum, small_list + [dc_mine])[-1]
    def silu_bwd(d, cc):
        s = _sigmoid(cc)
        return (d * (s * (1.0 + cc * (1.0 - s))),)

    g_c_ctx = _elementwise("c_ctx_silu_bwd", silu_bwd, [dcs.reshape(-1, HEAD), c_ctx.reshape(-1, HEAD)], [F32])[0]
    wd3 = dn_conv_w.shape[-1]
    g_small["dn_conv_w"] = lax.dynamic_slice_in_dim(g_small["dn_conv_w"], chip * wd3, wd3, axis=2)

    grads = dict(g_small, c_ctx=g_c_ctx, w_ada=g_w_ada, b_ada=d_b_ada[:, 0], w_in=g_in, w_out=g_out, w_gate=g_gate, w_up=g_up,
                 w_down=g_down)
    weights = dict(c_ctx=c_ctx, w_ada=w_ada, b_ada=b_ada, norm1_g=norm1_g, norm2_g=norm2_g, w_in=w_in, swa_q_g=swa_q_g,
                   swa_k_g=swa_k_g, swa_sink=swa_sink, dn_conv_w=dn_conv_w, dn_A_log=dn_A_log, dn_dt_bias=dn_dt_bias,
                   dn_out_g=dn_out_g, na_q_g=na_q_g, na_k_g=na_k_g, na_rpb=na_rpb, w_out=w_out, w_gate=w_gate, w_up=w_up,
                   w_down=w_down)
    ms = dict(c_ctx=m_c_ctx, w_ada=m_w_ada, b_ada=m_b_ada, norm1_g=m_norm1_g, norm2_g=m_norm2_g, w_in=m_w_in, swa_q_g=m_swa_q_g,
              swa_k_g=m_swa_k_g, swa_sink=m_swa_sink, dn_conv_w=m_dn_conv_w, dn_A_log=m_dn_A_log, dn_dt_bias=m_dn_dt_bias,
              dn_out_g=m_dn_out_g, na_q_g=m_na_q_g, na_k_g=m_na_k_g, na_rpb=m_na_rpb, w_out=m_w_out, w_gate=m_w_gate, w_up=m_w_up,
              w_down=m_w_down)
    vs = dict(c_ctx=v_c_ctx, w_ada=v_w_ada, b_ada=v_b_ada, norm1_g=v_norm1_g, norm2_g=v_norm2_g, w_in=v_w_in, swa_q_g=v_swa_q_g,
              swa_k_g=v_swa_k_g, swa_sink=v_swa_sink, dn_conv_w=v_dn_conv_w, dn_A_log=v_dn_A_log, dn_dt_bias=v_dn_dt_bias,
              dn_out_g=v_dn_out_g, na_q_g=v_na_q_g, na_k_g=v_na_k_g, na_rpb=v_na_rpb, w_out=v_w_out, w_gate=v_w_gate, w_up=v_w_up,
              w_down=v_w_down)
    order = ("c_ctx", "w_ada", "b_ada", "norm1_g", "norm2_g", "w_in", "swa_q_g", "swa_k_g", "swa_sink", "dn_conv_w", "dn_A_log",
             "dn_dt_bias", "dn_out_g", "na_q_g", "na_k_g", "na_rpb", "w_out", "w_gate", "w_up", "w_down")
    big_names = ("w_ada", "w_in", "w_out", "w_gate", "w_up", "w_down")
    grads = {n: grads[n].reshape(weights[n].shape) for n in order}
    delta, new_m, new_v = {}, {}, {}
    for n in big_names:
        delta[n], new_m[n], new_v[n] = _adamw(f"adamw_{n}", weights[n], grads[n], ms[n], vs[n])
    small_names = [n for n in order if n not in big_names]
    packed = [_pack([d[n] for n in small_names]) for d in (weights, grads, ms, vs)]
    outs = _adamw("adamw_small", *packed)
    like = [weights[n] for n in small_names]
    for d, o in zip((delta, new_m, new_v), outs):
        d.update(dict(zip(small_names, _unpack(o, like))))
    return (loss, grad_x, *[grads[n] for n in order], *[delta[n] for n in order], *[new_m[n] for n in order],
            *[new_v[n] for n in order])
```

```python
import functools
import math

import jax
import jax.numpy as jnp
import numpy as np
from jax import lax
from jax.experimental import pallas as pl
from jax.experimental.pallas import tpu as pltpu

F32, BF16 = jnp.float32, jnp.bfloat16
MESH = pl.DeviceIdType.MESH

GRID_W = 64
HEAD = 128
SWA_WINDOW = 128
DN_CONV = 5
DN_CHUNK = 64
NA_KH, NA_KW = 8, 16
ROPE_THETA = 10000.0
EPS = 1e-6
ADAM_LR, ADAM_B1, ADAM_B2, ADAM_EPS, ADAM_WD, ADAM_STEP = 0.001, 0.9, 0.999, 1e-08, 0.01, 10
N_CHIPS = 4
N_DEV = 8
TOK = 256
VMEM_LIMIT = 56 * 2 ** 20
MATMUL_VMEM = 40 * 2 ** 20

_CALL_KW = {}


def _cparams(sem=None, **kw):
    if sem is not None:
        kw["dimension_semantics"] = sem
    return pltpu.CompilerParams(vmem_limit_bytes=VMEM_LIMIT, **kw)


def _pick(n, cands):
    for cnd in cands:
        if n % cnd == 0:
            return cnd
    raise ValueError(f"no tile for {n} in {cands}")


def _axes():
    return lax.axis_index("x"), lax.axis_index("y"), lax.axis_index("c")


def _matmul(name, a, b, kind, out_dtype=F32, tm=None, tn=None, tk=None, comm=None):
    if kind == "nn":
        (M, K), (K2, N) = a.shape, b.shape
    elif kind == "nt":
        (M, K), (N, K2) = a.shape, b.shape
    else:
        (K, M), (K2, N) = a.shape, b.shape
    assert K == K2, (name, a.shape, b.shape)
    tm = tm or _pick(M, (1024, 512, 256, 128) if kind == "tn" else (1088, 1024, 704, 512, 256, 128, 64, 32, 16, 8))

    def vmem(tn_, tk_):
        acc = tm * tn_ * 4 if tk_ < K else 0
        return 2 * (tm * tk_ * a.dtype.itemsize + tk_ * tn_ * b.dtype.itemsize + tm * tn_ * jnp.dtype(out_dtype).itemsize) + acc

    if tn is None or tk is None:
        cands = [(n_, k_) for k_ in ((tk,) if tk else (K, 3328, 2816, 2048, 1024, 512)) if K % k_ == 0
                 for n_ in ((tn,) if tn else (1024, 512, 256, 128)) if N % n_ == 0]
        tn, tk = next((c for c in cands if vmem(*c) <= MATMUL_VMEM), cands[-1])
    nk = K // tk
    dims = {"nn": (((1,), (0,)), ((), ())), "nt": (((1,), (1,)), ((), ())), "tn": (((0,), (0,)), ((), ()))}[kind]

    ni, nj = M // tm, N // tn

    def body(*refs):
        (a_ref, b_ref), (o_ref,), scr, cref = _hosted(comm, 2, 1, refs)
        i, j, k = pl.program_id(0), pl.program_id(1), pl.program_id(2)
        _hosted_start(comm, cref, (i == 0) & (j == 0) & (k == 0))
        part = lax.dot_general(a_ref[...].astype(BF16), b_ref[...].astype(BF16), dims, preferred_element_type=F32)
        if nk == 1:
            o_ref[...] = part.astype(o_ref.dtype)
        else:
            acc = scr[0]

            @pl.when(k == 0)
            def _():
                acc[...] = part

            @pl.when(k > 0)
            def _():
                acc[...] += part

            @pl.when(k == nk - 1)
            def _():
                o_ref[...] = acc[...].astype(o_ref.dtype)
        _hosted_wait(comm, cref, (i == ni - 1) & (j == nj - 1) & (k == nk - 1))

    a_spec = {"nn": pl.BlockSpec((tm, tk), lambda i, j, k: (i, k)), "nt": pl.BlockSpec((tm, tk), lambda i, j, k: (i, k)),
              "tn": pl.BlockSpec((tk, tm), lambda i, j, k: (k, i))}[kind]
    b_spec = {"nn": pl.BlockSpec((tk, tn), lambda i, j, k: (k, j)), "nt": pl.BlockSpec((tn, tk), lambda i, j, k: (j, k)),
              "tn": pl.BlockSpec((tk, tn), lambda i, j, k: (k, j))}[kind]
    any_spec = pl.BlockSpec(memory_space=pl.ANY)
    c_ins, c_sds, c_sems = (comm.ins, comm.out_sds(), comm.sem_shapes()) if comm is not None else ([], [], [])
    sem = ("parallel", "parallel", "arbitrary") if comm is None else ("arbitrary",) * 3
    res = pl.pallas_call(
        body, name=name, grid=(ni, nj, nk), in_specs=[a_spec, b_spec] + [any_spec] * len(c_ins),
        out_specs=[pl.BlockSpec((tm, tn), lambda i, j, k: (i, j))] + [any_spec] * len(c_sds),
        out_shape=[jax.ShapeDtypeStruct((M, N), out_dtype)] + c_sds,
        scratch_shapes=([pltpu.VMEM((tm, tn), F32)] if nk > 1 else []) + c_sems,
        compiler_params=_cparams(sem, has_side_effects=comm is not None), **_CALL_KW)(a, b, *c_ins)
    return res[0] if comm is None else (res[0], list(res[1:]))


class _Comm:
    def __init__(self, ins, out_shapes, plan, n_local, n_remote, aliases=None):
        self.ins, self.out_shapes, self.plan = list(ins), list(out_shapes), plan
        self.n_local, self.n_remote, self.aliases = n_local, n_remote, aliases or {}

    def sem_shapes(self):
        return [pltpu.SemaphoreType.DMA((max(self.n_remote, 1),)), pltpu.SemaphoreType.DMA((max(self.n_remote, 1),)),
                pltpu.SemaphoreType.DMA((max(self.n_local, 1),))]

    def out_sds(self):
        return [jax.ShapeDtypeStruct(s, d) for s, d in self.out_shapes]

    def copies(self, in_refs, out_refs, sems):
        send_sems, recv_sems, loc_sems = sems
        x, y, c = _axes()
        local, remote = self.plan(x, y, c, in_refs, out_refs)
        assert len(local) == self.n_local and len(remote) == self.n_remote, (len(local), len(remote))
        lcs = [pltpu.make_async_copy(s, d, loc_sems.at[i]) for i, (s, d) in enumerate(local)]
        rcs = [pltpu.make_async_remote_copy(src_ref=s, dst_ref=d, send_sem=send_sems.at[i], recv_sem=recv_sems.at[i],
                                            device_id=dev, device_id_type=MESH) for i, (s, d, dev) in enumerate(remote)]
        return lcs + rcs


def _exchange(name, comm):
    n_in, n_out = len(comm.ins), len(comm.out_shapes)

    def body(*refs):
        cps = comm.copies(refs[:n_in], refs[n_in:n_in + n_out], refs[n_in + n_out:])
        for cp in cps:
            cp.start()
        for cp in cps:
            cp.wait()

    any_spec = pl.BlockSpec(memory_space=pl.ANY)
    return pl.pallas_call(
        body, name=name, in_specs=[any_spec] * n_in, out_specs=[any_spec] * n_out, out_shape=comm.out_sds(),
        scratch_shapes=comm.sem_shapes(), input_output_aliases=comm.aliases,
        compiler_params=pltpu.CompilerParams(has_side_effects=True), **_CALL_KW)(*comm.ins)


def _chip_of(k):
    return k // 2, k % 2


def _gather_ici_comm(shards, layer):
    def plan(x, y, c, ins, outs):
        me = 2 * x + y
        remote = []
        for w, g in zip(ins, outs):
            half = w.shape[1] // 2
            rows = pl.ds(c * half, half)
            for j in (1, 2, 3):
                px, py = _chip_of(me ^ j)
                remote.append((w.at[layer, rows], g.at[me, rows], (px, py, c)))
        return [], remote

    return _Comm(shards, [((N_CHIPS,) + w.shape[1:], w.dtype) for w in shards], plan, 0, 3 * len(shards))


def _gather_d2d(name, gath):
    def plan(x, y, c, ins, outs):
        me = 2 * x + y
        remote = []
        for g in outs:
            half = g.shape[1] // 2
            rows = pl.ds(c * half, half)
            for j in (1, 2, 3):
                remote.append((g.at[me ^ j, rows], g.at[me ^ j, rows], (x, y, 1 - c)))
        return [], remote

    n = len(gath)
    return _exchange(name, _Comm(gath, [(g.shape, g.dtype) for g in gath], plan, 0, 3 * n, aliases={i: i for i in range(n)}))


def _elementwise(name, fn, ins, out_dtypes, block_rows=None, n_out=None):
    shape = ins[0].shape
    lead, (R, C) = shape[:-2], shape[-2:]
    budget = (16 * 2 ** 20) // (8 * (len(ins) + len(out_dtypes)) * (-(-C // 128) * 128))
    br = block_rows or _pick(R, [r for r in (512, 256, 128, 352, 64, 32, 16, 8) if r <= max(budget, 8)] + [R])
    nl = len(lead)

    def body(*refs):
        outs = fn(*[r[...] for r in refs[:len(ins)]])
        for r, o in zip(refs[len(ins):], outs):
            r[...] = o.astype(r.dtype)

    blk = (None,) * nl + (br, C)
    spec = pl.BlockSpec(blk, lambda *g: tuple(g[:nl]) + (g[nl], 0))
    return pl.pallas_call(
        body, name=name, grid=tuple(lead) + (R // br,), in_specs=[spec] * len(ins), out_specs=[spec] * len(out_dtypes),
        out_shape=[jax.ShapeDtypeStruct(shape, d) for d in out_dtypes],
        compiler_params=_cparams(("parallel",) * (nl + 1)), **_CALL_KW)(*ins)


def _reduce_pre(tag, parts):
    n = len(parts)

    def plan_a(x, y, c, ins, outs):
        remote = []
        for p, r in zip(ins, outs):
            half = p.shape[2] // 2
            remote.append((p.at[:, :, pl.ds((1 - c) * half, half)], r, (x, y, 1 - c)))
        return [], remote

    halves = [((p.shape[0], p.shape[1], p.shape[2] // 2, p.shape[3]), p.dtype) for p in parts]
    got = _exchange(f"reduce_{tag}_d2d", _Comm(parts, halves, plan_a, 0, n))

    c = lax.axis_index("c")
    pair = []
    for idx, (p, r) in enumerate(zip(parts, got)):
        half = p.shape[2] // 2
        br = _pick(half, (512, 256, 352, 128, 64, 32, 16))
        nb = half // br

        def body(c_ref, p_ref, r_ref, o_ref):
            o_ref[...] = (p_ref[...].astype(F32) + r_ref[...].astype(F32)).astype(o_ref.dtype)

        blk = (None, None, br, p.shape[3])
        pair.append(pl.pallas_call(
            body, name=f"reduce_{tag}_pair{idx}",
            grid_spec=pltpu.PrefetchScalarGridSpec(
                num_scalar_prefetch=1, grid=(N_CHIPS, p.shape[1], nb),
                in_specs=[pl.BlockSpec(blk, lambda k, l, i, cr, nb=nb: (k, l, cr[0] * nb + i, 0)),
                          pl.BlockSpec(blk, lambda k, l, i, cr: (k, l, i, 0))],
                out_specs=pl.BlockSpec(blk, lambda k, l, i, cr: (k, l, i, 0))),
            out_shape=jax.ShapeDtypeStruct(r.shape, BF16),
            compiler_params=_cparams(("parallel",) * 3), **_CALL_KW)(jnp.reshape(c, (1,)).astype(jnp.int32), p, r))
    return pair


def _reduce_ici_comm(pair):
    def plan_b(x, y, c, ins, outs):
        me = 2 * x + y
        remote = []
        for p, r in zip(ins, outs):
            for j in (1, 2, 3):
                px, py = _chip_of(me ^ j)
                remote.append((p.at[me ^ j], r.at[me], (px, py, c)))
        return [], remote

    return _Comm(pair, [(p.shape, p.dtype) for p in pair], plan_b, 0, 3 * len(pair))


def _reduce_post(tag, pair, got):
    n = len(pair)
    c = lax.axis_index("c")
    me_chip = 2 * lax.axis_index("x") + lax.axis_index("y")

    sums = []
    for idx, r in enumerate(got):
        _, L, half, C = r.shape
        br = _pick(half, (512, 256, 352, 128, 64, 32, 16))
        nb = half // br

        def body(c_ref, p_ref, r_ref, o_ref):
            me = c_ref[1]
            acc = None
            for k in range(N_CHIPS):
                term = jnp.where(me == k, p_ref[k], r_ref[k]).astype(F32)
                acc = term if acc is None else acc + term
            o_ref[...] = acc

        blk4 = pl.BlockSpec((N_CHIPS, None, br, C), lambda l, i, cr: (0, l, i, 0))
        sums.append(pl.pallas_call(
            body, name=f"reduce_{tag}_sum{idx}",
            grid_spec=pltpu.PrefetchScalarGridSpec(
                num_scalar_prefetch=1, grid=(L, nb), in_specs=[blk4, blk4],
                out_specs=pl.BlockSpec((None, br, C), lambda l, i, cr, nb=nb: (l, cr[0] * nb + i, 0))),
            out_shape=jax.ShapeDtypeStruct((L, 2 * half, C), F32),
            compiler_params=_cparams(("parallel",) * 2), **_CALL_KW)(jnp.stack([c, me_chip]).astype(jnp.int32), pair[idx], r))

    def plan_c(x, y, c, ins, outs):
        remote = []
        for f in outs:
            half = f.shape[1] // 2
            rows = pl.ds(c * half, half)
            remote.append((f.at[:, rows], f.at[:, rows], (x, y, 1 - c)))
        return [], remote

    return _exchange(f"reduce_{tag}_bcast", _Comm(sums, [(s.shape, F32) for s in sums], plan_c, 0, n, aliases={i: i for i in range(n)}))


def _adamw_math(w, g, m, v):
    m = ADAM_B1 * m + (1.0 - ADAM_B1) * g
    v = ADAM_B2 * v + (1.0 - ADAM_B2) * (g * g)
    m_hat = m / (1.0 - ADAM_B1 ** ADAM_STEP)
    v_hat = v / (1.0 - ADAM_B2 ** ADAM_STEP)
    delta = -ADAM_LR * (m_hat / (jnp.sqrt(v_hat) + ADAM_EPS) + ADAM_WD * w)
    return delta, m, v


def _adamw(name, w, g, m, v):
    return _elementwise(name, _adamw_math, [w, g, m, v], [F32, F32, F32])


def _allgather_small(name, v):
    def plan(x, y, c, ins, outs):
        me = 4 * x + 2 * y + c
        (src,), (dst,) = ins, outs
        remote = []
        for j in range(1, N_DEV):
            p = me ^ j
            remote.append((src, dst.at[me], (p // 4, (p // 2) % 2, p % 2)))
        return [(src, dst.at[me])], remote

    return _exchange(name, _Comm([v], [((N_DEV,) + v.shape, v.dtype)], plan, 1, N_DEV - 1))[0]


def _seg_spec(rows, D, ctx_tiles):
    return pl.BlockSpec((None, rows, D), lambda i: (jnp.minimum(i // ctx_tiles, 1), 0, 0))


def _norm_mod(name, x, g, modv, r0, ctx_tiles):
    T, D = x.shape

    def body(x_ref, g_ref, m_ref, o_ref):
        xv = x_ref[...]
        r = lax.rsqrt(jnp.mean(xv * xv, axis=-1, keepdims=True) + EPS)
        y = xv * r * g_ref[...]
        o_ref[...] = (y * (1.0 + m_ref[r0 + 1:r0 + 2, :]) + m_ref[r0:r0 + 1, :]).astype(BF16)

    row = pl.BlockSpec((TOK, D), lambda i: (i, 0))
    return pl.pallas_call(
        body, name=name, grid=(T // TOK,), in_specs=[row, pl.BlockSpec((1, D), lambda i: (0, 0)), _seg_spec(6, D, ctx_tiles)],
        out_specs=row, out_shape=jax.ShapeDtypeStruct((T, D), BF16), compiler_params=_cparams(("parallel",)), **_CALL_KW)(x, g, modv)


def _norm_mod_bwd(name, x, g, modv, r0, dh, dh_b, dres, ctx_tiles):
    T, D = x.shape

    def body(x_ref, g_ref, m_ref, dh_ref, dhb_ref, dres_ref, dx_ref, dg_ref, dsh_ref, dsc_ref):
        i = pl.program_id(0)
        xv = x_ref[...]
        r = lax.rsqrt(jnp.mean(xv * xv, axis=-1, keepdims=True) + EPS)
        xn = xv * r
        y = xn * g_ref[...]
        dhv = dh_ref[...] + dhb_ref[...]

        @pl.when(i == 0)
        def _():
            dg_ref[...] = jnp.zeros_like(dg_ref)

        @pl.when((i == 0) | (i == ctx_tiles))
        def _():
            dsh_ref[...] = jnp.zeros_like(dsh_ref)
            dsc_ref[...] = jnp.zeros_like(dsc_ref)

        dsh_ref[...] += jnp.sum(dhv, axis=0, keepdims=True)
        dsc_ref[...] += jnp.sum(dhv * y, axis=0, keepdims=True)
        dy = dhv * (1.0 + m_ref[r0 + 1:r0 + 2, :])
        dg_ref[...] += jnp.sum(dy * xn, axis=0, keepdims=True)
        u = dy * g_ref[...]
        dx_ref[...] = dres_ref[...] + r * (u - xn * jnp.mean(u * xn, axis=-1, keepdims=True))

    row = pl.BlockSpec((TOK, D), lambda i: (i, 0))
    one = pl.BlockSpec((1, D), lambda i: (0, 0))
    return pl.pallas_call(
        body, name=name, grid=(T // TOK,), in_specs=[row, one, _seg_spec(6, D, ctx_tiles), row, row, row],
        out_specs=[row, one, _seg_spec(1, D, ctx_tiles), _seg_spec(1, D, ctx_tiles)],
        out_shape=[jax.ShapeDtypeStruct((T, D), F32), jax.ShapeDtypeStruct((1, D), F32),
                   jax.ShapeDtypeStruct((2, 1, D), F32), jax.ShapeDtypeStruct((2, 1, D), F32)],
        compiler_params=_cparams(("arbitrary",)), **_CALL_KW)(x, g, modv, dh, dh_b, dres)


def _resid_gate(name, x, y, modv, r, ctx_tiles):
    T, D = x.shape

    def body(x_ref, y_ref, m_ref, o_ref):
        o_ref[...] = x_ref[...] + m_ref[r:r + 1, :] * y_ref[...]

    row = pl.BlockSpec((TOK, D), lambda i: (i, 0))
    return pl.pallas_call(
        body, name=name, grid=(T // TOK,), in_specs=[row, row, _seg_spec(6, D, ctx_tiles)], out_specs=row,
        out_shape=jax.ShapeDtypeStruct((T, D), F32), compiler_params=_cparams(("parallel",)), **_CALL_KW)(x, y, modv)


def _resid_gate_bwd(name, dx, y, modv, r, ctx_tiles):
    T, D = dx.shape

    def body(dx_ref, y_ref, m_ref, dy_ref, dgt_ref):
        i = pl.program_id(0)

        @pl.when((i == 0) | (i == ctx_tiles))
        def _():
            dgt_ref[...] = jnp.zeros_like(dgt_ref)

        dxv = dx_ref[...]
        dgt_ref[...] += jnp.sum(dxv * y_ref[...], axis=0, keepdims=True)
        dy_ref[...] = (dxv * m_ref[r:r + 1, :]).astype(BF16)

    row = pl.BlockSpec((TOK, D), lambda i: (i, 0))
    return pl.pallas_call(
        body, name=name, grid=(T // TOK,), in_specs=[row, row, _seg_spec(6, D, ctx_tiles)],
        out_specs=[row, _seg_spec(1, D, ctx_tiles)],
        out_shape=[jax.ShapeDtypeStruct((T, D), BF16), jax.ShapeDtypeStruct((2, 1, D), F32)],
        compiler_params=_cparams(("arbitrary",)), **_CALL_KW)(dx, y, modv)


def _sigmoid(x):
    return 1.0 / (1.0 + jnp.exp(-x))


SWI_ROWS = 128


def _swiglu(name, gu):
    T, F2 = gu.shape
    F = F2 // 2

    def body(gu_ref, o_ref):
        g, u = gu_ref[:, :F].astype(F32), gu_ref[:, F:].astype(F32)
        o_ref[...] = ((g * _sigmoid(g)) * u).astype(BF16)

    return pl.pallas_call(
        body, name=name, grid=(T // SWI_ROWS,), in_specs=[pl.BlockSpec((SWI_ROWS, F2), lambda i: (i, 0))],
        out_specs=pl.BlockSpec((SWI_ROWS, F), lambda i: (i, 0)), out_shape=jax.ShapeDtypeStruct((T, F), BF16),
        compiler_params=_cparams(("parallel",)), **_CALL_KW)(gu)


def _swiglu_bwd(name, gu, dact):
    T, F2 = gu.shape
    F = F2 // 2

    def body(gu_ref, d_ref, o_ref):
        g, u, d = gu_ref[:, :F].astype(F32), gu_ref[:, F:].astype(F32), d_ref[...]
        s = _sigmoid(g)
        o_ref[:, :F] = (d * u * (s * (1.0 + g * (1.0 - s)))).astype(BF16)
        o_ref[:, F:] = (d * (g * s)).astype(BF16)

    return pl.pallas_call(
        body, name=name, grid=(T // SWI_ROWS,),
        in_specs=[pl.BlockSpec((SWI_ROWS, F2), lambda i: (i, 0)), pl.BlockSpec((SWI_ROWS, F), lambda i: (i, 0))],
        out_specs=pl.BlockSpec((SWI_ROWS, F2), lambda i: (i, 0)), out_shape=jax.ShapeDtypeStruct((T, F2), BF16),
        compiler_params=_cparams(("parallel",)), **_CALL_KW)(gu, dact)


def _loss_and_grad(name, y, target, ctx_tiles):
    T, D = y.shape

    def body(y_ref, t_ref, l_ref, dy_ref):
        i = pl.program_id(0)

        @pl.when(i == 0)
        def _():
            l_ref[...] = jnp.zeros_like(l_ref)

        lat = i >= ctx_tiles
        e = jnp.where(lat, y_ref[...] - t_ref[...], 0.0)
        dy_ref[...] = e * (1.0 / D)
        l_ref[...] += 0.5 * jnp.sum(jnp.sum(e * e, axis=-1, keepdims=True) * (1.0 / D), axis=0, keepdims=True)

    row = pl.BlockSpec((TOK, D), lambda i: (i, 0))
    return pl.pallas_call(
        body, name=name, grid=(T // TOK,),
        in_specs=[row, pl.BlockSpec((TOK, D), lambda i: (jnp.maximum(i - ctx_tiles, 0), 0))],
        out_specs=[pl.BlockSpec((8, 128), lambda i: (0, 0)), row],
        out_shape=[jax.ShapeDtypeStruct((8, 128), F32), jax.ShapeDtypeStruct((T, D), F32)],
        compiler_params=_cparams(("arbitrary",)), **_CALL_KW)(y, target)


def _rot_half(x):
    lane = lax.broadcasted_iota(jnp.int32, x.shape, 1)
    return jnp.where((lane % 64) < 32, -pltpu.roll(x, 96, 1), pltpu.roll(x, 32, 1))


def _head_prep(name, src, col_blk, n_heads, g, cos, sin, t_pad, norm, rope):
    T = src.shape[0]
    W = n_heads * HEAD
    nt = T // TOK

    def body(s_ref, g_ref, cos_ref, sin_ref, o_ref):
        i = pl.program_id(0)
        outs = []
        for h in range(n_heads):
            xv = s_ref[:, h * HEAD:(h + 1) * HEAD].astype(F32)
            if norm:
                xv = xv * lax.rsqrt(jnp.mean(xv * xv, axis=-1, keepdims=True) + EPS) * g_ref[...]
            if rope:
                xv = xv * cos_ref[...] + _rot_half(xv) * sin_ref[...]
            outs.append(jnp.where(i < nt, xv, 0.0).astype(BF16))
        o_ref[...] = jnp.concatenate(outs, axis=-1) if n_heads > 1 else outs[0]

    tab = pl.BlockSpec((TOK, HEAD), lambda i: (i, 0))
    return pl.pallas_call(
        body, name=name, grid=(t_pad // TOK,),
        in_specs=[pl.BlockSpec((TOK, W), lambda i: (jnp.minimum(i, nt - 1), col_blk)), pl.BlockSpec((1, HEAD), lambda i: (0, 0)), tab, tab],
        out_specs=pl.BlockSpec((TOK, W), lambda i: (i, 0)), out_shape=jax.ShapeDtypeStruct((t_pad, W), BF16),
        compiler_params=_cparams(("parallel",)), **_CALL_KW)(src, g, cos, sin)


def _head_prep_bwd(name, src, col_blk, n_heads, g, cos, sin, dout, norm, rope, dst):
    T = src.shape[0]
    W = n_heads * HEAD

    def body(s_ref, g_ref, cos_ref, sin_ref, d_ref, dst_ref, ds_ref, dg_ref):
        i = pl.program_id(0)

        @pl.when(i == 0)
        def _():
            dg_ref[...] = jnp.zeros_like(dg_ref)

        outs = []
        dg = jnp.zeros((1, HEAD), F32)
        for h in range(n_heads):
            dz = d_ref[:, h * HEAD:(h + 1) * HEAD]
            if rope:
                dz = dz * cos_ref[...] - _rot_half(dz * sin_ref[...])
            if norm:
                xv = s_ref[:, h * HEAD:(h + 1) * HEAD].astype(F32)
                r = lax.rsqrt(jnp.mean(xv * xv, axis=-1, keepdims=True) + EPS)
                xn = xv * r
                dg = dg + jnp.sum(dz * xn, axis=0, keepdims=True)
                u = dz * g_ref[...]
                dz = r * (u - xn * jnp.mean(u * xn, axis=-1, keepdims=True))
            outs.append(dz.astype(BF16))
        dg_ref[...] += dg
        ds_ref[...] = jnp.concatenate(outs, axis=-1) if n_heads > 1 else outs[0]

    tab = pl.BlockSpec((TOK, HEAD), lambda i: (i, 0))
    col = pl.BlockSpec((TOK, W), lambda i: (i, col_blk))
    one = pl.BlockSpec((1, HEAD), lambda i: (0, 0))
    return pl.pallas_call(
        body, name=name, grid=(T // TOK,),
        in_specs=[col, one, tab, tab, pl.BlockSpec((TOK, W), lambda i: (i, 0)), pl.BlockSpec(memory_space=pl.ANY)],
        out_specs=[col, one], out_shape=[jax.ShapeDtypeStruct(dst.shape, dst.dtype), jax.ShapeDtypeStruct((1, HEAD), F32)],
        input_output_aliases={5: 0}, compiler_params=_cparams(("arbitrary",)), **_CALL_KW)(src, g, cos, sin, dout, dst)


NEG = -1e30


def _attn_geometry(kind, blk, ctx, seq):
    if kind == "swa":
        bq, W = 128, 384
        nctx = ctx // bq
        lat = blk >= nctx
        n = blk - nctx
        s0 = jnp.where(lat, ctx + (n - 1) * bq, 0)
        i = lax.broadcasted_iota(jnp.int32, (bq, W), 0)
        j = lax.broadcasted_iota(jnp.int32, (bq, W), 1)
        kpos = (n - 1) * bq + j
        rel = j - bq - i
        valid = lat & (rel <= SWA_WINDOW) & (rel >= -SWA_WINDOW) & (kpos >= 0) & (kpos < seq)
        return s0, valid, 0
    bq, W = GRID_W, NA_KH * GRID_W
    nctx = ctx // bq
    rows = seq // GRID_W
    lat = blk >= nctx
    rr = jnp.clip(blk - nctx, 0, rows - 1)
    rs = jnp.clip(rr - NA_KH // 2, 0, rows - NA_KH)
    s0 = ctx + rs * GRID_W
    i = lax.broadcasted_iota(jnp.int32, (bq, W), 0)
    j = lax.broadcasted_iota(jnp.int32, (bq, W), 1)
    kcol = j % GRID_W
    cs = jnp.clip(i - NA_KW // 2, 0, GRID_W - NA_KW)
    valid = lat & (kcol >= cs) & (kcol < cs + NA_KW)
    return s0, valid, rr - rs


HP = 2


def _attn_probs(q, kl, kc, sk, bias, valid):
    scale = HEAD ** -0.5
    nt_dims = (((1,), (1,)), ((), ()))
    sl = lax.dot_general(q, kl, nt_dims, preferred_element_type=F32) * scale
    if bias is not None:
        sl = sl + bias
    sl = jnp.where(valid, sl, NEG)
    sc = lax.dot_general(q, kc, nt_dims, preferred_element_type=F32) * scale
    m = jnp.maximum(jnp.maximum(jnp.max(sl, axis=-1, keepdims=True), jnp.max(sc, axis=-1, keepdims=True)), sk)
    el, ec, es = jnp.exp(sl - m), jnp.exp(sc - m), jnp.exp(sk - m)
    inv = 1.0 / (jnp.sum(el, axis=-1, keepdims=True) + jnp.sum(ec, axis=-1, keepdims=True) + es)
    return el * inv, ec * inv, es * inv


def _attn_specs(kind, n_q, n_kv, t_pad):
    bq = 128 if kind == "swa" else GRID_W
    rep = n_q // n_kv
    assert n_q % HP == 0 and HP % rep == 0
    kvw = HP // rep
    qspec = pl.BlockSpec((bq, HP * HEAD), lambda g, b: (b, g))
    kvspec = pl.BlockSpec((t_pad, kvw * HEAD), lambda g, b: (0, g))
    specs = [qspec, kvspec, kvspec, pl.BlockSpec(memory_space=pltpu.SMEM)]
    return bq, rep, qspec, kvspec, specs


def _bias_spec(ctx, seq):
    W = NA_KH * GRID_W

    def idx(g, b):
        rows = seq // GRID_W
        rr = jnp.clip(b - ctx // GRID_W, 0, rows - 1)
        return (g, rr - jnp.clip(rr - NA_KH // 2, 0, rows - NA_KH), 0, 0)

    return pl.BlockSpec((HP, None, GRID_W, W), idx)


def _attn_loads(kind, blk, q_ref, k_ref, v_ref, sink_ref, bias_ref, rep, ctx, seq):
    bq, W = (128, 384) if kind == "swa" else (GRID_W, NA_KH * GRID_W)
    g = pl.program_id(0)
    s0, valid, _ = _attn_geometry(kind, blk, ctx, seq)
    s0 = pl.multiple_of(s0, GRID_W)
    heads = []
    for j in range(HP):
        kv = slice((j // rep) * HEAD, (j // rep + 1) * HEAD)
        heads.append((q_ref[:, j * HEAD:(j + 1) * HEAD], k_ref[pl.ds(s0, W), kv], k_ref[0:ctx, kv], v_ref[pl.ds(s0, W), kv],
                      v_ref[0:ctx, kv], sink_ref[g * HP + j], bias_ref[j] if bias_ref is not None else None))
    return s0, W, valid, heads


def _attn_fwd(name, kind, q, k, v, sink, bias, ctx, seq, dst, head0, comm=None):
    t_pad = q.shape[0]
    T = ctx + seq
    n_q, n_kv = q.shape[1] // HEAD, k.shape[1] // HEAD
    bq, rep, qspec, kvspec, specs = _attn_specs(kind, n_q, n_kv, t_pad)
    assert head0 % HP == 0
    ins = [q, k, v, sink] + ([bias] if bias is not None else []) + [dst]
    ng, nb = n_q // HP, T // bq

    def body(*refs):
        mine, (o_ref,), _, cref = _hosted(comm, len(ins), 1, refs)
        q_ref, k_ref, v_ref, sink_ref = mine[:4]
        bias_ref = mine[4] if bias is not None else None
        g, blk = pl.program_id(0), pl.program_id(1)
        _hosted_start(comm, cref, (g == 0) & (blk == 0))
        _, _, valid, heads = _attn_loads(kind, blk, q_ref, k_ref, v_ref, sink_ref, bias_ref, rep, ctx, seq)
        outs = []
        for qv, kl, kc, vl, vc, sk, bv in heads:
            p_l, p_c, _ = _attn_probs(qv, kl, kc, sk, bv, valid)
            o = jnp.dot(p_l.astype(BF16), vl, preferred_element_type=F32) + jnp.dot(p_c.astype(BF16), vc, preferred_element_type=F32)
            outs.append(o.astype(o_ref.dtype))
        o_ref[...] = jnp.concatenate(outs, axis=-1)
        _hosted_wait(comm, cref, (g == ng - 1) & (blk == nb - 1))

    if bias is not None:
        specs = specs + [_bias_spec(ctx, seq)]
    any_spec = pl.BlockSpec(memory_space=pl.ANY)
    c_ins, c_sds, c_sems = (comm.ins, comm.out_sds(), comm.sem_shapes()) if comm is not None else ([], [], [])
    res = pl.pallas_call(
        body, name=name, grid=(ng, nb), in_specs=specs + [any_spec] * (1 + len(c_ins)),
        out_specs=[pl.BlockSpec((bq, HP * HEAD), lambda g, b: (b, head0 // HP + g))] + [any_spec] * len(c_sds),
        out_shape=[jax.ShapeDtypeStruct(dst.shape, dst.dtype)] + c_sds, input_output_aliases={len(ins) - 1: 0},
        scratch_shapes=c_sems,
        compiler_params=_cparams(("arbitrary", "arbitrary"), has_side_effects=comm is not None), **_CALL_KW)(*ins, *c_ins)
    return res[0], list(res[1:])


def _attn_bwd(name, kind, q, k, v, sink, bias, do, do_head0, ctx, seq):
    t_pad = q.shape[0]
    T = ctx + seq
    n_q, n_kv = q.shape[1] // HEAD, k.shape[1] // HEAD
    bq, rep, qspec, kvspec, specs = _attn_specs(kind, n_q, n_kv, t_pad)
    assert do_head0 % HP == 0
    scale = HEAD ** -0.5
    tn_dims = (((0,), (0,)), ((), ()))
    nt_dims = (((1,), (1,)), ((), ()))
    bdot = functools.partial(lax.dot_general, preferred_element_type=F32)

    def body(q_ref, k_ref, v_ref, sink_ref, *rest):
        if bias is not None:
            bias_ref, do_ref, dq_ref, dk_ref, dv_ref, dsk_ref, db_ref = rest
        else:
            bias_ref, db_ref = None, None
            do_ref, dq_ref, dk_ref, dv_ref, dsk_ref = rest
        blk = pl.program_id(1)
        s0, W, valid, heads = _attn_loads(kind, blk, q_ref, k_ref, v_ref, sink_ref, bias_ref, rep, ctx, seq)
        dos = [do_ref[:, j * HEAD:(j + 1) * HEAD] for j in range(HP)]

        @pl.when(blk == 0)
        def _():
            dk_ref[...] = jnp.zeros_like(dk_ref)
            dv_ref[...] = jnp.zeros_like(dv_ref)
            dsk_ref[...] = jnp.zeros_like(dsk_ref)

        if bias is not None:
            _, _, pat = _attn_geometry(kind, blk, ctx, seq)
            _, _, pat_prev = _attn_geometry(kind, jnp.maximum(blk - 1, 0), ctx, seq)

            @pl.when((blk == 0) | (pat != pat_prev))
            def _():
                db_ref[...] = jnp.zeros_like(db_ref)

        res = []
        for (qv, kl, kc, vl, vc, sk, bv), dov in zip(heads, dos):
            p_l, p_c, p_s = _attn_probs(qv, kl, kc, sk, bv, valid)
            dob = dov.astype(BF16)
            pl_b, pc_b = p_l.astype(BF16), p_c.astype(BF16)
            o = jnp.dot(pl_b, vl, preferred_element_type=F32) + jnp.dot(pc_b, vc, preferred_element_type=F32)
            delta = jnp.sum(dov * o, axis=-1, keepdims=True)
            ds_l = p_l * (bdot(dob, vl, nt_dims) - delta)
            ds_c = p_c * (bdot(dob, vc, nt_dims) - delta)
            dsl_b, dsc_b = ds_l.astype(BF16), ds_c.astype(BF16)
            dq = (jnp.dot(dsl_b, kl, preferred_element_type=F32) + jnp.dot(dsc_b, kc, preferred_element_type=F32)) * scale
            res.append((dq, bdot(dsl_b, qv, tn_dims) * scale, bdot(pl_b, dob, tn_dims), bdot(dsc_b, qv, tn_dims) * scale,
                        bdot(pc_b, dob, tn_dims), jnp.sum(-p_s * delta, axis=0, keepdims=True), ds_l))
        dq_ref[...] = jnp.concatenate([r[0] for r in res], axis=-1)
        for j, (_, dkl, dvl, dkc, dvc, dsk, ds_l) in enumerate(res):
            kv = slice((j // rep) * HEAD, (j // rep + 1) * HEAD)
            dk_ref[pl.ds(s0, W), kv] += dkl
            dv_ref[pl.ds(s0, W), kv] += dvl
            dk_ref[0:ctx, kv] += dkc
            dv_ref[0:ctx, kv] += dvc
            dsk_ref[j] += jnp.broadcast_to(dsk, (8, HEAD))
            if bias is not None:
                db_ref[j] += ds_l

    ins = [q, k, v, sink] + ([bias] if bias is not None else []) + [do]
    in_specs = specs + ([_bias_spec(ctx, seq)] if bias is not None else []) + [
        pl.BlockSpec((bq, HP * HEAD), lambda g, b: (b, do_head0 // HP + g))]
    out_specs = [qspec, kvspec, kvspec, pl.BlockSpec((HP, 8, HEAD), lambda g, b: (g, 0, 0))]
    out_shape = [jax.ShapeDtypeStruct((T, n_q * HEAD), F32), jax.ShapeDtypeStruct((t_pad, n_kv * HEAD), F32),
                 jax.ShapeDtypeStruct((t_pad, n_kv * HEAD), F32), jax.ShapeDtypeStruct((n_q, 8, HEAD), F32)]
    if bias is not None:
        out_specs.append(_bias_spec(ctx, seq))
        out_shape.append(jax.ShapeDtypeStruct(bias.shape, F32))
    res = pl.pallas_call(
        body, name=name, grid=(n_q // HP, T // bq), in_specs=in_specs, out_specs=out_specs, out_shape=out_shape,
        compiler_params=_cparams(("arbitrary", "arbitrary")), **_CALL_KW)(*ins)
    return res if bias is not None else list(res) + [None]


HALO = 8


def _halo_specs(width, col0, T, ctx_tiles):
    per = TOK // HALO
    main = pl.BlockSpec((TOK, width), lambda jc, i: (i, col0 + jc))
    prev = pl.BlockSpec((HALO, width), lambda jc, i: (jnp.maximum(i * per - 1, 0), col0 + jc))
    nxt = pl.BlockSpec((HALO, width), lambda jc, i: (jnp.minimum((i + 1) * per, T // HALO - 1), col0 + jc))
    return main, prev, nxt


def _with_halo(i, nt, ctx_tiles, prev, main, nxt):
    has_prev = (i != 0) & (i != ctx_tiles)
    has_next = (i != ctx_tiles - 1) & (i != nt - 1)
    return jnp.concatenate([jnp.where(has_prev, prev, 0.0), main, jnp.where(has_next, nxt, 0.0)], axis=0)


def _shifted(ext, s):
    n = ext.shape[0]
    return pltpu.roll(ext, (-s) % n, 0)[HALO:HALO + TOK]


def _conv_fwd(name, p, col0, conv_w, ctx_tiles):
    T = p.shape[0]
    nt = T // TOK
    ncol = 3
    Wc = conv_w.shape[1] // ncol
    pad = (DN_CONV - 1) // 2

    def body(m_ref, p_ref, n_ref, w_ref, o_ref):
        i = pl.program_id(1)
        ext = _with_halo(i, nt, ctx_tiles, p_ref[...].astype(F32), m_ref[...].astype(F32), n_ref[...].astype(F32))
        acc = jnp.zeros((TOK, Wc), F32)
        for j in range(DN_CONV):
            acc = acc + w_ref[j:j + 1, :] * _shifted(ext, j - pad)
        o_ref[...] = acc

    main, prev, nxt = _halo_specs(Wc, col0, T, ctx_tiles)
    return pl.pallas_call(
        body, name=name, grid=(ncol, nt), in_specs=[main, prev, nxt, pl.BlockSpec((DN_CONV, Wc), lambda jc, i: (0, jc))],
        out_specs=pl.BlockSpec((TOK, Wc), lambda jc, i: (i, jc)), out_shape=jax.ShapeDtypeStruct((T, ncol * Wc), F32),
        compiler_params=_cparams(("parallel", "parallel")), **_CALL_KW)(p, p, p, conv_w)


def _conv_bwd(name, p, col0, conv_w, dpre, ctx_tiles, dst):
    T = p.shape[0]
    nt = T // TOK
    ncol = 3
    Wc = conv_w.shape[1] // ncol
    pad = (DN_CONV - 1) // 2

    def body(m_ref, p_ref, n_ref, dm_ref, dp_ref, dn_ref, w_ref, dst_ref, dx_ref, dw_ref):
        i = pl.program_id(1)
        ext_x = _with_halo(i, nt, ctx_tiles, p_ref[...].astype(F32), m_ref[...].astype(F32), n_ref[...].astype(F32))
        ext_d = _with_halo(i, nt, ctx_tiles, dp_ref[...], dm_ref[...], dn_ref[...])
        dmain = dm_ref[...]

        @pl.when(i == 0)
        def _():
            dw_ref[...] = jnp.zeros_like(dw_ref)

        acc = jnp.zeros((TOK, Wc), F32)
        for j in range(DN_CONV):
            acc = acc + w_ref[j:j + 1, :] * _shifted(ext_d, pad - j)
            dw_ref[j:j + 1, :] += jnp.sum(dmain * _shifted(ext_x, j - pad), axis=0, keepdims=True)
        dx_ref[...] = acc.astype(BF16)

    main, prev, nxt = _halo_specs(Wc, col0, T, ctx_tiles)
    dmain, dprev, dnxt = _halo_specs(Wc, 0, T, ctx_tiles)
    return pl.pallas_call(
        body, name=name, grid=(ncol, nt),
        in_specs=[main, prev, nxt, dmain, dprev, dnxt, pl.BlockSpec((DN_CONV, Wc), lambda jc, i: (0, jc)),
                  pl.BlockSpec(memory_space=pl.ANY)],
        out_specs=[pl.BlockSpec((TOK, Wc), lambda jc, i: (i, col0 + jc)), pl.BlockSpec((8, Wc), lambda jc, i: (0, jc))],
        out_shape=[jax.ShapeDtypeStruct(dst.shape, dst.dtype), jax.ShapeDtypeStruct((8, ncol * Wc), F32)],
        input_output_aliases={7: 0},
        compiler_params=_cparams(("parallel", "arbitrary")), **_CALL_KW)(p, p, p, dpre, dpre, dpre, conv_w, dst)


def _softplus(x):
    return jnp.maximum(x, 0.0) + jnp.log(1.0 + jnp.exp(-jnp.abs(x)))


def _gdn_point(name, pre, dab, a_log, dt_bias, n_heads):
    T = pre.shape[0]
    Wd = n_heads * HEAD
    ng = 2 * n_heads

    def body(pre_ref, ab_ref, al_ref, dt_ref, q_ref, k_ref, v_ref, la_ref, be_ref):
        for h in range(n_heads):
            for part, ref in enumerate((q_ref, k_ref, v_ref)):
                xv = pre_ref[:, part * Wd + h * HEAD:part * Wd + (h + 1) * HEAD]
                s = xv * _sigmoid(xv)
                if part < 2:
                    s = s * lax.rsqrt(jnp.sum(s * s, axis=-1, keepdims=True) + EPS) * (HEAD ** -0.5 if part == 0 else 1.0)
                ref[:, h * HEAD:(h + 1) * HEAD] = s
        ab = ab_ref[...].astype(F32)
        lane = lax.broadcasted_iota(jnp.int32, ab.shape, 1)
        la_ref[...] = jnp.where(lane < ng, -jnp.exp(al_ref[...]) * _softplus(ab + dt_ref[...]), 0.0)
        be_ref[...] = jnp.where(lane < ng, _sigmoid(pltpu.roll(ab, HEAD - ng, 1)), 0.0)

    row = lambda w: pl.BlockSpec((TOK, w), lambda i: (i, 0))
    one = pl.BlockSpec((1, HEAD), lambda i: (0, 0))
    return pl.pallas_call(
        body, name=name, grid=(T // TOK,), in_specs=[row(3 * Wd), row(HEAD), one, one],
        out_specs=[row(Wd), row(Wd), row(Wd), row(HEAD), row(HEAD)],
        out_shape=[jax.ShapeDtypeStruct((T, Wd), F32)] * 3 + [jax.ShapeDtypeStruct((T, HEAD), F32)] * 2,
        compiler_params=_cparams(("parallel",)), **_CALL_KW)(pre, dab, a_log, dt_bias)


def _gdn_point_bwd(name, pre, dab, a_log, dt_bias, n_heads, dq, dk, dv, dla, dbe):
    T = pre.shape[0]
    Wd = n_heads * HEAD
    ng = 2 * n_heads

    def body(pre_ref, ab_ref, al_ref, dt_ref, dq_ref, dk_ref, dv_ref, dla_ref, dbe_ref, dpre_ref, dab_ref, dal_ref, ddt_ref):
        i = pl.program_id(0)

        @pl.when(i == 0)
        def _():
            dal_ref[...] = jnp.zeros_like(dal_ref)
            ddt_ref[...] = jnp.zeros_like(ddt_ref)

        for h in range(n_heads):
            for part, ref in enumerate((dq_ref, dk_ref, dv_ref)):
                cols = slice(part * Wd + h * HEAD, part * Wd + (h + 1) * HEAD)
                xv = pre_ref[:, cols]
                sg = _sigmoid(xv)
                s = xv * sg
                dy = ref[0, :, h * HEAD:(h + 1) * HEAD] + ref[1, :, h * HEAD:(h + 1) * HEAD]
                if part < 2:
                    c0 = HEAD ** -0.5 if part == 0 else 1.0
                    r = lax.rsqrt(jnp.sum(s * s, axis=-1, keepdims=True) + EPS)
                    ds = c0 * (r * dy - s * (r * r * r) * jnp.sum(dy * s, axis=-1, keepdims=True))
                else:
                    ds = dy
                dpre_ref[:, cols] = ds * (sg * (1.0 + xv * (1.0 - sg)))
        ab = ab_ref[...].astype(F32)
        lane = lax.broadcasted_iota(jnp.int32, ab.shape, 1)
        ea = jnp.exp(al_ref[...])
        z = ab + dt_ref[...]
        dlav = jnp.where(lane < ng, dla_ref[0] + dla_ref[1], 0.0)
        da = dlav * (-ea) * _sigmoid(z)
        dal_ref[...] += jnp.sum(dlav * (-ea) * _softplus(z), axis=0, keepdims=True)
        ddt_ref[...] += jnp.sum(da, axis=0, keepdims=True)
        be = _sigmoid(pltpu.roll(ab, HEAD - ng, 1))
        db = jnp.where(lane < ng, (dbe_ref[0] + dbe_ref[1]) * be * (1.0 - be), 0.0)
        dab_ref[...] = (da + pltpu.roll(db, ng, 1)).astype(BF16)

    row = lambda w: pl.BlockSpec((TOK, w), lambda i: (i, 0))
    two = lambda w: pl.BlockSpec((2, TOK, w), lambda i: (0, i, 0))
    one = pl.BlockSpec((1, HEAD), lambda i: (0, 0))
    return pl.pallas_call(
        body, name=name, grid=(T // TOK,),
        in_specs=[row(3 * Wd), row(HEAD), one, one, two(Wd), two(Wd), two(Wd), two(HEAD), two(HEAD)],
        out_specs=[row(3 * Wd), row(HEAD), one, one],
        out_shape=[jax.ShapeDtypeStruct((T, 3 * Wd), F32), jax.ShapeDtypeStruct((T, HEAD), BF16),
                   jax.ShapeDtypeStruct((1, HEAD), F32), jax.ShapeDtypeStruct((1, HEAD), F32)],
        compiler_params=_cparams(("arbitrary",)), **_CALL_KW)(pre, dab, a_log, dt_bias, dq, dk, dv, dla, dbe)


_NN, _NT, _TN = "nn", "nt", "tn"
_DIMS = {"nn": (((1,), (0,)), ((), ())), "nt": (((1,), (1,)), ((), ())), "tn": (((0,), (0,)), ((), ()))}
_BDIMS = {"nn": (((2,), (1,)), ((0,), (0,))), "nt": (((2,), (2,)), ((0,), (0,))), "tn": (((1,), (1,)), ((0,), (0,)))}


def _dims(a, kind):
    return _BDIMS[kind] if a.ndim == 3 else _DIMS[kind]


def _mm3_raw(a, b, kind=_NN):
    ah, bh = a.astype(BF16), b.astype(BF16)
    al, bl = (a - ah.astype(F32)).astype(BF16), (b - bh.astype(F32)).astype(BF16)
    d = functools.partial(lax.dot_general, dimension_numbers=_dims(a, kind), preferred_element_type=F32)
    return d(ah, bh) + (d(ah, bl) + d(al, bh))


@jax.custom_vjp
def _mm3(a, b):
    return _mm3_raw(a, b)


def _mm3_fwd(a, b):
    return _mm3_raw(a, b), (a, b)


def _mm3_bwd(res, g):
    a, b = res
    return _mm3_raw(g, b, _NT), _mm3_raw(a, g, _TN)


_mm3.defvjp(_mm3_fwd, _mm3_bwd)


def _bdot_raw(a, b, kind):
    return lax.dot_general(a.astype(BF16), b.astype(BF16), _dims(a, kind), preferred_element_type=F32)


@functools.partial(jax.custom_vjp, nondiff_argnums=(2,))
def _bdot(a, b, kind=_NN):
    return _bdot_raw(a, b, kind)


def _bdot_fwd(a, b, kind):
    return _bdot_raw(a, b, kind), (a, b)


def _bdot_bwd(kind, res, g):
    a, b = res
    if kind == "nn":
        return _bdot_raw(g, b, "nt"), _bdot_raw(a, g, "tn")
    if kind == "nt":
        return _bdot_raw(g, b, "nn"), _bdot_raw(g, a, "tn")
    return _bdot_raw(b, g, "nt"), _bdot_raw(a, g, "nn")


_bdot.defvjp(_bdot_fwd, _bdot_bwd)


def _chunk_masks(rev):
    C = DN_CHUNK
    ii = lax.broadcasted_iota(jnp.int32, (C, C), 0)
    jj = lax.broadcasted_iota(jnp.int32, (C, C), 1)
    diff = jnp.where(rev, jj - ii, ii - jj)
    incl = diff >= 0
    strict = diff > 0
    rowsel = (lax.broadcasted_iota(jnp.int32, (C, 1), 0) == jnp.where(rev, 0, C - 1)).astype(F32)
    return incl, strict, rowsel, (ii == jj).astype(F32)


def _head_stack(ref, n_heads, rows=slice(None)):
    return jnp.stack([ref[rows, h * HEAD:(h + 1) * HEAD] for h in range(n_heads)])


def _cat(parts):
    return jnp.concatenate(parts, axis=0)


def _gate_views(g, gt, be, d, n_heads):
    lane = lax.broadcasted_iota(jnp.int32, (1, HEAD), 1)
    sub = lax.broadcasted_iota(jnp.int32, (HEAD, 1), 0)
    sels = [(lane == d * n_heads + h).astype(F32) for h in range(n_heads)]
    selts = [(sub == d * n_heads + h).astype(F32) for h in range(n_heads)]
    g_col = jnp.stack([jnp.sum(g * s, axis=1, keepdims=True) for s in sels])
    b_col = jnp.stack([jnp.sum(be * s, axis=1, keepdims=True) for s in sels])
    g_row = jnp.stack([jnp.sum(gt * s, axis=0, keepdims=True) for s in selts])
    return g_col, g_row, b_col, sels, selts


def _chunk_decay(g_col, g_row, incl):
    return jnp.where(incl, jnp.exp(jnp.where(incl, g_col - g_row, 0.0)), 0.0)


def _chunk_lower(k, g_col, g_row, b_col, incl, strict):
    return jnp.where(strict, _bdot(k * b_col, k, _NT) * _chunk_decay(g_col, g_row, incl), 0.0)


def _chunk_inverse(low, eye):
    m = -low
    x = eye + m
    p = m
    for _ in range(int(math.log2(DN_CHUNK)) - 1):
        p = _mm3(p, p)
        x = x + _mm3(x, p)
    return x


def _chunk_step(q, k, v, g_col, g_row, b_col, S, X, incl, rowsel):
    decay = _chunk_decay(g_col, g_row, incl)
    eg = jnp.exp(g_col)
    u = _mm3(X, v * b_col)
    w = _mm3(X, k * (b_col * eg))
    intra = _bdot(q, k, _NT) * decay
    g_last = jnp.sum(g_col * rowsel, axis=1, keepdims=True)
    v_new = u - _bdot(w, S)
    o = _bdot(q * eg, S) + _bdot(intra, v_new)
    S_new = S * jnp.exp(g_last) + _bdot(k * jnp.exp(g_last - g_col), v_new, _TN)
    return o, S_new


def _scan_index(ctx_chunks, n_chunks):
    def idx(d, n):
        return jnp.where(d == 0, n, jnp.where(n < ctx_chunks, ctx_chunks - 1 - n, n_chunks + ctx_chunks - 1 - n))
    return idx


def _cumsum_mats(rev):
    C = DN_CHUNK
    ii = lax.broadcasted_iota(jnp.int32, (C, C), 0)
    jj = lax.broadcasted_iota(jnp.int32, (C, C), 1)
    return jnp.where(jnp.where(rev, jj - ii, ii - jj) >= 0, 1.0, 0.0).astype(F32)


def _hosted(comm, n_in, n_out, refs):
    n_ci, n_co = (len(comm.ins), len(comm.out_shapes)) if comm is not None else (0, 0)
    ins, cin = refs[:n_in], refs[n_in:n_in + n_ci]
    outs, cout = refs[n_in + n_ci:n_in + n_ci + n_out], refs[n_in + n_ci + n_out:n_in + n_ci + n_out + n_co]
    rest = refs[n_in + n_ci + n_out + n_co:]
    n_sem = 3 if comm is not None else 0
    return ins, outs, rest[:len(rest) - n_sem], (cin, cout, rest[len(rest) - n_sem:])


def _hosted_start(comm, cref, first):
    if comm is not None:
        @pl.when(first)
        def _():
            for cp in comm.copies(*cref):
                cp.start()


def _hosted_wait(comm, cref, last):
    if comm is not None:
        @pl.when(last)
        def _():
            for cp in comm.copies(*cref):
                cp.wait()


def _gdn_scan(name, q, k, v, la, be, n_heads, ctx, comm=None):
    T, Wd = q.shape
    C = DN_CHUNK
    nch = T // C
    npair = nch // 2
    assert nch % 2 == 0 and (ctx // C) % 2 == 0
    pidx = _scan_index(ctx // C // 2, npair)
    hi = lax.Precision.HIGHEST

    def body(*refs):
        (q_ref, k_ref, v_ref, la_ref, be_ref), (o_ref, s_ref, x_ref), (state,), cref = _hosted(comm, 5, 3, refs)
        d, n = pl.program_id(0), pl.program_id(1)
        rev = d == 1
        _hosted_start(comm, cref, (d == 0) & (n == 0))

        @pl.when(n == 0)
        def _():
            state[...] = jnp.zeros_like(state)

        incl, strict, rowsel, eye = _chunk_masks(rev)
        tri = _cumsum_mats(rev)
        offs = [pl.multiple_of(jnp.where(rev, C, 0), C), pl.multiple_of(jnp.where(rev, 0, C), C)]
        views, qkv = [], []
        for off in offs:
            rows = pl.ds(off, C)
            g = jnp.dot(tri, la_ref[rows, :], precision=hi, preferred_element_type=F32)
            views.append(_gate_views(g, g.T, be_ref[rows, :], d, n_heads)[:3])
            qkv.append(tuple(_head_stack(r, n_heads, rows) for r in (q_ref, k_ref, v_ref)))
        low = _chunk_lower(_cat([qkv[0][1], qkv[1][1]]), *[_cat([views[0][j], views[1][j]]) for j in range(3)], incl, strict)
        X = _chunk_inverse(low, eye)
        S = state[...]
        for i, off in enumerate(offs):
            Xi = X[i * n_heads:(i + 1) * n_heads]
            s_ref[i] = S
            x_ref[i] = Xi
            o, S = _chunk_step(*qkv[i], *views[i], S, Xi, incl, rowsel)
            for h in range(n_heads):
                o_ref[pl.ds(off, C), h * HEAD:(h + 1) * HEAD] = o[h]
        state[...] = S
        _hosted_wait(comm, cref, (d == 1) & (n == npair - 1))

    tok = lambda w: pl.BlockSpec((2 * C, w), lambda d, n: (pidx(d, n), 0))
    any_spec = pl.BlockSpec(memory_space=pl.ANY)
    c_ins, c_sds, c_sems = (comm.ins, comm.out_sds(), comm.sem_shapes()) if comm is not None else ([], [], [])
    res = pl.pallas_call(
        body, name=name, grid=(2, npair), in_specs=[tok(Wd), tok(Wd), tok(Wd), tok(HEAD), tok(HEAD)] + [any_spec] * len(c_ins),
        out_specs=[pl.BlockSpec((None, 2 * C, Wd), lambda d, n: (d, pidx(d, n), 0)),
                   pl.BlockSpec((None, 2, n_heads, HEAD, HEAD), lambda d, n: (d, n, 0, 0, 0)),
                   pl.BlockSpec((None, 2, n_heads, C, C), lambda d, n: (d, n, 0, 0, 0))] + [any_spec] * len(c_sds),
        out_shape=[jax.ShapeDtypeStruct((2, T, Wd), F32), jax.ShapeDtypeStruct((2, nch, n_heads, HEAD, HEAD), F32),
                   jax.ShapeDtypeStruct((2, nch, n_heads, C, C), F32)] + c_sds,
        scratch_shapes=[pltpu.VMEM((n_heads, HEAD, HEAD), F32)] + c_sems,
        compiler_params=_cparams(("arbitrary", "arbitrary"), has_side_effects=comm is not None), **_CALL_KW)(q, k, v, la, be, *c_ins)
    return res[0], res[1], res[2], list(res[3:])


def _gdn_scan_bwd(name, q, k, v, la, be, states, invs, do, n_heads, ctx, comm=None):
    T, Wd = q.shape
    C = DN_CHUNK
    nch = T // C
    npair = nch // 2
    pidx = _scan_index(ctx // C // 2, npair)
    hi = lax.Precision.HIGHEST

    def body(*refs):
        ins, outs, (dstate,), cref = _hosted(comm, 8, 5, refs)
        q_ref, k_ref, v_ref, la_ref, be_ref, s_ref, x_ref, do_ref = ins
        dq_ref, dk_ref, dv_ref, dla_ref, dbe_ref = outs
        d, n = pl.program_id(0), pl.program_id(1)
        rev = d == 1
        _hosted_start(comm, cref, (d == 0) & (n == 0))

        @pl.when(n == 0)
        def _():
            dstate[...] = jnp.zeros_like(dstate)

        incl, strict, rowsel, eye = _chunk_masks(rev)
        tri = _cumsum_mats(rev)
        offs = [pl.multiple_of(jnp.where(rev, C, 0), C), pl.multiple_of(jnp.where(rev, 0, C), C)]
        views, qkv, dos = [], [], []
        for off in offs:
            rows = pl.ds(off, C)
            g = jnp.dot(tri, la_ref[rows, :], precision=hi, preferred_element_type=F32)
            g_col, g_row, b_col, sels, selts = _gate_views(g, g.T, be_ref[rows, :], d, n_heads)
            views.append((g_col, g_row, b_col))
            qkv.append(tuple(_head_stack(r, n_heads, rows) for r in (q_ref, k_ref, v_ref)))
            dos.append(_head_stack(do_ref, n_heads, rows))
        step = functools.partial(_chunk_step, incl=incl, rowsel=rowsel)
        dS = dstate[...]
        part = [None, None]
        for i in (1, 0):
            _, vjp_step = jax.vjp(step, *qkv[i], *views[i], s_ref[i], x_ref[i])
            *part[i], dS, dX = vjp_step((dos[i], dS))
            part[i].append(dX)
        dstate[...] = dS
        X, dX = _cat([x_ref[0], x_ref[1]]), _cat([part[0][6], part[1][6]])
        dlow = -_mm3_raw(_mm3_raw(X, dX, _TN), X, _NT)
        low_fn = functools.partial(_chunk_lower, incl=incl, strict=strict)
        _, vjp_low = jax.vjp(low_fn, _cat([qkv[0][1], qkv[1][1]]), *[_cat([views[0][j], views[1][j]]) for j in range(3)])
        dk2, dgc2, dgr2, dbc2 = vjp_low(dlow)
        for i, off in enumerate(offs):
            rows = pl.ds(off, C)
            sl = slice(i * n_heads, (i + 1) * n_heads)
            dq, dk1, dv_, dgc1, dgr1, dbc1, _ = part[i]
            dk, dgc, dgr, dbc = dk1 + dk2[sl], dgc1 + dgc2[sl], dgr1 + dgr2[sl], dbc1 + dbc2[sl]
            dg = jnp.zeros((C, HEAD), F32)
            dgt = jnp.zeros((HEAD, C), F32)
            dbe = jnp.zeros((C, HEAD), F32)
            for h in range(n_heads):
                cols = slice(h * HEAD, (h + 1) * HEAD)
                dq_ref[rows, cols], dk_ref[rows, cols], dv_ref[rows, cols] = dq[h], dk[h], dv_[h]
                dg = dg + dgc[h] * sels[h]
                dgt = dgt + selts[h] * dgr[h]
                dbe = dbe + dbc[h] * sels[h]
            dla_ref[rows, :] = lax.dot_general(tri, dg + dgt.T, _DIMS["tn"], precision=hi, preferred_element_type=F32)
            dbe_ref[rows, :] = dbe
        _hosted_wait(comm, cref, (d == 1) & (n == npair - 1))

    rn = lambda d, n: pidx(d, npair - 1 - n)
    tok = lambda w: pl.BlockSpec((2 * C, w), lambda d, n: (rn(d, n), 0))
    otok = lambda w: pl.BlockSpec((None, 2 * C, w), lambda d, n: (d, rn(d, n), 0))
    any_spec = pl.BlockSpec(memory_space=pl.ANY)
    c_ins, c_sds, c_sems = (comm.ins, comm.out_sds(), comm.sem_shapes()) if comm is not None else ([], [], [])
    res = pl.pallas_call(
        body, name=name, grid=(2, npair),
        in_specs=[tok(Wd), tok(Wd), tok(Wd), tok(HEAD), tok(HEAD),
                  pl.BlockSpec((None, 2, n_heads, HEAD, HEAD), lambda d, n: (d, npair - 1 - n, 0, 0, 0)),
                  pl.BlockSpec((None, 2, n_heads, C, C), lambda d, n: (d, npair - 1 - n, 0, 0, 0)), tok(Wd)] + [any_spec] * len(c_ins),
        out_specs=[otok(Wd), otok(Wd), otok(Wd), otok(HEAD), otok(HEAD)] + [any_spec] * len(c_sds),
        out_shape=[jax.ShapeDtypeStruct((2, T, Wd), F32)] * 3 + [jax.ShapeDtypeStruct((2, T, HEAD), F32)] * 2 + c_sds,
        scratch_shapes=[pltpu.VMEM((n_heads, HEAD, HEAD), F32)] + c_sems,
        compiler_params=_cparams(("arbitrary", "arbitrary"), has_side_effects=comm is not None), **_CALL_KW)(
            q, k, v, la, be, states, invs, do, *c_ins)
    return tuple(res[:5]) + (list(res[5:]),)


def _gated_norm(name, o2, p, zblk, g, n_heads, width):
    _, T, Wd = o2.shape

    def body(o_ref, z_ref, g_ref, y_ref):
        for h in range(n_heads):
            cols = slice(h * HEAD, (h + 1) * HEAD)
            ov = o_ref[0, :, cols] + o_ref[1, :, cols]
            zv = z_ref[:, cols].astype(F32)
            y = ov * lax.rsqrt(jnp.mean(ov * ov, axis=-1, keepdims=True) + EPS) * g_ref[...]
            y_ref[:, cols] = (y * (zv * _sigmoid(zv))).astype(BF16)

    return pl.pallas_call(
        body, name=name, grid=(T // TOK,),
        in_specs=[pl.BlockSpec((2, TOK, Wd), lambda i: (0, i, 0)), pl.BlockSpec((TOK, Wd), lambda i: (i, zblk)),
                  pl.BlockSpec((1, HEAD), lambda i: (0, 0))],
        out_specs=pl.BlockSpec((TOK, Wd), lambda i: (i, 0)), out_shape=jax.ShapeDtypeStruct((T, width), BF16),
        compiler_params=_cparams(("parallel",)), **_CALL_KW)(o2, p, g)


def _gated_norm_bwd(name, o2, p, zblk, g, n_heads, dmix, dblk):
    _, T, Wd = o2.shape

    def body(o_ref, z_ref, g_ref, dy_ref, do_ref, dz_ref, dg_ref):
        i = pl.program_id(0)

        @pl.when(i == 0)
        def _():
            dg_ref[...] = jnp.zeros_like(dg_ref)

        dg = jnp.zeros((1, HEAD), F32)
        for h in range(n_heads):
            cols = slice(h * HEAD, (h + 1) * HEAD)
            ov = o_ref[0, :, cols] + o_ref[1, :, cols]
            zv = z_ref[:, cols].astype(F32)
            dy = dy_ref[:, cols].astype(F32)
            r = lax.rsqrt(jnp.mean(ov * ov, axis=-1, keepdims=True) + EPS)
            on = ov * r
            sg = _sigmoid(zv)
            sz = zv * sg
            dz_ref[:, cols] = (dy * (on * g_ref[...]) * (sg * (1.0 + zv * (1.0 - sg)))).astype(BF16)
            dyn = dy * sz
            dg = dg + jnp.sum(dyn * on, axis=0, keepdims=True)
            u = dyn * g_ref[...]
            do_ref[:, cols] = r * (u - on * jnp.mean(u * on, axis=-1, keepdims=True))
        dg_ref[...] += dg

    row = pl.BlockSpec((TOK, Wd), lambda i: (i, 0))
    one = pl.BlockSpec((1, HEAD), lambda i: (0, 0))
    return pl.pallas_call(
        body, name=name, grid=(T // TOK,),
        in_specs=[pl.BlockSpec((2, TOK, Wd), lambda i: (0, i, 0)), pl.BlockSpec((TOK, Wd), lambda i: (i, zblk)), one,
                  pl.BlockSpec((TOK, Wd), lambda i: (i, dblk))],
        out_specs=[row, pl.BlockSpec((TOK, Wd), lambda i: (i, zblk)), one],
        out_shape=[jax.ShapeDtypeStruct((T, Wd), F32), jax.ShapeDtypeStruct(p.shape, BF16), jax.ShapeDtypeStruct((1, HEAD), F32)],
        compiler_params=_cparams(("arbitrary",)), **_CALL_KW)(o2, p, g, dmix)


class _Dims:
    def __init__(self, D, seq, ctx, ffn):
        self.D, self.seq, self.ctx, self.ffn = D, seq, ctx, ffn
        self.T = seq + ctx
        self.t_pad = -(-(self.T + 128) // TOK) * TOK
        self.ctx_tiles = ctx // TOK
        nh = D // HEAD
        self.swa_h, self.kv_h, self.dn_h = nh // 4, nh // 8, nh // 2
        self.na_h = nh - self.swa_h - self.dn_h
        self.swa_q, self.swa_kv, self.Wd, self.na = self.swa_h * HEAD, self.kv_h * HEAD, self.dn_h * HEAD, self.na_h * HEAD
        self.n_ab = 4 * self.dn_h
        self.o_ab = self.swa_q + 2 * self.swa_kv + 4 * self.Wd
        self.n_in = self.o_ab + self.n_ab + 3 * self.na
        self.n_main = self.n_in - self.n_ab
        assert ctx % TOK == 0 and seq % TOK == 0 and self.swa_q == 2 * self.swa_kv == self.na and 2 * self.na == self.Wd


def _rope_tables(dm):
    t = jnp.arange(dm.t_pad, dtype=jnp.int32) - dm.ctx
    lat = (t >= 0) & (t < dm.seq)
    row = (t // GRID_W).astype(F32)
    col = (t % GRID_W).astype(F32)
    n_freq = HEAD // 4
    inv = ROPE_THETA ** (-jnp.arange(n_freq, dtype=F32) / n_freq)
    ang = jnp.concatenate([row[:, None] * inv, row[:, None] * inv, col[:, None] * inv, col[:, None] * inv], axis=-1)
    ang = jnp.where(lat[:, None], ang, 0.0)
    return jnp.cos(ang), jnp.sin(ang)


def _bias_indices():
    o = np.arange(NA_KH)[:, None]
    jr = np.arange(NA_KH)[None, :]
    idx_r = jr - o + (NA_KH - 1)
    cols = np.arange(GRID_W)
    idx_c = np.clip(cols[None, :] - cols[:, None], -(NA_KW - 1), NA_KW - 1) + (NA_KW - 1)
    return idx_r, idx_c


def _bias_onehots():
    idx_r, idx_c = _bias_indices()
    sel_r = (idx_r.reshape(-1)[:, None] == np.arange(2 * NA_KH)[None, :]).astype(np.float32)
    sel_c = (np.arange(HEAD)[:, None] == idx_c.reshape(-1)[None, :]).astype(np.float32)
    return jnp.asarray(sel_r), jnp.asarray(sel_c)


def _rpb_pad(rpb):
    return jnp.pad(rpb, ((0, 0), (0, 2 * NA_KH - rpb.shape[1]), (0, HEAD - rpb.shape[2])))


def _bias_table(name, rpb):
    H = rpb.shape[0]
    sel_r, sel_c = _bias_onehots()
    hi = lax.Precision.HIGHEST

    def body(r_ref, sr_ref, sc_ref, o_ref):
        t = jnp.dot(sr_ref[...], r_ref[...], precision=hi, preferred_element_type=F32)
        o_ref[...] = jnp.dot(t, sc_ref[...], precision=hi, preferred_element_type=F32)

    n_r, n_c = sel_r.shape[0], sel_c.shape[1]
    tab = pl.pallas_call(
        body, name=name, grid=(H,),
        in_specs=[pl.BlockSpec((None, 2 * NA_KH, HEAD), lambda h: (h, 0, 0)), pl.BlockSpec(sel_r.shape, lambda h: (0, 0)),
                  pl.BlockSpec(sel_c.shape, lambda h: (0, 0))],
        out_specs=pl.BlockSpec((None, n_r, n_c), lambda h: (h, 0, 0)), out_shape=jax.ShapeDtypeStruct((H, n_r, n_c), F32),
        compiler_params=_cparams(("parallel",)), **_CALL_KW)(_rpb_pad(rpb), sel_r, sel_c)
    tab = tab.reshape(H, NA_KH, NA_KH, GRID_W, GRID_W).transpose(0, 1, 3, 2, 4)
    return tab.reshape(H, NA_KH, GRID_W, NA_KH * GRID_W)


def _bias_table_bwd(name, dbias, rpb_shape):
    H = dbias.shape[0]
    sel_r, sel_c = _bias_onehots()
    hi = lax.Precision.HIGHEST
    d = dbias.reshape(H, NA_KH, GRID_W, NA_KH, GRID_W).transpose(0, 1, 3, 2, 4).reshape(H, NA_KH * NA_KH, GRID_W * GRID_W)

    def body(d_ref, sr_ref, sc_ref, o_ref):
        dt = lax.dot_general(d_ref[...], sc_ref[...], _DIMS["nt"], precision=hi, preferred_element_type=F32)
        o_ref[...] = lax.dot_general(sr_ref[...], dt, _DIMS["tn"], precision=hi, preferred_element_type=F32)

    out = pl.pallas_call(
        body, name=name, grid=(H,),
        in_specs=[pl.BlockSpec((None,) + d.shape[1:], lambda h: (h, 0, 0)), pl.BlockSpec(sel_r.shape, lambda h: (0, 0)),
                  pl.BlockSpec(sel_c.shape, lambda h: (0, 0))],
        out_specs=pl.BlockSpec((None, 2 * NA_KH, HEAD), lambda h: (h, 0, 0)),
        out_shape=jax.ShapeDtypeStruct((H, 2 * NA_KH, HEAD), F32),
        compiler_params=_cparams(("parallel",)), **_CALL_KW)(d, sel_r, sel_c)
    return out[:, :rpb_shape[1], :rpb_shape[2]]


def _lane_row(v):
    v = v.reshape(-1)
    return jnp.pad(v, (0, HEAD - v.shape[0])).reshape(1, HEAD)


def _chunks_of(l, chip, gathered, own):
    return [jnp.where(chip == k, own[l], gathered[k]) for k in range(N_CHIPS)]


def _weights_in(dm, l, chip, g_in, own_in):
    w_in = jnp.concatenate(_chunks_of(l, chip, g_in, own_in), axis=1)
    w_main = jnp.concatenate([w_in[:, :dm.o_ab], w_in[:, dm.o_ab + dm.n_ab:]], axis=1)
    w_ab = jnp.pad(w_in[:, dm.o_ab:dm.o_ab + dm.n_ab], ((0, 0), (0, HEAD - dm.n_ab)))
    return dict(main=w_main, ab=w_ab)


def _weights_rest(dm, l, chip, gathered, own):
    g_out, g_gate, g_up, g_down = [_chunks_of(l, chip, g, o) for g, o in zip(gathered, own)]
    w_out = jnp.stack(g_out).reshape(dm.D, dm.D)
    w_out = jnp.concatenate([w_out[dm.swa_q:dm.swa_q + dm.Wd], w_out[:dm.swa_q], w_out[dm.swa_q + dm.Wd:]], axis=0)
    return dict(out=w_out, gu=jnp.concatenate(g_gate + g_up, axis=1), down=jnp.stack(g_down).reshape(dm.ffn, dm.D))


def _layer_fwd(dm, x, W, sp, modv, cos, sin, hosts=None, late_weights=None):
    ct = dm.ctx_tiles
    hosts = hosts or {}
    got = {}
    h = _norm_mod("norm1", x, sp["norm1_g"], modv, 0, ct)
    P = _matmul("in_proj", h, W["main"], "nn")
    Pab = _matmul("in_proj_ab", h, W["ab"], "nn", tn=HEAD)
    one = jnp.ones((1, HEAD), F32)
    qa = _head_prep("swa_q_prep", P, 0, dm.swa_h, sp["swa_q_g"], cos, sin, dm.t_pad, True, True)
    ka = _head_prep("swa_k_prep", P, 2, dm.kv_h, sp["swa_k_g"], cos, sin, dm.t_pad, True, True)
    va = _head_prep("swa_v_prep", P, 3, dm.kv_h, one, cos, sin, dm.t_pad, False, False)
    qn = _head_prep("na_q_prep", P, 10, dm.na_h, sp["na_q_g"], cos, sin, dm.t_pad, True, False)
    kn = _head_prep("na_k_prep", P, 11, dm.na_h, sp["na_k_g"], cos, sin, dm.t_pad, True, False)
    vn = _head_prep("na_v_prep", P, 12, dm.na_h, one, cos, sin, dm.t_pad, False, False)
    no_sink = jnp.full((dm.na_h,), NEG, F32)
    bias = _bias_table("na_bias", sp["na_rpb"])
    pre = _conv_fwd("dn_conv", P, 1, sp["dn_conv_w"], ct)
    a_row, dt_row = _lane_row(sp["dn_A_log"]), _lane_row(sp["dn_dt_bias"])
    qh, kh, vh, la, be = _gdn_point("dn_point", pre, Pab, a_row, dt_row, dm.dn_h)
    o2, states, invs, got["scan"] = _gdn_scan("dn_scan", qh, kh, vh, la, be, dm.dn_h, dm.ctx, hosts.get("scan"))
    if late_weights is not None:
        W = dict(W, **late_weights(got["scan"]))
    mix = _gated_norm("dn_out_norm", o2, P, 4, sp["dn_out_g"], dm.dn_h, dm.D)
    mix, _ = _attn_fwd("swa_fwd", "swa", qa, ka, va, sp["swa_sink"], None, dm.ctx, dm.seq, mix, dm.Wd // HEAD)
    mix, got["na"] = _attn_fwd("na_fwd", "na", qn, kn, vn, no_sink, bias, dm.ctx, dm.seq, mix, (dm.Wd + dm.swa_q) // HEAD,
                               hosts.get("na"))
    ao = _matmul("out_proj", mix, W["out"], "nn")
    x1 = _resid_gate("resid1", x, ao, modv, 2, ct)
    h2 = _norm_mod("norm2", x1, sp["norm2_g"], modv, 3, ct)
    gu = _matmul("ffn_gate_up", h2, W["gu"], "nn", out_dtype=BF16, comm=hosts.get("gu"))
    if hosts.get("gu") is not None:
        gu, got["gu"] = gu
    act = _swiglu("ffn_act", gu)
    fo = _matmul("ffn_down", act, W["down"], "nn", comm=hosts.get("down"))
    if hosts.get("down") is not None:
        fo, got["down"] = fo
    x2 = _resid_gate("resid2", x1, fo, modv, 5, ct)
    res = dict(x=x, h=h, P=P, Pab=Pab, qa=qa, ka=ka, va=va, qn=qn, kn=kn, vn=vn, bias=bias, no_sink=no_sink, pre=pre,
               a_row=a_row, dt_row=dt_row, qh=qh, kh=kh, vh=vh, la=la, be=be, o2=o2, states=states, invs=invs, mix=mix,
               ao=ao, x1=x1, h2=h2, gu=gu, act=act, fo=fo)
    return x2, res, got, W


def _layer_bwd(dm, dx2, W, sp, modv, cos, sin, r, host, host_in):
    ct = dm.ctx_tiles
    T, D = dm.T, dm.D
    one = jnp.ones((1, HEAD), F32)
    dfo, dgate2 = _resid_gate_bwd("resid2_bwd", dx2, r["fo"], modv, 5, ct)
    dact = _matmul("ffn_down_dx", dfo, W["down"], "nt")
    dw_down = _matmul("ffn_down_dw", r["act"], dfo, "tn")
    dgu = _swiglu_bwd("ffn_act_bwd", r["gu"], dact)
    dh2 = _matmul("ffn_gate_up_dx", dgu, W["gu"], "nt")
    dw_gu = _matmul("ffn_gate_up_dw", r["h2"], dgu, "tn")
    zero = jnp.zeros((T, D), F32)
    dx1, dn2g, dsh2, dsc2 = _norm_mod_bwd("norm2_bwd", r["x1"], sp["norm2_g"], modv, 3, dh2, zero, dx2, ct)
    dao, dgate1 = _resid_gate_bwd("resid1_bwd", dx1, r["ao"], modv, 2, ct)
    dmix = _matmul("out_proj_dx", dao, W["out"], "nt")
    dw_out = _matmul("out_proj_dw", r["mix"], dao, "tn")
    do_, dP, d_out_g = _gated_norm_bwd("dn_out_norm_bwd", r["o2"], r["P"], 4, sp["dn_out_g"], dm.dn_h, dmix, 0)
    comm = host(dict(out=dw_out, gu=dw_gu, down=dw_down)) if host is not None else None
    dq2, dk2, dv2, dla2, dbe2, hosted = _gdn_scan_bwd("dn_scan_bwd", r["qh"], r["kh"], r["vh"], r["la"], r["be"], r["states"],
                                                      r["invs"], do_, dm.dn_h, dm.ctx, comm)
    dpre, dPab, d_alog, d_dtb = _gdn_point_bwd("dn_point_bwd", r["pre"], r["Pab"], r["a_row"], r["dt_row"], dm.dn_h,
                                               dq2, dk2, dv2, dla2, dbe2)
    dP, d_conv = _conv_bwd("dn_conv_bwd", r["P"], 1, sp["dn_conv_w"], dpre, ct, dP)
    dqa, dka, dva, dsink, _ = _attn_bwd("swa_bwd", "swa", r["qa"], r["ka"], r["va"], sp["swa_sink"], None, dmix,
                                        dm.Wd // HEAD, dm.ctx, dm.seq)
    dP, d_swa_q_g = _head_prep_bwd("swa_q_prep_bwd", r["P"], 0, dm.swa_h, sp["swa_q_g"], cos, sin, dqa, True, True, dP)
    dP, d_swa_k_g = _head_prep_bwd("swa_k_prep_bwd", r["P"], 2, dm.kv_h, sp["swa_k_g"], cos, sin, dka, True, True, dP)
    dP, _ = _head_prep_bwd("swa_v_prep_bwd", r["P"], 3, dm.kv_h, one, cos, sin, dva, False, False, dP)
    dqn, dkn, dvn, _, dbias = _attn_bwd("na_bwd", "na", r["qn"], r["kn"], r["vn"], r["no_sink"], r["bias"], dmix,
                                        (dm.Wd + dm.swa_q) // HEAD, dm.ctx, dm.seq)
    dP, d_na_q_g = _head_prep_bwd("na_q_prep_bwd", r["P"], 10, dm.na_h, sp["na_q_g"], cos, sin, dqn, True, False, dP)
    dP, d_na_k_g = _head_prep_bwd("na_k_prep_bwd", r["P"], 11, dm.na_h, sp["na_k_g"], cos, sin, dkn, True, False, dP)
    dP, _ = _head_prep_bwd("na_v_prep_bwd", r["P"], 12, dm.na_h, one, cos, sin, dvn, False, False, dP)
    d_rpb = _bias_table_bwd("na_bias_bwd", dbias, sp["na_rpb"].shape)
    dw_main = _matmul("in_proj_dw", r["h"], dP, "tn")
    dw_ab = _matmul("in_proj_ab_dw", r["h"], dPab, "tn", tn=HEAD)
    dh, hosted_in = _matmul("in_proj_dx", dP, W["main"], "nt", comm=host_in(dict(main=dw_main, ab=dw_ab)))
    dh_b = _matmul("in_proj_ab_dx", dPab, W["ab"], "nt")
    dx, dn1g, dsh1, dsc1 = _norm_mod_bwd("norm1_bwd", r["x"], sp["norm1_g"], modv, 0, dh, dh_b, dx1, ct)
    dmodv = jnp.concatenate([dsh1, dsc1, dgate1, dsh2, dsc2, dgate2], axis=1)
    big = dict(main=dw_main, ab=dw_ab, out=dw_out, gu=dw_gu, down=dw_down)
    small = dict(norm1_g=dn1g[0], norm2_g=dn2g[0], swa_q_g=d_swa_q_g[0], swa_k_g=d_swa_k_g[0], swa_sink=dsink[:, 0, 0],
                 dn_conv_w=d_conv[:DN_CONV], dn_A_log=d_alog[0, :2 * dm.dn_h].reshape(2, dm.dn_h),
                 dn_dt_bias=d_dtb[0, :2 * dm.dn_h].reshape(2, dm.dn_h), dn_out_g=d_out_g[0], na_q_g=d_na_q_g[0],
                 na_k_g=d_na_k_g[0], na_rpb=d_rpb)
    return dx, big, small, dmodv, hosted, hosted_in


def _cols(w):
    return w.reshape(w.shape[0], N_CHIPS, -1).transpose(1, 0, 2)


def _rows(w):
    return w.reshape(N_CHIPS, -1, w.shape[1])


def _chunks_in(dm, bigs):
    g = [_cols(jnp.concatenate([b["main"][:, :dm.o_ab], b["ab"][:, :dm.n_ab], b["main"][:, dm.o_ab:]], axis=1)) for b in bigs]
    return jnp.stack(g, axis=1).astype(BF16)


def _chunks_rest(dm, bigs):
    g_out = [_rows(jnp.concatenate([b["out"][dm.Wd:dm.Wd + dm.swa_q], b["out"][:dm.Wd], b["out"][dm.Wd + dm.swa_q:]], axis=0))
             for b in bigs]
    g_gate = [_cols(b["gu"][:, :dm.ffn]) for b in bigs]
    g_up = [_cols(b["gu"][:, dm.ffn:]) for b in bigs]
    g_down = [_rows(b["down"]) for b in bigs]
    return [jnp.stack(g, axis=1).astype(BF16) for g in (g_out, g_gate, g_up, g_down)]


SMALL = ("norm1_g", "norm2_g", "swa_q_g", "swa_k_g", "swa_sink", "dn_conv_w", "dn_A_log", "dn_dt_bias", "dn_out_g",
         "na_q_g", "na_k_g", "na_rpb")


def _pack(arrs):
    flat = jnp.concatenate([a.reshape(-1).astype(F32) for a in arrs])
    n = flat.shape[0]
    rows = -(-n // (8 * HEAD)) * 8
    return jnp.pad(flat, (0, rows * HEAD - n)).reshape(rows, HEAD)


def _unpack(packed, like):
    flat = packed.reshape(-1)
    out, o = [], 0
    for a in like:
        out.append(flat[o:o + a.size].reshape(a.shape))
        o += a.size
    return out


def _sum_devices(name, g, which):
    _, R, _ = g.shape

    def body(g_ref, o_ref):
        acc = g_ref[which[0]]
        for b in which[1:]:
            acc = acc + g_ref[b]
        o_ref[...] = acc

    return pl.pallas_call(
        body, name=name, grid=(R // 8,), in_specs=[pl.BlockSpec((N_DEV, 8, HEAD), lambda i: (0, i, 0))],
        out_specs=pl.BlockSpec((8, HEAD), lambda i: (i, 0)), out_shape=jax.ShapeDtypeStruct((R, HEAD), F32),
        compiler_params=_cparams(("parallel",)), **_CALL_KW)(g)


def _silu_rows(name, c_rows):
    return _elementwise(name, lambda c: (c * _sigmoid(c),), [c_rows], [BF16])[0]


def _ada_cotangent(name, dm_all, b_ada_shape):
    _, L, _, N6 = dm_all.shape
    tn = _pick(N6, (1024, 512, 256, 128))

    def body(d_ref, o_ref, b_ref):
        csum = d_ref[0, 0:1, :]
        for b in range(1, N_DEV):
            csum = csum + d_ref[b, 0:1, :]
        tot = csum
        for b in range(N_DEV):
            o_ref[b:b + 1, :] = d_ref[b, 1:2, :]
            tot = tot + d_ref[b, 1:2, :]
        first = lax.broadcasted_iota(jnp.int32, (8, tn), 0) == 0
        o_ref[N_DEV:, :] = jnp.where(first, jnp.broadcast_to(csum, (8, tn)), 0.0)
        b_ref[...] = jnp.broadcast_to(tot, (8, tn))

    return pl.pallas_call(
        body, name=name, grid=(L, N6 // tn), in_specs=[pl.BlockSpec((N_DEV, None, 2, tn), lambda l, j: (0, l, 0, j))],
        out_specs=[pl.BlockSpec((None, 16, tn), lambda l, j: (l, 0, j)), pl.BlockSpec((None, 8, tn), lambda l, j: (l, 0, j))],
        out_shape=[jax.ShapeDtypeStruct((L, 16, N6), F32), jax.ShapeDtypeStruct((L, 8, N6), F32)],
        compiler_params=_cparams(("parallel", "parallel")), **_CALL_KW)(dm_all)


def kernel(x, c, ctx, c_ctx, w_ada, b_ada, norm1_g, norm2_g, w_in, swa_q_g, swa_k_g, swa_sink, dn_conv_w, dn_A_log, dn_dt_bias, dn_out_g, na_q_g, na_k_g, na_rpb, w_out, w_gate, w_up, w_down, loss_target, m_c_ctx, m_w_ada, m_b_ada, m_norm1_g, m_norm2_g, m_w_in, m_swa_q_g, m_swa_k_g, m_swa_sink, m_dn_conv_w, m_dn_A_log, m_dn_dt_bias, m_dn_out_g, m_na_q_g, m_na_k_g, m_na_rpb, m_w_out, m_w_gate, m_w_up, m_w_down, v_c_ctx, v_w_ada, v_b_ada, v_norm1_g, v_norm2_g, v_w_in, v_swa_q_g, v_swa_k_g, v_swa_sink, v_dn_conv_w, v_dn_A_log, v_dn_dt_bias, v_dn_out_g, v_na_q_g, v_na_k_g, v_na_rpb, v_w_out, v_w_gate, v_w_up, v_w_down):
    L = w_in.shape[0]
    D, seq, n_ctx = x.shape[-1], x.shape[1], ctx.shape[1]
    dm = _Dims(D, seq, n_ctx, w_gate.shape[-1] * N_CHIPS)
    xi, yi, ci = _axes()
    chip = 2 * xi + yi
    dev = 4 * xi + 2 * yi + ci
    n6 = 6 * D
    n6s = n6 // N_CHIPS

    shards = [_elementwise(f"cast_{n}", lambda w: (w,), [w], [BF16])[0]
              for n, w in (("w_in", w_in), ("w_out", w_out), ("w_gate", w_gate), ("w_up", w_up), ("w_down", w_down))]
    assert L == 2
    g_in0 = _gather_d2d("gather_w0a_d2d", _exchange("gather_w0a_ici", _gather_ici_comm(shards[:1], 0)))
    Ws = [_weights_in(dm, 0, chip, g_in0[0], shards[0]), None]
    conv_all = _allgather_small("gather_conv_w", _pack([dn_conv_w]))
    conv_full = jnp.concatenate([_unpack(conv_all[2 * k], [dn_conv_w])[0] for k in range(N_CHIPS)], axis=-1)

    c_all = _allgather_small("gather_c", _pack([c]))
    c_rows = jnp.concatenate([c_all[:, :D // HEAD].reshape(N_DEV, D), c_ctx[None], jnp.zeros((16 - N_DEV - 1, D), F32)], axis=0)
    a_rows = _silu_rows("ada_silu", c_rows)
    b_sh = lax.dynamic_slice_in_dim(b_ada, chip * n6s, n6s, axis=1)
    mod_sh = [_matmul(f"ada_mod{l}", a_rows, w_ada[l], "nn", tm=16) for l in range(L)]
    mod_all = _allgather_small("gather_mod", _pack(mod_sh))
    mods = []
    for l in range(L):
        per_chip = [_unpack(mod_all[2 * k], mod_sh)[l] for k in range(N_CHIPS)]
        mods.append(jnp.concatenate(per_chip, axis=1))
    modvs = []
    for l in range(L):
        rows = jnp.stack([mods[l][N_DEV], lax.dynamic_index_in_dim(mods[l], dev, 0, keepdims=False)])
        modvs.append(_elementwise(f"ada_bias{l}", lambda m, b: (m + b,), [rows, jnp.broadcast_to(b_ada[l][None], (2, n6))], [F32])[0]
                     .reshape(2, 6, D))

    cos, sin = _rope_tables(dm)
    sps = [dict(norm1_g=norm1_g[l][None], norm2_g=norm2_g[l][None], swa_q_g=swa_q_g[l][None], swa_k_g=swa_k_g[l][None],
                swa_sink=swa_sink[l], dn_conv_w=conv_full[l], dn_A_log=dn_A_log[l], dn_dt_bias=dn_dt_bias[l],
                dn_out_g=dn_out_g[l][None], na_q_g=na_q_g[l][None], na_k_g=na_k_g[l][None], na_rpb=na_rpb[l]) for l in range(L)]
    xs = jnp.concatenate([ctx[0], x[0]], axis=0)
    ress = [None] * L
    w_in_s, w_out_s, w_gate_s, w_up_s, w_down_s = shards
    hosts0 = dict(scan=_gather_ici_comm(shards[1:], 0),
                  na=_gather_ici_comm([w_in_s], 1), gu=_gather_ici_comm([w_gate_s, w_up_s], 1),
                  down=_gather_ici_comm([w_out_s, w_down_s], 1))

    def late0(arrived):
        return _weights_rest(dm, 0, chip, _gather_d2d("gather_w0b_d2d", arrived), shards[1:])

    xs, ress[0], got0, Ws[0] = _layer_fwd(dm, xs, Ws[0], sps[0], modvs[0], cos, sin, hosts0, late0)
    g1 = _gather_d2d("gather_w1_d2d", got0["na"] + [got0["down"][0]] + got0["gu"] + [got0["down"][1]])
    Ws[1] = dict(_weights_in(dm, 1, chip, g1[0], w_in_s), **_weights_rest(dm, 1, chip, g1[1:], shards[1:]))
    xs, ress[1], _, _ = _layer_fwd(dm, xs, Ws[1], sps[1], modvs[1], cos, sin)
    loss_blk, dxs = _loss_and_grad("loss", xs, loss_target[0], dm.ctx_tiles)
    loss = lax.psum(loss_blk[0, 0], ("x", "y", "c"))

    bigs, smalls, dmodvs = [None] * L, [None] * L, [None] * L
    pairs, reduced = {}, {}

    def hosts_of(l):
        def host(big):
            pairs["r", l] = _reduce_pre(f"rest{l}", _chunks_rest(dm, [big]))
            return _reduce_ici_comm(pairs["r", l])

        def host_in(big):
            pairs["i", l] = _reduce_pre(f"in{l}", [_chunks_in(dm, [big])])
            return _reduce_ici_comm(pairs["i", l])

        return host, host_in

    for l in reversed(range(L)):
        dxs, bigs[l], smalls[l], dmodvs[l], got_r, got_i = _layer_bwd(dm, dxs, Ws[l], sps[l], modvs[l], cos, sin, ress[l], *hosts_of(l))
        reduced["r", l] = _reduce_post(f"rest{l}", pairs["r", l], got_r)
        reduced["i", l] = _reduce_post(f"in{l}", pairs["i", l], got_i)
    g_in = jnp.concatenate([reduced["i", l][0] for l in range(L)], axis=0)
    g_out, g_gate, g_up, g_down = [jnp.concatenate([reduced["r", l][j] for l in range(L)], axis=0) for j in range(4)]
    grad_x = dxs[n_ctx:][None]

    dm_mine = jnp.stack([d.reshape(2, n6) for d in dmodvs])
    dm_all = _allgather_small("gather_dmod", _pack([dm_mine]))
    dm_all = jnp.stack([_unpack(dm_all[b], [dm_mine])[0] for b in range(N_DEV)])
    dm_rows, d_b_ada = _ada_cotangent("ada_cot", dm_all, b_ada.shape)
    dm_sh = lax.dynamic_slice_in_dim(dm_rows, chip * n6s, n6s, axis=2).astype(BF16)
    g_w_ada = jnp.stack([_matmul(f"ada_dw{l}", a_rows, dm_sh[l], "tn") for l in range(L)])
    dc_part = [_matmul(f"ada_dc{l}", dm_sh[l], w_ada[l], "nt", tm=16) for l in range(L)]
    dc_mine = dc_part[0][N_DEV]
    for l in range(1, L):
        dc_mine = dc_mine + dc_part[l][N_DEV]

    small_list = [jnp.stack([smalls[l][n] for l in range(L)]) for n in SMALL]
    sm_all = _allgather_small("gather_small", _pack(small_list + [dc_mine]))
    sm_sum = _sum_devices("sum_small", sm_all, tuple(range(N_DEV)))
    dc_sum = _sum_devices("sum_dc", sm_all, tuple(range(0, N_DEV, 2)))
    g_small = dict(zip(SMALL, _unpack(sm_sum, small_list)))
    dcs = _unpack(dc_sum, small_list + [dc_mine])[-1]
    def silu_bwd(d, cc):
        s = _sigmoid(cc)
        return (d * (s * (1.0 + cc * (1.0 - s))),)

    g_c_ctx = _elementwise("c_ctx_silu_bwd", silu_bwd, [dcs.reshape(-1, HEAD), c_ctx.reshape(-1, HEAD)], [F32])[0]
    wd3 = dn_conv_w.shape[-1]
    g_small["dn_conv_w"] = lax.dynamic_slice_in_dim(g_small["dn_conv_w"], chip * wd3, wd3, axis=2)

    grads = dict(g_small, c_ctx=g_c_ctx, w_ada=g_w_ada, b_ada=d_b_ada[:, 0], w_in=g_in, w_out=g_out, w_gate=g_gate, w_up=g_up,
                 w_down=g_down)
    weights = dict(c_ctx=c_ctx, w_ada=w_ada, b_ada=b_ada, norm1_g=norm1_g, norm2_g=norm2_g, w_in=w_in, swa_q_g=swa_q_g,
                   swa_k_g=swa_k_g, swa_sink=swa_sink, dn_conv_w=dn_conv_w, dn_A_log=dn_A_log, dn_dt_bias=dn_dt_bias,
                   dn_out_g=dn_out_g, na_q_g=na_q_g, na_k_g=na_k_g, na_rpb=na_rpb, w_out=w_out, w_gate=w_gate, w_up=w_up,
                   w_down=w_down)
    ms = dict(c_ctx=m_c_ctx, w_ada=m_w_ada, b_ada=m_b_ada, norm1_g=m_norm1_g, norm2_g=m_norm2_g, w_in=m_w_in, swa_q_g=m_swa_q_g,
              swa_k_g=m_swa_k_g, swa_sink=m_swa_sink, dn_conv_w=m_dn_conv_w, dn_A_log=m_dn_A_log, dn_dt_bias=m_dn_dt_bias,
              dn_out_g=m_dn_out_g, na_q_g=m_na_q_g, na_k_g=m_na_k_g, na_rpb=m_na_rpb, w_out=m_w_out, w_gate=m_w_gate, w_up=m_w_up,
              w_down=m_w_down)
    vs = dict(c_ctx=v_c_ctx, w_ada=v_w_ada, b_ada=v_b_ada, norm1_g=v_norm1_g, norm2_g=v_norm2_g, w_in=v_w_in, swa_q_g=v_swa_q_g,
              swa_k_g=v_swa_k_g, swa_sink=v_swa_sink, dn_conv_w=v_dn_conv_w, dn_A_log=v_dn_A_log, dn_dt_bias=v_dn_dt_bias,
              dn_out_g=v_dn_out_g, na_q_g=v_na_q_g, na_k_g=v_na_k_g, na_rpb=v_na_rpb, w_out=v_w_out, w_gate=v_w_gate, w_up=v_w_up,
              w_down=v_w_down)
    order = ("c_ctx", "w_ada", "b_ada", "norm1_g", "norm2_g", "w_in", "swa_q_g", "swa_k_g", "swa_sink", "dn_conv_w", "dn_A_log",
             "dn_dt_bias", "dn_out_g", "na_q_g", "na_k_g", "na_rpb", "w_out", "w_gate", "w_up", "w_down")
    big_names = ("w_ada", "w_in", "w_out", "w_gate", "w_up", "w_down")
    grads = {n: grads[n].reshape(weights[n].shape) for n in order}
    delta, new_m, new_v = {}, {}, {}
    for n in big_names:
        delta[n], new_m[n], new_v[n] = _adamw(f"adamw_{n}", weights[n], grads[n], ms[n], vs[n])
    small_names = [n for n in order if n not in big_names]
    packed = [_pack([d[n] for n in small_names]) for d in (weights, grads, ms, vs)]
    outs = _adamw("adamw_small", *packed)
    like = [weights[n] for n in small_names]
    for d, o in zip((delta, new_m, new_v), outs):
        d.update(dict(zip(small_names, _unpack(o, like))))
    return (loss, grad_x, *[grads[n] for n in order], *[delta[n] for n in order], *[new_m[n] for n in order],
            *[new_v[n] for n in order])
```
